```python
import jax
import jax.numpy as jnp
from jax import lax
import numpy as np

D_MODEL = 1024
BATCH = 16
SEQ = 2048
DEPTH = 2

GRID_W = 64
CTX_LEN = 256
RMS_EPS = 1e-6

NA_HEADS = 8
NA_HEAD_DIM = 64
NA_WIDTH = NA_HEADS * NA_HEAD_DIM
NA_KH = 8
NA_KW = 16

POOL_WINDOWS = (2, 4, 8, 16)
POOL_GROUPS = len(POOL_WINDOWS)
POOL_WIDTH = D_MODEL - NA_WIDTH
POOL_GC = POOL_WIDTH // POOL_GROUPS
AB_IN_WIDTH = 3 * NA_WIDTH + POOL_WIDTH
AB_OUT_WIDTH = NA_WIDTH + POOL_WIDTH

RW_HEAD_DIM = 64
RW_HEADS = D_MODEL // RW_HEAD_DIM
RW_DECAY_LORA = 64
RW_AAA_LORA = 64
RW_GATE_LORA = 128
RW_GN_EPS = 64e-5

N_EXPERTS = 32
TOP_K = 4
D_FF = D_MODEL
SWIGLU_LIMIT = 7.0
SWIGLU_ALPHA = 1.702

N_EVEN = (DEPTH + 1) // 2
N_ODD = DEPTH // 2

f32 = jnp.float32

kernel_name = 'hybrid_natten_pool_rwkv7_moe_dit'


def rms_norm(x, g):
    x32 = x.astype(f32)
    y = x32 * lax.rsqrt(jnp.mean(x32 * x32, axis=-1, keepdims=True) + RMS_EPS)
    return (y * g.astype(f32)).astype(x.dtype)


def modulate(x, shift, scale):
    return x * (1.0 + scale) + shift


def ctx_attention(q, k, v):
    s = jnp.einsum('bqhd,bkhd->bhqk', q, k).astype(f32) * (q.shape[-1] ** -0.5)
    p = jax.nn.softmax(s, axis=-1).astype(v.dtype)
    return jnp.einsum('bhqk,bkhd->bqhd', p, v)


def neighbourhood_attention(q, k, v, k_ctx, v_ctx, rpb):
    b, l, h, dh = q.shape
    rows = l // GRID_W
    kh = min(NA_KH, rows)
    scale = dh ** -0.5
    qg = q.reshape(b, rows, GRID_W, h, dh)
    kg = k.reshape(b, rows, GRID_W, h, dh)
    vg = v.reshape(b, rows, GRID_W, h, dh)
    col = jnp.arange(GRID_W)
    c_start = jnp.clip(col - NA_KW // 2, 0, GRID_W - NA_KW)
    c_valid = (col[None, :] >= c_start[:, None]) & (col[None, :] < c_start[:, None] + NA_KW)
    c_rel = jnp.clip(col[None, :] - col[:, None] + NA_KW - 1, 0, 2 * NA_KW - 2)
    n_loc = kh * GRID_W

    def row_block(r):
        r_start = jnp.clip(r - kh // 2, 0, rows - kh)
        q_r = lax.dynamic_index_in_dim(qg, r, axis=1, keepdims=False)
        k_r = lax.dynamic_slice_in_dim(kg, r_start, kh, axis=1)
        v_r = lax.dynamic_slice_in_dim(vg, r_start, kh, axis=1)
        r_rel = r_start + jnp.arange(kh) - r + NA_KH - 1
        bias = rpb[:, r_rel[None, :, None], c_rel[:, None, :]]
        s_loc = jnp.einsum('bqhd,bjkhd->bhqjk', q_r, k_r).astype(f32) * scale + bias.astype(f32)
        s_loc = jnp.where(c_valid[:, None, :], s_loc, -jnp.inf).reshape(b, h, GRID_W, n_loc)
        s_ctx = jnp.einsum('bqhd,bchd->bhqc', q_r, k_ctx).astype(f32) * scale
        p = jax.nn.softmax(jnp.concatenate([s_loc, s_ctx], axis=-1), axis=-1).astype(v.dtype)
        p_loc = p[..., :n_loc].reshape(b, h, GRID_W, kh, GRID_W)
        return (jnp.einsum('bhqjk,bjkhd->bqhd', p_loc, v_r)
                + jnp.einsum('bhqc,bchd->bqhd', p[..., n_loc:], v_ctx))

    o = lax.map(row_block, jnp.arange(rows))
    return jnp.moveaxis(o, 0, 1).reshape(b, l, h * dh)


def multiscale_pool(u, pool_w, pool_scale):
    b, l, _ = u.shape
    ug = u.reshape(b, l, POOL_GROUPS, POOL_GC).astype(f32)
    csum = jnp.concatenate([jnp.zeros_like(ug[:, :1]), jnp.cumsum(ug, axis=1)], axis=1)
    t = jnp.arange(l)
    pooled = []
    for g, w in enumerate(POOL_WINDOWS):
        lo = jnp.clip(t - w // 2, 0, l)
        hi = jnp.clip(t + w - w // 2, 0, l)
        mean = (csum[:, hi, g] - csum[:, lo, g]) / (hi - lo).astype(f32)[:, None]
        pooled.append(mean - ug[:, :, g])
    pooled = jnp.stack(pooled, axis=2)
    y = jnp.einsum('blgc,gcd->blgd', pooled, pool_w.astype(f32)).reshape(b, l, POOL_WIDTH)
    return (y * pool_scale.astype(f32)).astype(u.dtype)


def ab_mixer(xl, xc, need_ctx, w_in, q_g, k_g, rpb, pool_w, pool_scale, w_out):
    def split(p):
        b, l, _ = p.shape
        q, k, v, u = jnp.split(p, [NA_WIDTH, 2 * NA_WIDTH, 3 * NA_WIDTH], axis=-1)
        heads = lambda t: t.reshape(b, l, NA_HEADS, NA_HEAD_DIM)
        return rms_norm(heads(q), q_g), rms_norm(heads(k), k_g), heads(v), u

    q_l, k_l, v_l, u_l = split(xl @ w_in)
    q_c, k_c, v_c, u_c = split(xc @ w_in)
    o_na = neighbourhood_attention(q_l, k_l, v_l, k_c, v_c, rpb)
    out_l = jnp.concatenate([o_na, multiscale_pool(u_l, pool_w, pool_scale)], axis=-1) @ w_out
    out_c = None
    if need_ctx:
        b, lc, _ = xc.shape
        o_na_c = ctx_attention(q_c, k_c, v_c).reshape(b, lc, NA_WIDTH)
        out_c = jnp.concatenate([o_na_c, multiscale_pool(u_c, pool_w, pool_scale)], axis=-1) @ w_out
    return out_l, out_c


def rwkv_features(x, mu, w_r, w_k, w_v, w0, w1, w2, a0, a1, a2, g1, g2, k_k, k_a):
    b, l, _ = x.shape
    x32 = x.astype(f32)
    prev = jnp.pad(x32[:, :-1], ((0, 0), (1, 0), (0, 0)))
    nxt = jnp.pad(x32[:, 1:], ((0, 0), (0, 1), (0, 0)))
    xx = 0.5 * (prev + nxt) - x32
    xr, xw, xk, xv, xa, xg = (x32 + xx * mu[j] for j in range(6))
    heads = lambda t: t.reshape(b, l, RW_HEADS, RW_HEAD_DIM)
    r = xr @ w_r
    k = xk @ w_k
    v = xv @ w_v
    g = jax.nn.sigmoid(xg @ g1) @ g2
    kk = heads(k * k_k)
    kk = kk * lax.rsqrt(jnp.sum(kk * kk, axis=-1, keepdims=True) + 1e-12)
    dirs = []
    for d in range(2):
        w_log = -jax.nn.softplus(-(w0[d] + jnp.tanh(xw @ w1[d]) @ w2[d])) - 0.5
        decay = jnp.exp(-jnp.exp(w_log))
        a = jax.nn.sigmoid(a0[d] + (xa @ a1[d]) @ a2[d])
        k_d = k * (1.0 + (a - 1.0) * k_a)
        dirs.append((heads(decay), heads(k_d), heads(a)))
    return heads(r), heads(v), g, kk, dirs


def rwkv_scan(state0, r, decay, k, v, kk, a, reverse):
    def step(S, inp):
        r_t, w_t, k_t, v_t, kk_t, a_t = inp
        sa = jnp.einsum('bhij,bhj->bhi', S, -kk_t)
        S = (S * w_t[:, :, None, :] + sa[..., None] * (kk_t * a_t)[:, :, None, :]
             + v_t[..., None] * k_t[:, :, None, :])
        return S, jnp.einsum('bhij,bhj->bhi', S, r_t)
    xs = tuple(jnp.moveaxis(t, 1, 0) for t in (r, decay, k, v, kk, a))
    state, ys = lax.scan(step, state0, xs, reverse=reverse)
    return state, jnp.moveaxis(ys, 0, 1)


def rwkv_readout(y, r, v, g, k_dirs, ln_w, ln_b, r_k, w_o, out_dtype):
    b, l, h, dh = y.shape
    mean = jnp.mean(y, axis=-1, keepdims=True)
    var = jnp.mean(jnp.square(y - mean), axis=-1, keepdims=True)
    yn = ((y - mean) * lax.rsqrt(var + RW_GN_EPS)).reshape(b, l, h * dh) * ln_w + ln_b
    bonus = sum(jnp.sum(r * k_d * r_k, axis=-1, keepdims=True) * v for k_d in k_dirs)
    yn = yn + bonus.reshape(b, l, h * dh)
    return ((yn * g) @ w_o).astype(out_dtype)


def rwkv_mixer(xl, xc, need_ctx, mu, w_r, w_k, w_v, w_o, w0, w1, w2, a0, a1, a2, g1, g2,
               k_k, k_a, r_k, ln_w, ln_b):
    feat = (mu, w_r, w_k, w_v, w0, w1, w2, a0, a1, a2, g1, g2, k_k, k_a)
    r_l, v_l, g_l, kk_l, dirs_l = rwkv_features(xl, *feat)
    r_c, v_c, g_c, kk_c, dirs_c = rwkv_features(xc, *feat)
    b = xl.shape[0]
    y_l = 0.0
    y_c = 0.0
    for d, rev in enumerate((False, True)):
        s0 = jnp.zeros((b, RW_HEADS, RW_HEAD_DIM, RW_HEAD_DIM), f32)
        dec_c, kd_c, a_c = dirs_c[d]
        s_ctx, yc = rwkv_scan(s0, r_c, dec_c, kd_c, v_c, kk_c, a_c, rev)
        dec_l, kd_l, a_l = dirs_l[d]
        _, yl = rwkv_scan(s_ctx, r_l, dec_l, kd_l, v_l, kk_l, a_l, rev)
        y_l = y_l + yl
        if need_ctx:
            y_c = y_c + yc
    out_l = rwkv_readout(y_l, r_l, v_l, g_l, [t[1] for t in dirs_l], ln_w, ln_b, r_k, w_o, xl.dtype)
    out_c = None
    if need_ctx:
        out_c = rwkv_readout(y_c, r_c, v_c, g_c, [t[1] for t in dirs_c], ln_w, ln_b, r_k, w_o, xc.dtype)
    return out_l, out_c


def moe_ffn(t, router_w, router_b, w_in, b_in, w_out, b_out):
    logits = (t @ router_w + router_b).astype(f32)
    top_val, top_idx = lax.top_k(logits, TOP_K)
    gates = jax.nn.softmax(top_val, axis=-1)
    combine = jnp.sum(jax.nn.one_hot(top_idx, N_EXPERTS, dtype=f32) * gates[..., None], axis=1)
    y = jnp.zeros(t.shape, f32)
    for e in range(N_EXPERTS):
        hdn = t @ w_in[e] + b_in[e]
        glu = jnp.minimum(hdn[:, 0::2], SWIGLU_LIMIT)
        lin = jnp.clip(hdn[:, 1::2], -SWIGLU_LIMIT, SWIGLU_LIMIT)
        act = glu * jax.nn.sigmoid(SWIGLU_ALPHA * glu) * (lin + 1.0)
        y = y + combine[:, e:e + 1] * (act @ w_out[e] + b_out[e])
    return y.astype(t.dtype)


def setup_inputs(seed: int = 0) -> dict:
    key = jax.random.key(seed)
    ks = iter(jax.random.split(key, 48))
    nrm = lambda shape, s: jax.random.normal(next(ks), shape, f32) * s
    uni = lambda shape, lo, hi: jax.random.uniform(next(ks), shape, f32, lo, hi)
    D = D_MODEL
    return {
        'x': nrm((BATCH, SEQ, D), 1.0),
        'c': nrm((BATCH, D), 1.0),
        'ctx': nrm((BATCH, CTX_LEN, D), 1.0),
        'c_ctx': nrm((D,), 1.0),
        'ada_w': nrm((DEPTH, D, 6 * D), 0.5 * D ** -0.5),
        'ada_b': nrm((DEPTH, 6 * D), 0.02),
        'norm_mix_g': 1.0 + nrm((DEPTH, D), 0.02),
        'norm_ffn_g': 1.0 + nrm((DEPTH, D), 0.02),
        'router_w': nrm((DEPTH, D, N_EXPERTS), D ** -0.5),
        'router_b': nrm((DEPTH, N_EXPERTS), 0.01),
        'exp_w_in': nrm((DEPTH, N_EXPERTS, D, 2 * D_FF), D ** -0.5),
        'exp_b_in': nrm((DEPTH, N_EXPERTS, 2 * D_FF), 0.02),
        'exp_w_out': nrm((DEPTH, N_EXPERTS, D_FF, D), D_FF ** -0.5),
        'exp_b_out': nrm((DEPTH, N_EXPERTS, D), 0.02),
        'ab_w_in': nrm((N_EVEN, D, AB_IN_WIDTH), D ** -0.5),
        'na_q_g': 1.0 + nrm((N_EVEN, NA_HEAD_DIM), 0.02),
        'na_k_g': 1.0 + nrm((N_EVEN, NA_HEAD_DIM), 0.02),
        'na_rpb': nrm((N_EVEN, NA_HEADS, 2 * NA_KH - 1, 2 * NA_KW - 1), 0.5),
        'pool_w': nrm((N_EVEN, POOL_GROUPS, POOL_GC, POOL_GC), POOL_GC ** -0.5),
        'pool_scale': 1.0 + nrm((N_EVEN, POOL_WIDTH), 0.1),
        'ab_w_out': nrm((N_EVEN, AB_OUT_WIDTH, D), AB_OUT_WIDTH ** -0.5),
        'rw_mu': uni((N_ODD, 6, D), 0.0, 1.0),
        'rw_w_r': nrm((N_ODD, D, D), D ** -0.5),
        'rw_w_k': nrm((N_ODD, D, D), D ** -0.5),
        'rw_w_v': nrm((N_ODD, D, D), D ** -0.5),
        'rw_w_o': nrm((N_ODD, D, D), D ** -0.5),
        'rw_w0': uni((N_ODD, 2, D), -4.0, 0.0),
        'rw_w1': nrm((N_ODD, 2, D, RW_DECAY_LORA), D ** -0.5),
        'rw_w2': nrm((N_ODD, 2, RW_DECAY_LORA, D), 0.5 * RW_DECAY_LORA ** -0.5),
        'rw_a0': nrm((N_ODD, 2, D), 0.1),
        'rw_a1': nrm((N_ODD, 2, D, RW_AAA_LORA), D ** -0.5),
        'rw_a2': nrm((N_ODD, 2, RW_AAA_LORA, D), 0.5 * RW_AAA_LORA ** -0.5),
        'rw_g1': nrm((N_ODD, D, RW_GATE_LORA), D ** -0.5),
        'rw_g2': nrm((N_ODD, RW_GATE_LORA, D), RW_GATE_LORA ** -0.5),
        'rw_k_k': 0.85 + nrm((N_ODD, D), 0.05),
        'rw_k_a': 1.0 + nrm((N_ODD, D), 0.05),
        'rw_r_k': nrm((N_ODD, RW_HEADS, RW_HEAD_DIM), 0.1),
        'rw_ln_w': 1.0 + nrm((N_ODD, D), 0.02),
        'rw_ln_b': nrm((N_ODD, D), 0.02),
    }


def reference(x, c, ctx, c_ctx, ada_w, ada_b, norm_mix_g, norm_ffn_g,
              router_w, router_b, exp_w_in, exp_b_in, exp_w_out, exp_b_out,
              ab_w_in, na_q_g, na_k_g, na_rpb, pool_w, pool_scale, ab_w_out,
              rw_mu, rw_w_r, rw_w_k, rw_w_v, rw_w_o, rw_w0, rw_w1, rw_w2, rw_a0, rw_a1, rw_a2,
              rw_g1, rw_g2, rw_k_k, rw_k_a, rw_r_k, rw_ln_w, rw_ln_b):
    h_lat, h_ctx = x, ctx
    for layer in range(DEPTH):
        need_ctx = layer < DEPTH - 1
        mod_l = jax.nn.silu(c) @ ada_w[layer] + ada_b[layer]
        mod_c = jax.nn.silu(c_ctx) @ ada_w[layer] + ada_b[layer]
        sh1_l, sc1_l, g1_l, sh2_l, sc2_l, g2_l = (m[:, None, :] for m in jnp.split(mod_l, 6, axis=-1))
        sh1_c, sc1_c, g1_c, sh2_c, sc2_c, g2_c = jnp.split(mod_c, 6, axis=-1)

        xl = modulate(rms_norm(h_lat, norm_mix_g[layer]), sh1_l, sc1_l)
        xc = modulate(rms_norm(h_ctx, norm_mix_g[layer]), sh1_c, sc1_c)
        i = layer // 2
        if layer % 2 == 0:
            out_l, out_c = ab_mixer(xl, xc, need_ctx, ab_w_in[i], na_q_g[i], na_k_g[i], na_rpb[i],
                                    pool_w[i], pool_scale[i], ab_w_out[i])
        else:
            out_l, out_c = rwkv_mixer(xl, xc, need_ctx, rw_mu[i], rw_w_r[i], rw_w_k[i], rw_w_v[i], rw_w_o[i],
                                      rw_w0[i], rw_w1[i], rw_w2[i], rw_a0[i], rw_a1[i], rw_a2[i],
                                      rw_g1[i], rw_g2[i], rw_k_k[i], rw_k_a[i], rw_r_k[i],
                                      rw_ln_w[i], rw_ln_b[i])
        h_lat = h_lat + g1_l * out_l
        if need_ctx:
            h_ctx = h_ctx + g1_c * out_c

        moe_args = (router_w[layer], router_b[layer], exp_w_in[layer], exp_b_in[layer],
                    exp_w_out[layer], exp_b_out[layer])
        yl = modulate(rms_norm(h_lat, norm_ffn_g[layer]), sh2_l, sc2_l)
        if need_ctx:
            yc = modulate(rms_norm(h_ctx, norm_ffn_g[layer]), sh2_c, sc2_c)
            n_lat = yl.shape[0] * yl.shape[1]
            f = moe_ffn(jnp.concatenate([yl.reshape(-1, D_MODEL), yc.reshape(-1, D_MODEL)], axis=0), *moe_args)
            h_lat = h_lat + g2_l * f[:n_lat].reshape(h_lat.shape)
            h_ctx = h_ctx + g2_c * f[n_lat:].reshape(h_ctx.shape)
        else:
            h_lat = h_lat + g2_l * moe_ffn(yl.reshape(-1, D_MODEL), *moe_args).reshape(h_lat.shape)
    return h_lat
```

```python
import functools

import jax
import jax.numpy as jnp
from jax import lax
from jax.experimental import pallas as pl
from jax.experimental.pallas import tpu as pltpu

F32 = jnp.float32
BF16 = jnp.bfloat16

D_MODEL = 1024
GRID_W = 64
CTX_LEN = 256
RMS_EPS = 1e-6
NA_HEADS = 8
NA_HEAD_DIM = 64
NA_WIDTH = NA_HEADS * NA_HEAD_DIM
NA_KH = 8
NA_KW = 16
POOL_WINDOWS = (2, 4, 8, 16)
POOL_GC = 128
POOL_WIDTH = 512
RW_HEAD_DIM = 64
RW_HEADS = D_MODEL // RW_HEAD_DIM
RW_GN_EPS = 64e-5
N_EXPERTS = 32
TOP_K = 4
D_FF = D_MODEL
SWIGLU_LIMIT = 7.0
SWIGLU_ALPHA = 1.702

TILE = 256
LANES = 128
SUBLANES = 8
MOE_TM = 512
NEG_BIG = -1e30
VMEM_LIMIT = 48 * 1024 * 1024


def _cparams(sem):
    return pltpu.CompilerParams(dimension_semantics=sem, vmem_limit_bytes=VMEM_LIMIT)


def _mod_row(mods_ref, k):
    return mods_ref[0, 0, pl.ds(k, 1), :]


def _norm_mod(x, gain, shift, scale):
    ms = jnp.mean(x * x, axis=-1, keepdims=True)
    return (x * lax.rsqrt(ms + RMS_EPS) * gain) * (1.0 + scale) + shift


def _ada_kernel(c_ref, w_ref, b_ref, o_ref):
    c = c_ref[...]
    s = c * jax.nn.sigmoid(c)
    o_ref[...] = jnp.dot(s, w_ref[...], preferred_element_type=F32,
                         precision=lax.Precision.HIGHEST) + b_ref[...]


def ada_mod(cvec, w, b):
    r, d = cvec.shape
    n = w.shape[1]
    tn = 1536
    return pl.pallas_call(
        _ada_kernel,
        out_shape=jax.ShapeDtypeStruct((r, n), F32),
        grid=(n // tn,),
        in_specs=[pl.BlockSpec((r, d), lambda j: (0, 0)),
                  pl.BlockSpec((d, tn), lambda j: (0, j)),
                  pl.BlockSpec((1, tn), lambda j: (0, j))],
        out_specs=pl.BlockSpec((r, tn), lambda j: (0, j)),
        compiler_params=_cparams(("arbitrary",)),
        name="ada_mod",
    )(cvec, w, b.reshape(1, n))


def _ab_proj_kernel(h_ref, g_ref, mods_ref, w_ref, bd_ref, qkg_ref, qk_ref, v_ref, u_ref):
    x = _norm_mod(h_ref[0], g_ref[...], _mod_row(mods_ref, 0), _mod_row(mods_ref, 1))
    p = jnp.dot(x.astype(BF16), w_ref[...], preferred_element_type=F32)
    qk = p[:, :2 * NA_WIDTH]
    ss = jnp.dot((qk * qk).astype(BF16), bd_ref[...], preferred_element_type=F32)
    qk_ref[0] = (qk * lax.rsqrt(ss + RMS_EPS) * qkg_ref[...]).astype(BF16)
    v_ref[0] = p[:, 2 * NA_WIDTH:3 * NA_WIDTH].astype(BF16)
    u_ref[0] = p[:, 3 * NA_WIDTH:]


def ab_proj(h, gain, mods, w_in, q_g, k_g):
    b, lt, d = h.shape
    nt = lt // TILE
    n_in = w_in.shape[1]
    hid = jnp.arange(2 * NA_WIDTH) // NA_HEAD_DIM
    bd = jnp.where(hid[:, None] == hid[None, :], 1.0 / NA_HEAD_DIM, 0.0).astype(BF16)
    qkg = jnp.concatenate([jnp.tile(q_g, NA_HEADS), jnp.tile(k_g, NA_HEADS)]).reshape(1, -1).astype(F32)
    return pl.pallas_call(
        _ab_proj_kernel,
        out_shape=(jax.ShapeDtypeStruct((b, lt, 2 * NA_WIDTH), BF16),
                   jax.ShapeDtypeStruct((b, lt, NA_WIDTH), BF16),
                   jax.ShapeDtypeStruct((b, lt, POOL_WIDTH), F32)),
        grid=(b, nt),
        in_specs=[pl.BlockSpec((1, TILE, d), lambda bi, i: (bi, i, 0)),
                  pl.BlockSpec((1, d), lambda bi, i: (0, 0)),
                  pl.BlockSpec((1, 1, 6, d), lambda bi, i: (bi, jnp.minimum(i, 1), 0, 0)),
                  pl.BlockSpec((d, n_in), lambda bi, i: (0, 0)),
                  pl.BlockSpec((2 * NA_WIDTH, 2 * NA_WIDTH), lambda bi, i: (0, 0)),
                  pl.BlockSpec((1, 2 * NA_WIDTH), lambda bi, i: (0, 0))],
        out_specs=(pl.BlockSpec((1, TILE, 2 * NA_WIDTH), lambda bi, i: (bi, i, 0)),
                   pl.BlockSpec((1, TILE, NA_WIDTH), lambda bi, i: (bi, i, 0)),
                   pl.BlockSpec((1, TILE, POOL_WIDTH), lambda bi, i: (bi, i, 0))),
        compiler_params=_cparams(("arbitrary", "arbitrary")),
        name="ab_proj",
    )(h, gain.reshape(1, d), mods, w_in.astype(BF16), bd, qkg)


NA_QROWS = TILE // GRID_W
NA_KTILES = 3
NA_NLOC = NA_KTILES * TILE
NA_NKEY = NA_NLOC + CTX_LEN


def _na_bias_tables(rpb, rows):
    ql = jnp.arange(TILE)
    kl = jnp.arange(NA_NLOC)
    a, qcol = ql // GRID_W, ql % GRID_W
    jr, kcol = kl // GRID_W, kl % GRID_W
    c_start = jnp.clip(qcol - NA_KW // 2, 0, GRID_W - NA_KW)
    valid_c = (kcol[None, :] >= c_start[:, None]) & (kcol[None, :] < c_start[:, None] + NA_KW)
    c_rel = jnp.clip(kcol[None, :] - qcol[:, None] + NA_KW - 1, 0, 2 * NA_KW - 2)
    n_qt = rows // NA_QROWS
    tabs = []
    for jq in (0, 1, n_qt - 1):
        w0 = NA_QROWS * min(max(jq - 1, 0), n_qt - NA_KTILES)
        q_row = NA_QROWS * jq + a
        key_row = w0 + jr
        r_start = jnp.clip(q_row - NA_KH // 2, 0, rows - NA_KH)
        valid_r = (key_row[None, :] >= r_start[:, None]) & (key_row[None, :] < r_start[:, None] + NA_KH)
        r_rel = jnp.clip(key_row[None, :] - q_row[:, None] + NA_KH - 1, 0, 2 * NA_KH - 2)
        bias = rpb[:, r_rel, c_rel].astype(F32)
        tabs.append(jnp.where((valid_r & valid_c)[None], bias, NEG_BIG))
    tabs = [jnp.full_like(tabs[0], NEG_BIG)] + tabs
    loc = jnp.stack(tabs)
    return jnp.concatenate([loc, jnp.zeros(loc.shape[:3] + (CTX_LEN,), F32)], axis=-1)


def _na_kernel(q_ref, k0_ref, k1_ref, k2_ref, kc_ref, v0_ref, v1_ref, v2_ref, vc_ref, bias_ref, o_ref):
    q = q_ref[0]
    kall = jnp.concatenate([k0_ref[0], k1_ref[0], k2_ref[0], kc_ref[0]], axis=0)
    vall = jnp.concatenate([v0_ref[0], v1_ref[0], v2_ref[0], vc_ref[0]], axis=0)
    lane = lax.broadcasted_iota(jnp.int32, q.shape, 1)
    scale = NA_HEAD_DIM ** -0.5
    outs = []
    for hh in range(2):
        in_head = (lane >= hh * NA_HEAD_DIM) & (lane < (hh + 1) * NA_HEAD_DIM)
        qh = jnp.where(in_head, q, jnp.zeros_like(q)) * jnp.asarray(scale, BF16)
        s = lax.dot_general(qh, kall, (((1,), (1,)), ((), ())), preferred_element_type=F32)
        s = s + bias_ref[0, hh]
        m = jnp.max(s, axis=-1, keepdims=True)
        p = jnp.exp(s - m)
        l = jnp.sum(p, axis=-1, keepdims=True)
        o = jnp.dot(p.astype(BF16), vall, preferred_element_type=F32)
        outs.append(o / l)
    o_ref[0] = jnp.where(lane < NA_HEAD_DIM, outs[0], outs[1]).astype(BF16)


def na_attention(qk, v, bias):
    b, lt, _ = qk.shape
    nt = lt // TILE
    n_qt = nt - 1
    assert NA_HEAD_DIM ** -0.5 == 0.125
    hp_n = NA_HEADS // 2
    kblk = NA_WIDTH // LANES

    def w0(j):
        return 1 + jnp.clip(j - 2, 0, n_qt - NA_KTILES)

    def kind(j):
        return jnp.where(j == 0, 0, jnp.where(j == 1, 1, jnp.where(j == nt - 1, 3, 2)))

    qspec = pl.BlockSpec((1, TILE, LANES), lambda j, hp, bi: (bi, j, hp))
    kspecs = [pl.BlockSpec((1, TILE, LANES), functools.partial(lambda j, hp, bi, t: (bi, w0(j) + t, kblk + hp), t=t))
              for t in range(NA_KTILES)]
    kcspec = pl.BlockSpec((1, TILE, LANES), lambda j, hp, bi: (bi, 0, kblk + hp))
    vspecs = [pl.BlockSpec((1, TILE, LANES), functools.partial(lambda j, hp, bi, t: (bi, w0(j) + t, hp), t=t))
              for t in range(NA_KTILES)]
    vcspec = pl.BlockSpec((1, TILE, LANES), lambda j, hp, bi: (bi, 0, hp))
    bspec = pl.BlockSpec((1, 2, TILE, NA_NKEY), lambda j, hp, bi: (kind(j), hp, 0, 0))
    return pl.pallas_call(
        _na_kernel,
        out_shape=jax.ShapeDtypeStruct((b, lt, NA_WIDTH), BF16),
        grid=(nt, hp_n, b),
        in_specs=[qspec] + kspecs + [kcspec] + vspecs + [vcspec, bspec],
        out_specs=pl.BlockSpec((1, TILE, LANES), lambda j, hp, bi: (bi, j, hp)),
        compiler_params=_cparams(("arbitrary", "arbitrary", "arbitrary")),
        name="na_attention",
    )(qk, qk, qk, qk, qk, v, v, v, v, bias)


POOL_HALO = 8
POOL_ROWS = TILE + 2 * POOL_HALO


def _pool_kernel(cur_ref, prev_ref, next_ref, pw_ref, ps_ref, o_ref, xp_ref, *, n_tiles):
    i = pl.program_id(1)
    has_prev = (i >= 2).astype(F32)
    has_next = ((i >= 1) & (i < n_tiles - 1)).astype(F32)
    xp_ref[0:POOL_HALO, :] = prev_ref[0] * has_prev
    xp_ref[POOL_HALO:POOL_HALO + TILE, :] = cur_ref[0]
    xp_ref[POOL_HALO + TILE:POOL_ROWS, :] = next_ref[0] * has_next
    row = lax.broadcasted_iota(jnp.int32, (TILE, POOL_GC), 0)
    tpos = jnp.where(i == 0, row, row + (i - 1) * TILE)
    seq_len = jnp.where(i == 0, CTX_LEN, (n_tiles - 1) * TILE)
    for g, w in enumerate(POOL_WINDOWS):
        cols = slice(g * POOL_GC, (g + 1) * POOL_GC)
        x = xp_ref[:, cols]
        s = x + pltpu.roll(x, 1, 0)
        half = 1
        while 2 * half < w:
            s = pltpu.roll(s, half, 0) + pltpu.roll(s, POOL_ROWS - half, 0)
            half *= 2
        s = s[POOL_HALO:POOL_HALO + TILE]
        cnt = jnp.minimum(w // 2, tpos) + jnp.minimum(w // 2, seq_len - tpos)
        pooled = s / cnt.astype(F32) - cur_ref[0, :, cols]
        y = jnp.dot(pooled.astype(BF16), pw_ref[g], preferred_element_type=F32)
        o_ref[0, :, cols] = (y * ps_ref[:, cols]).astype(BF16)


def multiscale_pool(u, pool_w, pool_scale):
    b, lt, c = u.shape
    nt = lt // TILE
    hb = TILE // POOL_HALO
    nhb = lt // POOL_HALO
    assert POOL_HALO >= max(POOL_WINDOWS) // 2 and POOL_HALO == SUBLANES
    return pl.pallas_call(
        functools.partial(_pool_kernel, n_tiles=nt),
        out_shape=jax.ShapeDtypeStruct((b, lt, c), BF16),
        grid=(b, nt),
        in_specs=[pl.BlockSpec((1, TILE, c), lambda bi, i: (bi, i, 0)),
                  pl.BlockSpec((1, POOL_HALO, c), lambda bi, i: (bi, jnp.maximum(i * hb - 1, 0), 0)),
                  pl.BlockSpec((1, POOL_HALO, c), lambda bi, i: (bi, jnp.minimum((i + 1) * hb, nhb - 1), 0)),
                  pl.BlockSpec((len(POOL_WINDOWS), POOL_GC, POOL_GC), lambda bi, i: (0, 0, 0)),
                  pl.BlockSpec((1, c), lambda bi, i: (0, 0))],
        out_specs=pl.BlockSpec((1, TILE, c), lambda bi, i: (bi, i, 0)),
        scratch_shapes=[pltpu.VMEM((POOL_ROWS, c), F32)],
        compiler_params=_cparams(("arbitrary", "arbitrary")),
        name="multiscale_pool",
    )(u, u, u, pool_w.astype(BF16), pool_scale.reshape(1, c).astype(F32))


def _out_proj_kernel(a_ref, b_ref, h_ref, mods_ref, w_ref, o_ref):
    ka = a_ref.shape[-1]
    y = jnp.dot(a_ref[0], w_ref[:ka, :], preferred_element_type=F32)
    y = y + jnp.dot(b_ref[0], w_ref[ka:, :], preferred_element_type=F32)
    o_ref[0] = h_ref[0] + _mod_row(mods_ref, 2) * y


def out_proj_residual(a, bb, h, mods, w):
    b, lt, d = h.shape
    nt = lt // TILE
    ka, kb = a.shape[-1], bb.shape[-1]
    return pl.pallas_call(
        _out_proj_kernel,
        out_shape=jax.ShapeDtypeStruct((b, lt, d), F32),
        grid=(b, nt),
        in_specs=[pl.BlockSpec((1, TILE, ka), lambda bi, i: (bi, i, 0)),
                  pl.BlockSpec((1, TILE, kb), lambda bi, i: (bi, i, 0)),
                  pl.BlockSpec((1, TILE, d), lambda bi, i: (bi, i, 0)),
                  pl.BlockSpec((1, 1, 6, d), lambda bi, i: (bi, jnp.minimum(i, 1), 0, 0)),
                  pl.BlockSpec((ka + kb, d), lambda bi, i: (0, 0))],
        out_specs=pl.BlockSpec((1, TILE, d), lambda bi, i: (bi, i, 0)),
        compiler_params=_cparams(("arbitrary", "arbitrary")),
        name="out_proj_residual",
    )(a, bb, h, mods, w.astype(BF16))


def _moe_pre_kernel(h_ref, g_ref, mods_ref, rw_ref, rb_ref, y_ref, lg_ref):
    x = _norm_mod(h_ref[0], g_ref[...], _mod_row(mods_ref, 3), _mod_row(mods_ref, 4))
    y_ref[0] = x.astype(BF16)
    lg_ref[0] = jnp.dot(x, rw_ref[...], preferred_element_type=F32,
                        precision=lax.Precision.HIGHEST) + rb_ref[...]


def moe_pre(h, gain, mods, router_w, router_b, tile0, ctx_tiles):
    b, lt, d = h.shape
    nt = lt // TILE - tile0
    e = router_w.shape[1]
    return pl.pallas_call(
        _moe_pre_kernel,
        out_shape=(jax.ShapeDtypeStruct((b, nt * TILE, d), BF16),
                   jax.ShapeDtypeStruct((b, nt * TILE, e), F32)),
        grid=(b, nt),
        in_specs=[pl.BlockSpec((1, TILE, d), lambda bi, i: (bi, i + tile0, 0)),
                  pl.BlockSpec((1, d), lambda bi, i: (0, 0)),
                  pl.BlockSpec((1, 1, 6, d), lambda bi, i: (bi, jnp.minimum(i + tile0 + 1 - ctx_tiles, 1), 0, 0)),
                  pl.BlockSpec((d, e), lambda bi, i: (0, 0)),
                  pl.BlockSpec((1, e), lambda bi, i: (0, 0))],
        out_specs=(pl.BlockSpec((1, TILE, d), lambda bi, i: (bi, i, 0)),
                   pl.BlockSpec((1, TILE, e), lambda bi, i: (bi, i, 0))),
        compiler_params=_cparams(("arbitrary", "arbitrary")),
        name="moe_pre",
    )(h, gain.reshape(1, d), mods, router_w, router_b.reshape(1, e))


def _moe_ffn_kernel(te_ref, tv_ref, x_ref, wg_ref, wl_ref, bg_ref, bl_ref, wo_ref, bo_ref, o_ref):
    t = pl.program_id(0)

    @pl.when(tv_ref[t] > 0)
    def _():
        x = x_ref[...]
        hg = jnp.dot(x, wg_ref[0], preferred_element_type=F32) + bg_ref[0]
        hl = jnp.dot(x, wl_ref[0], preferred_element_type=F32) + bl_ref[0]
        glu = jnp.minimum(hg, SWIGLU_LIMIT)
        lin = jnp.clip(hl, -SWIGLU_LIMIT, SWIGLU_LIMIT)
        act = glu * jax.nn.sigmoid(SWIGLU_ALPHA * glu) * (lin + 1.0)
        o_ref[...] = (jnp.dot(act.astype(BF16), wo_ref[0], preferred_element_type=F32)
                      + bo_ref[0]).astype(o_ref.dtype)


def moe_grouped_ffn(x_sorted, tile_expert, tile_valid, wg, wl, bg, bl, wo, bo):
    p, d = x_sorted.shape
    e, _, f = wg.shape
    nt = p // MOE_TM
    grid_spec = pltpu.PrefetchScalarGridSpec(
        num_scalar_prefetch=2,
        grid=(nt,),
        in_specs=[pl.BlockSpec((MOE_TM, d), lambda t, te, tv: (t, 0)),
                  pl.BlockSpec((1, d, f), lambda t, te, tv: (te[t], 0, 0)),
                  pl.BlockSpec((1, d, f), lambda t, te, tv: (te[t], 0, 0)),
                  pl.BlockSpec((1, 1, f), lambda t, te, tv: (te[t], 0, 0)),
                  pl.BlockSpec((1, 1, f), lambda t, te, tv: (te[t], 0, 0)),
                  pl.BlockSpec((1, f, d), lambda t, te, tv: (te[t], 0, 0)),
                  pl.BlockSpec((1, 1, d), lambda t, te, tv: (te[t], 0, 0))],
        out_specs=pl.BlockSpec((MOE_TM, d), lambda t, te, tv: (t, 0)),
    )
    return pl.pallas_call(
        _moe_ffn_kernel,
        out_shape=jax.ShapeDtypeStruct((p, d), BF16),
        grid_spec=grid_spec,
        compiler_params=_cparams(("arbitrary",)),
        name="moe_grouped_ffn",
    )(tile_expert, tile_valid, x_sorted, wg, wl, bg, bl, wo, bo)


def _moe_combine_kernel(z_ref, gt_ref, h_ref, mods_ref, o_ref):
    gt = gt_ref[0]
    acc = gt[:, 0:1] * z_ref[0, 0].astype(F32)
    for k in range(1, TOP_K):
        acc = acc + gt[:, k:k + 1] * z_ref[k, 0].astype(F32)
    o_ref[0] = h_ref[0] + _mod_row(mods_ref, 5) * acc


def moe_combine(z4, gates, h, mods, tile0, ctx_tiles):
    k, b, lp, d = z4.shape
    nt = lp // TILE
    return pl.pallas_call(
        _moe_combine_kernel,
        out_shape=jax.ShapeDtypeStruct((b, lp, d), F32),
        grid=(b, nt),
        in_specs=[pl.BlockSpec((k, 1, TILE, d), lambda bi, i: (0, bi, i, 0)),
                  pl.BlockSpec((1, TILE, k), lambda bi, i: (bi, i, 0)),
                  pl.BlockSpec((1, TILE, d), lambda bi, i: (bi, i + tile0, 0)),
                  pl.BlockSpec((1, 1, 6, d), lambda bi, i: (bi, jnp.minimum(i + tile0 + 1 - ctx_tiles, 1), 0, 0))],
        out_specs=pl.BlockSpec((1, TILE, d), lambda bi, i: (bi, i, 0)),
        compiler_params=_cparams(("arbitrary", "arbitrary")),
        name="moe_combine",
    )(z4, gates, h, mods)


def moe_layer(h, gain, mods, router_w, router_b, w_in, b_in, w_out, b_out, tile0, ctx_tiles):
    b, lt, d = h.shape
    y, logits = moe_pre(h, gain, mods, router_w, router_b, tile0, ctx_tiles)
    lp = y.shape[1]
    n = b * lp
    y2 = y.reshape(n, d)
    top_val, top_idx = lax.top_k(logits.reshape(n, N_EXPERTS), TOP_K)
    gates = jax.nn.softmax(top_val, axis=-1)

    flat_e = top_idx.reshape(-1)
    onehot = (flat_e[:, None] == jnp.arange(N_EXPERTS)[None, :]).astype(jnp.int32)
    rank = jnp.take_along_axis(jnp.cumsum(onehot, axis=0), flat_e[:, None], axis=1)[:, 0] - 1
    counts = jnp.sum(onehot, axis=0)
    padded = ((counts + MOE_TM - 1) // MOE_TM) * MOE_TM
    ends = jnp.cumsum(padded)
    starts = ends - padded
    slot = starts[flat_e] + rank
    n_tiles = (n * TOP_K) // MOE_TM + N_EXPERTS
    p_rows = n_tiles * MOE_TM
    src_tok = jnp.zeros((p_rows,), jnp.int32).at[slot].set(jnp.arange(n * TOP_K, dtype=jnp.int32) // TOP_K)
    tile_start = jnp.arange(n_tiles, dtype=jnp.int32) * MOE_TM
    tile_expert = jnp.minimum(jnp.searchsorted(ends, tile_start, side="right"), N_EXPERTS - 1).astype(jnp.int32)
    tile_valid = (tile_start < ends[-1]).astype(jnp.int32)

    x_sorted = jnp.take(y2, src_tok, axis=0)
    wg = w_in[:, :, 0::2].astype(BF16)
    wl = w_in[:, :, 1::2].astype(BF16)
    bg = b_in[:, 0::2].reshape(N_EXPERTS, 1, D_FF)
    bl = b_in[:, 1::2].reshape(N_EXPERTS, 1, D_FF)
    z = moe_grouped_ffn(x_sorted, tile_expert, tile_valid, wg, wl, bg, bl,
                        w_out.astype(BF16), b_out.reshape(N_EXPERTS, 1, d))
    z4 = jnp.take(z, slot.reshape(n, TOP_K).T, axis=0).reshape(TOP_K, b, lp, d)
    return moe_combine(z4, gates.reshape(b, lp, TOP_K), h, mods, tile0, ctx_tiles)


def _softplus(z):
    return jnp.maximum(z, 0.0) + jnp.log(1.0 + jnp.exp(-jnp.abs(z)))


def _rwkv_feat_kernel(h_ref, hp_ref, hn_ref, g_ref, mods_ref, mu_ref, wr_ref, wk_ref, wv_ref,
                      w1_ref, w2_ref, a1_ref, a2_ref, g1_ref, g2_ref, w0_ref, a0_ref, kk_ref, ka_ref,
                      bd_ref, r_out, v_out, kk_out, g_out, lw_out, kd_out, a_out, *, n_tiles):
    i = pl.program_id(1)
    gain, shift, scale = g_ref[...], _mod_row(mods_ref, 0), _mod_row(mods_ref, 1)
    x = _norm_mod(h_ref[0], gain, shift, scale)
    has_prev = (i >= 2).astype(F32)
    has_next = ((i >= 1) & (i < n_tiles - 1)).astype(F32)
    x_before = _norm_mod(hp_ref[0], gain, shift, scale)[SUBLANES - 1:SUBLANES] * has_prev
    x_after = _norm_mod(hn_ref[0], gain, shift, scale)[0:1] * has_next
    row = lax.broadcasted_iota(jnp.int32, x.shape, 0)
    prev = jnp.where(row == 0, x_before, pltpu.roll(x, 1, 0))
    nxt = jnp.where(row == TILE - 1, x_after, pltpu.roll(x, TILE - 1, 0))
    xx = 0.5 * (prev + nxt) - x
    xr, xw, xk, xv, xa, xg = (x + xx * mu_ref[pl.ds(j, 1), :] for j in range(6))

    def mm(a, w):
        return jnp.dot(a.astype(BF16), w, preferred_element_type=F32)

    r = mm(xr, wr_ref[...])
    k = mm(xk, wk_ref[...])
    v = mm(xv, wv_ref[...])
    g = mm(jax.nn.sigmoid(mm(xg, g1_ref[...])), g2_ref[...])
    kk = k * kk_ref[...]
    kk = kk * lax.rsqrt(mm(kk * kk, bd_ref[...]) + 1e-12)
    r_out[0] = r.astype(BF16)
    v_out[0] = v.astype(BF16)
    kk_out[0] = kk.astype(BF16)
    g_out[0] = g.astype(BF16)
    tw = jnp.tanh(mm(xw, w1_ref[...]))
    av = mm(xa, a1_ref[...])
    lane = lax.broadcasted_iota(jnp.int32, tw.shape, 1)
    half = tw.shape[1] // 2
    for d in range(2):
        in_dir = (lane >= d * half) & (lane < (d + 1) * half)
        wz = w0_ref[pl.ds(d, 1), :] + mm(jnp.where(in_dir, tw, 0.0), w2_ref[...])
        w_log = -_softplus(-wz) - 0.5
        lw_out[d, 0] = -jnp.exp(w_log)
        a = jax.nn.sigmoid(a0_ref[pl.ds(d, 1), :] + mm(jnp.where(in_dir, av, 0.0), a2_ref[...]))
        kd_out[d, 0] = (k * (1.0 + (a - 1.0) * ka_ref[...])).astype(BF16)
        a_out[d, 0] = a.astype(BF16)


def rwkv_features(h, gain, mods, mu, w_r, w_k, w_v, w0, w1, w2, a0, a1, a2, g1, g2, k_k, k_a):
    b, lt, d = h.shape
    nt = lt // TILE
    hb = TILE // SUBLANES
    nhb = lt // SUBLANES
    hid = jnp.arange(d) // RW_HEAD_DIM
    bd = (hid[:, None] == hid[None, :]).astype(BF16)
    w1c = jnp.concatenate([w1[0], w1[1]], axis=1).astype(BF16)
    a1c = jnp.concatenate([a1[0], a1[1]], axis=1).astype(BF16)
    w2c = jnp.concatenate([w2[0], w2[1]], axis=0).astype(BF16)
    a2c = jnp.concatenate([a2[0], a2[1]], axis=0).astype(BF16)
    full = lambda shape: pl.BlockSpec(shape, lambda bi, i: (0,) * len(shape))
    tile_spec = pl.BlockSpec((1, TILE, d), lambda bi, i: (bi, i, 0))
    dir_spec = pl.BlockSpec((2, 1, TILE, d), lambda bi, i: (0, bi, i, 0))
    seq_bf = jax.ShapeDtypeStruct((b, lt, d), BF16)
    return pl.pallas_call(
        functools.partial(_rwkv_feat_kernel, n_tiles=nt),
        out_shape=(seq_bf, seq_bf, seq_bf, seq_bf,
                   jax.ShapeDtypeStruct((2, b, lt, d), F32),
                   jax.ShapeDtypeStruct((2, b, lt, d), BF16),
                   jax.ShapeDtypeStruct((2, b, lt, d), BF16)),
        grid=(b, nt),
        in_specs=[tile_spec,
                  pl.BlockSpec((1, SUBLANES, d), lambda bi, i: (bi, jnp.maximum(i * hb - 1, 0), 0)),
                  pl.BlockSpec((1, SUBLANES, d), lambda bi, i: (bi, jnp.minimum((i + 1) * hb, nhb - 1), 0)),
                  full((1, d)),
                  pl.BlockSpec((1, 1, 6, d), lambda bi, i: (bi, jnp.minimum(i, 1), 0, 0)),
                  full((6, d)), full((d, d)), full((d, d)), full((d, d)),
                  full(w1c.shape), full(w2c.shape), full(a1c.shape), full(a2c.shape),
                  full(g1.shape), full(g2.shape), full((2, d)), full((2, d)), full((1, d)), full((1, d)),
                  full((d, d))],
        out_specs=(tile_spec, tile_spec, tile_spec, tile_spec, dir_spec, dir_spec, dir_spec),
        compiler_params=_cparams(("arbitrary", "arbitrary")),
        name="rwkv_features",
    )(h, h, h, gain.reshape(1, d), mods, mu, w_r.astype(BF16), w_k.astype(BF16), w_v.astype(BF16),
      w1c, w2c, a1c, a2c, g1.astype(BF16), g2.astype(BF16), w0, a0, k_k.reshape(1, d), k_a.reshape(1, d), bd)


SCAN_C = 64
SCAN_G = 4
SCAN_GW = SCAN_G * RW_HEAD_DIM


def _rwkv_scan_kernel(r_ref, v_ref, kk_ref, lw_ref, kd_ref, a_ref, y_ref, h_ref):
    c = pl.program_id(2)
    sgn = 1 - 2 * pl.program_id(1)

    @pl.when(c == 0)
    def _():
        h_ref[...] = jnp.zeros_like(h_ref)

    cc = SCAN_C
    ti = lax.broadcasted_iota(jnp.int32, (cc, cc), 0)
    si = lax.broadcasted_iota(jnp.int32, (cc, cc), 1)
    before_incl = sgn * (si - ti) <= 0
    lw = lw_ref[0, 0]
    cl = jnp.dot(before_incl.astype(F32), lw, preferred_element_type=F32, precision=lax.Precision.HIGHEST)
    g_in = jnp.exp(cl)
    g_ex = jnp.exp(cl - lw)
    g_inv = jnp.exp(-cl)
    g_end = jnp.exp(jnp.sum(lw, axis=0, keepdims=True))
    kk = kk_ref[0].astype(F32)
    rt = (r_ref[0].astype(F32) * g_in).astype(BF16)
    at = (-kk * g_ex).astype(BF16)
    bt = (kk * a_ref[0, 0].astype(F32) * g_inv).astype(BF16)
    kt = (kd_ref[0, 0].astype(F32) * g_inv).astype(BF16)
    v = v_ref[0]

    gw = SCAN_GW
    tg = lax.broadcasted_iota(jnp.int32, (cc, gw), 0)
    sg = lax.broadcasted_iota(jnp.int32, (cc, gw), 1) % cc
    m_strict = sgn * (sg - tg) < 0
    m_incl = sgn * (sg - tg) <= 0
    eye_g = (sg == tg).astype(F32)
    bi = lax.broadcasted_iota(jnp.int32, (gw, gw), 0) // cc
    bj = lax.broadcasted_iota(jnp.int32, (gw, gw), 1) // cc
    blk = bi == bj

    def bdiag(x):
        return jnp.where(blk, jnp.concatenate([x] * SCAN_G, axis=0), jnp.zeros((), x.dtype))

    def mm(a, b):
        return jnp.dot(a, b, preferred_element_type=F32)

    def mm_nt(a, b):
        return lax.dot_general(a, b, (((1,), (1,)), ((), ())), preferred_element_type=F32)

    for g in range(RW_HEADS // SCAN_G):
        cols = slice(g * gw, (g + 1) * gw)
        rt_g, at_g, bt_g, kt_g, v_g = rt[:, cols], at[:, cols], bt[:, cols], kt[:, cols], v[:, cols]
        h0 = h_ref[g]
        h0b = h0.astype(BF16)
        kb = jnp.concatenate([bdiag(bt_g), bdiag(kt_g)], axis=0)
        mats = mm_nt(jnp.concatenate([at_g, rt_g], axis=0), kb)
        a_ab = jnp.where(m_strict, mats[:cc, :gw], 0.0)
        a_ak = jnp.where(m_strict, mats[:cc, gw:], 0.0).astype(BF16)
        m_rb = jnp.where(m_incl, mats[cc:, :gw], 0.0).astype(BF16)
        m_rk = jnp.where(m_incl, mats[cc:, gw:], 0.0).astype(BF16)
        pw = a_ab
        inv = eye_g + a_ab
        n = 2
        while n < cc:
            pw_bd = bdiag(pw.astype(BF16))
            pw = mm(pw.astype(BF16), pw_bd)
            inv = inv + mm(inv.astype(BF16), bdiag(pw.astype(BF16)))
            n *= 2
        v_bd = bdiag(v_g)
        pq = mm(jnp.concatenate([at_g, a_ak], axis=1), jnp.concatenate([h0b, v_bd], axis=0))
        u = mm(inv.astype(BF16), bdiag(pq.astype(BF16)))
        ub = u.astype(BF16)
        y = mm(jnp.concatenate([rt_g, m_rb, m_rk], axis=1),
               jnp.concatenate([h0b, bdiag(ub), v_bd], axis=0))
        y_ref[0, 0, :, cols] = y
        upd = lax.dot_general(jnp.concatenate([bt_g, kt_g], axis=0), jnp.concatenate([ub, v_g], axis=0),
                              (((0,), (0,)), ((), ())), preferred_element_type=F32)
        ge = jnp.transpose(jnp.broadcast_to(g_end[:, cols], (gw, gw)))
        h_ref[g] = ge * (h0 + jnp.where(blk, upd, 0.0))


def rwkv_scan(r, v, kk, lw, kd, a):
    b, lt, d = r.shape
    nc = lt // SCAN_C
    n_ctx = CTX_LEN // SCAN_C
    assert SCAN_C == RW_HEAD_DIM and CTX_LEN % SCAN_C == 0

    def chunk(di, c):
        rev_c = jnp.where(c < n_ctx, n_ctx - 1 - c, nc - 1 + n_ctx - c)
        return jnp.where(di == 1, rev_c, c)

    shared = pl.BlockSpec((1, SCAN_C, d), lambda bi, di, c: (bi, chunk(di, c), 0))
    per_dir = pl.BlockSpec((1, 1, SCAN_C, d), lambda bi, di, c: (di, bi, chunk(di, c), 0))
    return pl.pallas_call(
        _rwkv_scan_kernel,
        out_shape=jax.ShapeDtypeStruct((2, b, lt, d), F32),
        grid=(b, 2, nc),
        in_specs=[shared, shared, shared, per_dir, per_dir, per_dir],
        out_specs=per_dir,
        scratch_shapes=[pltpu.VMEM((RW_HEADS // SCAN_G, SCAN_GW, SCAN_GW), F32)],
        compiler_params=_cparams(("arbitrary", "arbitrary", "arbitrary")),
        name="rwkv_scan",
    )(r, v, kk, lw, kd, a)


def _rwkv_out_kernel(y_ref, r_ref, v_ref, g_ref, kd_ref, h_ref, mods_ref, lnw_ref, lnb_ref, rk_ref,
                     bd_ref, wo_ref, o_ref):
    def head_sum(x):
        return jnp.dot(x.astype(BF16), bd_ref[...], preferred_element_type=F32)

    y = y_ref[0, 0] + y_ref[1, 0]
    inv_n = 1.0 / RW_HEAD_DIM
    mean = head_sum(y) * inv_n
    yc = y - mean
    var = head_sum(yc * yc) * inv_n
    yn = yc * lax.rsqrt(var + RW_GN_EPS) * lnw_ref[...] + lnb_ref[...]
    r = r_ref[0].astype(F32)
    kd = kd_ref[0, 0].astype(F32) + kd_ref[1, 0].astype(F32)
    yn = yn + head_sum(r * kd * rk_ref[...]) * v_ref[0].astype(F32)
    out = jnp.dot((yn * g_ref[0].astype(F32)).astype(BF16), wo_ref[...], preferred_element_type=F32)
    o_ref[0] = h_ref[0] + _mod_row(mods_ref, 2) * out


def rwkv_readout(y, r, v, g, kd, h, mods, ln_w, ln_b, r_k, w_o):
    b, lt, d = h.shape
    nt = lt // TILE - 1
    hid = jnp.arange(d) // RW_HEAD_DIM
    bd = (hid[:, None] == hid[None, :]).astype(BF16)
    full = lambda shape: pl.BlockSpec(shape, lambda bi, i: (0,) * len(shape))
    tile_spec = pl.BlockSpec((1, TILE, d), lambda bi, i: (bi, i + 1, 0))
    dir_spec = pl.BlockSpec((2, 1, TILE, d), lambda bi, i: (0, bi, i + 1, 0))
    return pl.pallas_call(
        _rwkv_out_kernel,
        out_shape=jax.ShapeDtypeStruct((b, nt * TILE, d), F32),
        grid=(b, nt),
        in_specs=[dir_spec, tile_spec, tile_spec, tile_spec, dir_spec, tile_spec,
                  pl.BlockSpec((1, 1, 6, d), lambda bi, i: (bi, 1, 0, 0)),
                  full((1, d)), full((1, d)), full((1, d)), full((d, d)), full((d, d))],
        out_specs=pl.BlockSpec((1, TILE, d), lambda bi, i: (bi, i, 0)),
        compiler_params=_cparams(("arbitrary", "arbitrary")),
        name="rwkv_readout",
    )(y, r, v, g, kd, h, mods, ln_w.reshape(1, d), ln_b.reshape(1, d), r_k.reshape(1, d), bd, w_o.astype(BF16))


def kernel(x, c, ctx, c_ctx, ada_w, ada_b, norm_mix_g, norm_ffn_g, router_w, router_b, exp_w_in, exp_b_in,
           exp_w_out, exp_b_out, ab_w_in, na_q_g, na_k_g, na_rpb, pool_w, pool_scale, ab_w_out,
           rw_mu, rw_w_r, rw_w_k, rw_w_v, rw_w_o, rw_w0, rw_w1, rw_w2, rw_a0, rw_a1, rw_a2,
           rw_g1, rw_g2, rw_k_k, rw_k_a, rw_r_k, rw_ln_w, rw_ln_b):
    b, seq, d = x.shape
    depth = ada_w.shape[0]
    assert ctx.shape[1] == CTX_LEN == TILE and seq % TILE == 0 and d == D_MODEL
    assert depth == 2, "layer schedule below is written for [neighbourhood/pool layer, RWKV layer]"
    rows = seq // GRID_W
    assert rows >= NA_QROWS * NA_KTILES and NA_QROWS * NA_KTILES >= NA_QROWS + NA_KH - 1

    n_c = 1 + b
    n_c_pad = -(-n_c // SUBLANES) * SUBLANES
    cvec = jnp.concatenate([c_ctx[None], c, jnp.zeros((n_c_pad - n_c, d), F32)], axis=0)

    def layer_mods(layer):
        m = ada_mod(cvec, ada_w[layer], ada_b[layer])
        m_ctx = jnp.broadcast_to(m[0].reshape(1, 6, d), (b, 6, d))
        return jnp.stack([m_ctx, m[1:n_c].reshape(b, 6, d)], axis=1)

    def moe(h, layer, mods, tile0, ctx_tiles):
        return moe_layer(h, norm_ffn_g[layer], mods, router_w[layer], router_b[layer], exp_w_in[layer],
                         exp_b_in[layer], exp_w_out[layer], exp_b_out[layer], tile0, ctx_tiles)

    h = jnp.concatenate([ctx, x], axis=1)

    mods = layer_mods(0)
    qk, v, u = ab_proj(h, norm_mix_g[0], mods, ab_w_in[0], na_q_g[0], na_k_g[0])
    o_na = na_attention(qk, v, _na_bias_tables(na_rpb[0], rows))
    o_pool = multiscale_pool(u, pool_w[0], pool_scale[0])
    h = out_proj_residual(o_na, o_pool, h, mods, ab_w_out[0])
    h = moe(h, 0, mods, 0, 1)

    mods = layer_mods(1)
    r, v, kk, g, lw, kd, a = rwkv_features(h, norm_mix_g[1], mods, rw_mu[0], rw_w_r[0], rw_w_k[0], rw_w_v[0],
                                           rw_w0[0], rw_w1[0], rw_w2[0], rw_a0[0], rw_a1[0], rw_a2[0],
                                           rw_g1[0], rw_g2[0], rw_k_k[0], rw_k_a[0])
    y = rwkv_scan(r, v, kk, lw, kd, a)
    h_lat = rwkv_readout(y, r, v, g, kd, h, mods, rw_ln_w[0], rw_ln_b[0], rw_r_k[0], rw_w_o[0])
    return moe(h_lat, 1, mods, 0, 0)
```

```python
import functools

import jax
import jax.numpy as jnp
from jax import lax
from jax.experimental import pallas as pl
from jax.experimental.pallas import tpu as pltpu

F32 = jnp.float32
BF16 = jnp.bfloat16

D_MODEL = 1024
GRID_W = 64
CTX_LEN = 256
RMS_EPS = 1e-6
NA_HEADS = 8
NA_HEAD_DIM = 64
NA_WIDTH = NA_HEADS * NA_HEAD_DIM
NA_KH = 8
NA_KW = 16
POOL_WINDOWS = (2, 4, 8, 16)
POOL_GC = 128
POOL_WIDTH = 512
RW_HEAD_DIM = 64
RW_HEADS = D_MODEL // RW_HEAD_DIM
RW_GN_EPS = 64e-5
N_EXPERTS = 32
TOP_K = 4
D_FF = D_MODEL
SWIGLU_LIMIT = 7.0
SWIGLU_ALPHA = 1.702

TILE = 256
LANES = 128
SUBLANES = 8
MOE_TM = 512
NEG_BIG = -1e30
VMEM_LIMIT = 48 * 1024 * 1024
MOE_FFN_VMEM_LIMIT = 56 * 1024 * 1024


def _cparams(sem):
    return pltpu.CompilerParams(dimension_semantics=sem, vmem_limit_bytes=VMEM_LIMIT)


def _mod_row(mods_ref, k):
    return mods_ref[0, 0, pl.ds(k, 1), :]


def _norm_mod(x, gain, shift, scale):
    ms = jnp.mean(x * x, axis=-1, keepdims=True)
    return (x * lax.rsqrt(ms + RMS_EPS) * gain) * (1.0 + scale) + shift


def _ada_kernel(c_ref, w_ref, b_ref, o_ref):
    c = c_ref[...]
    s = c * jax.nn.sigmoid(c)
    o_ref[...] = jnp.dot(s, w_ref[...], preferred_element_type=F32,
                         precision=lax.Precision.HIGHEST) + b_ref[...]


def ada_mod(cvec, w, b):
    r, d = cvec.shape
    n = w.shape[1]
    tn = 1536
    return pl.pallas_call(
        _ada_kernel,
        out_shape=jax.ShapeDtypeStruct((r, n), F32),
        grid=(n // tn,),
        in_specs=[pl.BlockSpec((r, d), lambda j: (0, 0)),
                  pl.BlockSpec((d, tn), lambda j: (0, j)),
                  pl.BlockSpec((1, tn), lambda j: (0, j))],
        out_specs=pl.BlockSpec((r, tn), lambda j: (0, j)),
        compiler_params=_cparams(("arbitrary",)),
        name="ada_mod",
    )(cvec, w, b.reshape(1, n))


def _ab_proj_kernel(h_ref, g_ref, mods_ref, w_ref, bd_ref, qkg_ref, qk_ref, v_ref, u_ref):
    x = _norm_mod(h_ref[0], g_ref[...], _mod_row(mods_ref, 0), _mod_row(mods_ref, 1))
    p = jnp.dot(x.astype(BF16), w_ref[...], preferred_element_type=F32)
    qk = p[:, :2 * NA_WIDTH]
    ss = jnp.dot((qk * qk).astype(BF16), bd_ref[...], preferred_element_type=F32)
    qk_ref[0] = (qk * lax.rsqrt(ss + RMS_EPS) * qkg_ref[...]).astype(BF16)
    v_ref[0] = p[:, 2 * NA_WIDTH:3 * NA_WIDTH].astype(BF16)
    u_ref[0] = p[:, 3 * NA_WIDTH:]


def ab_proj(h, gain, mods, w_in, q_g, k_g):
    b, lt, d = h.shape
    nt = lt // TILE
    n_in = w_in.shape[1]
    hid = jnp.arange(2 * NA_WIDTH) // NA_HEAD_DIM
    bd = jnp.where(hid[:, None] == hid[None, :], 1.0 / NA_HEAD_DIM, 0.0).astype(BF16)
    qkg = jnp.concatenate([jnp.tile(q_g, NA_HEADS), jnp.tile(k_g, NA_HEADS)]).reshape(1, -1).astype(F32)
    return pl.pallas_call(
        _ab_proj_kernel,
        out_shape=(jax.ShapeDtypeStruct((b, lt, 2 * NA_WIDTH), BF16),
                   jax.ShapeDtypeStruct((b, lt, NA_WIDTH), BF16),
                   jax.ShapeDtypeStruct((b, lt, POOL_WIDTH), F32)),
        grid=(b, nt),
        in_specs=[pl.BlockSpec((1, TILE, d), lambda bi, i: (bi, i, 0)),
                  pl.BlockSpec((1, d), lambda bi, i: (0, 0)),
                  pl.BlockSpec((1, 1, 6, d), lambda bi, i: (bi, jnp.minimum(i, 1), 0, 0)),
                  pl.BlockSpec((d, n_in), lambda bi, i: (0, 0)),
                  pl.BlockSpec((2 * NA_WIDTH, 2 * NA_WIDTH), lambda bi, i: (0, 0)),
                  pl.BlockSpec((1, 2 * NA_WIDTH), lambda bi, i: (0, 0))],
        out_specs=(pl.BlockSpec((1, TILE, 2 * NA_WIDTH), lambda bi, i: (bi, i, 0)),
                   pl.BlockSpec((1, TILE, NA_WIDTH), lambda bi, i: (bi, i, 0)),
                   pl.BlockSpec((1, TILE, POOL_WIDTH), lambda bi, i: (bi, i, 0))),
        compiler_params=_cparams(("arbitrary", "arbitrary")),
        name="ab_proj",
    )(h, gain.reshape(1, d), mods, w_in.astype(BF16), bd, qkg)


NA_QROWS = TILE // GRID_W
NA_KTILES = 3
NA_NLOC = NA_KTILES * TILE
NA_NKEY = NA_NLOC + CTX_LEN


def _na_bias_tables(rpb, rows):
    ql = jnp.arange(TILE)
    kl = jnp.arange(NA_NLOC)
    a, qcol = ql // GRID_W, ql % GRID_W
    jr, kcol = kl // GRID_W, kl % GRID_W
    c_start = jnp.clip(qcol - NA_KW // 2, 0, GRID_W - NA_KW)
    valid_c = (kcol[None, :] >= c_start[:, None]) & (kcol[None, :] < c_start[:, None] + NA_KW)
    c_rel = jnp.clip(kcol[None, :] - qcol[:, None] + NA_KW - 1, 0, 2 * NA_KW - 2)
    n_qt = rows // NA_QROWS
    tabs = []
    for jq in (0, 1, n_qt - 1):
        w0 = NA_QROWS * min(max(jq - 1, 0), n_qt - NA_KTILES)
        q_row = NA_QROWS * jq + a
        key_row = w0 + jr
        r_start = jnp.clip(q_row - NA_KH // 2, 0, rows - NA_KH)
        valid_r = (key_row[None, :] >= r_start[:, None]) & (key_row[None, :] < r_start[:, None] + NA_KH)
        r_rel = jnp.clip(key_row[None, :] - q_row[:, None] + NA_KH - 1, 0, 2 * NA_KH - 2)
        bias = rpb[:, r_rel, c_rel].astype(F32)
        tabs.append(jnp.where((valid_r & valid_c)[None], bias, NEG_BIG))
    tabs = [jnp.full_like(tabs[0], NEG_BIG)] + tabs
    loc = jnp.stack(tabs)
    return jnp.concatenate([loc, jnp.zeros(loc.shape[:3] + (CTX_LEN,), F32)], axis=-1)


def _na_kernel(q_ref, k0_ref, k1_ref, k2_ref, kc_ref, v0_ref, v1_ref, v2_ref, vc_ref, bias_ref, o_ref):
    q = q_ref[0]
    kall = jnp.concatenate([k0_ref[0], k1_ref[0], k2_ref[0], kc_ref[0]], axis=0)
    vall = jnp.concatenate([v0_ref[0], v1_ref[0], v2_ref[0], vc_ref[0]], axis=0)
    lane = lax.broadcasted_iota(jnp.int32, q.shape, 1)
    scale = NA_HEAD_DIM ** -0.5
    outs = []
    for hh in range(2):
        in_head = (lane >= hh * NA_HEAD_DIM) & (lane < (hh + 1) * NA_HEAD_DIM)
        qh = jnp.where(in_head, q, jnp.zeros_like(q)) * jnp.asarray(scale, BF16)
        s = lax.dot_general(qh, kall, (((1,), (1,)), ((), ())), preferred_element_type=F32)
        s = s + bias_ref[0, hh]
        m = jnp.max(s, axis=-1, keepdims=True)
        p = jnp.exp(s - m)
        l = jnp.sum(p, axis=-1, keepdims=True)
        o = jnp.dot(p.astype(BF16), vall, preferred_element_type=F32)
        outs.append(o / l)
    o_ref[0] = jnp.where(lane < NA_HEAD_DIM, outs[0], outs[1]).astype(BF16)


def na_attention(qk, v, bias):
    b, lt, _ = qk.shape
    nt = lt // TILE
    n_qt = nt - 1
    assert NA_HEAD_DIM ** -0.5 == 0.125
    hp_n = NA_HEADS // 2
    kblk = NA_WIDTH // LANES

    def w0(j):
        return 1 + jnp.clip(j - 2, 0, n_qt - NA_KTILES)

    def kind(j):
        return jnp.where(j == 0, 0, jnp.where(j == 1, 1, jnp.where(j == nt - 1, 3, 2)))

    qspec = pl.BlockSpec((1, TILE, LANES), lambda j, hp, bi: (bi, j, hp))
    kspecs = [pl.BlockSpec((1, TILE, LANES), functools.partial(lambda j, hp, bi, t: (bi, w0(j) + t, kblk + hp), t=t))
              for t in range(NA_KTILES)]
    kcspec = pl.BlockSpec((1, TILE, LANES), lambda j, hp, bi: (bi, 0, kblk + hp))
    vspecs = [pl.BlockSpec((1, TILE, LANES), functools.partial(lambda j, hp, bi, t: (bi, w0(j) + t, hp), t=t))
              for t in range(NA_KTILES)]
    vcspec = pl.BlockSpec((1, TILE, LANES), lambda j, hp, bi: (bi, 0, hp))
    bspec = pl.BlockSpec((1, 2, TILE, NA_NKEY), lambda j, hp, bi: (kind(j), hp, 0, 0))
    return pl.pallas_call(
        _na_kernel,
        out_shape=jax.ShapeDtypeStruct((b, lt, NA_WIDTH), BF16),
        grid=(nt, hp_n, b),
        in_specs=[qspec] + kspecs + [kcspec] + vspecs + [vcspec, bspec],
        out_specs=pl.BlockSpec((1, TILE, LANES), lambda j, hp, bi: (bi, j, hp)),
        compiler_params=_cparams(("arbitrary", "arbitrary", "arbitrary")),
        name="na_attention",
    )(qk, qk, qk, qk, qk, v, v, v, v, bias)


POOL_HALO = 8
POOL_ROWS = TILE + 2 * POOL_HALO


def _pool_kernel(cur_ref, prev_ref, next_ref, pw_ref, ps_ref, o_ref, xp_ref, *, n_tiles):
    i = pl.program_id(1)
    has_prev = (i >= 2).astype(F32)
    has_next = ((i >= 1) & (i < n_tiles - 1)).astype(F32)
    xp_ref[0:POOL_HALO, :] = prev_ref[0] * has_prev
    xp_ref[POOL_HALO:POOL_HALO + TILE, :] = cur_ref[0]
    xp_ref[POOL_HALO + TILE:POOL_ROWS, :] = next_ref[0] * has_next
    row = lax.broadcasted_iota(jnp.int32, (TILE, POOL_GC), 0)
    tpos = jnp.where(i == 0, row, row + (i - 1) * TILE)
    seq_len = jnp.where(i == 0, CTX_LEN, (n_tiles - 1) * TILE)
    for g, w in enumerate(POOL_WINDOWS):
        cols = slice(g * POOL_GC, (g + 1) * POOL_GC)
        x = xp_ref[:, cols]
        s = x + pltpu.roll(x, 1, 0)
        half = 1
        while 2 * half < w:
            s = pltpu.roll(s, half, 0) + pltpu.roll(s, POOL_ROWS - half, 0)
            half *= 2
        s = s[POOL_HALO:POOL_HALO + TILE]
        cnt = jnp.minimum(w // 2, tpos) + jnp.minimum(w // 2, seq_len - tpos)
        pooled = s / cnt.astype(F32) - cur_ref[0, :, cols]
        y = jnp.dot(pooled.astype(BF16), pw_ref[g], preferred_element_type=F32)
        o_ref[0, :, cols] = (y * ps_ref[:, cols]).astype(BF16)


def multiscale_pool(u, pool_w, pool_scale):
    b, lt, c = u.shape
    nt = lt // TILE
    hb = TILE // POOL_HALO
    nhb = lt // POOL_HALO
    assert POOL_HALO >= max(POOL_WINDOWS) // 2 and POOL_HALO == SUBLANES
    return pl.pallas_call(
        functools.partial(_pool_kernel, n_tiles=nt),
        out_shape=jax.ShapeDtypeStruct((b, lt, c), BF16),
        grid=(b, nt),
        in_specs=[pl.BlockSpec((1, TILE, c), lambda bi, i: (bi, i, 0)),
                  pl.BlockSpec((1, POOL_HALO, c), lambda bi, i: (bi, jnp.maximum(i * hb - 1, 0), 0)),
                  pl.BlockSpec((1, POOL_HALO, c), lambda bi, i: (bi, jnp.minimum((i + 1) * hb, nhb - 1), 0)),
                  pl.BlockSpec((len(POOL_WINDOWS), POOL_GC, POOL_GC), lambda bi, i: (0, 0, 0)),
                  pl.BlockSpec((1, c), lambda bi, i: (0, 0))],
        out_specs=pl.BlockSpec((1, TILE, c), lambda bi, i: (bi, i, 0)),
        scratch_shapes=[pltpu.VMEM((POOL_ROWS, c), F32)],
        compiler_params=_cparams(("arbitrary", "arbitrary")),
        name="multiscale_pool",
    )(u, u, u, pool_w.astype(BF16), pool_scale.reshape(1, c).astype(F32))


def _out_proj_kernel(a_ref, b_ref, h_ref, mods_ref, w_ref, o_ref):
    ka = a_ref.shape[-1]
    y = jnp.dot(a_ref[0], w_ref[:ka, :], preferred_element_type=F32)
    y = y + jnp.dot(b_ref[0], w_ref[ka:, :], preferred_element_type=F32)
    o_ref[0] = h_ref[0] + _mod_row(mods_ref, 2) * y


def out_proj_residual(a, bb, h, mods, w):
    b, lt, d = h.shape
    nt = lt // TILE
    ka, kb = a.shape[-1], bb.shape[-1]
    return pl.pallas_call(
        _out_proj_kernel,
        out_shape=jax.ShapeDtypeStruct((b, lt, d), F32),
        grid=(b, nt),
        in_specs=[pl.BlockSpec((1, TILE, ka), lambda bi, i: (bi, i, 0)),
                  pl.BlockSpec((1, TILE, kb), lambda bi, i: (bi, i, 0)),
                  pl.BlockSpec((1, TILE, d), lambda bi, i: (bi, i, 0)),
                  pl.BlockSpec((1, 1, 6, d), lambda bi, i: (bi, jnp.minimum(i, 1), 0, 0)),
                  pl.BlockSpec((ka + kb, d), lambda bi, i: (0, 0))],
        out_specs=pl.BlockSpec((1, TILE, d), lambda bi, i: (bi, i, 0)),
        compiler_params=_cparams(("arbitrary", "arbitrary")),
        name="out_proj_residual",
    )(a, bb, h, mods, w.astype(BF16))


def _moe_pre_kernel(h_ref, g_ref, mods_ref, rw_ref, rb_ref, y_ref, lg_ref):
    x = _norm_mod(h_ref[0], g_ref[...], _mod_row(mods_ref, 3), _mod_row(mods_ref, 4))
    y_ref[0] = x.astype(BF16)
    lg_ref[0] = jnp.dot(x, rw_ref[...], preferred_element_type=F32,
                        precision=lax.Precision.HIGHEST) + rb_ref[...]


def moe_pre(h, gain, mods, router_w, router_b, tile0, ctx_tiles):
    b, lt, d = h.shape
    nt = lt // TILE - tile0
    e = router_w.shape[1]
    return pl.pallas_call(
        _moe_pre_kernel,
        out_shape=(jax.ShapeDtypeStruct((b, nt * TILE, d), BF16),
                   jax.ShapeDtypeStruct((b, nt * TILE, e), F32)),
        grid=(b, nt),
        in_specs=[pl.BlockSpec((1, TILE, d), lambda bi, i: (bi, i + tile0, 0)),
                  pl.BlockSpec((1, d), lambda bi, i: (0, 0)),
                  pl.BlockSpec((1, 1, 6, d), lambda bi, i: (bi, jnp.minimum(i + tile0 + 1 - ctx_tiles, 1), 0, 0)),
                  pl.BlockSpec((d, e), lambda bi, i: (0, 0)),
                  pl.BlockSpec((1, e), lambda bi, i: (0, 0))],
        out_specs=(pl.BlockSpec((1, TILE, d), lambda bi, i: (bi, i, 0)),
                   pl.BlockSpec((1, TILE, e), lambda bi, i: (bi, i, 0))),
        compiler_params=_cparams(("arbitrary", "arbitrary")),
        name="moe_pre",
    )(h, gain.reshape(1, d), mods, router_w, router_b.reshape(1, e))


MOE_PERM_W = 256


def _moe_ffn_kernel(te_ref, tv_ref, x_ref, win_ref, bg_ref, bl_ref, wo_ref, bo_ref, perm_ref, o_ref,
                    wg_s, wl_s, wo_s):
    t = pl.program_id(0)
    valid = tv_ref[t] > 0
    new_expert = (t == 0) | (te_ref[t] != te_ref[jnp.maximum(t - 1, 0)])

    @pl.when(valid & new_expert)
    def _():
        half = MOE_PERM_W // 2
        for cb in range(win_ref.shape[2] // MOE_PERM_W):
            w = win_ref[0, :, cb * MOE_PERM_W:(cb + 1) * MOE_PERM_W].astype(BF16)
            wp = jnp.dot(w, perm_ref[...], preferred_element_type=F32).astype(BF16)
            wg_s[:, cb * half:(cb + 1) * half] = wp[:, :half]
            wl_s[:, cb * half:(cb + 1) * half] = wp[:, half:]
        wo_s[...] = wo_ref[0].astype(BF16)

    @pl.when(valid)
    def _():
        x = x_ref[...]
        hg = jnp.dot(x, wg_s[...], preferred_element_type=F32) + bg_ref[0]
        hl = jnp.dot(x, wl_s[...], preferred_element_type=F32) + bl_ref[0]
        glu = jnp.minimum(hg, SWIGLU_LIMIT)
        lin = jnp.clip(hl, -SWIGLU_LIMIT, SWIGLU_LIMIT)
        act = glu * jax.nn.sigmoid(SWIGLU_ALPHA * glu) * (lin + 1.0)
        o_ref[...] = (jnp.dot(act.astype(BF16), wo_s[...], preferred_element_type=F32)
                      + bo_ref[0]).astype(o_ref.dtype)


def moe_grouped_ffn(x_sorted, tile_expert, tile_valid, w_in, bg, bl, w_out, bo):
    p, d = x_sorted.shape
    e, _, f2 = w_in.shape
    f = f2 // 2
    nt = p // MOE_TM
    assert f2 % MOE_PERM_W == 0 and MOE_PERM_W % (2 * LANES) == 0
    src = jnp.arange(MOE_PERM_W)
    dst = jnp.where(src % 2 == 0, src // 2, MOE_PERM_W // 2 + src // 2)
    perm = (dst[:, None] == jnp.arange(MOE_PERM_W)[None, :]).astype(BF16)
    grid_spec = pltpu.PrefetchScalarGridSpec(
        num_scalar_prefetch=2,
        grid=(nt,),
        in_specs=[pl.BlockSpec((MOE_TM, d), lambda t, te, tv: (t, 0)),
                  pl.BlockSpec((1, d, f2), lambda t, te, tv: (te[t], 0, 0)),
                  pl.BlockSpec((1, 1, f), lambda t, te, tv: (te[t], 0, 0)),
                  pl.BlockSpec((1, 1, f), lambda t, te, tv: (te[t], 0, 0)),
                  pl.BlockSpec((1, f, d), lambda t, te, tv: (te[t], 0, 0)),
                  pl.BlockSpec((1, 1, d), lambda t, te, tv: (te[t], 0, 0)),
                  pl.BlockSpec((MOE_PERM_W, MOE_PERM_W), lambda t, te, tv: (0, 0))],
        out_specs=pl.BlockSpec((MOE_TM, d), lambda t, te, tv: (t, 0)),
        scratch_shapes=[pltpu.VMEM((d, f), BF16), pltpu.VMEM((d, f), BF16), pltpu.VMEM((f, d), BF16)],
    )
    return pl.pallas_call(
        _moe_ffn_kernel,
        out_shape=jax.ShapeDtypeStruct((p, d), BF16),
        grid_spec=grid_spec,
        compiler_params=pltpu.CompilerParams(dimension_semantics=("arbitrary",),
                                             vmem_limit_bytes=MOE_FFN_VMEM_LIMIT),
        name="moe_grouped_ffn",
    )(tile_expert, tile_valid, x_sorted, w_in, bg, bl, w_out, bo, perm)


def _moe_combine_kernel(z_ref, gt_ref, h_ref, mods_ref, o_ref):
    gt = gt_ref[0]
    acc = gt[:, 0:1] * z_ref[0, 0].astype(F32)
    for k in range(1, TOP_K):
        acc = acc + gt[:, k:k + 1] * z_ref[k, 0].astype(F32)
    o_ref[0] = h_ref[0] + _mod_row(mods_ref, 5) * acc


def moe_combine(z4, gates, h, mods, tile0, ctx_tiles):
    k, b, lp, d = z4.shape
    nt = lp // TILE
    return pl.pallas_call(
        _moe_combine_kernel,
        out_shape=jax.ShapeDtypeStruct((b, lp, d), F32),
        grid=(b, nt),
        in_specs=[pl.BlockSpec((k, 1, TILE, d), lambda bi, i: (0, bi, i, 0)),
                  pl.BlockSpec((1, TILE, k), lambda bi, i: (bi, i, 0)),
                  pl.BlockSpec((1, TILE, d), lambda bi, i: (bi, i + tile0, 0)),
                  pl.BlockSpec((1, 1, 6, d), lambda bi, i: (bi, jnp.minimum(i + tile0 + 1 - ctx_tiles, 1), 0, 0))],
        out_specs=pl.BlockSpec((1, TILE, d), lambda bi, i: (bi, i, 0)),
        compiler_params=_cparams(("arbitrary", "arbitrary")),
        name="moe_combine",
    )(z4, gates, h, mods)


def moe_layer(h, gain, mods, router_w, router_b, w_in, b_in, w_out, b_out, tile0, ctx_tiles):
    b, lt, d = h.shape
    y, logits = moe_pre(h, gain, mods, router_w, router_b, tile0, ctx_tiles)
    lp = y.shape[1]
    n = b * lp
    y2 = y.reshape(n, d)
    top_val, top_idx = lax.top_k(logits.reshape(n, N_EXPERTS), TOP_K)
    gates = jax.nn.softmax(top_val, axis=-1)

    flat_e = top_idx.reshape(-1)
    onehot = (flat_e[:, None] == jnp.arange(N_EXPERTS)[None, :]).astype(jnp.int32)
    rank = jnp.take_along_axis(jnp.cumsum(onehot, axis=0), flat_e[:, None], axis=1)[:, 0] - 1
    counts = jnp.sum(onehot, axis=0)
    padded = ((counts + MOE_TM - 1) // MOE_TM) * MOE_TM
    ends = jnp.cumsum(padded)
    starts = ends - padded
    slot = starts[flat_e] + rank
    n_tiles = (n * TOP_K) // MOE_TM + N_EXPERTS
    p_rows = n_tiles * MOE_TM
    src_tok = jnp.zeros((p_rows,), jnp.int32).at[slot].set(jnp.arange(n * TOP_K, dtype=jnp.int32) // TOP_K)
    tile_start = jnp.arange(n_tiles, dtype=jnp.int32) * MOE_TM
    tile_expert = jnp.minimum(jnp.sum((tile_start[:, None] >= ends[None, :]).astype(jnp.int32), axis=1),
                              N_EXPERTS - 1)
    tile_valid = (tile_start < ends[-1]).astype(jnp.int32)

    x_sorted = jnp.take(y2, src_tok, axis=0)
    bg = b_in[:, 0::2].reshape(N_EXPERTS, 1, D_FF)
    bl = b_in[:, 1::2].reshape(N_EXPERTS, 1, D_FF)
    z = moe_grouped_ffn(x_sorted, tile_expert, tile_valid, w_in, bg, bl, w_out, b_out.reshape(N_EXPERTS, 1, d))
    z4 = jnp.take(z, slot.reshape(n, TOP_K).T, axis=0).reshape(TOP_K, b, lp, d)
    return moe_combine(z4, gates.reshape(b, lp, TOP_K), h, mods, tile0, ctx_tiles)


def _softplus(z):
    return jnp.maximum(z, 0.0) + jnp.log(1.0 + jnp.exp(-jnp.abs(z)))


def _rwkv_feat_kernel(h_ref, hp_ref, hn_ref, g_ref, mods_ref, mu_ref, wr_ref, wk_ref, wv_ref,
                      w1_ref, w2_ref, a1_ref, a2_ref, g1_ref, g2_ref, w0_ref, a0_ref, kk_ref, ka_ref,
                      bd_ref, r_out, v_out, kk_out, g_out, lw_out, kd_out, a_out, *, n_tiles):
    i = pl.program_id(1)
    gain, shift, scale = g_ref[...], _mod_row(mods_ref, 0), _mod_row(mods_ref, 1)
    x = _norm_mod(h_ref[0], gain, shift, scale)
    has_prev = (i >= 2).astype(F32)
    has_next = ((i >= 1) & (i < n_tiles - 1)).astype(F32)
    x_before = _norm_mod(hp_ref[0], gain, shift, scale)[SUBLANES - 1:SUBLANES] * has_prev
    x_after = _norm_mod(hn_ref[0], gain, shift, scale)[0:1] * has_next
    row = lax.broadcasted_iota(jnp.int32, x.shape, 0)
    prev = jnp.where(row == 0, x_before, pltpu.roll(x, 1, 0))
    nxt = jnp.where(row == TILE - 1, x_after, pltpu.roll(x, TILE - 1, 0))
    xx = 0.5 * (prev + nxt) - x
    xr, xw, xk, xv, xa, xg = (x + xx * mu_ref[pl.ds(j, 1), :] for j in range(6))

    def mm(a, w):
        return jnp.dot(a.astype(BF16), w, preferred_element_type=F32)

    r = mm(xr, wr_ref[...])
    k = mm(xk, wk_ref[...])
    v = mm(xv, wv_ref[...])
    g = mm(jax.nn.sigmoid(mm(xg, g1_ref[...])), g2_ref[...])
    kk = k * kk_ref[...]
    kk = kk * lax.rsqrt(mm(kk * kk, bd_ref[...]) + 1e-12)
    r_out[0] = r.astype(BF16)
    v_out[0] = v.astype(BF16)
    kk_out[0] = kk.astype(BF16)
    g_out[0] = g.astype(BF16)
    tw = jnp.tanh(mm(xw, w1_ref[...]))
    av = mm(xa, a1_ref[...])
    lane = lax.broadcasted_iota(jnp.int32, tw.shape, 1)
    half = tw.shape[1] // 2
    for d in range(2):
        in_dir = (lane >= d * half) & (lane < (d + 1) * half)
        wz = w0_ref[pl.ds(d, 1), :] + mm(jnp.where(in_dir, tw, 0.0), w2_ref[...])
        w_log = -_softplus(-wz) - 0.5
        lw_out[d, 0] = -jnp.exp(w_log)
        a = jax.nn.sigmoid(a0_ref[pl.ds(d, 1), :] + mm(jnp.where(in_dir, av, 0.0), a2_ref[...]))
        kd_out[d, 0] = (k * (1.0 + (a - 1.0) * ka_ref[...])).astype(BF16)
        a_out[d, 0] = a.astype(BF16)


def rwkv_features(h, gain, mods, mu, w_r, w_k, w_v, w0, w1, w2, a0, a1, a2, g1, g2, k_k, k_a):
    b, lt, d = h.shape
    nt = lt // TILE
    hb = TILE // SUBLANES
    nhb = lt // SUBLANES
    hid = jnp.arange(d) // RW_HEAD_DIM
    bd = (hid[:, None] == hid[None, :]).astype(BF16)
    w1c = jnp.concatenate([w1[0], w1[1]], axis=1).astype(BF16)
    a1c = jnp.concatenate([a1[0], a1[1]], axis=1).astype(BF16)
    w2c = jnp.concatenate([w2[0], w2[1]], axis=0).astype(BF16)
    a2c = jnp.concatenate([a2[0], a2[1]], axis=0).astype(BF16)
    full = lambda shape: pl.BlockSpec(shape, lambda bi, i: (0,) * len(shape))
    tile_spec = pl.BlockSpec((1, TILE, d), lambda bi, i: (bi, i, 0))
    dir_spec = pl.BlockSpec((2, 1, TILE, d), lambda bi, i: (0, bi, i, 0))
    seq_bf = jax.ShapeDtypeStruct((b, lt, d), BF16)
    return pl.pallas_call(
        functools.partial(_rwkv_feat_kernel, n_tiles=nt),
        out_shape=(seq_bf, seq_bf, seq_bf, seq_bf,
                   jax.ShapeDtypeStruct((2, b, lt, d), F32),
                   jax.ShapeDtypeStruct((2, b, lt, d), BF16),
                   jax.ShapeDtypeStruct((2, b, lt, d), BF16)),
        grid=(b, nt),
        in_specs=[tile_spec,
                  pl.BlockSpec((1, SUBLANES, d), lambda bi, i: (bi, jnp.maximum(i * hb - 1, 0), 0)),
                  pl.BlockSpec((1, SUBLANES, d), lambda bi, i: (bi, jnp.minimum((i + 1) * hb, nhb - 1), 0)),
                  full((1, d)),
                  pl.BlockSpec((1, 1, 6, d), lambda bi, i: (bi, jnp.minimum(i, 1), 0, 0)),
                  full((6, d)), full((d, d)), full((d, d)), full((d, d)),
                  full(w1c.shape), full(w2c.shape), full(a1c.shape), full(a2c.shape),
                  full(g1.shape), full(g2.shape), full((2, d)), full((2, d)), full((1, d)), full((1, d)),
                  full((d, d))],
        out_specs=(tile_spec, tile_spec, tile_spec, tile_spec, dir_spec, dir_spec, dir_spec),
        compiler_params=_cparams(("arbitrary", "arbitrary")),
        name="rwkv_features",
    )(h, h, h, gain.reshape(1, d), mods, mu, w_r.astype(BF16), w_k.astype(BF16), w_v.astype(BF16),
      w1c, w2c, a1c, a2c, g1.astype(BF16), g2.astype(BF16), w0, a0, k_k.reshape(1, d), k_a.reshape(1, d), bd)


SCAN_C = 64
SCAN_G = 4
SCAN_GW = SCAN_G * RW_HEAD_DIM


def _rwkv_scan_kernel(r_ref, v_ref, kk_ref, lw_ref, kd_ref, a_ref, y_ref, h_ref):
    c = pl.program_id(2)
    sgn = 1 - 2 * pl.program_id(1)

    @pl.when(c == 0)
    def _():
        h_ref[...] = jnp.zeros_like(h_ref)

    cc = SCAN_C
    ti = lax.broadcasted_iota(jnp.int32, (cc, cc), 0)
    si = lax.broadcasted_iota(jnp.int32, (cc, cc), 1)
    before_incl = sgn * (si - ti) <= 0
    lw = lw_ref[0, 0]
    cl = jnp.dot(before_incl.astype(F32), lw, preferred_element_type=F32, precision=lax.Precision.HIGHEST)
    g_in = jnp.exp(cl)
    g_ex = jnp.exp(cl - lw)
    g_inv = jnp.exp(-cl)
    g_end = jnp.exp(jnp.sum(lw, axis=0, keepdims=True))
    kk = kk_ref[0].astype(F32)
    rt = (r_ref[0].astype(F32) * g_in).astype(BF16)
    at = (-kk * g_ex).astype(BF16)
    bt = (kk * a_ref[0, 0].astype(F32) * g_inv).astype(BF16)
    kt = (kd_ref[0, 0].astype(F32) * g_inv).astype(BF16)
    v = v_ref[0]

    gw = SCAN_GW
    tg = lax.broadcasted_iota(jnp.int32, (cc, gw), 0)
    sg = lax.broadcasted_iota(jnp.int32, (cc, gw), 1) % cc
    m_strict = sgn * (sg - tg) < 0
    m_incl = sgn * (sg - tg) <= 0
    eye_g = (sg == tg).astype(F32)
    bi = lax.broadcasted_iota(jnp.int32, (gw, gw), 0) // cc
    bj = lax.broadcasted_iota(jnp.int32, (gw, gw), 1) // cc
    blk = bi == bj

    def bdiag(x):
        return jnp.where(blk, jnp.concatenate([x] * SCAN_G, axis=0), jnp.zeros((), x.dtype))

    def mm(a, b):
        return jnp.dot(a, b, preferred_element_type=F32)

    def mm_nt(a, b):
        return lax.dot_general(a, b, (((1,), (1,)), ((), ())), preferred_element_type=F32)

    groups = range(RW_HEADS // SCAN_G)
    cols = [slice(g * gw, (g + 1) * gw) for g in groups]
    h0 = [h_ref[g] for g in groups]
    h0b = [x.astype(BF16) for x in h0]
    v_bd = [bdiag(v[:, c]) for c in cols]
    mats = [mm_nt(jnp.concatenate([at[:, c], rt[:, c]], axis=0),
                  jnp.concatenate([bdiag(bt[:, c]), bdiag(kt[:, c])], axis=0)) for c in cols]
    a_ab = [jnp.where(m_strict, m[:cc, :gw], 0.0) for m in mats]
    a_ak = [jnp.where(m_strict, m[:cc, gw:], 0.0).astype(BF16) for m in mats]
    m_rb = [jnp.where(m_incl, m[cc:, :gw], 0.0).astype(BF16) for m in mats]
    m_rk = [jnp.where(m_incl, m[cc:, gw:], 0.0).astype(BF16) for m in mats]
    pq = [mm(jnp.concatenate([at[:, cols[g]], a_ak[g]], axis=1), jnp.concatenate([h0b[g], v_bd[g]], axis=0))
          for g in groups]
    pw = [x.astype(BF16) for x in a_ab]
    inv = [eye_g + x for x in a_ab]
    n = 2
    while n < cc:
        pw = [mm(x, bdiag(x)).astype(BF16) for x in pw]
        inv = [inv[g] + mm(inv[g].astype(BF16), bdiag(pw[g])) for g in groups]
        n *= 2
    ub = [mm(inv[g].astype(BF16), bdiag(pq[g].astype(BF16))).astype(BF16) for g in groups]
    for g in groups:
        cs = cols[g]
        y_ref[0, 0, :, cs] = mm(jnp.concatenate([rt[:, cs], m_rb[g], m_rk[g]], axis=1),
                                jnp.concatenate([h0b[g], bdiag(ub[g]), v_bd[g]], axis=0))
    for g in groups:
        cs = cols[g]
        upd = lax.dot_general(jnp.concatenate([bt[:, cs], kt[:, cs]], axis=0),
                              jnp.concatenate([ub[g], v[:, cs]], axis=0),
                              (((0,), (0,)), ((), ())), preferred_element_type=F32)
        ge = jnp.transpose(jnp.broadcast_to(g_end[:, cs], (gw, gw)))
        h_ref[g] = ge * (h0[g] + jnp.where(blk, upd, 0.0))


def rwkv_scan(r, v, kk, lw, kd, a):
    b, lt, d = r.shape
    nc = lt // SCAN_C
    n_ctx = CTX_LEN // SCAN_C
    assert SCAN_C == RW_HEAD_DIM and CTX_LEN % SCAN_C == 0

    def chunk(di, c):
        rev_c = jnp.where(c < n_ctx, n_ctx - 1 - c, nc - 1 + n_ctx - c)
        return jnp.where(di == 1, rev_c, c)

    shared = pl.BlockSpec((1, SCAN_C, d), lambda bi, di, c: (bi, chunk(di, c), 0))
    per_dir = pl.BlockSpec((1, 1, SCAN_C, d), lambda bi, di, c: (di, bi, chunk(di, c), 0))
    return pl.pallas_call(
        _rwkv_scan_kernel,
        out_shape=jax.ShapeDtypeStruct((2, b, lt, d), F32),
        grid=(b, 2, nc),
        in_specs=[shared, shared, shared, per_dir, per_dir, per_dir],
        out_specs=per_dir,
        scratch_shapes=[pltpu.VMEM((RW_HEADS // SCAN_G, SCAN_GW, SCAN_GW), F32)],
        compiler_params=_cparams(("arbitrary", "arbitrary", "arbitrary")),
        name="rwkv_scan",
    )(r, v, kk, lw, kd, a)


def _rwkv_out_kernel(y_ref, r_ref, v_ref, g_ref, kd_ref, h_ref, mods_ref, lnw_ref, lnb_ref, rk_ref,
                     bd_ref, wo_ref, o_ref):
    def head_sum(x):
        return jnp.dot(x.astype(BF16), bd_ref[...], preferred_element_type=F32)

    y = y_ref[0, 0] + y_ref[1, 0]
    inv_n = 1.0 / RW_HEAD_DIM
    mean = head_sum(y) * inv_n
    yc = y - mean
    var = head_sum(yc * yc) * inv_n
    yn = yc * lax.rsqrt(var + RW_GN_EPS) * lnw_ref[...] + lnb_ref[...]
    r = r_ref[0].astype(F32)
    kd = kd_ref[0, 0].astype(F32) + kd_ref[1, 0].astype(F32)
    yn = yn + head_sum(r * kd * rk_ref[...]) * v_ref[0].astype(F32)
    out = jnp.dot((yn * g_ref[0].astype(F32)).astype(BF16), wo_ref[...], preferred_element_type=F32)
    o_ref[0] = h_ref[0] + _mod_row(mods_ref, 2) * out


def rwkv_readout(y, r, v, g, kd, h, mods, ln_w, ln_b, r_k, w_o):
    b, lt, d = h.shape
    nt = lt // TILE - 1
    hid = jnp.arange(d) // RW_HEAD_DIM
    bd = (hid[:, None] == hid[None, :]).astype(BF16)
    full = lambda shape: pl.BlockSpec(shape, lambda bi, i: (0,) * len(shape))
    tile_spec = pl.BlockSpec((1, TILE, d), lambda bi, i: (bi, i + 1, 0))
    dir_spec = pl.BlockSpec((2, 1, TILE, d), lambda bi, i: (0, bi, i + 1, 0))
    return pl.pallas_call(
        _rwkv_out_kernel,
        out_shape=jax.ShapeDtypeStruct((b, nt * TILE, d), F32),
        grid=(b, nt),
        in_specs=[dir_spec, tile_spec, tile_spec, tile_spec, dir_spec, tile_spec,
                  pl.BlockSpec((1, 1, 6, d), lambda bi, i: (bi, 1, 0, 0)),
                  full((1, d)), full((1, d)), full((1, d)), full((d, d)), full((d, d))],
        out_specs=pl.BlockSpec((1, TILE, d), lambda bi, i: (bi, i, 0)),
        compiler_params=_cparams(("arbitrary", "arbitrary")),
        name="rwkv_readout",
    )(y, r, v, g, kd, h, mods, ln_w.reshape(1, d), ln_b.reshape(1, d), r_k.reshape(1, d), bd, w_o.astype(BF16))


def kernel(x, c, ctx, c_ctx, ada_w, ada_b, norm_mix_g, norm_ffn_g, router_w, router_b, exp_w_in, exp_b_in,
           exp_w_out, exp_b_out, ab_w_in, na_q_g, na_k_g, na_rpb, pool_w, pool_scale, ab_w_out,
           rw_mu, rw_w_r, rw_w_k, rw_w_v, rw_w_o, rw_w0, rw_w1, rw_w2, rw_a0, rw_a1, rw_a2,
           rw_g1, rw_g2, rw_k_k, rw_k_a, rw_r_k, rw_ln_w, rw_ln_b):
    b, seq, d = x.shape
    depth = ada_w.shape[0]
    assert ctx.shape[1] == CTX_LEN == TILE and seq % TILE == 0 and d == D_MODEL
    assert depth == 2, "layer schedule below is written for [neighbourhood/pool layer, RWKV layer]"
    rows = seq // GRID_W
    assert rows >= NA_QROWS * NA_KTILES and NA_QROWS * NA_KTILES >= NA_QROWS + NA_KH - 1

    n_c = 1 + b
    n_c_pad = -(-n_c // SUBLANES) * SUBLANES
    cvec = jnp.concatenate([c_ctx[None], c, jnp.zeros((n_c_pad - n_c, d), F32)], axis=0)

    def layer_mods(layer):
        m = ada_mod(cvec, ada_w[layer], ada_b[layer])
        m_ctx = jnp.broadcast_to(m[0].reshape(1, 6, d), (b, 6, d))
        return jnp.stack([m_ctx, m[1:n_c].reshape(b, 6, d)], axis=1)

    def moe(h, layer, mods, tile0, ctx_tiles):
        return moe_layer(h, norm_ffn_g[layer], mods, router_w[layer], router_b[layer], exp_w_in[layer],
                         exp_b_in[layer], exp_w_out[layer], exp_b_out[layer], tile0, ctx_tiles)

    h = jnp.concatenate([ctx, x], axis=1)

    mods = layer_mods(0)
    qk, v, u = ab_proj(h, norm_mix_g[0], mods, ab_w_in[0], na_q_g[0], na_k_g[0])
    o_na = na_attention(qk, v, _na_bias_tables(na_rpb[0], rows))
    o_pool = multiscale_pool(u, pool_w[0], pool_scale[0])
    h = out_proj_residual(o_na, o_pool, h, mods, ab_w_out[0])
    h = moe(h, 0, mods, 0, 1)

    mods = layer_mods(1)
    r, v, kk, g, lw, kd, a = rwkv_features(h, norm_mix_g[1], mods, rw_mu[0], rw_w_r[0], rw_w_k[0], rw_w_v[0],
                                           rw_w0[0], rw_w1[0], rw_w2[0], rw_a0[0], rw_a1[0], rw_a2[0],
                                           rw_g1[0], rw_g2[0], rw_k_k[0], rw_k_a[0])
    y = rwkv_scan(r, v, kk, lw, kd, a)
    h_lat = rwkv_readout(y, r, v, g, kd, h, mods, rw_ln_w[0], rw_ln_b[0], rw_r_k[0], rw_w_o[0])
    return moe(h_lat, 1, mods, 0, 0)
```

```python
import functools

import jax
import jax.numpy as jnp
import numpy as np
from jax import lax
from jax.experimental import pallas as pl
from jax.experimental.pallas import tpu as pltpu

F32 = jnp.float32
BF16 = jnp.bfloat16

D_MODEL = 1024
GRID_W = 64
CTX_LEN = 256
RMS_EPS = 1e-6
NA_HEADS = 8
NA_HEAD_DIM = 64
NA_WIDTH = NA_HEADS * NA_HEAD_DIM
NA_KH = 8
NA_KW = 16
POOL_WINDOWS = (2, 4, 8, 16)
POOL_GC = 128
POOL_WIDTH = 512
RW_HEAD_DIM = 64
RW_HEADS = D_MODEL // RW_HEAD_DIM
RW_GN_EPS = 64e-5
N_EXPERTS = 32
TOP_K = 4
D_FF = D_MODEL
SWIGLU_LIMIT = 7.0
SWIGLU_ALPHA = 1.702

TILE = 256
LANES = 128
SUBLANES = 8
MOE_TM = 512
NEG_BIG = -1e30
VMEM_LIMIT = 48 * 1024 * 1024
MOE_FFN_VMEM_LIMIT = 56 * 1024 * 1024


def _cparams(sem):
    return pltpu.CompilerParams(dimension_semantics=sem, vmem_limit_bytes=VMEM_LIMIT)


def _mod_row(mods_ref, k):
    return mods_ref[0, 0, pl.ds(k, 1), :]


def _norm_mod(x, gain, shift, scale):
    ms = jnp.mean(x * x, axis=-1, keepdims=True)
    return (x * lax.rsqrt(ms + RMS_EPS) * gain) * (1.0 + scale) + shift


def _ada_kernel(c_ref, w_ref, b_ref, o_ref):
    c = c_ref[...]
    s = c * jax.nn.sigmoid(c)
    o_ref[...] = jnp.dot(s, w_ref[...], preferred_element_type=F32,
                         precision=lax.Precision.HIGHEST) + b_ref[...]


def ada_mod(cvec, w, b):
    r, d = cvec.shape
    n = w.shape[1]
    tn = 1536
    return pl.pallas_call(
        _ada_kernel,
        out_shape=jax.ShapeDtypeStruct((r, n), F32),
        grid=(n // tn,),
        in_specs=[pl.BlockSpec((r, d), lambda j: (0, 0)),
                  pl.BlockSpec((d, tn), lambda j: (0, j)),
                  pl.BlockSpec((1, tn), lambda j: (0, j))],
        out_specs=pl.BlockSpec((r, tn), lambda j: (0, j)),
        compiler_params=_cparams(("arbitrary",)),
        name="ada_mod",
    )(cvec, w, b.reshape(1, n))


def _ab_proj_kernel(h_ref, g_ref, mods_ref, w_ref, bd_ref, qkg_ref, qk_ref, v_ref, u_ref):
    x = _norm_mod(h_ref[0], g_ref[...], _mod_row(mods_ref, 0), _mod_row(mods_ref, 1))
    p = jnp.dot(x.astype(BF16), w_ref[...], preferred_element_type=F32)
    qk = p[:, :2 * NA_WIDTH]
    ss = jnp.dot((qk * qk).astype(BF16), bd_ref[...], preferred_element_type=F32)
    qk_ref[0] = (qk * lax.rsqrt(ss + RMS_EPS) * qkg_ref[...]).astype(BF16)
    v_ref[0] = p[:, 2 * NA_WIDTH:3 * NA_WIDTH].astype(BF16)
    u_ref[0] = p[:, 3 * NA_WIDTH:]


def ab_proj(h, gain, mods, w_in, q_g, k_g):
    b, lt, d = h.shape
    nt = lt // TILE
    n_in = w_in.shape[1]
    hid = jnp.arange(2 * NA_WIDTH) // NA_HEAD_DIM
    bd = jnp.where(hid[:, None] == hid[None, :], 1.0 / NA_HEAD_DIM, 0.0).astype(BF16)
    qkg = jnp.concatenate([jnp.tile(q_g, NA_HEADS), jnp.tile(k_g, NA_HEADS)]).reshape(1, -1).astype(F32)
    return pl.pallas_call(
        _ab_proj_kernel,
        out_shape=(jax.ShapeDtypeStruct((b, lt, 2 * NA_WIDTH), BF16),
                   jax.ShapeDtypeStruct((b, lt, NA_WIDTH), BF16),
                   jax.ShapeDtypeStruct((b, lt, POOL_WIDTH), F32)),
        grid=(b, nt),
        in_specs=[pl.BlockSpec((1, TILE, d), lambda bi, i: (bi, i, 0)),
                  pl.BlockSpec((1, d), lambda bi, i: (0, 0)),
                  pl.BlockSpec((1, 1, 6, d), lambda bi, i: (bi, jnp.minimum(i, 1), 0, 0)),
                  pl.BlockSpec((d, n_in), lambda bi, i: (0, 0)),
                  pl.BlockSpec((2 * NA_WIDTH, 2 * NA_WIDTH), lambda bi, i: (0, 0)),
                  pl.BlockSpec((1, 2 * NA_WIDTH), lambda bi, i: (0, 0))],
        out_specs=(pl.BlockSpec((1, TILE, 2 * NA_WIDTH), lambda bi, i: (bi, i, 0)),
                   pl.BlockSpec((1, TILE, NA_WIDTH), lambda bi, i: (bi, i, 0)),
                   pl.BlockSpec((1, TILE, POOL_WIDTH), lambda bi, i: (bi, i, 0))),
        compiler_params=_cparams(("arbitrary", "arbitrary")),
        name="ab_proj",
    )(h, gain.reshape(1, d), mods, w_in.astype(BF16), bd, qkg)


NA_QROWS = TILE // GRID_W
NA_KTILES = 3
NA_NLOC = NA_KTILES * TILE
NA_NKEY = NA_NLOC + CTX_LEN


def _na_bias_tables(rpb, rows):
    n_r, n_c = 2 * NA_KH - 1, 2 * NA_KW - 1
    col = np.arange(GRID_W)
    c_start = np.clip(col - NA_KW // 2, 0, GRID_W - NA_KW)
    valid_c = (col[None, :] >= c_start[:, None]) & (col[None, :] < c_start[:, None] + NA_KW)
    c_rel = np.clip(col[None, :] - col[:, None] + NA_KW - 1, 0, n_c - 1)
    sel_c = (c_rel.reshape(-1)[None, :] == np.arange(n_c)[:, None]).astype(np.float32)
    by_col = jnp.einsum("hrc,cn->hrn", rpb.astype(F32), sel_c, precision=lax.Precision.HIGHEST)
    n_qt = rows // NA_QROWS
    qr, kr = np.arange(NA_QROWS), np.arange(NA_KTILES * NA_QROWS)
    tabs = []
    for jq in (0, 1, n_qt - 1):
        w0 = NA_QROWS * min(max(jq - 1, 0), n_qt - NA_KTILES)
        q_row = NA_QROWS * jq + qr
        key_row = w0 + kr
        r_start = np.clip(q_row - NA_KH // 2, 0, rows - NA_KH)
        valid_r = (key_row[None, :] >= r_start[:, None]) & (key_row[None, :] < r_start[:, None] + NA_KH)
        r_rel = np.clip(key_row[None, :] - q_row[:, None] + NA_KH - 1, 0, n_r - 1)
        sel_r = (r_rel.reshape(-1)[:, None] == np.arange(n_r)[None, :]).astype(np.float32)
        t = jnp.einsum("xr,hrn->hxn", sel_r, by_col, precision=lax.Precision.HIGHEST)
        t = t.reshape(NA_HEADS, NA_QROWS, NA_KTILES * NA_QROWS, GRID_W, GRID_W)
        t = t.transpose(0, 1, 3, 2, 4).reshape(NA_HEADS, TILE, NA_NLOC)
        valid = (valid_r[:, None, :, None] & valid_c[None, :, None, :]).reshape(TILE, NA_NLOC)
        tabs.append(jnp.where(valid[None], t, NEG_BIG))
    tabs = [jnp.full_like(tabs[0], NEG_BIG)] + tabs
    loc = jnp.stack(tabs)
    return jnp.concatenate([loc, jnp.zeros(loc.shape[:3] + (CTX_LEN,), F32)], axis=-1)


def _na_kernel(q_ref, k0_ref, k1_ref, k2_ref, kc_ref, v0_ref, v1_ref, v2_ref, vc_ref, bias_ref, o_ref):
    q = q_ref[0]
    kall = jnp.concatenate([k0_ref[0], k1_ref[0], k2_ref[0], kc_ref[0]], axis=0)
    vall = jnp.concatenate([v0_ref[0], v1_ref[0], v2_ref[0], vc_ref[0]], axis=0)
    lane = lax.broadcasted_iota(jnp.int32, q.shape, 1)
    scale = NA_HEAD_DIM ** -0.5
    outs = []
    for hh in range(2):
        in_head = (lane >= hh * NA_HEAD_DIM) & (lane < (hh + 1) * NA_HEAD_DIM)
        qh = jnp.where(in_head, q, jnp.zeros_like(q)) * jnp.asarray(scale, BF16)
        s = lax.dot_general(qh, kall, (((1,), (1,)), ((), ())), preferred_element_type=F32)
        s = s + bias_ref[0, hh]
        m = jnp.max(s, axis=-1, keepdims=True)
        p = jnp.exp(s - m)
        l = jnp.sum(p, axis=-1, keepdims=True)
        o = jnp.dot(p.astype(BF16), vall, preferred_element_type=F32)
        outs.append(o / l)
    o_ref[0] = jnp.where(lane < NA_HEAD_DIM, outs[0], outs[1]).astype(BF16)


def na_attention(qk, v, bias):
    b, lt, _ = qk.shape
    nt = lt // TILE
    n_qt = nt - 1
    assert NA_HEAD_DIM ** -0.5 == 0.125
    hp_n = NA_HEADS // 2
    kblk = NA_WIDTH // LANES

    def w0(j):
        return 1 + jnp.clip(j - 2, 0, n_qt - NA_KTILES)

    def kind(j):
        return jnp.where(j == 0, 0, jnp.where(j == 1, 1, jnp.where(j == nt - 1, 3, 2)))

    qspec = pl.BlockSpec((1, TILE, LANES), lambda j, hp, bi: (bi, j, hp))
    kspecs = [pl.BlockSpec((1, TILE, LANES), functools.partial(lambda j, hp, bi, t: (bi, w0(j) + t, kblk + hp), t=t))
              for t in range(NA_KTILES)]
    kcspec = pl.BlockSpec((1, TILE, LANES), lambda j, hp, bi: (bi, 0, kblk + hp))
    vspecs = [pl.BlockSpec((1, TILE, LANES), functools.partial(lambda j, hp, bi, t: (bi, w0(j) + t, hp), t=t))
              for t in range(NA_KTILES)]
    vcspec = pl.BlockSpec((1, TILE, LANES), lambda j, hp, bi: (bi, 0, hp))
    bspec = pl.BlockSpec((1, 2, TILE, NA_NKEY), lambda j, hp, bi: (kind(j), hp, 0, 0))
    return pl.pallas_call(
        _na_kernel,
        out_shape=jax.ShapeDtypeStruct((b, lt, NA_WIDTH), BF16),
        grid=(nt, hp_n, b),
        in_specs=[qspec] + kspecs + [kcspec] + vspecs + [vcspec, bspec],
        out_specs=pl.BlockSpec((1, TILE, LANES), lambda j, hp, bi: (bi, j, hp)),
        compiler_params=_cparams(("arbitrary", "arbitrary", "arbitrary")),
        name="na_attention",
    )(qk, qk, qk, qk, qk, v, v, v, v, bias)


POOL_HALO = 8
POOL_ROWS = TILE + 2 * POOL_HALO


def _pool_kernel(cur_ref, prev_ref, next_ref, pw_ref, ps_ref, o_ref, xp_ref, *, n_tiles):
    i = pl.program_id(1)
    has_prev = (i >= 2).astype(F32)
    has_next = ((i >= 1) & (i < n_tiles - 1)).astype(F32)
    xp_ref[0:POOL_HALO, :] = prev_ref[0] * has_prev
    xp_ref[POOL_HALO:POOL_HALO + TILE, :] = cur_ref[0]
    xp_ref[POOL_HALO + TILE:POOL_ROWS, :] = next_ref[0] * has_next
    row = lax.broadcasted_iota(jnp.int32, (TILE, POOL_GC), 0)
    tpos = jnp.where(i == 0, row, row + (i - 1) * TILE)
    seq_len = jnp.where(i == 0, CTX_LEN, (n_tiles - 1) * TILE)
    for g, w in enumerate(POOL_WINDOWS):
        cols = slice(g * POOL_GC, (g + 1) * POOL_GC)
        x = xp_ref[:, cols]
        s = x + pltpu.roll(x, 1, 0)
        half = 1
        while 2 * half < w:
            s = pltpu.roll(s, half, 0) + pltpu.roll(s, POOL_ROWS - half, 0)
            half *= 2
        s = s[POOL_HALO:POOL_HALO + TILE]
        cnt = jnp.minimum(w // 2, tpos) + jnp.minimum(w // 2, seq_len - tpos)
        pooled = s / cnt.astype(F32) - cur_ref[0, :, cols]
        y = jnp.dot(pooled.astype(BF16), pw_ref[g], preferred_element_type=F32)
        o_ref[0, :, cols] = (y * ps_ref[:, cols]).astype(BF16)


def multiscale_pool(u, pool_w, pool_scale):
    b, lt, c = u.shape
    nt = lt // TILE
    hb = TILE // POOL_HALO
    nhb = lt // POOL_HALO
    assert POOL_HALO >= max(POOL_WINDOWS) // 2 and POOL_HALO == SUBLANES
    return pl.pallas_call(
        functools.partial(_pool_kernel, n_tiles=nt),
        out_shape=jax.ShapeDtypeStruct((b, lt, c), BF16),
        grid=(b, nt),
        in_specs=[pl.BlockSpec((1, TILE, c), lambda bi, i: (bi, i, 0)),
                  pl.BlockSpec((1, POOL_HALO, c), lambda bi, i: (bi, jnp.maximum(i * hb - 1, 0), 0)),
                  pl.BlockSpec((1, POOL_HALO, c), lambda bi, i: (bi, jnp.minimum((i + 1) * hb, nhb - 1), 0)),
                  pl.BlockSpec((len(POOL_WINDOWS), POOL_GC, POOL_GC), lambda bi, i: (0, 0, 0)),
                  pl.BlockSpec((1, c), lambda bi, i: (0, 0))],
        out_specs=pl.BlockSpec((1, TILE, c), lambda bi, i: (bi, i, 0)),
        scratch_shapes=[pltpu.VMEM((POOL_ROWS, c), F32)],
        compiler_params=_cparams(("arbitrary", "arbitrary")),
        name="multiscale_pool",
    )(u, u, u, pool_w.astype(BF16), pool_scale.reshape(1, c).astype(F32))


def _out_proj_kernel(a_ref, b_ref, h_ref, mods_ref, w_ref, o_ref):
    ka = a_ref.shape[-1]
    y = jnp.dot(a_ref[0], w_ref[:ka, :], preferred_element_type=F32)
    y = y + jnp.dot(b_ref[0], w_ref[ka:, :], preferred_element_type=F32)
    o_ref[0] = h_ref[0] + _mod_row(mods_ref, 2) * y


def out_proj_residual(a, bb, h, mods, w):
    b, lt, d = h.shape
    nt = lt // TILE
    ka, kb = a.shape[-1], bb.shape[-1]
    return pl.pallas_call(
        _out_proj_kernel,
        out_shape=jax.ShapeDtypeStruct((b, lt, d), F32),
        grid=(b, nt),
        in_specs=[pl.BlockSpec((1, TILE, ka), lambda bi, i: (bi, i, 0)),
                  pl.BlockSpec((1, TILE, kb), lambda bi, i: (bi, i, 0)),
                  pl.BlockSpec((1, TILE, d), lambda bi, i: (bi, i, 0)),
                  pl.BlockSpec((1, 1, 6, d), lambda bi, i: (bi, jnp.minimum(i, 1), 0, 0)),
                  pl.BlockSpec((ka + kb, d), lambda bi, i: (0, 0))],
        out_specs=pl.BlockSpec((1, TILE, d), lambda bi, i: (bi, i, 0)),
        compiler_params=_cparams(("arbitrary", "arbitrary")),
        name="out_proj_residual",
    )(a, bb, h, mods, w.astype(BF16))


def _moe_pre_kernel(h_ref, g_ref, mods_ref, rw_ref, rb_ref, y_ref, lg_ref):
    x = _norm_mod(h_ref[0], g_ref[...], _mod_row(mods_ref, 3), _mod_row(mods_ref, 4))
    y_ref[0] = x.astype(BF16)
    lg_ref[0] = jnp.dot(x, rw_ref[...], preferred_element_type=F32,
                        precision=lax.Precision.HIGHEST) + rb_ref[...]


def moe_pre(h, gain, mods, router_w, router_b, tile0, ctx_tiles):
    b, lt, d = h.shape
    nt = lt // TILE - tile0
    e = router_w.shape[1]
    return pl.pallas_call(
        _moe_pre_kernel,
        out_shape=(jax.ShapeDtypeStruct((b, nt * TILE, d), BF16),
                   jax.ShapeDtypeStruct((b, nt * TILE, e), F32)),
        grid=(b, nt),
        in_specs=[pl.BlockSpec((1, TILE, d), lambda bi, i: (bi, i + tile0, 0)),
                  pl.BlockSpec((1, d), lambda bi, i: (0, 0)),
                  pl.BlockSpec((1, 1, 6, d), lambda bi, i: (bi, jnp.minimum(i + tile0 + 1 - ctx_tiles, 1), 0, 0)),
                  pl.BlockSpec((d, e), lambda bi, i: (0, 0)),
                  pl.BlockSpec((1, e), lambda bi, i: (0, 0))],
        out_specs=(pl.BlockSpec((1, TILE, d), lambda bi, i: (bi, i, 0)),
                   pl.BlockSpec((1, TILE, e), lambda bi, i: (bi, i, 0))),
        compiler_params=_cparams(("arbitrary", "arbitrary")),
        name="moe_pre",
    )(h, gain.reshape(1, d), mods, router_w, router_b.reshape(1, e))


MOE_PERM_W = 256


def _moe_ffn_kernel(te_ref, tv_ref, x_ref, win_ref, bg_ref, bl_ref, wo_ref, bo_ref, perm_ref, o_ref,
                    wg_s, wl_s, wo_s):
    t = pl.program_id(0)
    valid = tv_ref[t] > 0
    new_expert = (t == 0) | (te_ref[t] != te_ref[jnp.maximum(t - 1, 0)])

    @pl.when(valid & new_expert)
    def _():
        half = MOE_PERM_W // 2
        for cb in range(win_ref.shape[3] // MOE_PERM_W):
            w = win_ref[0, 0, :, cb * MOE_PERM_W:(cb + 1) * MOE_PERM_W].astype(BF16)
            wp = jnp.dot(w, perm_ref[...], preferred_element_type=F32).astype(BF16)
            wg_s[:, cb * half:(cb + 1) * half] = wp[:, :half]
            wl_s[:, cb * half:(cb + 1) * half] = wp[:, half:]
        wo_s[...] = wo_ref[0, 0].astype(BF16)

    @pl.when(valid)
    def _():
        x = x_ref[...]
        hg = jnp.dot(x, wg_s[...], preferred_element_type=F32) + bg_ref[0]
        hl = jnp.dot(x, wl_s[...], preferred_element_type=F32) + bl_ref[0]
        glu = jnp.minimum(hg, SWIGLU_LIMIT)
        lin = jnp.clip(hl, -SWIGLU_LIMIT, SWIGLU_LIMIT)
        act = glu * jax.nn.sigmoid(SWIGLU_ALPHA * glu) * (lin + 1.0)
        o_ref[...] = (jnp.dot(act.astype(BF16), wo_s[...], preferred_element_type=F32)
                      + bo_ref[0]).astype(o_ref.dtype)


def moe_grouped_ffn(x_sorted, tile_expert, tile_valid, layer, w_in, bg, bl, w_out, bo):
    p, d = x_sorted.shape
    _, e, _, f2 = w_in.shape
    f = f2 // 2
    nt = p // MOE_TM
    assert f2 % MOE_PERM_W == 0 and MOE_PERM_W % (2 * LANES) == 0
    src = jnp.arange(MOE_PERM_W)
    dst = jnp.where(src % 2 == 0, src // 2, MOE_PERM_W // 2 + src // 2)
    perm = (dst[:, None] == jnp.arange(MOE_PERM_W)[None, :]).astype(BF16)
    grid_spec = pltpu.PrefetchScalarGridSpec(
        num_scalar_prefetch=2,
        grid=(nt,),
        in_specs=[pl.BlockSpec((MOE_TM, d), lambda t, te, tv: (t, 0)),
                  pl.BlockSpec((1, 1, d, f2), lambda t, te, tv: (layer, te[t], 0, 0)),
                  pl.BlockSpec((1, 1, f), lambda t, te, tv: (te[t], 0, 0)),
                  pl.BlockSpec((1, 1, f), lambda t, te, tv: (te[t], 0, 0)),
                  pl.BlockSpec((1, 1, f, d), lambda t, te, tv: (layer, te[t], 0, 0)),
                  pl.BlockSpec((1, 1, d), lambda t, te, tv: (te[t], 0, 0)),
                  pl.BlockSpec((MOE_PERM_W, MOE_PERM_W), lambda t, te, tv: (0, 0))],
        out_specs=pl.BlockSpec((MOE_TM, d), lambda t, te, tv: (t, 0)),
        scratch_shapes=[pltpu.VMEM((d, f), BF16), pltpu.VMEM((d, f), BF16), pltpu.VMEM((f, d), BF16)],
    )
    return pl.pallas_call(
        _moe_ffn_kernel,
        out_shape=jax.ShapeDtypeStruct((p, d), BF16),
        grid_spec=grid_spec,
        compiler_params=pltpu.CompilerParams(dimension_semantics=("arbitrary",),
                                             vmem_limit_bytes=MOE_FFN_VMEM_LIMIT),
        name="moe_grouped_ffn",
    )(tile_expert, tile_valid, x_sorted, w_in, bg, bl, w_out, bo, perm)


def _moe_combine_kernel(z_ref, gt_ref, h_ref, mods_ref, o_ref):
    gt = gt_ref[0]
    acc = gt[:, 0:1] * z_ref[0, 0].astype(F32)
    for k in range(1, TOP_K):
        acc = acc + gt[:, k:k + 1] * z_ref[k, 0].astype(F32)
    o_ref[0] = h_ref[0] + _mod_row(mods_ref, 5) * acc


def moe_combine(z4, gates, h, mods, tile0, ctx_tiles):
    k, b, lp, d = z4.shape
    nt = lp // TILE
    return pl.pallas_call(
        _moe_combine_kernel,
        out_shape=jax.ShapeDtypeStruct((b, lp, d), F32),
        grid=(b, nt),
        in_specs=[pl.BlockSpec((k, 1, TILE, d), lambda bi, i: (0, bi, i, 0)),
                  pl.BlockSpec((1, TILE, k), lambda bi, i: (bi, i, 0)),
                  pl.BlockSpec((1, TILE, d), lambda bi, i: (bi, i + tile0, 0)),
                  pl.BlockSpec((1, 1, 6, d), lambda bi, i: (bi, jnp.minimum(i + tile0 + 1 - ctx_tiles, 1), 0, 0))],
        out_specs=pl.BlockSpec((1, TILE, d), lambda bi, i: (bi, i, 0)),
        compiler_params=_cparams(("arbitrary", "arbitrary")),
        name="moe_combine",
    )(z4, gates, h, mods)


def moe_layer(h, gain, mods, router_w, router_b, layer, w_in, b_in, w_out, b_out, tile0, ctx_tiles):
    b, lt, d = h.shape
    y, logits = moe_pre(h, gain, mods, router_w, router_b, tile0, ctx_tiles)
    lp = y.shape[1]
    n = b * lp
    y2 = y.reshape(n, d)
    top_val, top_idx = lax.top_k(logits.reshape(n, N_EXPERTS), TOP_K)
    gates = jax.nn.softmax(top_val, axis=-1)

    flat_e = top_idx.reshape(-1)
    onehot = (flat_e[:, None] == jnp.arange(N_EXPERTS)[None, :]).astype(jnp.int32)
    rank = jnp.take_along_axis(jnp.cumsum(onehot, axis=0), flat_e[:, None], axis=1)[:, 0] - 1
    counts = jnp.sum(onehot, axis=0)
    padded = ((counts + MOE_TM - 1) // MOE_TM) * MOE_TM
    ends = jnp.cumsum(padded)
    starts = ends - padded
    slot = starts[flat_e] + rank
    n_tiles = (n * TOP_K) // MOE_TM + N_EXPERTS
    p_rows = n_tiles * MOE_TM
    src_tok = (jnp.arange(p_rows, dtype=jnp.int32) % n).at[slot].set(
        jnp.arange(n * TOP_K, dtype=jnp.int32) // TOP_K)
    tile_start = jnp.arange(n_tiles, dtype=jnp.int32) * MOE_TM
    tile_expert = jnp.minimum(jnp.sum((tile_start[:, None] >= ends[None, :]).astype(jnp.int32), axis=1),
                              N_EXPERTS - 1)
    tile_valid = (tile_start < ends[-1]).astype(jnp.int32)

    x_sorted = y2.at[src_tok].get(mode="promise_in_bounds")
    bg = b_in[layer, :, 0::2].reshape(N_EXPERTS, 1, D_FF)
    bl = b_in[layer, :, 1::2].reshape(N_EXPERTS, 1, D_FF)
    z = moe_grouped_ffn(x_sorted, tile_expert, tile_valid, layer, w_in, bg, bl, w_out,
                        b_out[layer].reshape(N_EXPERTS, 1, d))
    z4 = z.at[slot.reshape(n, TOP_K).T].get(mode="promise_in_bounds").reshape(TOP_K, b, lp, d)
    return moe_combine(z4, gates.reshape(b, lp, TOP_K), h, mods, tile0, ctx_tiles)


def _softplus(z):
    return jnp.maximum(z, 0.0) + jnp.log(1.0 + jnp.exp(-jnp.abs(z)))


def _rwkv_feat_kernel(h_ref, hp_ref, hn_ref, g_ref, mods_ref, mu_ref, wr_ref, wk_ref, wv_ref,
                      w1_ref, w2_ref, a1_ref, a2_ref, g1_ref, g2_ref, w0_ref, a0_ref, kk_ref, ka_ref,
                      bd_ref, r_out, v_out, kk_out, g_out, lw_out, kd_out, a_out, *, n_tiles):
    i = pl.program_id(1)
    gain, shift, scale = g_ref[...], _mod_row(mods_ref, 0), _mod_row(mods_ref, 1)
    x = _norm_mod(h_ref[0], gain, shift, scale)
    has_prev = (i >= 2).astype(F32)
    has_next = ((i >= 1) & (i < n_tiles - 1)).astype(F32)
    x_before = _norm_mod(hp_ref[0], gain, shift, scale)[SUBLANES - 1:SUBLANES] * has_prev
    x_after = _norm_mod(hn_ref[0], gain, shift, scale)[0:1] * has_next
    row = lax.broadcasted_iota(jnp.int32, x.shape, 0)
    prev = jnp.where(row == 0, x_before, pltpu.roll(x, 1, 0))
    nxt = jnp.where(row == TILE - 1, x_after, pltpu.roll(x, TILE - 1, 0))
    xx = 0.5 * (prev + nxt) - x
    xr, xw, xk, xv, xa, xg = (x + xx * mu_ref[pl.ds(j, 1), :] for j in range(6))

    def mm(a, w):
        return jnp.dot(a.astype(BF16), w, preferred_element_type=F32)

    r = mm(xr, wr_ref[...])
    k = mm(xk, wk_ref[...])
    v = mm(xv, wv_ref[...])
    g = mm(jax.nn.sigmoid(mm(xg, g1_ref[...])), g2_ref[...])
    kk = k * kk_ref[...]
    kk = kk * lax.rsqrt(mm(kk * kk, bd_ref[...]) + 1e-12)
    r_out[0] = r.astype(BF16)
    v_out[0] = v.astype(BF16)
    kk_out[0] = kk.astype(BF16)
    g_out[0] = g.astype(BF16)
    tw = jnp.tanh(mm(xw, w1_ref[...]))
    av = mm(xa, a1_ref[...])
    lane = lax.broadcasted_iota(jnp.int32, tw.shape, 1)
    half = tw.shape[1] // 2
    for d in range(2):
        in_dir = (lane >= d * half) & (lane < (d + 1) * half)
        wz = w0_ref[pl.ds(d, 1), :] + mm(jnp.where(in_dir, tw, 0.0), w2_ref[...])
        w_log = -_softplus(-wz) - 0.5
        lw_out[d, 0] = -jnp.exp(w_log)
        a = jax.nn.sigmoid(a0_ref[pl.ds(d, 1), :] + mm(jnp.where(in_dir, av, 0.0), a2_ref[...]))
        kd_out[d, 0] = (k * (1.0 + (a - 1.0) * ka_ref[...])).astype(BF16)
        a_out[d, 0] = a.astype(BF16)


def rwkv_features(h, gain, mods, mu, w_r, w_k, w_v, w0, w1, w2, a0, a1, a2, g1, g2, k_k, k_a):
    b, lt, d = h.shape
    nt = lt // TILE
    hb = TILE // SUBLANES
    nhb = lt // SUBLANES
    hid = jnp.arange(d) // RW_HEAD_DIM
    bd = (hid[:, None] == hid[None, :]).astype(BF16)
    w1c = jnp.concatenate([w1[0], w1[1]], axis=1).astype(BF16)
    a1c = jnp.concatenate([a1[0], a1[1]], axis=1).astype(BF16)
    w2c = jnp.concatenate([w2[0], w2[1]], axis=0).astype(BF16)
    a2c = jnp.concatenate([a2[0], a2[1]], axis=0).astype(BF16)
    full = lambda shape: pl.BlockSpec(shape, lambda bi, i: (0,) * len(shape))
    tile_spec = pl.BlockSpec((1, TILE, d), lambda bi, i: (bi, i, 0))
    dir_spec = pl.BlockSpec((2, 1, TILE, d), lambda bi, i: (0, bi, i, 0))
    seq_bf = jax.ShapeDtypeStruct((b, lt, d), BF16)
    return pl.pallas_call(
        functools.partial(_rwkv_feat_kernel, n_tiles=nt),
        out_shape=(seq_bf, seq_bf, seq_bf, seq_bf,
                   jax.ShapeDtypeStruct((2, b, lt, d), F32),
                   jax.ShapeDtypeStruct((2, b, lt, d), BF16),
                   jax.ShapeDtypeStruct((2, b, lt, d), BF16)),
        grid=(b, nt),
        in_specs=[tile_spec,
                  pl.BlockSpec((1, SUBLANES, d), lambda bi, i: (bi, jnp.maximum(i * hb - 1, 0), 0)),
                  pl.BlockSpec((1, SUBLANES, d), lambda bi, i: (bi, jnp.minimum((i + 1) * hb, nhb - 1), 0)),
                  full((1, d)),
                  pl.BlockSpec((1, 1, 6, d), lambda bi, i: (bi, jnp.minimum(i, 1), 0, 0)),
                  full((6, d)), full((d, d)), full((d, d)), full((d, d)),
                  full(w1c.shape), full(w2c.shape), full(a1c.shape), full(a2c.shape),
                  full(g1.shape), full(g2.shape), full((2, d)), full((2, d)), full((1, d)), full((1, d)),
                  full((d, d))],
        out_specs=(tile_spec, tile_spec, tile_spec, tile_spec, dir_spec, dir_spec, dir_spec),
        compiler_params=_cparams(("arbitrary", "arbitrary")),
        name="rwkv_features",
    )(h, h, h, gain.reshape(1, d), mods, mu, w_r.astype(BF16), w_k.astype(BF16), w_v.astype(BF16),
      w1c, w2c, a1c, a2c, g1.astype(BF16), g2.astype(BF16), w0, a0, k_k.reshape(1, d), k_a.reshape(1, d), bd)


SCAN_C = 64
SCAN_G = 4
SCAN_GW = SCAN_G * RW_HEAD_DIM


def _rwkv_scan_kernel(r_ref, v_ref, kk_ref, lw_ref, kd_ref, a_ref, y_ref, h_ref):
    c = pl.program_id(2)
    sgn = 1 - 2 * pl.program_id(1)

    @pl.when(c == 0)
    def _():
        h_ref[...] = jnp.zeros_like(h_ref)

    cc = SCAN_C
    ti = lax.broadcasted_iota(jnp.int32, (cc, cc), 0)
    si = lax.broadcasted_iota(jnp.int32, (cc, cc), 1)
    before_incl = sgn * (si - ti) <= 0
    lw = lw_ref[0, 0]
    cl = jnp.dot(before_incl.astype(F32), lw, preferred_element_type=F32, precision=lax.Precision.HIGHEST)
    g_in = jnp.exp(cl)
    g_ex = jnp.exp(cl - lw)
    g_inv = jnp.exp(-cl)
    g_end = jnp.exp(jnp.sum(lw, axis=0, keepdims=True))
    kk = kk_ref[0].astype(F32)
    rt = (r_ref[0].astype(F32) * g_in).astype(BF16)
    at = (-kk * g_ex).astype(BF16)
    bt = (kk * a_ref[0, 0].astype(F32) * g_inv).astype(BF16)
    kt = (kd_ref[0, 0].astype(F32) * g_inv).astype(BF16)
    v = v_ref[0]

    gw = SCAN_GW
    tg = lax.broadcasted_iota(jnp.int32, (cc, gw), 0)
    sg = lax.broadcasted_iota(jnp.int32, (cc, gw), 1) % cc
    m_strict = sgn * (sg - tg) < 0
    m_incl = sgn * (sg - tg) <= 0
    eye_g = (sg == tg).astype(F32)
    bi = lax.broadcasted_iota(jnp.int32, (gw, gw), 0) // cc
    bj = lax.broadcasted_iota(jnp.int32, (gw, gw), 1) // cc
    blk = bi == bj

    def bdiag(x):
        return jnp.where(blk, jnp.concatenate([x] * SCAN_G, axis=0), jnp.zeros((), x.dtype))

    def mm(a, b):
        return jnp.dot(a, b, preferred_element_type=F32)

    def mm_nt(a, b):
        return lax.dot_general(a, b, (((1,), (1,)), ((), ())), preferred_element_type=F32)

    groups = range(RW_HEADS // SCAN_G)
    cols = [slice(g * gw, (g + 1) * gw) for g in groups]
    h0 = [h_ref[g] for g in groups]
    h0b = [x.astype(BF16) for x in h0]
    v_bd = [bdiag(v[:, c]) for c in cols]
    mats = [mm_nt(jnp.concatenate([at[:, c], rt[:, c]], axis=0),
                  jnp.concatenate([bdiag(bt[:, c]), bdiag(kt[:, c])], axis=0)) for c in cols]
    a_ab = [jnp.where(m_strict, m[:cc, :gw], 0.0) for m in mats]
    a_ak = [jnp.where(m_strict, m[:cc, gw:], 0.0).astype(BF16) for m in mats]
    m_rb = [jnp.where(m_incl, m[cc:, :gw], 0.0).astype(BF16) for m in mats]
    m_rk = [jnp.where(m_incl, m[cc:, gw:], 0.0).astype(BF16) for m in mats]
    pq = [mm(jnp.concatenate([at[:, cols[g]], a_ak[g]], axis=1), jnp.concatenate([h0b[g], v_bd[g]], axis=0))
          for g in groups]
    pw = [x.astype(BF16) for x in a_ab]
    inv = [eye_g + x for x in a_ab]
    n = 2
    while n < cc:
        pw = [mm(x, bdiag(x)).astype(BF16) for x in pw]
        inv = [inv[g] + mm(inv[g].astype(BF16), bdiag(pw[g])) for g in groups]
        n *= 2
    ub = [mm(inv[g].astype(BF16), bdiag(pq[g].astype(BF16))).astype(BF16) for g in groups]
    for g in groups:
        cs = cols[g]
        y_ref[0, 0, :, cs] = mm(jnp.concatenate([rt[:, cs], m_rb[g], m_rk[g]], axis=1),
                                jnp.concatenate([h0b[g], bdiag(ub[g]), v_bd[g]], axis=0))
    for g in groups:
        cs = cols[g]
        upd = lax.dot_general(jnp.concatenate([bt[:, cs], kt[:, cs]], axis=0),
                              jnp.concatenate([ub[g], v[:, cs]], axis=0),
                              (((0,), (0,)), ((), ())), preferred_element_type=F32)
        ge = jnp.transpose(jnp.broadcast_to(g_end[:, cs], (gw, gw)))
        h_ref[g] = ge * (h0[g] + jnp.where(blk, upd, 0.0))


def rwkv_scan(r, v, kk, lw, kd, a):
    b, lt, d = r.shape
    nc = lt // SCAN_C
    n_ctx = CTX_LEN // SCAN_C
    assert SCAN_C == RW_HEAD_DIM and CTX_LEN % SCAN_C == 0

    def chunk(di, c):
        rev_c = jnp.where(c < n_ctx, n_ctx - 1 - c, nc - 1 + n_ctx - c)
        return jnp.where(di == 1, rev_c, c)

    shared = pl.BlockSpec((1, SCAN_C, d), lambda bi, di, c: (bi, chunk(di, c), 0))
    per_dir = pl.BlockSpec((1, 1, SCAN_C, d), lambda bi, di, c: (di, bi, chunk(di, c), 0))
    return pl.pallas_call(
        _rwkv_scan_kernel,
        out_shape=jax.ShapeDtypeStruct((2, b, lt, d), F32),
        grid=(b, 2, nc),
        in_specs=[shared, shared, shared, per_dir, per_dir, per_dir],
        out_specs=per_dir,
        scratch_shapes=[pltpu.VMEM((RW_HEADS // SCAN_G, SCAN_GW, SCAN_GW), F32)],
        compiler_params=_cparams(("arbitrary", "arbitrary", "arbitrary")),
        name="rwkv_scan",
    )(r, v, kk, lw, kd, a)


def _rwkv_out_kernel(y_ref, r_ref, v_ref, g_ref, kd_ref, h_ref, mods_ref, lnw_ref, lnb_ref, rk_ref,
                     bd_ref, wo_ref, o_ref):
    def head_sum(x):
        return jnp.dot(x.astype(BF16), bd_ref[...], preferred_element_type=F32)

    y = y_ref[0, 0] + y_ref[1, 0]
    inv_n = 1.0 / RW_HEAD_DIM
    mean = head_sum(y) * inv_n
    yc = y - mean
    var = head_sum(yc * yc) * inv_n
    yn = yc * lax.rsqrt(var + RW_GN_EPS) * lnw_ref[...] + lnb_ref[...]
    r = r_ref[0].astype(F32)
    kd = kd_ref[0, 0].astype(F32) + kd_ref[1, 0].astype(F32)
    yn = yn + head_sum(r * kd * rk_ref[...]) * v_ref[0].astype(F32)
    out = jnp.dot((yn * g_ref[0].astype(F32)).astype(BF16), wo_ref[...], preferred_element_type=F32)
    o_ref[0] = h_ref[0] + _mod_row(mods_ref, 2) * out


def rwkv_readout(y, r, v, g, kd, h, mods, ln_w, ln_b, r_k, w_o):
    b, lt, d = h.shape
    nt = lt // TILE - 1
    hid = jnp.arange(d) // RW_HEAD_DIM
    bd = (hid[:, None] == hid[None, :]).astype(BF16)
    full = lambda shape: pl.BlockSpec(shape, lambda bi, i: (0,) * len(shape))
    tile_spec = pl.BlockSpec((1, TILE, d), lambda bi, i: (bi, i + 1, 0))
    dir_spec = pl.BlockSpec((2, 1, TILE, d), lambda bi, i: (0, bi, i + 1, 0))
    return pl.pallas_call(
        _rwkv_out_kernel,
        out_shape=jax.ShapeDtypeStruct((b, nt * TILE, d), F32),
        grid=(b, nt),
        in_specs=[dir_spec, tile_spec, tile_spec, tile_spec, dir_spec, tile_spec,
                  pl.BlockSpec((1, 1, 6, d), lambda bi, i: (bi, 1, 0, 0)),
                  full((1, d)), full((1, d)), full((1, d)), full((d, d)), full((d, d))],
        out_specs=pl.BlockSpec((1, TILE, d), lambda bi, i: (bi, i, 0)),
        compiler_params=_cparams(("arbitrary", "arbitrary")),
        name="rwkv_readout",
    )(y, r, v, g, kd, h, mods, ln_w.reshape(1, d), ln_b.reshape(1, d), r_k.reshape(1, d), bd, w_o.astype(BF16))


def kernel(x, c, ctx, c_ctx, ada_w, ada_b, norm_mix_g, norm_ffn_g, router_w, router_b, exp_w_in, exp_b_in,
           exp_w_out, exp_b_out, ab_w_in, na_q_g, na_k_g, na_rpb, pool_w, pool_scale, ab_w_out,
           rw_mu, rw_w_r, rw_w_k, rw_w_v, rw_w_o, rw_w0, rw_w1, rw_w2, rw_a0, rw_a1, rw_a2,
           rw_g1, rw_g2, rw_k_k, rw_k_a, rw_r_k, rw_ln_w, rw_ln_b):
    b, seq, d = x.shape
    depth = ada_w.shape[0]
    assert ctx.shape[1] == CTX_LEN == TILE and seq % TILE == 0 and d == D_MODEL
    assert depth == 2, "layer schedule below is written for [neighbourhood/pool layer, RWKV layer]"
    rows = seq // GRID_W
    assert rows >= NA_QROWS * NA_KTILES and NA_QROWS * NA_KTILES >= NA_QROWS + NA_KH - 1

    n_c = 1 + b
    n_c_pad = -(-n_c // SUBLANES) * SUBLANES
    cvec = jnp.concatenate([c_ctx[None], c, jnp.zeros((n_c_pad - n_c, d), F32)], axis=0)

    def layer_mods(layer):
        m = ada_mod(cvec, ada_w[layer], ada_b[layer])
        m_ctx = jnp.broadcast_to(m[0].reshape(1, 6, d), (b, 6, d))
        return jnp.stack([m_ctx, m[1:n_c].reshape(b, 6, d)], axis=1)

    def moe(h, layer, mods, tile0, ctx_tiles):
        return moe_layer(h, norm_ffn_g[layer], mods, router_w[layer], router_b[layer], layer, exp_w_in,
                         exp_b_in, exp_w_out, exp_b_out, tile0, ctx_tiles)

    h = jnp.concatenate([ctx, x], axis=1)

    mods = layer_mods(0)
    qk, v, u = ab_proj(h, norm_mix_g[0], mods, ab_w_in[0], na_q_g[0], na_k_g[0])
    o_na = na_attention(qk, v, _na_bias_tables(na_rpb[0], rows))
    o_pool = multiscale_pool(u, pool_w[0], pool_scale[0])
    h = out_proj_residual(o_na, o_pool, h, mods, ab_w_out[0])
    h = moe(h, 0, mods, 0, 1)

    mods = layer_mods(1)
    r, v, kk, g, lw, kd, a = rwkv_features(h, norm_mix_g[1], mods, rw_mu[0], rw_w_r[0], rw_w_k[0], rw_w_v[0],
                                           rw_w0[0], rw_w1[0], rw_w2[0], rw_a0[0], rw_a1[0], rw_a2[0],
                                           rw_g1[0], rw_g2[0], rw_k_k[0], rw_k_a[0])
    y = rwkv_scan(r, v, kk, lw, kd, a)
    h_lat = rwkv_readout(y, r, v, g, kd, h, mods, rw_ln_w[0], rw_ln_b[0], rw_r_k[0], rw_w_o[0])
    return moe(h_lat, 1, mods, 0, 0)
```

```python
import functools

import jax
import jax.numpy as jnp
import numpy as np
from jax import lax
from jax.experimental import pallas as pl
from jax.experimental.pallas import tpu as pltpu

F32 = jnp.float32
BF16 = jnp.bfloat16

D_MODEL = 1024
GRID_W = 64
CTX_LEN = 256
RMS_EPS = 1e-6
NA_HEADS = 8
NA_HEAD_DIM = 64
NA_WIDTH = NA_HEADS * NA_HEAD_DIM
NA_KH = 8
NA_KW = 16
POOL_WINDOWS = (2, 4, 8, 16)
POOL_GC = 128
POOL_WIDTH = 512
RW_HEAD_DIM = 64
RW_HEADS = D_MODEL // RW_HEAD_DIM
RW_GN_EPS = 64e-5
N_EXPERTS = 32
TOP_K = 4
D_FF = D_MODEL
SWIGLU_LIMIT = 7.0
SWIGLU_ALPHA = 1.702

TILE = 256
LANES = 128
SUBLANES = 8
MOE_TM = 512
NEG_BIG = -1e30
VMEM_LIMIT = 48 * 1024 * 1024
MOE_FFN_VMEM_LIMIT = 56 * 1024 * 1024


def _cparams(sem):
    return pltpu.CompilerParams(dimension_semantics=sem, vmem_limit_bytes=VMEM_LIMIT)


def _mod_row(mods_ref, k):
    return mods_ref[0, 0, pl.ds(k, 1), :]


def _norm_mod(x, gain, shift, scale):
    ms = jnp.mean(x * x, axis=-1, keepdims=True)
    return (x * lax.rsqrt(ms + RMS_EPS) * gain) * (1.0 + scale) + shift


def _ada_kernel(c_ref, w_ref, b_ref, o_ref):
    c = c_ref[...]
    s = c * jax.nn.sigmoid(c)
    o_ref[...] = jnp.dot(s, w_ref[...], preferred_element_type=F32,
                         precision=lax.Precision.HIGHEST) + b_ref[...]


def ada_mod(cvec, w, b):
    r, d = cvec.shape
    n = w.shape[1]
    tn = 1536
    return pl.pallas_call(
        _ada_kernel,
        out_shape=jax.ShapeDtypeStruct((r, n), F32),
        grid=(n // tn,),
        in_specs=[pl.BlockSpec((r, d), lambda j: (0, 0)),
                  pl.BlockSpec((d, tn), lambda j: (0, j)),
                  pl.BlockSpec((1, tn), lambda j: (0, j))],
        out_specs=pl.BlockSpec((r, tn), lambda j: (0, j)),
        compiler_params=_cparams(("arbitrary",)),
        name="ada_mod",
    )(cvec, w, b.reshape(1, n))


def _ab_proj_kernel(h_ref, g_ref, mods_ref, w_ref, bd_ref, qkg_ref, qk_ref, v_ref, u_ref):
    x = _norm_mod(h_ref[0], g_ref[...], _mod_row(mods_ref, 0), _mod_row(mods_ref, 1))
    p = jnp.dot(x.astype(BF16), w_ref[...], preferred_element_type=F32)
    qk = p[:, :2 * NA_WIDTH]
    ss = jnp.dot((qk * qk).astype(BF16), bd_ref[...], preferred_element_type=F32)
    qk_ref[0] = (qk * lax.rsqrt(ss + RMS_EPS) * qkg_ref[...]).astype(BF16)
    v_ref[0] = p[:, 2 * NA_WIDTH:3 * NA_WIDTH].astype(BF16)
    u_ref[0] = p[:, 3 * NA_WIDTH:]


def ab_proj(h, gain, mods, w_in, q_g, k_g):
    b, lt, d = h.shape
    nt = lt // TILE
    n_in = w_in.shape[1]
    hid = jnp.arange(2 * NA_WIDTH) // NA_HEAD_DIM
    bd = jnp.where(hid[:, None] == hid[None, :], 1.0 / NA_HEAD_DIM, 0.0).astype(BF16)
    qkg = jnp.concatenate([jnp.tile(q_g, NA_HEADS), jnp.tile(k_g, NA_HEADS)]).reshape(1, -1).astype(F32)
    return pl.pallas_call(
        _ab_proj_kernel,
        out_shape=(jax.ShapeDtypeStruct((b, lt, 2 * NA_WIDTH), BF16),
                   jax.ShapeDtypeStruct((b, lt, NA_WIDTH), BF16),
                   jax.ShapeDtypeStruct((b, lt, POOL_WIDTH), F32)),
        grid=(b, nt),
        in_specs=[pl.BlockSpec((1, TILE, d), lambda bi, i: (bi, i, 0)),
                  pl.BlockSpec((1, d), lambda bi, i: (0, 0)),
                  pl.BlockSpec((1, 1, 6, d), lambda bi, i: (bi, jnp.minimum(i, 1), 0, 0)),
                  pl.BlockSpec((d, n_in), lambda bi, i: (0, 0)),
                  pl.BlockSpec((2 * NA_WIDTH, 2 * NA_WIDTH), lambda bi, i: (0, 0)),
                  pl.BlockSpec((1, 2 * NA_WIDTH), lambda bi, i: (0, 0))],
        out_specs=(pl.BlockSpec((1, TILE, 2 * NA_WIDTH), lambda bi, i: (bi, i, 0)),
                   pl.BlockSpec((1, TILE, NA_WIDTH), lambda bi, i: (bi, i, 0)),
                   pl.BlockSpec((1, TILE, POOL_WIDTH), lambda bi, i: (bi, i, 0))),
        compiler_params=_cparams(("arbitrary", "arbitrary")),
        name="ab_proj",
    )(h, gain.reshape(1, d), mods, w_in.astype(BF16), bd, qkg)


NA_QROWS = TILE // GRID_W
NA_KTILES = 3
NA_NLOC = NA_KTILES * TILE
NA_NKEY = NA_NLOC + CTX_LEN


def _na_bias_tables(rpb, rows):
    n_r, n_c = 2 * NA_KH - 1, 2 * NA_KW - 1
    col = np.arange(GRID_W)
    c_start = np.clip(col - NA_KW // 2, 0, GRID_W - NA_KW)
    valid_c = (col[None, :] >= c_start[:, None]) & (col[None, :] < c_start[:, None] + NA_KW)
    c_rel = np.clip(col[None, :] - col[:, None] + NA_KW - 1, 0, n_c - 1)
    sel_c = (c_rel.reshape(-1)[None, :] == np.arange(n_c)[:, None]).astype(np.float32)
    by_col = jnp.einsum("hrc,cn->hrn", rpb.astype(F32), sel_c, precision=lax.Precision.HIGHEST)
    n_qt = rows // NA_QROWS
    qr, kr = np.arange(NA_QROWS), np.arange(NA_KTILES * NA_QROWS)
    tabs = []
    for jq in (0, 1, n_qt - 1):
        w0 = NA_QROWS * min(max(jq - 1, 0), n_qt - NA_KTILES)
        q_row = NA_QROWS * jq + qr
        key_row = w0 + kr
        r_start = np.clip(q_row - NA_KH // 2, 0, rows - NA_KH)
        valid_r = (key_row[None, :] >= r_start[:, None]) & (key_row[None, :] < r_start[:, None] + NA_KH)
        r_rel = np.clip(key_row[None, :] - q_row[:, None] + NA_KH - 1, 0, n_r - 1)
        sel_r = (r_rel.reshape(-1)[:, None] == np.arange(n_r)[None, :]).astype(np.float32)
        t = jnp.einsum("xr,hrn->hxn", sel_r, by_col, precision=lax.Precision.HIGHEST)
        t = t.reshape(NA_HEADS, NA_QROWS, NA_KTILES * NA_QROWS, GRID_W, GRID_W)
        t = t.transpose(0, 1, 3, 2, 4).reshape(NA_HEADS, TILE, NA_NLOC)
        valid = (valid_r[:, None, :, None] & valid_c[None, :, None, :]).reshape(TILE, NA_NLOC)
        tabs.append(jnp.where(valid[None], t, NEG_BIG))
    tabs = [jnp.full_like(tabs[0], NEG_BIG)] + tabs
    loc = jnp.stack(tabs)
    return jnp.concatenate([loc, jnp.zeros(loc.shape[:3] + (CTX_LEN,), F32)], axis=-1)


def _na_kernel(q_ref, k0_ref, k1_ref, k2_ref, kc_ref, v0_ref, v1_ref, v2_ref, vc_ref, bias_ref, o_ref):
    q = q_ref[0]
    kall = jnp.concatenate([k0_ref[0], k1_ref[0], k2_ref[0], kc_ref[0]], axis=0)
    vall = jnp.concatenate([v0_ref[0], v1_ref[0], v2_ref[0], vc_ref[0]], axis=0)
    lane = lax.broadcasted_iota(jnp.int32, q.shape, 1)
    scale = NA_HEAD_DIM ** -0.5
    outs = []
    for hh in range(2):
        in_head = (lane >= hh * NA_HEAD_DIM) & (lane < (hh + 1) * NA_HEAD_DIM)
        qh = jnp.where(in_head, q, jnp.zeros_like(q)) * jnp.asarray(scale, BF16)
        s = lax.dot_general(qh, kall, (((1,), (1,)), ((), ())), preferred_element_type=F32)
        s = s + bias_ref[0, hh]
        m = jnp.max(s, axis=-1, keepdims=True)
        p = jnp.exp(s - m)
        l = jnp.sum(p, axis=-1, keepdims=True)
        o = jnp.dot(p.astype(BF16), vall, preferred_element_type=F32)
        outs.append(o / l)
    o_ref[0] = jnp.where(lane < NA_HEAD_DIM, outs[0], outs[1]).astype(BF16)


def na_attention(qk, v, bias):
    b, lt, _ = qk.shape
    nt = lt // TILE
    n_qt = nt - 1
    assert NA_HEAD_DIM ** -0.5 == 0.125
    hp_n = NA_HEADS // 2
    kblk = NA_WIDTH // LANES

    def w0(j):
        return 1 + jnp.clip(j - 2, 0, n_qt - NA_KTILES)

    def kind(j):
        return jnp.where(j == 0, 0, jnp.where(j == 1, 1, jnp.where(j == nt - 1, 3, 2)))

    qspec = pl.BlockSpec((1, TILE, LANES), lambda j, hp, bi: (bi, j, hp))
    kspecs = [pl.BlockSpec((1, TILE, LANES), functools.partial(lambda j, hp, bi, t: (bi, w0(j) + t, kblk + hp), t=t))
              for t in range(NA_KTILES)]
    kcspec = pl.BlockSpec((1, TILE, LANES), lambda j, hp, bi: (bi, 0, kblk + hp))
    vspecs = [pl.BlockSpec((1, TILE, LANES), functools.partial(lambda j, hp, bi, t: (bi, w0(j) + t, hp), t=t))
              for t in range(NA_KTILES)]
    vcspec = pl.BlockSpec((1, TILE, LANES), lambda j, hp, bi: (bi, 0, hp))
    bspec = pl.BlockSpec((1, 2, TILE, NA_NKEY), lambda j, hp, bi: (kind(j), hp, 0, 0))
    return pl.pallas_call(
        _na_kernel,
        out_shape=jax.ShapeDtypeStruct((b, lt, NA_WIDTH), BF16),
        grid=(nt, hp_n, b),
        in_specs=[qspec] + kspecs + [kcspec] + vspecs + [vcspec, bspec],
        out_specs=pl.BlockSpec((1, TILE, LANES), lambda j, hp, bi: (bi, j, hp)),
        compiler_params=_cparams(("arbitrary", "arbitrary", "arbitrary")),
        name="na_attention",
    )(qk, qk, qk, qk, qk, v, v, v, v, bias)


POOL_HALO = 8
POOL_ROWS = TILE + 2 * POOL_HALO


def _pool_kernel(cur_ref, prev_ref, next_ref, pw_ref, ps_ref, o_ref, xp_ref, *, n_tiles):
    i = pl.program_id(1)
    has_prev = (i >= 2).astype(F32)
    has_next = ((i >= 1) & (i < n_tiles - 1)).astype(F32)
    xp_ref[0:POOL_HALO, :] = prev_ref[0] * has_prev
    xp_ref[POOL_HALO:POOL_HALO + TILE, :] = cur_ref[0]
    xp_ref[POOL_HALO + TILE:POOL_ROWS, :] = next_ref[0] * has_next
    row = lax.broadcasted_iota(jnp.int32, (TILE, POOL_GC), 0)
    tpos = jnp.where(i == 0, row, row + (i - 1) * TILE)
    seq_len = jnp.where(i == 0, CTX_LEN, (n_tiles - 1) * TILE)
    for g, w in enumerate(POOL_WINDOWS):
        cols = slice(g * POOL_GC, (g + 1) * POOL_GC)
        x = xp_ref[:, cols]
        s = x + pltpu.roll(x, 1, 0)
        half = 1
        while 2 * half < w:
            s = pltpu.roll(s, half, 0) + pltpu.roll(s, POOL_ROWS - half, 0)
            half *= 2
        s = s[POOL_HALO:POOL_HALO + TILE]
        cnt = jnp.minimum(w // 2, tpos) + jnp.minimum(w // 2, seq_len - tpos)
        pooled = s / cnt.astype(F32) - cur_ref[0, :, cols]
        y = jnp.dot(pooled.astype(BF16), pw_ref[g], preferred_element_type=F32)
        o_ref[0, :, cols] = (y * ps_ref[:, cols]).astype(BF16)


def multiscale_pool(u, pool_w, pool_scale):
    b, lt, c = u.shape
    nt = lt // TILE
    hb = TILE // POOL_HALO
    nhb = lt // POOL_HALO
    assert POOL_HALO >= max(POOL_WINDOWS) // 2 and POOL_HALO == SUBLANES
    return pl.pallas_call(
        functools.partial(_pool_kernel, n_tiles=nt),
        out_shape=jax.ShapeDtypeStruct((b, lt, c), BF16),
        grid=(b, nt),
        in_specs=[pl.BlockSpec((1, TILE, c), lambda bi, i: (bi, i, 0)),
                  pl.BlockSpec((1, POOL_HALO, c), lambda bi, i: (bi, jnp.maximum(i * hb - 1, 0), 0)),
                  pl.BlockSpec((1, POOL_HALO, c), lambda bi, i: (bi, jnp.minimum((i + 1) * hb, nhb - 1), 0)),
                  pl.BlockSpec((len(POOL_WINDOWS), POOL_GC, POOL_GC), lambda bi, i: (0, 0, 0)),
                  pl.BlockSpec((1, c), lambda bi, i: (0, 0))],
        out_specs=pl.BlockSpec((1, TILE, c), lambda bi, i: (bi, i, 0)),
        scratch_shapes=[pltpu.VMEM((POOL_ROWS, c), F32)],
        compiler_params=_cparams(("arbitrary", "arbitrary")),
        name="multiscale_pool",
    )(u, u, u, pool_w.astype(BF16), pool_scale.reshape(1, c).astype(F32))


def _out_proj_kernel(a_ref, b_ref, h_ref, mods_ref, w_ref, o_ref):
    ka = a_ref.shape[-1]
    y = jnp.dot(a_ref[0], w_ref[:ka, :], preferred_element_type=F32)
    y = y + jnp.dot(b_ref[0], w_ref[ka:, :], preferred_element_type=F32)
    o_ref[0] = h_ref[0] + _mod_row(mods_ref, 2) * y


def out_proj_residual(a, bb, h, mods, w):
    b, lt, d = h.shape
    nt = lt // TILE
    ka, kb = a.shape[-1], bb.shape[-1]
    return pl.pallas_call(
        _out_proj_kernel,
        out_shape=jax.ShapeDtypeStruct((b, lt, d), F32),
        grid=(b, nt),
        in_specs=[pl.BlockSpec((1, TILE, ka), lambda bi, i: (bi, i, 0)),
                  pl.BlockSpec((1, TILE, kb), lambda bi, i: (bi, i, 0)),
                  pl.BlockSpec((1, TILE, d), lambda bi, i: (bi, i, 0)),
                  pl.BlockSpec((1, 1, 6, d), lambda bi, i: (bi, jnp.minimum(i, 1), 0, 0)),
                  pl.BlockSpec((ka + kb, d), lambda bi, i: (0, 0))],
        out_specs=pl.BlockSpec((1, TILE, d), lambda bi, i: (bi, i, 0)),
        compiler_params=_cparams(("arbitrary", "arbitrary")),
        name="out_proj_residual",
    )(a, bb, h, mods, w.astype(BF16))


def _moe_pre_kernel(h_ref, g_ref, mods_ref, rw_ref, rb_ref, y_ref, lg_ref):
    x = _norm_mod(h_ref[0], g_ref[...], _mod_row(mods_ref, 3), _mod_row(mods_ref, 4))
    y_ref[0] = x.astype(BF16)
    lg_ref[0] = jnp.dot(x, rw_ref[...], preferred_element_type=F32,
                        precision=lax.Precision.HIGHEST) + rb_ref[...]


def moe_pre(h, gain, mods, router_w, router_b, tile0, ctx_tiles):
    b, lt, d = h.shape
    nt = lt // TILE - tile0
    e = router_w.shape[1]
    return pl.pallas_call(
        _moe_pre_kernel,
        out_shape=(jax.ShapeDtypeStruct((b, nt * TILE, d), BF16),
                   jax.ShapeDtypeStruct((b, nt * TILE, e), F32)),
        grid=(b, nt),
        in_specs=[pl.BlockSpec((1, TILE, d), lambda bi, i: (bi, i + tile0, 0)),
                  pl.BlockSpec((1, d), lambda bi, i: (0, 0)),
                  pl.BlockSpec((1, 1, 6, d), lambda bi, i: (bi, jnp.minimum(i + tile0 + 1 - ctx_tiles, 1), 0, 0)),
                  pl.BlockSpec((d, e), lambda bi, i: (0, 0)),
                  pl.BlockSpec((1, e), lambda bi, i: (0, 0))],
        out_specs=(pl.BlockSpec((1, TILE, d), lambda bi, i: (bi, i, 0)),
                   pl.BlockSpec((1, TILE, e), lambda bi, i: (bi, i, 0))),
        compiler_params=_cparams(("arbitrary", "arbitrary")),
        name="moe_pre",
    )(h, gain.reshape(1, d), mods, router_w, router_b.reshape(1, e))


MOE_PERM_W = 256


def _moe_ffn_kernel(te_ref, tv_ref, x_ref, win_ref, bg_ref, bl_ref, wo_ref, bo_ref, perm_ref, o_ref,
                    wg_s, wl_s, wo_s):
    t = pl.program_id(0)
    valid = tv_ref[t] > 0
    new_expert = (t == 0) | (te_ref[t] != te_ref[jnp.maximum(t - 1, 0)])

    @pl.when(valid & new_expert)
    def _():
        half = MOE_PERM_W // 2
        for cb in range(win_ref.shape[3] // MOE_PERM_W):
            w = win_ref[0, 0, :, cb * MOE_PERM_W:(cb + 1) * MOE_PERM_W].astype(BF16)
            wp = jnp.dot(w, perm_ref[...], preferred_element_type=F32).astype(BF16)
            wg_s[:, cb * half:(cb + 1) * half] = wp[:, :half]
            wl_s[:, cb * half:(cb + 1) * half] = wp[:, half:]
        wo_s[...] = wo_ref[0, 0].astype(BF16)

    @pl.when(valid)
    def _():
        x = x_ref[...]
        hg = jnp.dot(x, wg_s[...], preferred_element_type=F32) + bg_ref[0]
        hl = jnp.dot(x, wl_s[...], preferred_element_type=F32) + bl_ref[0]
        glu = jnp.minimum(hg, SWIGLU_LIMIT)
        lin = jnp.clip(hl, -SWIGLU_LIMIT, SWIGLU_LIMIT)
        act = glu * jax.nn.sigmoid(SWIGLU_ALPHA * glu) * (lin + 1.0)
        o_ref[...] = (jnp.dot(act.astype(BF16), wo_s[...], preferred_element_type=F32)
                      + bo_ref[0]).astype(o_ref.dtype)


def moe_grouped_ffn(x_sorted, tile_expert, tile_valid, layer, w_in, bg, bl, w_out, bo):
    p, d = x_sorted.shape
    _, e, _, f2 = w_in.shape
    f = f2 // 2
    nt = p // MOE_TM
    assert f2 % MOE_PERM_W == 0 and MOE_PERM_W % (2 * LANES) == 0
    src = jnp.arange(MOE_PERM_W)
    dst = jnp.where(src % 2 == 0, src // 2, MOE_PERM_W // 2 + src // 2)
    perm = (dst[:, None] == jnp.arange(MOE_PERM_W)[None, :]).astype(BF16)
    grid_spec = pltpu.PrefetchScalarGridSpec(
        num_scalar_prefetch=2,
        grid=(nt,),
        in_specs=[pl.BlockSpec((MOE_TM, d), lambda t, te, tv: (t, 0)),
                  pl.BlockSpec((1, 1, d, f2), lambda t, te, tv: (layer, te[t], 0, 0)),
                  pl.BlockSpec((1, 1, f), lambda t, te, tv: (te[t], 0, 0)),
                  pl.BlockSpec((1, 1, f), lambda t, te, tv: (te[t], 0, 0)),
                  pl.BlockSpec((1, 1, f, d), lambda t, te, tv: (layer, te[t], 0, 0)),
                  pl.BlockSpec((1, 1, d), lambda t, te, tv: (te[t], 0, 0)),
                  pl.BlockSpec((MOE_PERM_W, MOE_PERM_W), lambda t, te, tv: (0, 0))],
        out_specs=pl.BlockSpec((MOE_TM, d), lambda t, te, tv: (t, 0)),
        scratch_shapes=[pltpu.VMEM((d, f), BF16), pltpu.VMEM((d, f), BF16), pltpu.VMEM((f, d), BF16)],
    )
    return pl.pallas_call(
        _moe_ffn_kernel,
        out_shape=jax.ShapeDtypeStruct((p, d), BF16),
        grid_spec=grid_spec,
        compiler_params=pltpu.CompilerParams(dimension_semantics=("arbitrary",),
                                             vmem_limit_bytes=MOE_FFN_VMEM_LIMIT),
        name="moe_grouped_ffn",
    )(tile_expert, tile_valid, x_sorted, w_in, bg, bl, w_out, bo, perm)


def _moe_combine_kernel(z_ref, gt_ref, h_ref, mods_ref, o_ref):
    gt = gt_ref[0]
    acc = gt[:, 0:1] * z_ref[0, 0].astype(F32)
    for k in range(1, TOP_K):
        acc = acc + gt[:, k:k + 1] * z_ref[k, 0].astype(F32)
    o_ref[0] = h_ref[0] + _mod_row(mods_ref, 5) * acc


def moe_combine(z4, gates, h, mods, tile0, ctx_tiles):
    k, b, lp, d = z4.shape
    nt = lp // TILE
    return pl.pallas_call(
        _moe_combine_kernel,
        out_shape=jax.ShapeDtypeStruct((b, lp, d), F32),
        grid=(b, nt),
        in_specs=[pl.BlockSpec((k, 1, TILE, d), lambda bi, i: (0, bi, i, 0)),
                  pl.BlockSpec((1, TILE, k), lambda bi, i: (bi, i, 0)),
                  pl.BlockSpec((1, TILE, d), lambda bi, i: (bi, i + tile0, 0)),
                  pl.BlockSpec((1, 1, 6, d), lambda bi, i: (bi, jnp.minimum(i + tile0 + 1 - ctx_tiles, 1), 0, 0))],
        out_specs=pl.BlockSpec((1, TILE, d), lambda bi, i: (bi, i, 0)),
        compiler_params=_cparams(("arbitrary", "arbitrary")),
        name="moe_combine",
    )(z4, gates, h, mods)


def moe_layer(h, gain, mods, router_w, router_b, layer, w_in, b_in, w_out, b_out, tile0, ctx_tiles):
    b, lt, d = h.shape
    y, logits = moe_pre(h, gain, mods, router_w, router_b, tile0, ctx_tiles)
    lp = y.shape[1]
    n = b * lp
    y2 = y.reshape(n, d)
    top_val, top_idx = lax.top_k(logits.reshape(n, N_EXPERTS), TOP_K)
    gates = jax.nn.softmax(top_val, axis=-1)

    flat_e = top_idx.reshape(-1)
    onehot = (flat_e[:, None] == jnp.arange(N_EXPERTS)[None, :]).astype(jnp.int32)
    rank = jnp.take_along_axis(jnp.cumsum(onehot, axis=0), flat_e[:, None], axis=1)[:, 0] - 1
    counts = jnp.sum(onehot, axis=0)
    padded = ((counts + MOE_TM - 1) // MOE_TM) * MOE_TM
    ends = jnp.cumsum(padded)
    starts = ends - padded
    slot = starts[flat_e] + rank
    n_tiles = (n * TOP_K) // MOE_TM + N_EXPERTS
    p_rows = n_tiles * MOE_TM
    src_tok = (jnp.arange(p_rows, dtype=jnp.int32) % n).at[slot].set(
        jnp.arange(n * TOP_K, dtype=jnp.int32) // TOP_K)
    tile_start = jnp.arange(n_tiles, dtype=jnp.int32) * MOE_TM
    tile_expert = jnp.minimum(jnp.sum((tile_start[:, None] >= ends[None, :]).astype(jnp.int32), axis=1),
                              N_EXPERTS - 1)
    tile_valid = (tile_start < ends[-1]).astype(jnp.int32)

    x_sorted = y2.at[src_tok].get(mode="promise_in_bounds")
    bg = b_in[layer, :, 0::2].reshape(N_EXPERTS, 1, D_FF)
    bl = b_in[layer, :, 1::2].reshape(N_EXPERTS, 1, D_FF)
    z = moe_grouped_ffn(x_sorted, tile_expert, tile_valid, layer, w_in, bg, bl, w_out,
                        b_out[layer].reshape(N_EXPERTS, 1, d))
    z4 = z.at[slot.reshape(n, TOP_K).T].get(mode="promise_in_bounds").reshape(TOP_K, b, lp, d)
    return moe_combine(z4, gates.reshape(b, lp, TOP_K), h, mods, tile0, ctx_tiles)


def _softplus(z):
    return jnp.maximum(z, 0.0) + jnp.log(1.0 + jnp.exp(-jnp.abs(z)))


def _rwkv_feat_kernel(h_ref, hp_ref, hn_ref, g_ref, mods_ref, mu_ref, wr_ref, wk_ref, wv_ref,
                      w1_ref, w2_ref, a1_ref, a2_ref, g1_ref, g2_ref, w0_ref, a0_ref, kk_ref, ka_ref,
                      bd_ref, r_out, v_out, kk_out, g_out, lw_out, kd_out, a_out, *, n_tiles):
    i = pl.program_id(1)
    gain, shift, scale = g_ref[...], _mod_row(mods_ref, 0), _mod_row(mods_ref, 1)
    x = _norm_mod(h_ref[0], gain, shift, scale)
    has_prev = (i >= 2).astype(F32)
    has_next = ((i >= 1) & (i < n_tiles - 1)).astype(F32)
    x_before = _norm_mod(hp_ref[0], gain, shift, scale)[SUBLANES - 1:SUBLANES] * has_prev
    x_after = _norm_mod(hn_ref[0], gain, shift, scale)[0:1] * has_next
    row = lax.broadcasted_iota(jnp.int32, x.shape, 0)
    prev = jnp.where(row == 0, x_before, pltpu.roll(x, 1, 0))
    nxt = jnp.where(row == TILE - 1, x_after, pltpu.roll(x, TILE - 1, 0))
    xx = 0.5 * (prev + nxt) - x
    xr, xw, xk, xv, xa, xg = (x + xx * mu_ref[pl.ds(j, 1), :] for j in range(6))

    def mm(a, w):
        return jnp.dot(a.astype(BF16), w, preferred_element_type=F32)

    r = mm(xr, wr_ref[...])
    k = mm(xk, wk_ref[...])
    v = mm(xv, wv_ref[...])
    g = mm(jax.nn.sigmoid(mm(xg, g1_ref[...])), g2_ref[...])
    kk = k * kk_ref[...]
    kk = kk * lax.rsqrt(mm(kk * kk, bd_ref[...]) + 1e-12)
    r_out[0] = r.astype(BF16)
    v_out[0] = v.astype(BF16)
    kk_out[0] = kk.astype(BF16)
    g_out[0] = g.astype(BF16)
    tw = jnp.tanh(mm(xw, w1_ref[...]))
    av = mm(xa, a1_ref[...])
    lane = lax.broadcasted_iota(jnp.int32, tw.shape, 1)
    half = tw.shape[1] // 2
    for d in range(2):
        in_dir = (lane >= d * half) & (lane < (d + 1) * half)
        wz = w0_ref[pl.ds(d, 1), :] + mm(jnp.where(in_dir, tw, 0.0), w2_ref[...])
        w_log = -_softplus(-wz) - 0.5
        lw_out[d, 0] = -jnp.exp(w_log)
        a = jax.nn.sigmoid(a0_ref[pl.ds(d, 1), :] + mm(jnp.where(in_dir, av, 0.0), a2_ref[...]))
        kd_out[d, 0] = (k * (1.0 + (a - 1.0) * ka_ref[...])).astype(BF16)
        a_out[d, 0] = a.astype(BF16)


def rwkv_features(h, gain, mods, mu, w_r, w_k, w_v, w0, w1, w2, a0, a1, a2, g1, g2, k_k, k_a):
    b, lt, d = h.shape
    nt = lt // TILE
    hb = TILE // SUBLANES
    nhb = lt // SUBLANES
    hid = jnp.arange(d) // RW_HEAD_DIM
    bd = (hid[:, None] == hid[None, :]).astype(BF16)
    w1c = jnp.concatenate([w1[0], w1[1]], axis=1).astype(BF16)
    a1c = jnp.concatenate([a1[0], a1[1]], axis=1).astype(BF16)
    w2c = jnp.concatenate([w2[0], w2[1]], axis=0).astype(BF16)
    a2c = jnp.concatenate([a2[0], a2[1]], axis=0).astype(BF16)
    full = lambda shape: pl.BlockSpec(shape, lambda bi, i: (0,) * len(shape))
    tile_spec = pl.BlockSpec((1, TILE, d), lambda bi, i: (bi, i, 0))
    dir_spec = pl.BlockSpec((2, 1, TILE, d), lambda bi, i: (0, bi, i, 0))
    seq_bf = jax.ShapeDtypeStruct((b, lt, d), BF16)
    return pl.pallas_call(
        functools.partial(_rwkv_feat_kernel, n_tiles=nt),
        out_shape=(seq_bf, seq_bf, seq_bf, seq_bf,
                   jax.ShapeDtypeStruct((2, b, lt, d), F32),
                   jax.ShapeDtypeStruct((2, b, lt, d), BF16),
                   jax.ShapeDtypeStruct((2, b, lt, d), BF16)),
        grid=(b, nt),
        in_specs=[tile_spec,
                  pl.BlockSpec((1, SUBLANES, d), lambda bi, i: (bi, jnp.maximum(i * hb - 1, 0), 0)),
                  pl.BlockSpec((1, SUBLANES, d), lambda bi, i: (bi, jnp.minimum((i + 1) * hb, nhb - 1), 0)),
                  full((1, d)),
                  pl.BlockSpec((1, 1, 6, d), lambda bi, i: (bi, jnp.minimum(i, 1), 0, 0)),
                  full((6, d)), full((d, d)), full((d, d)), full((d, d)),
                  full(w1c.shape), full(w2c.shape), full(a1c.shape), full(a2c.shape),
                  full(g1.shape), full(g2.shape), full((2, d)), full((2, d)), full((1, d)), full((1, d)),
                  full((d, d))],
        out_specs=(tile_spec, tile_spec, tile_spec, tile_spec, dir_spec, dir_spec, dir_spec),
        compiler_params=_cparams(("arbitrary", "arbitrary")),
        name="rwkv_features",
    )(h, h, h, gain.reshape(1, d), mods, mu, w_r.astype(BF16), w_k.astype(BF16), w_v.astype(BF16),
      w1c, w2c, a1c, a2c, g1.astype(BF16), g2.astype(BF16), w0, a0, k_k.reshape(1, d), k_a.reshape(1, d), bd)


SCAN_C = 64
SCAN_G = 4
SCAN_GW = SCAN_G * RW_HEAD_DIM


def _scan_chunk_inputs(reverse, r_ref, v_ref, kk_ref, lw_ref, kd_ref, a_ref):
    cc = SCAN_C
    ti = lax.broadcasted_iota(jnp.int32, (cc, cc), 0)
    si = lax.broadcasted_iota(jnp.int32, (cc, cc), 1)
    before_incl = (si >= ti) if reverse else (si <= ti)
    lw = lw_ref[0, 0]
    cl = jnp.dot(before_incl.astype(F32), lw, preferred_element_type=F32, precision=lax.Precision.HIGHEST)
    g_in = jnp.exp(cl)
    g_ex = jnp.exp(cl - lw)
    g_inv = jnp.exp(-cl)
    g_end = jnp.exp(jnp.sum(lw, axis=0, keepdims=True))
    kk = kk_ref[0].astype(F32)
    rt = (r_ref[0].astype(F32) * g_in).astype(BF16)
    at = (-kk * g_ex).astype(BF16)
    bt = (kk * a_ref[0, 0].astype(F32) * g_inv).astype(BF16)
    kt = (kd_ref[0, 0].astype(F32) * g_inv).astype(BF16)
    return rt, at, bt, kt, v_ref[0], g_end


def _rwkv_scan_kernel(rf_ref, vf_ref, kkf_ref, rb_ref, vb_ref, kkb_ref, lwf_ref, kdf_ref, af_ref,
                      lwb_ref, kdb_ref, ab_ref, yf_ref, yb_ref, h_ref):
    @pl.when(pl.program_id(1) == 0)
    def _():
        h_ref[...] = jnp.zeros_like(h_ref)

    cc, gw = SCAN_C, SCAN_GW
    ops = (_scan_chunk_inputs(False, rf_ref, vf_ref, kkf_ref, lwf_ref, kdf_ref, af_ref),
           _scan_chunk_inputs(True, rb_ref, vb_ref, kkb_ref, lwb_ref, kdb_ref, ab_ref))
    y_refs = (yf_ref, yb_ref)

    tg = lax.broadcasted_iota(jnp.int32, (cc, gw), 0)
    sg = lax.broadcasted_iota(jnp.int32, (cc, gw), 1) % cc
    m_strict = (sg < tg, sg > tg)
    m_incl = (sg <= tg, sg >= tg)
    eye_g = (sg == tg).astype(F32)
    bi = lax.broadcasted_iota(jnp.int32, (gw, gw), 0) // cc
    bj = lax.broadcasted_iota(jnp.int32, (gw, gw), 1) // cc
    blk = bi == bj

    def bdiag(x):
        return jnp.where(blk, jnp.concatenate([x] * SCAN_G, axis=0), jnp.zeros((), x.dtype))

    def mm(a, b):
        return jnp.dot(a, b, preferred_element_type=F32)

    def mm_nt(a, b):
        return lax.dot_general(a, b, (((1,), (1,)), ((), ())), preferred_element_type=F32)

    n_g = RW_HEADS // SCAN_G
    chains = [(d, g) for d in range(2) for g in range(n_g)]
    cols = [slice(g * gw, (g + 1) * gw) for _, g in chains]
    rt, at, bt, kt, v, g_end = ([ops[d][j][:, cols[i]] for i, (d, _) in enumerate(chains)] for j in range(6))
    idx = range(len(chains))
    h0 = [h_ref[i] for i in idx]
    h0b = [x.astype(BF16) for x in h0]
    v_bd = [bdiag(x) for x in v]
    mats = [mm_nt(jnp.concatenate([at[i], rt[i]], axis=0),
                  jnp.concatenate([bdiag(bt[i]), bdiag(kt[i])], axis=0)) for i in idx]
    a_ab = [jnp.where(m_strict[chains[i][0]], mats[i][:cc, :gw], 0.0) for i in idx]
    a_ak = [jnp.where(m_strict[chains[i][0]], mats[i][:cc, gw:], 0.0).astype(BF16) for i in idx]
    m_rb = [jnp.where(m_incl[chains[i][0]], mats[i][cc:, :gw], 0.0).astype(BF16) for i in idx]
    m_rk = [jnp.where(m_incl[chains[i][0]], mats[i][cc:, gw:], 0.0).astype(BF16) for i in idx]
    pq = [mm(jnp.concatenate([at[i], a_ak[i]], axis=1), jnp.concatenate([h0b[i], v_bd[i]], axis=0)) for i in idx]
    pw = [x.astype(BF16) for x in a_ab]
    s_acc = [eye_g + x for x in a_ab]
    pw = [mm(x, bdiag(x)).astype(BF16) for x in pw]
    n = 2
    while n < cc:
        last = 2 * n >= cc
        lhs = [s_acc[i].astype(BF16) if last else jnp.concatenate([pw[i], s_acc[i].astype(BF16)], axis=0)
               for i in idx]
        prod = [mm(lhs[i], bdiag(pw[i])) for i in idx]
        s_acc = [s_acc[i] + (prod[i] if last else prod[i][cc:]) for i in idx]
        if not last:
            pw = [prod[i][:cc].astype(BF16) for i in idx]
        n *= 2
    ub = [mm(s_acc[i].astype(BF16), bdiag(pq[i].astype(BF16))).astype(BF16) for i in idx]
    for i in idx:
        y_refs[chains[i][0]][0, :, cols[i]] = mm(jnp.concatenate([rt[i], m_rb[i], m_rk[i]], axis=1),
                                                 jnp.concatenate([h0b[i], bdiag(ub[i]), v_bd[i]], axis=0))
    for i in idx:
        upd = lax.dot_general(jnp.concatenate([bt[i], kt[i]], axis=0), jnp.concatenate([ub[i], v[i]], axis=0),
                              (((0,), (0,)), ((), ())), preferred_element_type=F32)
        ge = jnp.transpose(jnp.broadcast_to(g_end[i], (gw, gw)))
        h_ref[i] = ge * (h0[i] + jnp.where(blk, upd, 0.0))


def rwkv_scan(r, v, kk, lw, kd, a):
    b, lt, d = r.shape
    nc = lt // SCAN_C
    n_ctx = CTX_LEN // SCAN_C
    assert SCAN_C == RW_HEAD_DIM and CTX_LEN % SCAN_C == 0

    def rev_chunk(c):
        return jnp.where(c < n_ctx, n_ctx - 1 - c, nc - 1 + n_ctx - c)

    fwd = pl.BlockSpec((1, SCAN_C, d), lambda bi, c: (bi, c, 0))
    bwd = pl.BlockSpec((1, SCAN_C, d), lambda bi, c: (bi, rev_chunk(c), 0))
    fwd_dir = pl.BlockSpec((1, 1, SCAN_C, d), lambda bi, c: (0, bi, c, 0))
    bwd_dir = pl.BlockSpec((1, 1, SCAN_C, d), lambda bi, c: (1, bi, rev_chunk(c), 0))
    y_shape = jax.ShapeDtypeStruct((b, lt, d), F32)
    return pl.pallas_call(
        _rwkv_scan_kernel,
        out_shape=(y_shape, y_shape),
        grid=(b, nc),
        in_specs=[fwd, fwd, fwd, bwd, bwd, bwd, fwd_dir, fwd_dir, fwd_dir, bwd_dir, bwd_dir, bwd_dir],
        out_specs=(fwd, bwd),
        scratch_shapes=[pltpu.VMEM((2 * RW_HEADS // SCAN_G, SCAN_GW, SCAN_GW), F32)],
        compiler_params=_cparams(("arbitrary", "arbitrary")),
        name="rwkv_scan",
    )(r, v, kk, r, v, kk, lw, kd, a, lw, kd, a)


def _rwkv_out_kernel(yf_ref, yb_ref, r_ref, v_ref, g_ref, kd_ref, h_ref, mods_ref, lnw_ref, lnb_ref, rk_ref,
                     bd_ref, wo_ref, o_ref):
    def head_sum(x):
        return jnp.dot(x.astype(BF16), bd_ref[...], preferred_element_type=F32)

    y = yf_ref[0] + yb_ref[0]
    inv_n = 1.0 / RW_HEAD_DIM
    mean = head_sum(y) * inv_n
    yc = y - mean
    var = head_sum(yc * yc) * inv_n
    yn = yc * lax.rsqrt(var + RW_GN_EPS) * lnw_ref[...] + lnb_ref[...]
    r = r_ref[0].astype(F32)
    kd = kd_ref[0, 0].astype(F32) + kd_ref[1, 0].astype(F32)
    yn = yn + head_sum(r * kd * rk_ref[...]) * v_ref[0].astype(F32)
    out = jnp.dot((yn * g_ref[0].astype(F32)).astype(BF16), wo_ref[...], preferred_element_type=F32)
    o_ref[0] = h_ref[0] + _mod_row(mods_ref, 2) * out


def rwkv_readout(y_f, y_b, r, v, g, kd, h, mods, ln_w, ln_b, r_k, w_o):
    b, lt, d = h.shape
    nt = lt // TILE - 1
    hid = jnp.arange(d) // RW_HEAD_DIM
    bd = (hid[:, None] == hid[None, :]).astype(BF16)
    full = lambda shape: pl.BlockSpec(shape, lambda bi, i: (0,) * len(shape))
    tile_spec = pl.BlockSpec((1, TILE, d), lambda bi, i: (bi, i + 1, 0))
    dir_spec = pl.BlockSpec((2, 1, TILE, d), lambda bi, i: (0, bi, i + 1, 0))
    return pl.pallas_call(
        _rwkv_out_kernel,
        out_shape=jax.ShapeDtypeStruct((b, nt * TILE, d), F32),
        grid=(b, nt),
        in_specs=[tile_spec, tile_spec, tile_spec, tile_spec, tile_spec, dir_spec, tile_spec,
                  pl.BlockSpec((1, 1, 6, d), lambda bi, i: (bi, 1, 0, 0)),
                  full((1, d)), full((1, d)), full((1, d)), full((d, d)), full((d, d))],
        out_specs=pl.BlockSpec((1, TILE, d), lambda bi, i: (bi, i, 0)),
        compiler_params=_cparams(("arbitrary", "arbitrary")),
        name="rwkv_readout",
    )(y_f, y_b, r, v, g, kd, h, mods, ln_w.reshape(1, d), ln_b.reshape(1, d), r_k.reshape(1, d), bd,
      w_o.astype(BF16))


def kernel(x, c, ctx, c_ctx, ada_w, ada_b, norm_mix_g, norm_ffn_g, router_w, router_b, exp_w_in, exp_b_in,
           exp_w_out, exp_b_out, ab_w_in, na_q_g, na_k_g, na_rpb, pool_w, pool_scale, ab_w_out,
           rw_mu, rw_w_r, rw_w_k, rw_w_v, rw_w_o, rw_w0, rw_w1, rw_w2, rw_a0, rw_a1, rw_a2,
           rw_g1, rw_g2, rw_k_k, rw_k_a, rw_r_k, rw_ln_w, rw_ln_b):
    b, seq, d = x.shape
    depth = ada_w.shape[0]
    assert ctx.shape[1] == CTX_LEN == TILE and seq % TILE == 0 and d == D_MODEL
    assert depth == 2, "layer schedule below is written for [neighbourhood/pool layer, RWKV layer]"
    rows = seq // GRID_W
    assert rows >= NA_QROWS * NA_KTILES and NA_QROWS * NA_KTILES >= NA_QROWS + NA_KH - 1

    n_c = 1 + b
    n_c_pad = -(-n_c // SUBLANES) * SUBLANES
    cvec = jnp.concatenate([c_ctx[None], c, jnp.zeros((n_c_pad - n_c, d), F32)], axis=0)

    def layer_mods(layer):
        m = ada_mod(cvec, ada_w[layer], ada_b[layer])
        m_ctx = jnp.broadcast_to(m[0].reshape(1, 6, d), (b, 6, d))
        return jnp.stack([m_ctx, m[1:n_c].reshape(b, 6, d)], axis=1)

    def moe(h, layer, mods, tile0, ctx_tiles):
        return moe_layer(h, norm_ffn_g[layer], mods, router_w[layer], router_b[layer], layer, exp_w_in,
                         exp_b_in, exp_w_out, exp_b_out, tile0, ctx_tiles)

    h = jnp.concatenate([ctx, x], axis=1)

    mods = layer_mods(0)
    qk, v, u = ab_proj(h, norm_mix_g[0], mods, ab_w_in[0], na_q_g[0], na_k_g[0])
    o_na = na_attention(qk, v, _na_bias_tables(na_rpb[0], rows))
    o_pool = multiscale_pool(u, pool_w[0], pool_scale[0])
    h = out_proj_residual(o_na, o_pool, h, mods, ab_w_out[0])
    h = moe(h, 0, mods, 0, 1)

    mods = layer_mods(1)
    r, v, kk, g, lw, kd, a = rwkv_features(h, norm_mix_g[1], mods, rw_mu[0], rw_w_r[0], rw_w_k[0], rw_w_v[0],
                                           rw_w0[0], rw_w1[0], rw_w2[0], rw_a0[0], rw_a1[0], rw_a2[0],
                                           rw_g1[0], rw_g2[0], rw_k_k[0], rw_k_a[0])
    y_f, y_b = rwkv_scan(r, v, kk, lw, kd, a)
    h_lat = rwkv_readout(y_f, y_b, r, v, g, kd, h, mods, rw_ln_w[0], rw_ln_b[0], rw_r_k[0], rw_w_o[0])
    return moe(h_lat, 1, mods, 0, 0)
```

```python
import functools

import jax
import jax.numpy as jnp
import numpy as np
from jax import lax
from jax.experimental import pallas as pl
from jax.experimental.pallas import tpu as pltpu

F32 = jnp.float32
BF16 = jnp.bfloat16

D_MODEL = 1024
GRID_W = 64
CTX_LEN = 256
RMS_EPS = 1e-6
NA_HEADS = 8
NA_HEAD_DIM = 64
NA_WIDTH = NA_HEADS * NA_HEAD_DIM
NA_KH = 8
NA_KW = 16
POOL_WINDOWS = (2, 4, 8, 16)
POOL_GC = 128
POOL_WIDTH = 512
RW_HEAD_DIM = 64
RW_HEADS = D_MODEL // RW_HEAD_DIM
RW_GN_EPS = 64e-5
N_EXPERTS = 32
TOP_K = 4
D_FF = D_MODEL
SWIGLU_LIMIT = 7.0
SWIGLU_ALPHA = 1.702

TILE = 256
LANES = 128
SUBLANES = 8
MOE_TM = 512
NEG_BIG = -1e30
VMEM_LIMIT = 48 * 1024 * 1024
MOE_FFN_VMEM_LIMIT = 56 * 1024 * 1024


def _cparams(sem):
    return pltpu.CompilerParams(dimension_semantics=sem, vmem_limit_bytes=VMEM_LIMIT)


def _mod_row(mods_ref, k):
    return mods_ref[0, 0, pl.ds(k, 1), :]


def _norm_mod(x, gain, shift, scale):
    ms = jnp.mean(x * x, axis=-1, keepdims=True)
    return (x * lax.rsqrt(ms + RMS_EPS) * gain) * (1.0 + scale) + shift


def _ada_kernel(c_ref, w_ref, b_ref, o_ref):
    c = c_ref[...]
    s = c * jax.nn.sigmoid(c)
    o_ref[...] = jnp.dot(s, w_ref[...], preferred_element_type=F32,
                         precision=lax.Precision.HIGHEST) + b_ref[...]


def ada_mod(cvec, w, b):
    r, d = cvec.shape
    n = w.shape[1]
    tn = 1536
    return pl.pallas_call(
        _ada_kernel,
        out_shape=jax.ShapeDtypeStruct((r, n), F32),
        grid=(n // tn,),
        in_specs=[pl.BlockSpec((r, d), lambda j: (0, 0)),
                  pl.BlockSpec((d, tn), lambda j: (0, j)),
                  pl.BlockSpec((1, tn), lambda j: (0, j))],
        out_specs=pl.BlockSpec((r, tn), lambda j: (0, j)),
        compiler_params=_cparams(("arbitrary",)),
        name="ada_mod",
    )(cvec, w, b.reshape(1, n))


def _ab_proj_kernel(h_ref, g_ref, mods_ref, w_ref, bd_ref, qkg_ref, qk_ref, v_ref, u_ref):
    x = _norm_mod(h_ref[0], g_ref[...], _mod_row(mods_ref, 0), _mod_row(mods_ref, 1))
    p = jnp.dot(x.astype(BF16), w_ref[...], preferred_element_type=F32)
    qk = p[:, :2 * NA_WIDTH]
    ss = jnp.dot((qk * qk).astype(BF16), bd_ref[...], preferred_element_type=F32)
    qk_ref[0] = (qk * lax.rsqrt(ss + RMS_EPS) * qkg_ref[...]).astype(BF16)
    v_ref[0] = p[:, 2 * NA_WIDTH:3 * NA_WIDTH].astype(BF16)
    u_ref[0] = p[:, 3 * NA_WIDTH:]


def ab_proj(h, gain, mods, w_in, q_g, k_g):
    b, lt, d = h.shape
    nt = lt // TILE
    n_in = w_in.shape[1]
    hid = jnp.arange(2 * NA_WIDTH) // NA_HEAD_DIM
    bd = jnp.where(hid[:, None] == hid[None, :], 1.0 / NA_HEAD_DIM, 0.0).astype(BF16)
    qkg = jnp.concatenate([jnp.tile(q_g, NA_HEADS), jnp.tile(k_g, NA_HEADS)]).reshape(1, -1).astype(F32)
    return pl.pallas_call(
        _ab_proj_kernel,
        out_shape=(jax.ShapeDtypeStruct((b, lt, 2 * NA_WIDTH), BF16),
                   jax.ShapeDtypeStruct((b, lt, NA_WIDTH), BF16),
                   jax.ShapeDtypeStruct((b, lt, POOL_WIDTH), F32)),
        grid=(b, nt),
        in_specs=[pl.BlockSpec((1, TILE, d), lambda bi, i: (bi, i, 0)),
                  pl.BlockSpec((1, d), lambda bi, i: (0, 0)),
                  pl.BlockSpec((1, 1, 6, d), lambda bi, i: (bi, jnp.minimum(i, 1), 0, 0)),
                  pl.BlockSpec((d, n_in), lambda bi, i: (0, 0)),
                  pl.BlockSpec((2 * NA_WIDTH, 2 * NA_WIDTH), lambda bi, i: (0, 0)),
                  pl.BlockSpec((1, 2 * NA_WIDTH), lambda bi, i: (0, 0))],
        out_specs=(pl.BlockSpec((1, TILE, 2 * NA_WIDTH), lambda bi, i: (bi, i, 0)),
                   pl.BlockSpec((1, TILE, NA_WIDTH), lambda bi, i: (bi, i, 0)),
                   pl.BlockSpec((1, TILE, POOL_WIDTH), lambda bi, i: (bi, i, 0))),
        compiler_params=_cparams(("arbitrary", "arbitrary")),
        name="ab_proj",
    )(h, gain.reshape(1, d), mods, w_in.astype(BF16), bd, qkg)


NA_QROWS = TILE // GRID_W
NA_KTILES = 3
NA_NLOC = NA_KTILES * TILE
NA_NKEY = NA_NLOC + CTX_LEN


def _na_bias_tables(rpb, rows):
    n_r, n_c = 2 * NA_KH - 1, 2 * NA_KW - 1
    col = np.arange(GRID_W)
    c_start = np.clip(col - NA_KW // 2, 0, GRID_W - NA_KW)
    valid_c = (col[None, :] >= c_start[:, None]) & (col[None, :] < c_start[:, None] + NA_KW)
    c_rel = np.clip(col[None, :] - col[:, None] + NA_KW - 1, 0, n_c - 1)
    sel_c = (c_rel.reshape(-1)[None, :] == np.arange(n_c)[:, None]).astype(np.float32)
    by_col = jnp.einsum("hrc,cn->hrn", rpb.astype(F32), sel_c, precision=lax.Precision.HIGHEST)
    n_qt = rows // NA_QROWS
    qr, kr = np.arange(NA_QROWS), np.arange(NA_KTILES * NA_QROWS)
    tabs = []
    for jq in (0, 1, n_qt - 1):
        w0 = NA_QROWS * min(max(jq - 1, 0), n_qt - NA_KTILES)
        q_row = NA_QROWS * jq + qr
        key_row = w0 + kr
        r_start = np.clip(q_row - NA_KH // 2, 0, rows - NA_KH)
        valid_r = (key_row[None, :] >= r_start[:, None]) & (key_row[None, :] < r_start[:, None] + NA_KH)
        r_rel = np.clip(key_row[None, :] - q_row[:, None] + NA_KH - 1, 0, n_r - 1)
        sel_r = (r_rel.reshape(-1)[:, None] == np.arange(n_r)[None, :]).astype(np.float32)
        t = jnp.einsum("xr,hrn->hxn", sel_r, by_col, precision=lax.Precision.HIGHEST)
        t = t.reshape(NA_HEADS, NA_QROWS, NA_KTILES * NA_QROWS, GRID_W, GRID_W)
        t = t.transpose(0, 1, 3, 2, 4).reshape(NA_HEADS, TILE, NA_NLOC)
        valid = (valid_r[:, None, :, None] & valid_c[None, :, None, :]).reshape(TILE, NA_NLOC)
        tabs.append(jnp.where(valid[None], t, NEG_BIG))
    tabs = [jnp.full_like(tabs[0], NEG_BIG)] + tabs
    loc = jnp.stack(tabs)
    return jnp.concatenate([loc, jnp.zeros(loc.shape[:3] + (CTX_LEN,), F32)], axis=-1)


def _na_kernel(q_ref, k0_ref, k1_ref, k2_ref, kc_ref, v0_ref, v1_ref, v2_ref, vc_ref, bias_ref, o_ref):
    q = q_ref[0]
    kall = jnp.concatenate([k0_ref[0], k1_ref[0], k2_ref[0], kc_ref[0]], axis=0)
    vall = jnp.concatenate([v0_ref[0], v1_ref[0], v2_ref[0], vc_ref[0]], axis=0)
    lane = lax.broadcasted_iota(jnp.int32, q.shape, 1)
    scale = NA_HEAD_DIM ** -0.5
    outs = []
    for hh in range(2):
        in_head = (lane >= hh * NA_HEAD_DIM) & (lane < (hh + 1) * NA_HEAD_DIM)
        qh = jnp.where(in_head, q, jnp.zeros_like(q)) * jnp.asarray(scale, BF16)
        s = lax.dot_general(qh, kall, (((1,), (1,)), ((), ())), preferred_element_type=F32)
        s = s + bias_ref[0, hh]
        m = jnp.max(s, axis=-1, keepdims=True)
        p = jnp.exp(s - m)
        l = jnp.sum(p, axis=-1, keepdims=True)
        o = jnp.dot(p.astype(BF16), vall, preferred_element_type=F32)
        outs.append(o / l)
    o_ref[0] = jnp.where(lane < NA_HEAD_DIM, outs[0], outs[1]).astype(BF16)


def na_attention(qk, v, bias):
    b, lt, _ = qk.shape
    nt = lt // TILE
    n_qt = nt - 1
    assert NA_HEAD_DIM ** -0.5 == 0.125
    hp_n = NA_HEADS // 2
    kblk = NA_WIDTH // LANES

    def w0(j):
        return 1 + jnp.clip(j - 2, 0, n_qt - NA_KTILES)

    def kind(j):
        return jnp.where(j == 0, 0, jnp.where(j == 1, 1, jnp.where(j == nt - 1, 3, 2)))

    qspec = pl.BlockSpec((1, TILE, LANES), lambda j, hp, bi: (bi, j, hp))
    kspecs = [pl.BlockSpec((1, TILE, LANES), functools.partial(lambda j, hp, bi, t: (bi, w0(j) + t, kblk + hp), t=t))
              for t in range(NA_KTILES)]
    kcspec = pl.BlockSpec((1, TILE, LANES), lambda j, hp, bi: (bi, 0, kblk + hp))
    vspecs = [pl.BlockSpec((1, TILE, LANES), functools.partial(lambda j, hp, bi, t: (bi, w0(j) + t, hp), t=t))
              for t in range(NA_KTILES)]
    vcspec = pl.BlockSpec((1, TILE, LANES), lambda j, hp, bi: (bi, 0, hp))
    bspec = pl.BlockSpec((1, 2, TILE, NA_NKEY), lambda j, hp, bi: (kind(j), hp, 0, 0))
    return pl.pallas_call(
        _na_kernel,
        out_shape=jax.ShapeDtypeStruct((b, lt, NA_WIDTH), BF16),
        grid=(nt, hp_n, b),
        in_specs=[qspec] + kspecs + [kcspec] + vspecs + [vcspec, bspec],
        out_specs=pl.BlockSpec((1, TILE, LANES), lambda j, hp, bi: (bi, j, hp)),
        compiler_params=_cparams(("arbitrary", "arbitrary", "arbitrary")),
        name="na_attention",
    )(qk, qk, qk, qk, qk, v, v, v, v, bias)


POOL_HALO = 8
POOL_ROWS = TILE + 2 * POOL_HALO


def _pool_kernel(cur_ref, prev_ref, next_ref, pw_ref, ps_ref, o_ref, xp_ref, *, n_tiles):
    i = pl.program_id(1)
    has_prev = (i >= 2).astype(F32)
    has_next = ((i >= 1) & (i < n_tiles - 1)).astype(F32)
    xp_ref[0:POOL_HALO, :] = prev_ref[0] * has_prev
    xp_ref[POOL_HALO:POOL_HALO + TILE, :] = cur_ref[0]
    xp_ref[POOL_HALO + TILE:POOL_ROWS, :] = next_ref[0] * has_next
    row = lax.broadcasted_iota(jnp.int32, (TILE, POOL_GC), 0)
    tpos = jnp.where(i == 0, row, row + (i - 1) * TILE)
    seq_len = jnp.where(i == 0, CTX_LEN, (n_tiles - 1) * TILE)
    for g, w in enumerate(POOL_WINDOWS):
        cols = slice(g * POOL_GC, (g + 1) * POOL_GC)
        x = xp_ref[:, cols]
        s = x + pltpu.roll(x, 1, 0)
        half = 1
        while 2 * half < w:
            s = pltpu.roll(s, half, 0) + pltpu.roll(s, POOL_ROWS - half, 0)
            half *= 2
        s = s[POOL_HALO:POOL_HALO + TILE]
        cnt = jnp.minimum(w // 2, tpos) + jnp.minimum(w // 2, seq_len - tpos)
        pooled = s / cnt.astype(F32) - cur_ref[0, :, cols]
        y = jnp.dot(pooled.astype(BF16), pw_ref[g], preferred_element_type=F32)
        o_ref[0, :, cols] = (y * ps_ref[:, cols]).astype(BF16)


def multiscale_pool(u, pool_w, pool_scale):
    b, lt, c = u.shape
    nt = lt // TILE
    hb = TILE // POOL_HALO
    nhb = lt // POOL_HALO
    assert POOL_HALO >= max(POOL_WINDOWS) // 2 and POOL_HALO == SUBLANES
    return pl.pallas_call(
        functools.partial(_pool_kernel, n_tiles=nt),
        out_shape=jax.ShapeDtypeStruct((b, lt, c), BF16),
        grid=(b, nt),
        in_specs=[pl.BlockSpec((1, TILE, c), lambda bi, i: (bi, i, 0)),
                  pl.BlockSpec((1, POOL_HALO, c), lambda bi, i: (bi, jnp.maximum(i * hb - 1, 0), 0)),
                  pl.BlockSpec((1, POOL_HALO, c), lambda bi, i: (bi, jnp.minimum((i + 1) * hb, nhb - 1), 0)),
                  pl.BlockSpec((len(POOL_WINDOWS), POOL_GC, POOL_GC), lambda bi, i: (0, 0, 0)),
                  pl.BlockSpec((1, c), lambda bi, i: (0, 0))],
        out_specs=pl.BlockSpec((1, TILE, c), lambda bi, i: (bi, i, 0)),
        scratch_shapes=[pltpu.VMEM((POOL_ROWS, c), F32)],
        compiler_params=_cparams(("arbitrary", "arbitrary")),
        name="multiscale_pool",
    )(u, u, u, pool_w.astype(BF16), pool_scale.reshape(1, c).astype(F32))


def _out_proj_kernel(a_ref, b_ref, h_ref, mods_ref, w_ref, o_ref):
    ka = a_ref.shape[-1]
    y = jnp.dot(a_ref[0], w_ref[:ka, :], preferred_element_type=F32)
    y = y + jnp.dot(b_ref[0], w_ref[ka:, :], preferred_element_type=F32)
    o_ref[0] = h_ref[0] + _mod_row(mods_ref, 2) * y


def out_proj_residual(a, bb, h, mods, w):
    b, lt, d = h.shape
    nt = lt // TILE
    ka, kb = a.shape[-1], bb.shape[-1]
    return pl.pallas_call(
        _out_proj_kernel,
        out_shape=jax.ShapeDtypeStruct((b, lt, d), F32),
        grid=(b, nt),
        in_specs=[pl.BlockSpec((1, TILE, ka), lambda bi, i: (bi, i, 0)),
                  pl.BlockSpec((1, TILE, kb), lambda bi, i: (bi, i, 0)),
                  pl.BlockSpec((1, TILE, d), lambda bi, i: (bi, i, 0)),
                  pl.BlockSpec((1, 1, 6, d), lambda bi, i: (bi, jnp.minimum(i, 1), 0, 0)),
                  pl.BlockSpec((ka + kb, d), lambda bi, i: (0, 0))],
        out_specs=pl.BlockSpec((1, TILE, d), lambda bi, i: (bi, i, 0)),
        compiler_params=_cparams(("arbitrary", "arbitrary")),
        name="out_proj_residual",
    )(a, bb, h, mods, w.astype(BF16))


def _moe_pre_kernel(h_ref, g_ref, mods_ref, rw_ref, rb_ref, y_ref, lg_ref):
    x = _norm_mod(h_ref[0], g_ref[...], _mod_row(mods_ref, 3), _mod_row(mods_ref, 4))
    y_ref[0] = x.astype(BF16)
    lg_ref[0] = jnp.dot(x, rw_ref[...], preferred_element_type=F32,
                        precision=lax.Precision.HIGHEST) + rb_ref[...]


def moe_pre(h, gain, mods, router_w, router_b, tile0, ctx_tiles):
    b, lt, d = h.shape
    nt = lt // TILE - tile0
    e = router_w.shape[1]
    return pl.pallas_call(
        _moe_pre_kernel,
        out_shape=(jax.ShapeDtypeStruct((b, nt * TILE, d), BF16),
                   jax.ShapeDtypeStruct((b, nt * TILE, e), F32)),
        grid=(b, nt),
        in_specs=[pl.BlockSpec((1, TILE, d), lambda bi, i: (bi, i + tile0, 0)),
                  pl.BlockSpec((1, d), lambda bi, i: (0, 0)),
                  pl.BlockSpec((1, 1, 6, d), lambda bi, i: (bi, jnp.minimum(i + tile0 + 1 - ctx_tiles, 1), 0, 0)),
                  pl.BlockSpec((d, e), lambda bi, i: (0, 0)),
                  pl.BlockSpec((1, e), lambda bi, i: (0, 0))],
        out_specs=(pl.BlockSpec((1, TILE, d), lambda bi, i: (bi, i, 0)),
                   pl.BlockSpec((1, TILE, e), lambda bi, i: (bi, i, 0))),
        compiler_params=_cparams(("arbitrary", "arbitrary")),
        name="moe_pre",
    )(h, gain.reshape(1, d), mods, router_w, router_b.reshape(1, e))


MOE_PERM_W = 256


def _moe_ffn_kernel(te_ref, tv_ref, x_ref, win_ref, bg_ref, bl_ref, wo_ref, bo_ref, perm_ref, o_ref,
                    wg_s, wl_s, wo_s):
    t = pl.program_id(0)
    valid = tv_ref[t] > 0
    new_expert = (t == 0) | (te_ref[t] != te_ref[jnp.maximum(t - 1, 0)])

    @pl.when(valid & new_expert)
    def _():
        half = MOE_PERM_W // 2
        for cb in range(win_ref.shape[3] // MOE_PERM_W):
            w = win_ref[0, 0, :, cb * MOE_PERM_W:(cb + 1) * MOE_PERM_W].astype(BF16)
            wp = jnp.dot(w, perm_ref[...], preferred_element_type=F32).astype(BF16)
            wg_s[:, cb * half:(cb + 1) * half] = wp[:, :half]
            wl_s[:, cb * half:(cb + 1) * half] = wp[:, half:]
        wo_s[...] = wo_ref[0, 0].astype(BF16)

    @pl.when(jnp.logical_not(valid))
    def _():
        o_ref[...] = jnp.zeros_like(o_ref)

    @pl.when(valid)
    def _():
        x = x_ref[...].astype(BF16)
        hg = jnp.dot(x, wg_s[...], preferred_element_type=F32) + bg_ref[0]
        hl = jnp.dot(x, wl_s[...], preferred_element_type=F32) + bl_ref[0]
        glu = jnp.minimum(hg, SWIGLU_LIMIT)
        lin = jnp.clip(hl, -SWIGLU_LIMIT, SWIGLU_LIMIT)
        act = glu * jax.nn.sigmoid(SWIGLU_ALPHA * glu) * (lin + 1.0)
        o_ref[...] = (jnp.dot(act.astype(BF16), wo_s[...], preferred_element_type=F32)
                      + bo_ref[0]).astype(o_ref.dtype)


def moe_grouped_ffn(x_sorted, tile_expert, tile_valid, layer, w_in, bg, bl, w_out, bo):
    p, d = x_sorted.shape
    _, e, _, f2 = w_in.shape
    f = f2 // 2
    nt = p // MOE_TM
    assert f2 % MOE_PERM_W == 0 and MOE_PERM_W % (2 * LANES) == 0
    src = jnp.arange(MOE_PERM_W)
    dst = jnp.where(src % 2 == 0, src // 2, MOE_PERM_W // 2 + src // 2)
    perm = (dst[:, None] == jnp.arange(MOE_PERM_W)[None, :]).astype(BF16)
    grid_spec = pltpu.PrefetchScalarGridSpec(
        num_scalar_prefetch=2,
        grid=(nt,),
        in_specs=[pl.BlockSpec((MOE_TM, d), lambda t, te, tv: (t * tv[t], 0)),
                  pl.BlockSpec((1, 1, d, f2), lambda t, te, tv: (layer, te[t], 0, 0)),
                  pl.BlockSpec((1, 1, f), lambda t, te, tv: (te[t], 0, 0)),
                  pl.BlockSpec((1, 1, f), lambda t, te, tv: (te[t], 0, 0)),
                  pl.BlockSpec((1, 1, f, d), lambda t, te, tv: (layer, te[t], 0, 0)),
                  pl.BlockSpec((1, 1, d), lambda t, te, tv: (te[t], 0, 0)),
                  pl.BlockSpec((MOE_PERM_W, MOE_PERM_W), lambda t, te, tv: (0, 0))],
        out_specs=pl.BlockSpec((MOE_TM, d), lambda t, te, tv: (t, 0)),
        scratch_shapes=[pltpu.VMEM((d, f), BF16), pltpu.VMEM((d, f), BF16), pltpu.VMEM((f, d), BF16)],
    )
    return pl.pallas_call(
        _moe_ffn_kernel,
        out_shape=jax.ShapeDtypeStruct((p, d), BF16),
        grid_spec=grid_spec,
        compiler_params=pltpu.CompilerParams(dimension_semantics=("arbitrary",),
                                             vmem_limit_bytes=MOE_FFN_VMEM_LIMIT),
        name="moe_grouped_ffn",
    )(tile_expert, tile_valid, x_sorted, w_in, bg, bl, w_out, bo, perm)


def _moe_combine_kernel(z_ref, gt_ref, h_ref, mods_ref, o_ref):
    gt = gt_ref[0]
    acc = gt[:, 0:1] * z_ref[0, 0].astype(F32)
    for k in range(1, TOP_K):
        acc = acc + gt[:, k:k + 1] * z_ref[k, 0].astype(F32)
    o_ref[0] = h_ref[0] + _mod_row(mods_ref, 5) * acc


def moe_combine(z4, gates, h, mods, tile0, ctx_tiles):
    k, b, lp, d = z4.shape
    nt = lp // TILE
    return pl.pallas_call(
        _moe_combine_kernel,
        out_shape=jax.ShapeDtypeStruct((b, lp, d), F32),
        grid=(b, nt),
        in_specs=[pl.BlockSpec((k, 1, TILE, d), lambda bi, i: (0, bi, i, 0)),
                  pl.BlockSpec((1, TILE, k), lambda bi, i: (bi, i, 0)),
                  pl.BlockSpec((1, TILE, d), lambda bi, i: (bi, i + tile0, 0)),
                  pl.BlockSpec((1, 1, 6, d), lambda bi, i: (bi, jnp.minimum(i + tile0 + 1 - ctx_tiles, 1), 0, 0))],
        out_specs=pl.BlockSpec((1, TILE, d), lambda bi, i: (bi, i, 0)),
        compiler_params=_cparams(("arbitrary", "arbitrary")),
        name="moe_combine",
    )(z4, gates, h, mods)


DISPATCH_T = 512
DISPATCH_BLK = SUBLANES
DISPATCH_ROWS = -(-(DISPATCH_T * TOP_K + N_EXPERTS * (DISPATCH_BLK - 1)) // (2 * SUBLANES)) * (2 * SUBLANES)


def _moe_dispatch_kernel(nblk_ref, dst_ref, ntail_ref, tail_ref, y_ref, idx_ref, off_ref, base_ref,
                         xs_ref, slot_ref, local_ref, zero_ref, sem):
    t = pl.program_id(0)
    dt = DISPATCH_T
    idx = idx_ref[0]
    e_iota = lax.broadcasted_iota(jnp.int32, (N_EXPERTS, dt), 0)
    picked = [e_iota == idx[k:k + 1, :] for k in range(TOP_K)]
    cnt = sum(jnp.where(p, 1.0, 0.0) for p in picked).astype(BF16)
    r0 = lax.broadcasted_iota(jnp.int32, (dt, dt), 0)
    r1 = lax.broadcasted_iota(jnp.int32, (dt, dt), 1)
    earlier = jnp.where(r0 < r1, 1.0, 0.0).astype(BF16)
    rank = jnp.dot(cnt, earlier, preferred_element_type=F32)
    local_row = rank + off_ref[0].astype(F32)
    global_row = rank + base_ref[0].astype(F32)
    rows = lax.broadcasted_iota(jnp.int32, (DISPATCH_ROWS, dt), 0)
    place = None
    for k in range(TOP_K):
        pos = jnp.sum(jnp.where(picked[k], local_row, 0.0), axis=0, keepdims=True).astype(jnp.int32)
        slot_ref[0, k:k + 1, :] = jnp.sum(jnp.where(picked[k], global_row, 0.0), axis=0,
                                          keepdims=True).astype(jnp.int32)
        hit = rows == pos
        place = hit if place is None else (place | hit)
    placement = jnp.where(place, 1.0, 0.0).astype(BF16)
    local_ref[...] = jnp.dot(placement, y_ref[...], preferred_element_type=F32)

    def block_copy(j):
        src = local_ref.at[pl.ds(pl.multiple_of(j * DISPATCH_BLK, DISPATCH_BLK), DISPATCH_BLK), :]
        dst = xs_ref.at[pl.ds(pl.multiple_of(dst_ref[t, j] * DISPATCH_BLK, DISPATCH_BLK), DISPATCH_BLK), :]
        return pltpu.make_async_copy(src, dst, sem)

    def tail_copy(e, j):
        blk = tail_ref[e] + j
        dst = xs_ref.at[pl.ds(pl.multiple_of(blk * DISPATCH_BLK, DISPATCH_BLK), DISPATCH_BLK), :]
        return pltpu.make_async_copy(zero_ref, dst, sem)

    def run_all(copy, n):
        lax.fori_loop(0, n, lambda j, c: (copy(j).start(), c)[1], 0)
        lax.fori_loop(0, n, lambda j, c: (copy(j).wait(), c)[1], 0)

    run_all(block_copy, nblk_ref[t])

    @pl.when(t == 0)
    def _():
        zero_ref[...] = jnp.zeros_like(zero_ref)
        for e in range(N_EXPERTS + 1):
            run_all(functools.partial(tail_copy, e), ntail_ref[e])


def moe_dispatch(y2, idx_t, nblk, dst_blk, ntail, tail_blk, loc_off, base, p_rows):
    n, d = y2.shape
    nt = n // DISPATCH_T
    grid_spec = pltpu.PrefetchScalarGridSpec(
        num_scalar_prefetch=4,
        grid=(nt,),
        in_specs=[pl.BlockSpec((DISPATCH_T, d), lambda t, *_: (t, 0)),
                  pl.BlockSpec((1, TOP_K, DISPATCH_T), lambda t, *_: (t, 0, 0)),
                  pl.BlockSpec((1, N_EXPERTS, 1), lambda t, *_: (t, 0, 0)),
                  pl.BlockSpec((1, N_EXPERTS, 1), lambda t, *_: (t, 0, 0))],
        out_specs=(pl.BlockSpec(memory_space=pl.ANY),
                   pl.BlockSpec((1, TOP_K, DISPATCH_T), lambda t, *_: (t, 0, 0))),
        scratch_shapes=[pltpu.VMEM((DISPATCH_ROWS, d), F32), pltpu.VMEM((DISPATCH_BLK, d), F32),
                        pltpu.SemaphoreType.DMA(())],
    )
    return pl.pallas_call(
        _moe_dispatch_kernel,
        out_shape=(jax.ShapeDtypeStruct((p_rows, d), F32),
                   jax.ShapeDtypeStruct((nt, TOP_K, DISPATCH_T), jnp.int32)),
        grid_spec=grid_spec,
        compiler_params=_cparams(("arbitrary",)),
        name="moe_dispatch",
    )(nblk, dst_blk, ntail, tail_blk, y2, idx_t, loc_off, base)


def moe_layer(h, gain, mods, router_w, router_b, layer, w_in, b_in, w_out, b_out, tile0, ctx_tiles):
    b, lt, d = h.shape
    y, logits = moe_pre(h, gain, mods, router_w, router_b, tile0, ctx_tiles)
    lp = y.shape[1]
    n = b * lp
    y2 = y.reshape(n, d)
    top_val, top_idx = lax.top_k(logits.reshape(n, N_EXPERTS), TOP_K)
    gates = jax.nn.softmax(top_val, axis=-1)

    assert n % DISPATCH_T == 0
    t_n = n // DISPATCH_T
    blk = DISPATCH_BLK
    experts = jnp.arange(N_EXPERTS, dtype=jnp.int32)
    idx_t = top_idx.astype(jnp.int32).reshape(t_n, DISPATCH_T, TOP_K).transpose(0, 2, 1)
    tile_cnt = jnp.sum((idx_t[..., None] == experts).astype(jnp.int32), axis=(1, 2))
    seg = (tile_cnt + blk - 1) // blk * blk
    loc_end = jnp.cumsum(seg, axis=1)
    loc_off = loc_end - seg
    nblk = loc_end[:, -1] // blk
    tot = jnp.sum(seg, axis=0)
    padded = (tot + MOE_TM - 1) // MOE_TM * MOE_TM
    ends = jnp.cumsum(padded)
    starts = ends - padded
    base = starts[None, :] + jnp.cumsum(seg, axis=0) - seg
    row0 = jnp.arange(DISPATCH_ROWS // blk, dtype=jnp.int32) * blk
    seg_of = jnp.sum((loc_end[:, None, :] <= row0[None, :, None]).astype(jnp.int32), axis=2)
    shift = jnp.sum(jnp.where(seg_of[..., None] == experts, (base - loc_off)[:, None, :], 0), axis=2)
    dst_blk = (shift + row0[None, :]) // blk
    n_tiles = -(-(n * TOP_K + t_n * N_EXPERTS * (blk - 1)) // MOE_TM) + N_EXPERTS
    tile_start = jnp.arange(n_tiles, dtype=jnp.int32) * MOE_TM
    tile_expert = jnp.minimum(jnp.sum((tile_start[:, None] >= ends[None, :]).astype(jnp.int32), axis=1),
                              N_EXPERTS - 1)
    tile_valid = (tile_start < ends[-1]).astype(jnp.int32)

    p_rows = n_tiles * MOE_TM
    gap_len = jnp.concatenate([padded - tot, p_rows - ends[-1:]]) // blk
    gap_start = jnp.concatenate([starts + tot, ends[-1:]]) // blk
    x_sorted, slot = moe_dispatch(y2, idx_t, nblk, dst_blk, gap_len, gap_start,
                                  loc_off[..., None], base[..., None], p_rows)
    bg = b_in[layer, :, 0::2].reshape(N_EXPERTS, 1, D_FF)
    bl = b_in[layer, :, 1::2].reshape(N_EXPERTS, 1, D_FF)
    z = moe_grouped_ffn(x_sorted, tile_expert, tile_valid, layer, w_in, bg, bl, w_out,
                        b_out[layer].reshape(N_EXPERTS, 1, d))
    slot_kn = slot.transpose(1, 0, 2).reshape(TOP_K, n)
    z4 = z.at[slot_kn].get(mode="promise_in_bounds").reshape(TOP_K, b, lp, d)
    return moe_combine(z4, gates.reshape(b, lp, TOP_K), h, mods, tile0, ctx_tiles)


def _softplus(z):
    return jnp.maximum(z, 0.0) + jnp.log(1.0 + jnp.exp(-jnp.abs(z)))


def _rwkv_feat_kernel(h_ref, hp_ref, hn_ref, g_ref, mods_ref, mu_ref, wr_ref, wk_ref, wv_ref,
                      w1_ref, w2_ref, a1_ref, a2_ref, g1_ref, g2_ref, w0_ref, a0_ref, kk_ref, ka_ref,
                      bd_ref, r_out, v_out, kk_out, g_out, lw_out, kd_out, a_out, *, n_tiles):
    i = pl.program_id(1)
    gain, shift, scale = g_ref[...], _mod_row(mods_ref, 0), _mod_row(mods_ref, 1)
    x = _norm_mod(h_ref[0], gain, shift, scale)
    has_prev = (i >= 2).astype(F32)
    has_next = ((i >= 1) & (i < n_tiles - 1)).astype(F32)
    x_before = _norm_mod(hp_ref[0], gain, shift, scale)[SUBLANES - 1:SUBLANES] * has_prev
    x_after = _norm_mod(hn_ref[0], gain, shift, scale)[0:1] * has_next
    row = lax.broadcasted_iota(jnp.int32, x.shape, 0)
    prev = jnp.where(row == 0, x_before, pltpu.roll(x, 1, 0))
    nxt = jnp.where(row == TILE - 1, x_after, pltpu.roll(x, TILE - 1, 0))
    xx = 0.5 * (prev + nxt) - x
    xr, xw, xk, xv, xa, xg = (x + xx * mu_ref[pl.ds(j, 1), :] for j in range(6))

    def mm(a, w):
        return jnp.dot(a.astype(BF16), w, preferred_element_type=F32)

    r = mm(xr, wr_ref[...])
    k = mm(xk, wk_ref[...])
    v = mm(xv, wv_ref[...])
    g = mm(jax.nn.sigmoid(mm(xg, g1_ref[...])), g2_ref[...])
    kk = k * kk_ref[...]
    kk = kk * lax.rsqrt(mm(kk * kk, bd_ref[...]) + 1e-12)
    r_out[0] = r.astype(BF16)
    v_out[0] = v.astype(BF16)
    kk_out[0] = kk.astype(BF16)
    g_out[0] = g.astype(BF16)
    tw = jnp.tanh(mm(xw, w1_ref[...]))
    av = mm(xa, a1_ref[...])
    lane = lax.broadcasted_iota(jnp.int32, tw.shape, 1)
    half = tw.shape[1] // 2
    for d in range(2):
        in_dir = (lane >= d * half) & (lane < (d + 1) * half)
        wz = w0_ref[pl.ds(d, 1), :] + mm(jnp.where(in_dir, tw, 0.0), w2_ref[...])
        w_log = -_softplus(-wz) - 0.5
        lw_out[d, 0] = -jnp.exp(w_log)
        a = jax.nn.sigmoid(a0_ref[pl.ds(d, 1), :] + mm(jnp.where(in_dir, av, 0.0), a2_ref[...]))
        kd_out[d, 0] = (k * (1.0 + (a - 1.0) * ka_ref[...])).astype(BF16)
        a_out[d, 0] = a.astype(BF16)


def rwkv_features(h, gain, mods, mu, w_r, w_k, w_v, w0, w1, w2, a0, a1, a2, g1, g2, k_k, k_a):
    b, lt, d = h.shape
    nt = lt // TILE
    hb = TILE // SUBLANES
    nhb = lt // SUBLANES
    hid = jnp.arange(d) // RW_HEAD_DIM
    bd = (hid[:, None] == hid[None, :]).astype(BF16)
    w1c = jnp.concatenate([w1[0], w1[1]], axis=1).astype(BF16)
    a1c = jnp.concatenate([a1[0], a1[1]], axis=1).astype(BF16)
    w2c = jnp.concatenate([w2[0], w2[1]], axis=0).astype(BF16)
    a2c = jnp.concatenate([a2[0], a2[1]], axis=0).astype(BF16)
    full = lambda shape: pl.BlockSpec(shape, lambda bi, i: (0,) * len(shape))
    tile_spec = pl.BlockSpec((1, TILE, d), lambda bi, i: (bi, i, 0))
    dir_spec = pl.BlockSpec((2, 1, TILE, d), lambda bi, i: (0, bi, i, 0))
    seq_bf = jax.ShapeDtypeStruct((b, lt, d), BF16)
    return pl.pallas_call(
        functools.partial(_rwkv_feat_kernel, n_tiles=nt),
        out_shape=(seq_bf, seq_bf, seq_bf, seq_bf,
                   jax.ShapeDtypeStruct((2, b, lt, d), F32),
                   jax.ShapeDtypeStruct((2, b, lt, d), BF16),
                   jax.ShapeDtypeStruct((2, b, lt, d), BF16)),
        grid=(b, nt),
        in_specs=[tile_spec,
                  pl.BlockSpec((1, SUBLANES, d), lambda bi, i: (bi, jnp.maximum(i * hb - 1, 0), 0)),
                  pl.BlockSpec((1, SUBLANES, d), lambda bi, i: (bi, jnp.minimum((i + 1) * hb, nhb - 1), 0)),
                  full((1, d)),
                  pl.BlockSpec((1, 1, 6, d), lambda bi, i: (bi, jnp.minimum(i, 1), 0, 0)),
                  full((6, d)), full((d, d)), full((d, d)), full((d, d)),
                  full(w1c.shape), full(w2c.shape), full(a1c.shape), full(a2c.shape),
                  full(g1.shape), full(g2.shape), full((2, d)), full((2, d)), full((1, d)), full((1, d)),
                  full((d, d))],
        out_specs=(tile_spec, tile_spec, tile_spec, tile_spec, dir_spec, dir_spec, dir_spec),
        compiler_params=_cparams(("arbitrary", "arbitrary")),
        name="rwkv_features",
    )(h, h, h, gain.reshape(1, d), mods, mu, w_r.astype(BF16), w_k.astype(BF16), w_v.astype(BF16),
      w1c, w2c, a1c, a2c, g1.astype(BF16), g2.astype(BF16), w0, a0, k_k.reshape(1, d), k_a.reshape(1, d), bd)


SCAN_C = 64
SCAN_G = 4
SCAN_GW = SCAN_G * RW_HEAD_DIM


def _scan_chunk_inputs(reverse, r_ref, v_ref, kk_ref, lw_ref, kd_ref, a_ref):
    cc = SCAN_C
    ti = lax.broadcasted_iota(jnp.int32, (cc, cc), 0)
    si = lax.broadcasted_iota(jnp.int32, (cc, cc), 1)
    before_incl = (si >= ti) if reverse else (si <= ti)
    lw = lw_ref[0, 0]
    cl = jnp.dot(before_incl.astype(F32), lw, preferred_element_type=F32, precision=lax.Precision.HIGHEST)
    g_in = jnp.exp(cl)
    g_ex = jnp.exp(cl - lw)
    g_inv = jnp.exp(-cl)
    g_end = jnp.exp(jnp.sum(lw, axis=0, keepdims=True))
    kk = kk_ref[0].astype(F32)
    rt = (r_ref[0].astype(F32) * g_in).astype(BF16)
    at = (-kk * g_ex).astype(BF16)
    bt = (kk * a_ref[0, 0].astype(F32) * g_inv).astype(BF16)
    kt = (kd_ref[0, 0].astype(F32) * g_inv).astype(BF16)
    return rt, at, bt, kt, v_ref[0], g_end


def _rwkv_scan_kernel(rf_ref, vf_ref, kkf_ref, rb_ref, vb_ref, kkb_ref, lwf_ref, kdf_ref, af_ref,
                      lwb_ref, kdb_ref, ab_ref, yf_ref, yb_ref, h_ref):
    @pl.when(pl.program_id(1) == 0)
    def _():
        h_ref[...] = jnp.zeros_like(h_ref)

    cc, gw = SCAN_C, SCAN_GW
    ops = (_scan_chunk_inputs(False, rf_ref, vf_ref, kkf_ref, lwf_ref, kdf_ref, af_ref),
           _scan_chunk_inputs(True, rb_ref, vb_ref, kkb_ref, lwb_ref, kdb_ref, ab_ref))
    y_refs = (yf_ref, yb_ref)

    tg = lax.broadcasted_iota(jnp.int32, (cc, gw), 0)
    sg = lax.broadcasted_iota(jnp.int32, (cc, gw), 1) % cc
    m_strict = (sg < tg, sg > tg)
    m_incl = (sg <= tg, sg >= tg)
    eye_g = (sg == tg).astype(F32)
    bi = lax.broadcasted_iota(jnp.int32, (gw, gw), 0) // cc
    bj = lax.broadcasted_iota(jnp.int32, (gw, gw), 1) // cc
    blk = bi == bj

    def bdiag(x):
        return jnp.where(blk, jnp.concatenate([x] * SCAN_G, axis=0), jnp.zeros((), x.dtype))

    def mm(a, b):
        return jnp.dot(a, b, preferred_element_type=F32)

    def mm_nt(a, b):
        return lax.dot_general(a, b, (((1,), (1,)), ((), ())), preferred_element_type=F32)

    n_g = RW_HEADS // SCAN_G
    chains = [(d, g) for d in range(2) for g in range(n_g)]
    cols = [slice(g * gw, (g + 1) * gw) for _, g in chains]
    rt, at, bt, kt, v, g_end = ([ops[d][j][:, cols[i]] for i, (d, _) in enumerate(chains)] for j in range(6))
    idx = range(len(chains))
    h0 = [h_ref[i] for i in idx]
    h0b = [x.astype(BF16) for x in h0]
    v_bd = [bdiag(x) for x in v]
    mats = [mm_nt(jnp.concatenate([at[i], rt[i]], axis=0),
                  jnp.concatenate([bdiag(bt[i]), bdiag(kt[i])], axis=0)) for i in idx]
    a_ab = [jnp.where(m_strict[chains[i][0]], mats[i][:cc, :gw], 0.0) for i in idx]
    a_ak = [jnp.where(m_strict[chains[i][0]], mats[i][:cc, gw:], 0.0).astype(BF16) for i in idx]
    m_rb = [jnp.where(m_incl[chains[i][0]], mats[i][cc:, :gw], 0.0).astype(BF16) for i in idx]
    m_rk = [jnp.where(m_incl[chains[i][0]], mats[i][cc:, gw:], 0.0).astype(BF16) for i in idx]
    pq = [mm(jnp.concatenate([at[i], a_ak[i]], axis=1), jnp.concatenate([h0b[i], v_bd[i]], axis=0)) for i in idx]
    pw = [x.astype(BF16) for x in a_ab]
    s_acc = [eye_g + x for x in a_ab]
    pw = [mm(x, bdiag(x)).astype(BF16) for x in pw]
    n = 2
    while n < cc:
        last = 2 * n >= cc
        lhs = [s_acc[i].astype(BF16) if last else jnp.concatenate([pw[i], s_acc[i].astype(BF16)], axis=0)
               for i in idx]
        prod = [mm(lhs[i], bdiag(pw[i])) for i in idx]
        s_acc = [s_acc[i] + (prod[i] if last else prod[i][cc:]) for i in idx]
        if not last:
            pw = [prod[i][:cc].astype(BF16) for i in idx]
        n *= 2
    ub = [mm(s_acc[i].astype(BF16), bdiag(pq[i].astype(BF16))).astype(BF16) for i in idx]
    for i in idx:
        y_refs[chains[i][0]][0, :, cols[i]] = mm(jnp.concatenate([rt[i], m_rb[i], m_rk[i]], axis=1),
                                                 jnp.concatenate([h0b[i], bdiag(ub[i]), v_bd[i]], axis=0))
    for i in idx:
        upd = lax.dot_general(jnp.concatenate([bt[i], kt[i]], axis=0), jnp.concatenate([ub[i], v[i]], axis=0),
                              (((0,), (0,)), ((), ())), preferred_element_type=F32)
        ge = jnp.transpose(jnp.broadcast_to(g_end[i], (gw, gw)))
        h_ref[i] = ge * (h0[i] + jnp.where(blk, upd, 0.0))


def rwkv_scan(r, v, kk, lw, kd, a):
    b, lt, d = r.shape
    nc = lt // SCAN_C
    n_ctx = CTX_LEN // SCAN_C
    assert SCAN_C == RW_HEAD_DIM and CTX_LEN % SCAN_C == 0

    def rev_chunk(c):
        return jnp.where(c < n_ctx, n_ctx - 1 - c, nc - 1 + n_ctx - c)

    fwd = pl.BlockSpec((1, SCAN_C, d), lambda bi, c: (bi, c, 0))
    bwd = pl.BlockSpec((1, SCAN_C, d), lambda bi, c: (bi, rev_chunk(c), 0))
    fwd_dir = pl.BlockSpec((1, 1, SCAN_C, d), lambda bi, c: (0, bi, c, 0))
    bwd_dir = pl.BlockSpec((1, 1, SCAN_C, d), lambda bi, c: (1, bi, rev_chunk(c), 0))
    y_shape = jax.ShapeDtypeStruct((b, lt, d), F32)
    return pl.pallas_call(
        _rwkv_scan_kernel,
        out_shape=(y_shape, y_shape),
        grid=(b, nc),
        in_specs=[fwd, fwd, fwd, bwd, bwd, bwd, fwd_dir, fwd_dir, fwd_dir, bwd_dir, bwd_dir, bwd_dir],
        out_specs=(fwd, bwd),
        scratch_shapes=[pltpu.VMEM((2 * RW_HEADS // SCAN_G, SCAN_GW, SCAN_GW), F32)],
        compiler_params=_cparams(("arbitrary", "arbitrary")),
        name="rwkv_scan",
    )(r, v, kk, r, v, kk, lw, kd, a, lw, kd, a)


def _rwkv_out_kernel(yf_ref, yb_ref, r_ref, v_ref, g_ref, kd_ref, h_ref, mods_ref, lnw_ref, lnb_ref, rk_ref,
                     bd_ref, wo_ref, o_ref):
    def head_sum(x):
        return jnp.dot(x.astype(BF16), bd_ref[...], preferred_element_type=F32)

    y = yf_ref[0] + yb_ref[0]
    inv_n = 1.0 / RW_HEAD_DIM
    mean = head_sum(y) * inv_n
    yc = y - mean
    var = head_sum(yc * yc) * inv_n
    yn = yc * lax.rsqrt(var + RW_GN_EPS) * lnw_ref[...] + lnb_ref[...]
    r = r_ref[0].astype(F32)
    kd = kd_ref[0, 0].astype(F32) + kd_ref[1, 0].astype(F32)
    yn = yn + head_sum(r * kd * rk_ref[...]) * v_ref[0].astype(F32)
    out = jnp.dot((yn * g_ref[0].astype(F32)).astype(BF16), wo_ref[...], preferred_element_type=F32)
    o_ref[0] = h_ref[0] + _mod_row(mods_ref, 2) * out


def rwkv_readout(y_f, y_b, r, v, g, kd, h, mods, ln_w, ln_b, r_k, w_o):
    b, lt, d = h.shape
    nt = lt // TILE - 1
    hid = jnp.arange(d) // RW_HEAD_DIM
    bd = (hid[:, None] == hid[None, :]).astype(BF16)
    full = lambda shape: pl.BlockSpec(shape, lambda bi, i: (0,) * len(shape))
    tile_spec = pl.BlockSpec((1, TILE, d), lambda bi, i: (bi, i + 1, 0))
    dir_spec = pl.BlockSpec((2, 1, TILE, d), lambda bi, i: (0, bi, i + 1, 0))
    return pl.pallas_call(
        _rwkv_out_kernel,
        out_shape=jax.ShapeDtypeStruct((b, nt * TILE, d), F32),
        grid=(b, nt),
        in_specs=[tile_spec, tile_spec, tile_spec, tile_spec, tile_spec, dir_spec, tile_spec,
                  pl.BlockSpec((1, 1, 6, d), lambda bi, i: (bi, 1, 0, 0)),
                  full((1, d)), full((1, d)), full((1, d)), full((d, d)), full((d, d))],
        out_specs=pl.BlockSpec((1, TILE, d), lambda bi, i: (bi, i, 0)),
        compiler_params=_cparams(("arbitrary", "arbitrary")),
        name="rwkv_readout",
    )(y_f, y_b, r, v, g, kd, h, mods, ln_w.reshape(1, d), ln_b.reshape(1, d), r_k.reshape(1, d), bd,
      w_o.astype(BF16))


def kernel(x, c, ctx, c_ctx, ada_w, ada_b, norm_mix_g, norm_ffn_g, router_w, router_b, exp_w_in, exp_b_in,
           exp_w_out, exp_b_out, ab_w_in, na_q_g, na_k_g, na_rpb, pool_w, pool_scale, ab_w_out,
           rw_mu, rw_w_r, rw_w_k, rw_w_v, rw_w_o, rw_w0, rw_w1, rw_w2, rw_a0, rw_a1, rw_a2,
           rw_g1, rw_g2, rw_k_k, rw_k_a, rw_r_k, rw_ln_w, rw_ln_b):
    b, seq, d = x.shape
    depth = ada_w.shape[0]
    assert ctx.shape[1] == CTX_LEN == TILE and seq % TILE == 0 and d == D_MODEL
    assert depth == 2, "layer schedule below is written for [neighbourhood/pool layer, RWKV layer]"
    rows = seq // GRID_W
    assert rows >= NA_QROWS * NA_KTILES and NA_QROWS * NA_KTILES >= NA_QROWS + NA_KH - 1

    n_c = 1 + b
    n_c_pad = -(-n_c // SUBLANES) * SUBLANES
    cvec = jnp.concatenate([c_ctx[None], c, jnp.zeros((n_c_pad - n_c, d), F32)], axis=0)

    def layer_mods(layer):
        m = ada_mod(cvec, ada_w[layer], ada_b[layer])
        m_ctx = jnp.broadcast_to(m[0].reshape(1, 6, d), (b, 6, d))
        return jnp.stack([m_ctx, m[1:n_c].reshape(b, 6, d)], axis=1)

    def moe(h, layer, mods, tile0, ctx_tiles):
        return moe_layer(h, norm_ffn_g[layer], mods, router_w[layer], router_b[layer], layer, exp_w_in,
                         exp_b_in, exp_w_out, exp_b_out, tile0, ctx_tiles)

    h = jnp.concatenate([ctx, x], axis=1)

    mods = layer_mods(0)
    qk, v, u = ab_proj(h, norm_mix_g[0], mods, ab_w_in[0], na_q_g[0], na_k_g[0])
    o_na = na_attention(qk, v, _na_bias_tables(na_rpb[0], rows))
    o_pool = multiscale_pool(u, pool_w[0], pool_scale[0])
    h = out_proj_residual(o_na, o_pool, h, mods, ab_w_out[0])
    h = moe(h, 0, mods, 0, 1)

    mods = layer_mods(1)
    r, v, kk, g, lw, kd, a = rwkv_features(h, norm_mix_g[1], mods, rw_mu[0], rw_w_r[0], rw_w_k[0], rw_w_v[0],
                                           rw_w0[0], rw_w1[0], rw_w2[0], rw_a0[0], rw_a1[0], rw_a2[0],
                                           rw_g1[0], rw_g2[0], rw_k_k[0], rw_k_a[0])
    y_f, y_b = rwkv_scan(r, v, kk, lw, kd, a)
    h_lat = rwkv_readout(y_f, y_b, r, v, g, kd, h, mods, rw_ln_w[0], rw_ln_b[0], rw_r_k[0], rw_w_o[0])
    return moe(h_lat, 1, mods, 0, 0)
```

```python
import functools

import jax
import jax.numpy as jnp
import numpy as np
from jax import lax
from jax.experimental import pallas as pl
from jax.experimental.pallas import tpu as pltpu

F32 = jnp.float32
BF16 = jnp.bfloat16

D_MODEL = 1024
GRID_W = 64
CTX_LEN = 256
RMS_EPS = 1e-6
NA_HEADS = 8
NA_HEAD_DIM = 64
NA_WIDTH = NA_HEADS * NA_HEAD_DIM
NA_KH = 8
NA_KW = 16
POOL_WINDOWS = (2, 4, 8, 16)
POOL_GC = 128
POOL_WIDTH = 512
RW_HEAD_DIM = 64
RW_HEADS = D_MODEL // RW_HEAD_DIM
RW_GN_EPS = 64e-5
N_EXPERTS = 32
TOP_K = 4
D_FF = D_MODEL
SWIGLU_LIMIT = 7.0
SWIGLU_ALPHA = 1.702

TILE = 256
LANES = 128
SUBLANES = 8
MOE_TM = 512
NEG_BIG = -1e30
VMEM_LIMIT = 48 * 1024 * 1024
MOE_FFN_VMEM_LIMIT = 56 * 1024 * 1024


def _cparams(sem):
    return pltpu.CompilerParams(dimension_semantics=sem, vmem_limit_bytes=VMEM_LIMIT)


def _mod_row(mods_ref, k):
    return mods_ref[0, 0, pl.ds(k, 1), :]


def _norm_mod(x, gain, shift, scale):
    ms = jnp.mean(x * x, axis=-1, keepdims=True)
    return (x * lax.rsqrt(ms + RMS_EPS) * gain) * (1.0 + scale) + shift


def _ada_kernel(c_ref, w_ref, b_ref, o_ref):
    c = c_ref[...]
    s = c * jax.nn.sigmoid(c)
    o_ref[...] = jnp.dot(s, w_ref[...], preferred_element_type=F32,
                         precision=lax.Precision.HIGHEST) + b_ref[...]


def ada_mod(cvec, w, b):
    r, d = cvec.shape
    n = w.shape[1]
    tn = 1536
    return pl.pallas_call(
        _ada_kernel,
        out_shape=jax.ShapeDtypeStruct((r, n), F32),
        grid=(n // tn,),
        in_specs=[pl.BlockSpec((r, d), lambda j: (0, 0)),
                  pl.BlockSpec((d, tn), lambda j: (0, j)),
                  pl.BlockSpec((1, tn), lambda j: (0, j))],
        out_specs=pl.BlockSpec((r, tn), lambda j: (0, j)),
        compiler_params=_cparams(("arbitrary",)),
        name="ada_mod",
    )(cvec, w, b.reshape(1, n))


def _ab_proj_kernel(h_ref, g_ref, mods_ref, w_ref, bd_ref, qkg_ref, qk_ref, v_ref, u_ref):
    x = _norm_mod(h_ref[0], g_ref[...], _mod_row(mods_ref, 0), _mod_row(mods_ref, 1))
    p = jnp.dot(x.astype(BF16), w_ref[...], preferred_element_type=F32)
    qk = p[:, :2 * NA_WIDTH]
    ss = jnp.dot((qk * qk).astype(BF16), bd_ref[...], preferred_element_type=F32)
    qk_ref[0] = (qk * lax.rsqrt(ss + RMS_EPS) * qkg_ref[...]).astype(BF16)
    v_ref[0] = p[:, 2 * NA_WIDTH:3 * NA_WIDTH].astype(BF16)
    u_ref[0] = p[:, 3 * NA_WIDTH:]


def ab_proj(h, gain, mods, w_in, q_g, k_g):
    b, lt, d = h.shape
    nt = lt // TILE
    n_in = w_in.shape[1]
    hid = jnp.arange(2 * NA_WIDTH) // NA_HEAD_DIM
    bd = jnp.where(hid[:, None] == hid[None, :], 1.0 / NA_HEAD_DIM, 0.0).astype(BF16)
    qkg = jnp.concatenate([jnp.tile(q_g, NA_HEADS), jnp.tile(k_g, NA_HEADS)]).reshape(1, -1).astype(F32)
    return pl.pallas_call(
        _ab_proj_kernel,
        out_shape=(jax.ShapeDtypeStruct((b, lt, 2 * NA_WIDTH), BF16),
                   jax.ShapeDtypeStruct((b, lt, NA_WIDTH), BF16),
                   jax.ShapeDtypeStruct((b, lt, POOL_WIDTH), F32)),
        grid=(b, nt),
        in_specs=[pl.BlockSpec((1, TILE, d), lambda bi, i: (bi, i, 0)),
                  pl.BlockSpec((1, d), lambda bi, i: (0, 0)),
                  pl.BlockSpec((1, 1, 6, d), lambda bi, i: (bi, jnp.minimum(i, 1), 0, 0)),
                  pl.BlockSpec((d, n_in), lambda bi, i: (0, 0)),
                  pl.BlockSpec((2 * NA_WIDTH, 2 * NA_WIDTH), lambda bi, i: (0, 0)),
                  pl.BlockSpec((1, 2 * NA_WIDTH), lambda bi, i: (0, 0))],
        out_specs=(pl.BlockSpec((1, TILE, 2 * NA_WIDTH), lambda bi, i: (bi, i, 0)),
                   pl.BlockSpec((1, TILE, NA_WIDTH), lambda bi, i: (bi, i, 0)),
                   pl.BlockSpec((1, TILE, POOL_WIDTH), lambda bi, i: (bi, i, 0))),
        compiler_params=_cparams(("arbitrary", "arbitrary")),
        name="ab_proj",
    )(h, gain.reshape(1, d), mods, w_in.astype(BF16), bd, qkg)


NA_QROWS = TILE // GRID_W
NA_KTILES = 3
NA_NLOC = NA_KTILES * TILE
NA_NKEY = NA_NLOC + CTX_LEN


def _na_bias_tables(rpb, rows):
    n_r, n_c = 2 * NA_KH - 1, 2 * NA_KW - 1
    col = np.arange(GRID_W)
    c_start = np.clip(col - NA_KW // 2, 0, GRID_W - NA_KW)
    valid_c = (col[None, :] >= c_start[:, None]) & (col[None, :] < c_start[:, None] + NA_KW)
    c_rel = np.clip(col[None, :] - col[:, None] + NA_KW - 1, 0, n_c - 1)
    sel_c = (c_rel.reshape(-1)[None, :] == np.arange(n_c)[:, None]).astype(np.float32)
    by_col = jnp.einsum("hrc,cn->hrn", rpb.astype(F32), sel_c, precision=lax.Precision.HIGHEST)
    n_qt = rows // NA_QROWS
    qr, kr = np.arange(NA_QROWS), np.arange(NA_KTILES * NA_QROWS)
    tabs = []
    for jq in (0, 1, n_qt - 1):
        w0 = NA_QROWS * min(max(jq - 1, 0), n_qt - NA_KTILES)
        q_row = NA_QROWS * jq + qr
        key_row = w0 + kr
        r_start = np.clip(q_row - NA_KH // 2, 0, rows - NA_KH)
        valid_r = (key_row[None, :] >= r_start[:, None]) & (key_row[None, :] < r_start[:, None] + NA_KH)
        r_rel = np.clip(key_row[None, :] - q_row[:, None] + NA_KH - 1, 0, n_r - 1)
        sel_r = (r_rel.reshape(-1)[:, None] == np.arange(n_r)[None, :]).astype(np.float32)
        t = jnp.einsum("xr,hrn->hxn", sel_r, by_col, precision=lax.Precision.HIGHEST)
        t = t.reshape(NA_HEADS, NA_QROWS, NA_KTILES * NA_QROWS, GRID_W, GRID_W)
        t = t.transpose(0, 1, 3, 2, 4).reshape(NA_HEADS, TILE, NA_NLOC)
        valid = (valid_r[:, None, :, None] & valid_c[None, :, None, :]).reshape(TILE, NA_NLOC)
        tabs.append(jnp.where(valid[None], t, NEG_BIG))
    tabs = [jnp.full_like(tabs[0], NEG_BIG)] + tabs
    loc = jnp.stack(tabs)
    return jnp.concatenate([loc, jnp.zeros(loc.shape[:3] + (CTX_LEN,), F32)], axis=-1)


def _na_kernel(q_ref, k0_ref, k1_ref, k2_ref, kc_ref, v0_ref, v1_ref, v2_ref, vc_ref, bias_ref, o_ref):
    q = q_ref[0]
    kall = jnp.concatenate([k0_ref[0], k1_ref[0], k2_ref[0], kc_ref[0]], axis=0)
    vall = jnp.concatenate([v0_ref[0], v1_ref[0], v2_ref[0], vc_ref[0]], axis=0)
    lane = lax.broadcasted_iota(jnp.int32, q.shape, 1)
    scale = NA_HEAD_DIM ** -0.5
    outs = []
    for hh in range(2):
        in_head = (lane >= hh * NA_HEAD_DIM) & (lane < (hh + 1) * NA_HEAD_DIM)
        qh = jnp.where(in_head, q, jnp.zeros_like(q)) * jnp.asarray(scale, BF16)
        s = lax.dot_general(qh, kall, (((1,), (1,)), ((), ())), preferred_element_type=F32)
        s = s + bias_ref[0, hh]
        m = jnp.max(s, axis=-1, keepdims=True)
        p = jnp.exp(s - m)
        l = jnp.sum(p, axis=-1, keepdims=True)
        o = jnp.dot(p.astype(BF16), vall, preferred_element_type=F32)
        outs.append(o / l)
    o_ref[0] = jnp.where(lane < NA_HEAD_DIM, outs[0], outs[1]).astype(BF16)


def na_attention(qk, v, bias):
    b, lt, _ = qk.shape
    nt = lt // TILE
    n_qt = nt - 1
    assert NA_HEAD_DIM ** -0.5 == 0.125
    hp_n = NA_HEADS // 2
    kblk = NA_WIDTH // LANES

    def w0(j):
        return 1 + jnp.clip(j - 2, 0, n_qt - NA_KTILES)

    def kind(j):
        return jnp.where(j == 0, 0, jnp.where(j == 1, 1, jnp.where(j == nt - 1, 3, 2)))

    qspec = pl.BlockSpec((1, TILE, LANES), lambda j, hp, bi: (bi, j, hp))
    kspecs = [pl.BlockSpec((1, TILE, LANES), functools.partial(lambda j, hp, bi, t: (bi, w0(j) + t, kblk + hp), t=t))
              for t in range(NA_KTILES)]
    kcspec = pl.BlockSpec((1, TILE, LANES), lambda j, hp, bi: (bi, 0, kblk + hp))
    vspecs = [pl.BlockSpec((1, TILE, LANES), functools.partial(lambda j, hp, bi, t: (bi, w0(j) + t, hp), t=t))
              for t in range(NA_KTILES)]
    vcspec = pl.BlockSpec((1, TILE, LANES), lambda j, hp, bi: (bi, 0, hp))
    bspec = pl.BlockSpec((1, 2, TILE, NA_NKEY), lambda j, hp, bi: (kind(j), hp, 0, 0))
    return pl.pallas_call(
        _na_kernel,
        out_shape=jax.ShapeDtypeStruct((b, lt, NA_WIDTH), BF16),
        grid=(nt, hp_n, b),
        in_specs=[qspec] + kspecs + [kcspec] + vspecs + [vcspec, bspec],
        out_specs=pl.BlockSpec((1, TILE, LANES), lambda j, hp, bi: (bi, j, hp)),
        compiler_params=_cparams(("arbitrary", "arbitrary", "arbitrary")),
        name="na_attention",
    )(qk, qk, qk, qk, qk, v, v, v, v, bias)


POOL_HALO = 8
POOL_ROWS = TILE + 2 * POOL_HALO


def _pool_kernel(cur_ref, prev_ref, next_ref, pw_ref, ps_ref, o_ref, xp_ref, *, n_tiles):
    i = pl.program_id(1)
    has_prev = (i >= 2).astype(F32)
    has_next = ((i >= 1) & (i < n_tiles - 1)).astype(F32)
    xp_ref[0:POOL_HALO, :] = prev_ref[0] * has_prev
    xp_ref[POOL_HALO:POOL_HALO + TILE, :] = cur_ref[0]
    xp_ref[POOL_HALO + TILE:POOL_ROWS, :] = next_ref[0] * has_next
    row = lax.broadcasted_iota(jnp.int32, (TILE, POOL_GC), 0)
    tpos = jnp.where(i == 0, row, row + (i - 1) * TILE)
    seq_len = jnp.where(i == 0, CTX_LEN, (n_tiles - 1) * TILE)
    for g, w in enumerate(POOL_WINDOWS):
        cols = slice(g * POOL_GC, (g + 1) * POOL_GC)
        x = xp_ref[:, cols]
        s = x + pltpu.roll(x, 1, 0)
        half = 1
        while 2 * half < w:
            s = pltpu.roll(s, half, 0) + pltpu.roll(s, POOL_ROWS - half, 0)
            half *= 2
        s = s[POOL_HALO:POOL_HALO + TILE]
        cnt = jnp.minimum(w // 2, tpos) + jnp.minimum(w // 2, seq_len - tpos)
        pooled = s / cnt.astype(F32) - cur_ref[0, :, cols]
        y = jnp.dot(pooled.astype(BF16), pw_ref[g], preferred_element_type=F32)
        o_ref[0, :, cols] = (y * ps_ref[:, cols]).astype(BF16)


def multiscale_pool(u, pool_w, pool_scale):
    b, lt, c = u.shape
    nt = lt // TILE
    hb = TILE // POOL_HALO
    nhb = lt // POOL_HALO
    assert POOL_HALO >= max(POOL_WINDOWS) // 2 and POOL_HALO == SUBLANES
    return pl.pallas_call(
        functools.partial(_pool_kernel, n_tiles=nt),
        out_shape=jax.ShapeDtypeStruct((b, lt, c), BF16),
        grid=(b, nt),
        in_specs=[pl.BlockSpec((1, TILE, c), lambda bi, i: (bi, i, 0)),
                  pl.BlockSpec((1, POOL_HALO, c), lambda bi, i: (bi, jnp.maximum(i * hb - 1, 0), 0)),
                  pl.BlockSpec((1, POOL_HALO, c), lambda bi, i: (bi, jnp.minimum((i + 1) * hb, nhb - 1), 0)),
                  pl.BlockSpec((len(POOL_WINDOWS), POOL_GC, POOL_GC), lambda bi, i: (0, 0, 0)),
                  pl.BlockSpec((1, c), lambda bi, i: (0, 0))],
        out_specs=pl.BlockSpec((1, TILE, c), lambda bi, i: (bi, i, 0)),
        scratch_shapes=[pltpu.VMEM((POOL_ROWS, c), F32)],
        compiler_params=_cparams(("arbitrary", "arbitrary")),
        name="multiscale_pool",
    )(u, u, u, pool_w.astype(BF16), pool_scale.reshape(1, c).astype(F32))


def _out_proj_kernel(a_ref, b_ref, h_ref, mods_ref, w_ref, o_ref):
    ka = a_ref.shape[-1]
    y = jnp.dot(a_ref[0], w_ref[:ka, :], preferred_element_type=F32)
    y = y + jnp.dot(b_ref[0], w_ref[ka:, :], preferred_element_type=F32)
    o_ref[0] = h_ref[0] + _mod_row(mods_ref, 2) * y


def out_proj_residual(a, bb, h, mods, w):
    b, lt, d = h.shape
    nt = lt // TILE
    ka, kb = a.shape[-1], bb.shape[-1]
    return pl.pallas_call(
        _out_proj_kernel,
        out_shape=jax.ShapeDtypeStruct((b, lt, d), F32),
        grid=(b, nt),
        in_specs=[pl.BlockSpec((1, TILE, ka), lambda bi, i: (bi, i, 0)),
                  pl.BlockSpec((1, TILE, kb), lambda bi, i: (bi, i, 0)),
                  pl.BlockSpec((1, TILE, d), lambda bi, i: (bi, i, 0)),
                  pl.BlockSpec((1, 1, 6, d), lambda bi, i: (bi, jnp.minimum(i, 1), 0, 0)),
                  pl.BlockSpec((ka + kb, d), lambda bi, i: (0, 0))],
        out_specs=pl.BlockSpec((1, TILE, d), lambda bi, i: (bi, i, 0)),
        compiler_params=_cparams(("arbitrary", "arbitrary")),
        name="out_proj_residual",
    )(a, bb, h, mods, w.astype(BF16))


def _moe_pre_kernel(h_ref, g_ref, mods_ref, rw_ref, rb_ref, y_ref, lg_ref):
    x = _norm_mod(h_ref[0], g_ref[...], _mod_row(mods_ref, 3), _mod_row(mods_ref, 4))
    y_ref[0] = x.astype(BF16)
    lg_ref[0] = jnp.dot(x, rw_ref[...], preferred_element_type=F32,
                        precision=lax.Precision.HIGHEST) + rb_ref[...]


def moe_pre(h, gain, mods, router_w, router_b, tile0, ctx_tiles):
    b, lt, d = h.shape
    nt = lt // TILE - tile0
    e = router_w.shape[1]
    return pl.pallas_call(
        _moe_pre_kernel,
        out_shape=(jax.ShapeDtypeStruct((b, nt * TILE, d), BF16),
                   jax.ShapeDtypeStruct((b, nt * TILE, e), F32)),
        grid=(b, nt),
        in_specs=[pl.BlockSpec((1, TILE, d), lambda bi, i: (bi, i + tile0, 0)),
                  pl.BlockSpec((1, d), lambda bi, i: (0, 0)),
                  pl.BlockSpec((1, 1, 6, d), lambda bi, i: (bi, jnp.minimum(i + tile0 + 1 - ctx_tiles, 1), 0, 0)),
                  pl.BlockSpec((d, e), lambda bi, i: (0, 0)),
                  pl.BlockSpec((1, e), lambda bi, i: (0, 0))],
        out_specs=(pl.BlockSpec((1, TILE, d), lambda bi, i: (bi, i, 0)),
                   pl.BlockSpec((1, TILE, e), lambda bi, i: (bi, i, 0))),
        compiler_params=_cparams(("arbitrary", "arbitrary")),
        name="moe_pre",
    )(h, gain.reshape(1, d), mods, router_w, router_b.reshape(1, e))


MOE_PERM_W = 256


def _moe_ffn_kernel(te_ref, tv_ref, x_ref, win_ref, bg_ref, bl_ref, wo_ref, bo_ref, perm_ref, o_ref,
                    wg_s, wl_s, wo_s):
    t = pl.program_id(0)
    valid = tv_ref[t] > 0
    new_expert = (t == 0) | (te_ref[t] != te_ref[jnp.maximum(t - 1, 0)])

    @pl.when(valid & new_expert)
    def _():
        half = MOE_PERM_W // 2
        for cb in range(win_ref.shape[3] // MOE_PERM_W):
            w = win_ref[0, 0, :, cb * MOE_PERM_W:(cb + 1) * MOE_PERM_W].astype(BF16)
            wp = jnp.dot(w, perm_ref[...], preferred_element_type=F32).astype(BF16)
            wg_s[:, cb * half:(cb + 1) * half] = wp[:, :half]
            wl_s[:, cb * half:(cb + 1) * half] = wp[:, half:]
        wo_s[...] = wo_ref[0, 0].astype(BF16)

    @pl.when(jnp.logical_not(valid))
    def _():
        o_ref[...] = jnp.zeros_like(o_ref)

    @pl.when(valid)
    def _():
        x = x_ref[...].astype(BF16)
        hg = jnp.dot(x, wg_s[...], preferred_element_type=F32) + bg_ref[0]
        hl = jnp.dot(x, wl_s[...], preferred_element_type=F32) + bl_ref[0]
        glu = jnp.minimum(hg, SWIGLU_LIMIT)
        lin = jnp.clip(hl, -SWIGLU_LIMIT, SWIGLU_LIMIT)
        act = glu * jax.nn.sigmoid(SWIGLU_ALPHA * glu) * (lin + 1.0)
        o_ref[...] = (jnp.dot(act.astype(BF16), wo_s[...], preferred_element_type=F32)
                      + bo_ref[0]).astype(o_ref.dtype)


def moe_grouped_ffn(x_sorted, tile_expert, tile_valid, layer, w_in, bg, bl, w_out, bo):
    p, d = x_sorted.shape
    _, e, _, f2 = w_in.shape
    f = f2 // 2
    nt = p // MOE_TM
    assert f2 % MOE_PERM_W == 0 and MOE_PERM_W % (2 * LANES) == 0
    src = jnp.arange(MOE_PERM_W)
    dst = jnp.where(src % 2 == 0, src // 2, MOE_PERM_W // 2 + src // 2)
    perm = (dst[:, None] == jnp.arange(MOE_PERM_W)[None, :]).astype(BF16)
    grid_spec = pltpu.PrefetchScalarGridSpec(
        num_scalar_prefetch=2,
        grid=(nt,),
        in_specs=[pl.BlockSpec((MOE_TM, d), lambda t, te, tv: (t * tv[t], 0)),
                  pl.BlockSpec((1, 1, d, f2), lambda t, te, tv: (layer, te[t], 0, 0)),
                  pl.BlockSpec((1, 1, f), lambda t, te, tv: (te[t], 0, 0)),
                  pl.BlockSpec((1, 1, f), lambda t, te, tv: (te[t], 0, 0)),
                  pl.BlockSpec((1, 1, f, d), lambda t, te, tv: (layer, te[t], 0, 0)),
                  pl.BlockSpec((1, 1, d), lambda t, te, tv: (te[t], 0, 0)),
                  pl.BlockSpec((MOE_PERM_W, MOE_PERM_W), lambda t, te, tv: (0, 0))],
        out_specs=pl.BlockSpec((MOE_TM, d), lambda t, te, tv: (t, 0)),
        scratch_shapes=[pltpu.VMEM((d, f), BF16), pltpu.VMEM((d, f), BF16), pltpu.VMEM((f, d), BF16)],
    )
    return pl.pallas_call(
        _moe_ffn_kernel,
        out_shape=jax.ShapeDtypeStruct((p, d), BF16),
        grid_spec=grid_spec,
        compiler_params=pltpu.CompilerParams(dimension_semantics=("arbitrary",),
                                             vmem_limit_bytes=MOE_FFN_VMEM_LIMIT),
        name="moe_grouped_ffn",
    )(tile_expert, tile_valid, x_sorted, w_in, bg, bl, w_out, bo, perm)


def _moe_combine_kernel(z_ref, gt_ref, h_ref, mods_ref, o_ref):
    gt = gt_ref[0]
    acc = gt[:, 0:1] * z_ref[0, 0].astype(F32)
    for k in range(1, TOP_K):
        acc = acc + gt[:, k:k + 1] * z_ref[k, 0].astype(F32)
    o_ref[0] = h_ref[0] + _mod_row(mods_ref, 5) * acc


def moe_combine(z4, gates, h, mods, tile0, ctx_tiles):
    k, b, lp, d = z4.shape
    nt = lp // TILE
    return pl.pallas_call(
        _moe_combine_kernel,
        out_shape=jax.ShapeDtypeStruct((b, lp, d), F32),
        grid=(b, nt),
        in_specs=[pl.BlockSpec((k, 1, TILE, d), lambda bi, i: (0, bi, i, 0)),
                  pl.BlockSpec((1, TILE, k), lambda bi, i: (bi, i, 0)),
                  pl.BlockSpec((1, TILE, d), lambda bi, i: (bi, i + tile0, 0)),
                  pl.BlockSpec((1, 1, 6, d), lambda bi, i: (bi, jnp.minimum(i + tile0 + 1 - ctx_tiles, 1), 0, 0))],
        out_specs=pl.BlockSpec((1, TILE, d), lambda bi, i: (bi, i, 0)),
        compiler_params=_cparams(("arbitrary", "arbitrary")),
        name="moe_combine",
    )(z4, gates, h, mods)


DISPATCH_T = 512
DISPATCH_BLK = SUBLANES
DISPATCH_ROWS = -(-(DISPATCH_T * TOP_K + N_EXPERTS * (DISPATCH_BLK - 1)) // (2 * SUBLANES)) * (2 * SUBLANES)


def _moe_dispatch_kernel(nblk_ref, dst_ref, ntail_ref, tail_ref, y_ref, idx_ref, off_ref, base_ref,
                         xs_ref, slot_ref, local_ref, zero_ref, sem, zero_sem):
    t = pl.program_id(0)
    last_t = pl.num_programs(0) - 1
    dt = DISPATCH_T
    idx = idx_ref[0]
    e_iota = lax.broadcasted_iota(jnp.int32, (N_EXPERTS, dt), 0)
    picked = [e_iota == idx[k:k + 1, :] for k in range(TOP_K)]
    cnt = sum(jnp.where(p, 1.0, 0.0) for p in picked).astype(BF16)
    r0 = lax.broadcasted_iota(jnp.int32, (dt, dt), 0)
    r1 = lax.broadcasted_iota(jnp.int32, (dt, dt), 1)
    earlier = jnp.where(r0 < r1, 1.0, 0.0).astype(BF16)
    rank = jnp.dot(cnt, earlier, preferred_element_type=F32)
    local_row = rank + off_ref[0].astype(F32)
    global_row = rank + base_ref[0].astype(F32)
    rows = lax.broadcasted_iota(jnp.int32, (DISPATCH_ROWS, dt), 0)
    place = None
    for k in range(TOP_K):
        pos = jnp.sum(jnp.where(picked[k], local_row, 0.0), axis=0, keepdims=True).astype(jnp.int32)
        slot_ref[0, k:k + 1, :] = jnp.sum(jnp.where(picked[k], global_row, 0.0), axis=0,
                                          keepdims=True).astype(jnp.int32)
        hit = rows == pos
        place = hit if place is None else (place | hit)
    placement = jnp.where(place, 1.0, 0.0).astype(BF16)
    buf = lax.rem(t, 2)
    local_ref[buf] = jnp.dot(placement, y_ref[...], preferred_element_type=F32)

    def block_copy(step, j):
        src_buf = jnp.where(step == t, buf, 1 - buf)
        src = local_ref.at[src_buf, pl.ds(pl.multiple_of(j * DISPATCH_BLK, DISPATCH_BLK), DISPATCH_BLK), :]
        dst = xs_ref.at[pl.ds(pl.multiple_of(dst_ref[step, j] * DISPATCH_BLK, DISPATCH_BLK), DISPATCH_BLK), :]
        return pltpu.make_async_copy(src, dst, sem)

    def tail_copy(e, j):
        blk = tail_ref[e] + j
        dst = xs_ref.at[pl.ds(pl.multiple_of(blk * DISPATCH_BLK, DISPATCH_BLK), DISPATCH_BLK), :]
        return pltpu.make_async_copy(zero_ref, dst, zero_sem)

    def start_all(copy, n):
        lax.fori_loop(0, n, lambda j, c: (copy(j).start(), c)[1], 0)

    def wait_all(copy, n):
        lax.fori_loop(0, n, lambda j, c: (copy(j).wait(), c)[1], 0)

    @pl.when(t > 0)
    def _():
        wait_all(functools.partial(block_copy, t - 1), nblk_ref[t - 1])

    start_all(functools.partial(block_copy, t), nblk_ref[t])

    @pl.when(t == last_t)
    def _():
        wait_all(functools.partial(block_copy, t), nblk_ref[t])

    @pl.when(t == 0)
    def _():
        zero_ref[...] = jnp.zeros_like(zero_ref)
        for e in range(N_EXPERTS + 1):
            start_all(functools.partial(tail_copy, e), ntail_ref[e])
        for e in range(N_EXPERTS + 1):
            wait_all(functools.partial(tail_copy, e), ntail_ref[e])


def moe_dispatch(y2, idx_t, nblk, dst_blk, ntail, tail_blk, loc_off, base, p_rows):
    n, d = y2.shape
    nt = n // DISPATCH_T
    grid_spec = pltpu.PrefetchScalarGridSpec(
        num_scalar_prefetch=4,
        grid=(nt,),
        in_specs=[pl.BlockSpec((DISPATCH_T, d), lambda t, *_: (t, 0)),
                  pl.BlockSpec((1, TOP_K, DISPATCH_T), lambda t, *_: (t, 0, 0)),
                  pl.BlockSpec((1, N_EXPERTS, 1), lambda t, *_: (t, 0, 0)),
                  pl.BlockSpec((1, N_EXPERTS, 1), lambda t, *_: (t, 0, 0))],
        out_specs=(pl.BlockSpec(memory_space=pl.ANY),
                   pl.BlockSpec((1, TOP_K, DISPATCH_T), lambda t, *_: (t, 0, 0))),
        scratch_shapes=[pltpu.VMEM((2, DISPATCH_ROWS, d), F32), pltpu.VMEM((DISPATCH_BLK, d), F32),
                        pltpu.SemaphoreType.DMA(()), pltpu.SemaphoreType.DMA(())],
    )
    return pl.pallas_call(
        _moe_dispatch_kernel,
        out_shape=(jax.ShapeDtypeStruct((p_rows, d), F32),
                   jax.ShapeDtypeStruct((nt, TOP_K, DISPATCH_T), jnp.int32)),
        grid_spec=grid_spec,
        compiler_params=_cparams(("arbitrary",)),
        name="moe_dispatch",
    )(nblk, dst_blk, ntail, tail_blk, y2, idx_t, loc_off, base)


def moe_layer(h, gain, mods, router_w, router_b, layer, w_in, b_in, w_out, b_out, tile0, ctx_tiles):
    b, lt, d = h.shape
    y, logits = moe_pre(h, gain, mods, router_w, router_b, tile0, ctx_tiles)
    lp = y.shape[1]
    n = b * lp
    y2 = y.reshape(n, d)
    top_val, top_idx = lax.top_k(logits.reshape(n, N_EXPERTS), TOP_K)
    gates = jax.nn.softmax(top_val, axis=-1)

    assert n % DISPATCH_T == 0
    t_n = n // DISPATCH_T
    blk = DISPATCH_BLK
    experts = jnp.arange(N_EXPERTS, dtype=jnp.int32)
    idx_t = top_idx.astype(jnp.int32).reshape(t_n, DISPATCH_T, TOP_K).transpose(0, 2, 1)
    tile_cnt = jnp.sum((idx_t[..., None] == experts).astype(jnp.int32), axis=(1, 2))
    seg = (tile_cnt + blk - 1) // blk * blk
    loc_end = jnp.cumsum(seg, axis=1)
    loc_off = loc_end - seg
    nblk = loc_end[:, -1] // blk
    tot = jnp.sum(seg, axis=0)
    padded = (tot + MOE_TM - 1) // MOE_TM * MOE_TM
    ends = jnp.cumsum(padded)
    starts = ends - padded
    base = starts[None, :] + jnp.cumsum(seg, axis=0) - seg
    row0 = jnp.arange(DISPATCH_ROWS // blk, dtype=jnp.int32) * blk
    seg_of = jnp.sum((loc_end[:, None, :] <= row0[None, :, None]).astype(jnp.int32), axis=2)
    shift = jnp.sum(jnp.where(seg_of[..., None] == experts, (base - loc_off)[:, None, :], 0), axis=2)
    dst_blk = (shift + row0[None, :]) // blk
    n_tiles = -(-(n * TOP_K + t_n * N_EXPERTS * (blk - 1)) // MOE_TM) + N_EXPERTS
    tile_start = jnp.arange(n_tiles, dtype=jnp.int32) * MOE_TM
    tile_expert = jnp.minimum(jnp.sum((tile_start[:, None] >= ends[None, :]).astype(jnp.int32), axis=1),
                              N_EXPERTS - 1)
    tile_valid = (tile_start < ends[-1]).astype(jnp.int32)

    p_rows = n_tiles * MOE_TM
    gap_len = jnp.concatenate([padded - tot, p_rows - ends[-1:]]) // blk
    gap_start = jnp.concatenate([starts + tot, ends[-1:]]) // blk
    x_sorted, slot = moe_dispatch(y2, idx_t, nblk, dst_blk, gap_len, gap_start,
                                  loc_off[..., None], base[..., None], p_rows)
    bg = b_in[layer, :, 0::2].reshape(N_EXPERTS, 1, D_FF)
    bl = b_in[layer, :, 1::2].reshape(N_EXPERTS, 1, D_FF)
    z = moe_grouped_ffn(x_sorted, tile_expert, tile_valid, layer, w_in, bg, bl, w_out,
                        b_out[layer].reshape(N_EXPERTS, 1, d))
    slot_kn = slot.transpose(1, 0, 2).reshape(TOP_K, n)
    z4 = z.at[slot_kn].get(mode="promise_in_bounds").reshape(TOP_K, b, lp, d)
    return moe_combine(z4, gates.reshape(b, lp, TOP_K), h, mods, tile0, ctx_tiles)


def _softplus(z):
    return jnp.maximum(z, 0.0) + jnp.log(1.0 + jnp.exp(-jnp.abs(z)))


def _rwkv_feat_kernel(h_ref, hp_ref, hn_ref, g_ref, mods_ref, mu_ref, wr_ref, wk_ref, wv_ref,
                      w1_ref, w2_ref, a1_ref, a2_ref, g1_ref, g2_ref, w0_ref, a0_ref, kk_ref, ka_ref,
                      bd_ref, r_out, v_out, kk_out, g_out, lw_out, kd_out, a_out, *, n_tiles):
    i = pl.program_id(1)
    gain, shift, scale = g_ref[...], _mod_row(mods_ref, 0), _mod_row(mods_ref, 1)
    x = _norm_mod(h_ref[0], gain, shift, scale)
    has_prev = (i >= 2).astype(F32)
    has_next = ((i >= 1) & (i < n_tiles - 1)).astype(F32)
    x_before = _norm_mod(hp_ref[0], gain, shift, scale)[SUBLANES - 1:SUBLANES] * has_prev
    x_after = _norm_mod(hn_ref[0], gain, shift, scale)[0:1] * has_next
    row = lax.broadcasted_iota(jnp.int32, x.shape, 0)
    prev = jnp.where(row == 0, x_before, pltpu.roll(x, 1, 0))
    nxt = jnp.where(row == TILE - 1, x_after, pltpu.roll(x, TILE - 1, 0))
    xx = 0.5 * (prev + nxt) - x
    xr, xw, xk, xv, xa, xg = (x + xx * mu_ref[pl.ds(j, 1), :] for j in range(6))

    def mm(a, w):
        return jnp.dot(a.astype(BF16), w, preferred_element_type=F32)

    r = mm(xr, wr_ref[...])
    k = mm(xk, wk_ref[...])
    v = mm(xv, wv_ref[...])
    g = mm(jax.nn.sigmoid(mm(xg, g1_ref[...])), g2_ref[...])
    kk = k * kk_ref[...]
    kk = kk * lax.rsqrt(mm(kk * kk, bd_ref[...]) + 1e-12)
    r_out[0] = r.astype(BF16)
    v_out[0] = v.astype(BF16)
    kk_out[0] = kk.astype(BF16)
    g_out[0] = g.astype(BF16)
    tw = jnp.tanh(mm(xw, w1_ref[...]))
    av = mm(xa, a1_ref[...])
    lane = lax.broadcasted_iota(jnp.int32, tw.shape, 1)
    half = tw.shape[1] // 2
    for d in range(2):
        in_dir = (lane >= d * half) & (lane < (d + 1) * half)
        wz = w0_ref[pl.ds(d, 1), :] + mm(jnp.where(in_dir, tw, 0.0), w2_ref[...])
        w_log = -_softplus(-wz) - 0.5
        lw_out[d, 0] = -jnp.exp(w_log)
        a = jax.nn.sigmoid(a0_ref[pl.ds(d, 1), :] + mm(jnp.where(in_dir, av, 0.0), a2_ref[...]))
        kd_out[d, 0] = (k * (1.0 + (a - 1.0) * ka_ref[...])).astype(BF16)
        a_out[d, 0] = a.astype(BF16)


def rwkv_features(h, gain, mods, mu, w_r, w_k, w_v, w0, w1, w2, a0, a1, a2, g1, g2, k_k, k_a):
    b, lt, d = h.shape
    nt = lt // TILE
    hb = TILE // SUBLANES
    nhb = lt // SUBLANES
    hid = jnp.arange(d) // RW_HEAD_DIM
    bd = (hid[:, None] == hid[None, :]).astype(BF16)
    w1c = jnp.concatenate([w1[0], w1[1]], axis=1).astype(BF16)
    a1c = jnp.concatenate([a1[0], a1[1]], axis=1).astype(BF16)
    w2c = jnp.concatenate([w2[0], w2[1]], axis=0).astype(BF16)
    a2c = jnp.concatenate([a2[0], a2[1]], axis=0).astype(BF16)
    full = lambda shape: pl.BlockSpec(shape, lambda bi, i: (0,) * len(shape))
    tile_spec = pl.BlockSpec((1, TILE, d), lambda bi, i: (bi, i, 0))
    dir_spec = pl.BlockSpec((2, 1, TILE, d), lambda bi, i: (0, bi, i, 0))
    seq_bf = jax.ShapeDtypeStruct((b, lt, d), BF16)
    return pl.pallas_call(
        functools.partial(_rwkv_feat_kernel, n_tiles=nt),
        out_shape=(seq_bf, seq_bf, seq_bf, seq_bf,
                   jax.ShapeDtypeStruct((2, b, lt, d), F32),
                   jax.ShapeDtypeStruct((2, b, lt, d), BF16),
                   jax.ShapeDtypeStruct((2, b, lt, d), BF16)),
        grid=(b, nt),
        in_specs=[tile_spec,
                  pl.BlockSpec((1, SUBLANES, d), lambda bi, i: (bi, jnp.maximum(i * hb - 1, 0), 0)),
                  pl.BlockSpec((1, SUBLANES, d), lambda bi, i: (bi, jnp.minimum((i + 1) * hb, nhb - 1), 0)),
                  full((1, d)),
                  pl.BlockSpec((1, 1, 6, d), lambda bi, i: (bi, jnp.minimum(i, 1), 0, 0)),
                  full((6, d)), full((d, d)), full((d, d)), full((d, d)),
                  full(w1c.shape), full(w2c.shape), full(a1c.shape), full(a2c.shape),
                  full(g1.shape), full(g2.shape), full((2, d)), full((2, d)), full((1, d)), full((1, d)),
                  full((d, d))],
        out_specs=(tile_spec, tile_spec, tile_spec, tile_spec, dir_spec, dir_spec, dir_spec),
        compiler_params=_cparams(("arbitrary", "arbitrary")),
        name="rwkv_features",
    )(h, h, h, gain.reshape(1, d), mods, mu, w_r.astype(BF16), w_k.astype(BF16), w_v.astype(BF16),
      w1c, w2c, a1c, a2c, g1.astype(BF16), g2.astype(BF16), w0, a0, k_k.reshape(1, d), k_a.reshape(1, d), bd)


SCAN_C = 64
SCAN_G = 4
SCAN_GW = SCAN_G * RW_HEAD_DIM


def _scan_chunk_inputs(reverse, r_ref, v_ref, kk_ref, lw_ref, kd_ref, a_ref):
    cc = SCAN_C
    ti = lax.broadcasted_iota(jnp.int32, (cc, cc), 0)
    si = lax.broadcasted_iota(jnp.int32, (cc, cc), 1)
    before_incl = (si >= ti) if reverse else (si <= ti)
    lw = lw_ref[0, 0]
    cl = jnp.dot(before_incl.astype(F32), lw, preferred_element_type=F32, precision=lax.Precision.HIGHEST)
    g_in = jnp.exp(cl)
    g_ex = jnp.exp(cl - lw)
    g_inv = jnp.exp(-cl)
    g_end = jnp.exp(jnp.sum(lw, axis=0, keepdims=True))
    kk = kk_ref[0].astype(F32)
    rt = (r_ref[0].astype(F32) * g_in).astype(BF16)
    at = (-kk * g_ex).astype(BF16)
    bt = (kk * a_ref[0, 0].astype(F32) * g_inv).astype(BF16)
    kt = (kd_ref[0, 0].astype(F32) * g_inv).astype(BF16)
    return rt, at, bt, kt, v_ref[0], g_end


def _rwkv_scan_kernel(rf_ref, vf_ref, kkf_ref, rb_ref, vb_ref, kkb_ref, lwf_ref, kdf_ref, af_ref,
                      lwb_ref, kdb_ref, ab_ref, yf_ref, yb_ref, h_ref):
    @pl.when(pl.program_id(1) == 0)
    def _():
        h_ref[...] = jnp.zeros_like(h_ref)

    cc, gw = SCAN_C, SCAN_GW
    ops = (_scan_chunk_inputs(False, rf_ref, vf_ref, kkf_ref, lwf_ref, kdf_ref, af_ref),
           _scan_chunk_inputs(True, rb_ref, vb_ref, kkb_ref, lwb_ref, kdb_ref, ab_ref))
    y_refs = (yf_ref, yb_ref)

    tg = lax.broadcasted_iota(jnp.int32, (cc, gw), 0)
    sg = lax.broadcasted_iota(jnp.int32, (cc, gw), 1) % cc
    m_strict = (sg < tg, sg > tg)
    m_incl = (sg <= tg, sg >= tg)
    eye_g = (sg == tg).astype(F32)
    bi = lax.broadcasted_iota(jnp.int32, (gw, gw), 0) // cc
    bj = lax.broadcasted_iota(jnp.int32, (gw, gw), 1) // cc
    blk = bi == bj

    def bdiag(x):
        return jnp.where(blk, jnp.concatenate([x] * SCAN_G, axis=0), jnp.zeros((), x.dtype))

    def mm(a, b):
        return jnp.dot(a, b, preferred_element_type=F32)

    def mm_nt(a, b):
        return lax.dot_general(a, b, (((1,), (1,)), ((), ())), preferred_element_type=F32)

    n_g = RW_HEADS // SCAN_G
    chains = [(d, g) for d in range(2) for g in range(n_g)]
    cols = [slice(g * gw, (g + 1) * gw) for _, g in chains]
    rt, at, bt, kt, v, g_end = ([ops[d][j][:, cols[i]] for i, (d, _) in enumerate(chains)] for j in range(6))
    idx = range(len(chains))
    h0 = [h_ref[i] for i in idx]
    h0b = [x.astype(BF16) for x in h0]
    v_bd = [bdiag(x) for x in v]
    mats = [mm_nt(jnp.concatenate([at[i], rt[i]], axis=0),
                  jnp.concatenate([bdiag(bt[i]), bdiag(kt[i])], axis=0)) for i in idx]
    a_ab = [jnp.where(m_strict[chains[i][0]], mats[i][:cc, :gw], 0.0) for i in idx]
    a_ak = [jnp.where(m_strict[chains[i][0]], mats[i][:cc, gw:], 0.0).astype(BF16) for i in idx]
    m_rb = [jnp.where(m_incl[chains[i][0]], mats[i][cc:, :gw], 0.0).astype(BF16) for i in idx]
    m_rk = [jnp.where(m_incl[chains[i][0]], mats[i][cc:, gw:], 0.0).astype(BF16) for i in idx]
    from_h = [mm(jnp.concatenate([at[i], rt[i]], axis=0), h0b[i]) for i in idx]
    from_v = [mm(jnp.concatenate([a_ak[i], m_rk[i]], axis=0), v_bd[i]) for i in idx]
    pq = [from_h[i][:cc] + from_v[i][:cc] for i in idx]
    pw = [x.astype(BF16) for x in a_ab]
    s_acc = [eye_g + x for x in a_ab]
    pw = [mm(x, bdiag(x)).astype(BF16) for x in pw]
    n = 2
    while n < cc:
        last = 2 * n >= cc
        lhs = [s_acc[i].astype(BF16) if last else jnp.concatenate([pw[i], s_acc[i].astype(BF16)], axis=0)
               for i in idx]
        prod = [mm(lhs[i], bdiag(pw[i])) for i in idx]
        s_acc = [s_acc[i] + (prod[i] if last else prod[i][cc:]) for i in idx]
        if not last:
            pw = [prod[i][:cc].astype(BF16) for i in idx]
        n *= 2
    ub = [mm(s_acc[i].astype(BF16), bdiag(pq[i].astype(BF16))).astype(BF16) for i in idx]
    for i in idx:
        y_refs[chains[i][0]][0, :, cols[i]] = from_h[i][cc:] + from_v[i][cc:] + mm(m_rb[i], bdiag(ub[i]))
    for i in idx:
        upd = lax.dot_general(jnp.concatenate([bt[i], kt[i]], axis=0), jnp.concatenate([ub[i], v[i]], axis=0),
                              (((0,), (0,)), ((), ())), preferred_element_type=F32)
        ge = jnp.transpose(jnp.broadcast_to(g_end[i], (gw, gw)))
        h_ref[i] = ge * (h0[i] + jnp.where(blk, upd, 0.0))


def rwkv_scan(r, v, kk, lw, kd, a):
    b, lt, d = r.shape
    nc = lt // SCAN_C
    n_ctx = CTX_LEN // SCAN_C
    assert SCAN_C == RW_HEAD_DIM and CTX_LEN % SCAN_C == 0

    def rev_chunk(c):
        return jnp.where(c < n_ctx, n_ctx - 1 - c, nc - 1 + n_ctx - c)

    fwd = pl.BlockSpec((1, SCAN_C, d), lambda bi, c: (bi, c, 0))
    bwd = pl.BlockSpec((1, SCAN_C, d), lambda bi, c: (bi, rev_chunk(c), 0))
    fwd_dir = pl.BlockSpec((1, 1, SCAN_C, d), lambda bi, c: (0, bi, c, 0))
    bwd_dir = pl.BlockSpec((1, 1, SCAN_C, d), lambda bi, c: (1, bi, rev_chunk(c), 0))
    y_shape = jax.ShapeDtypeStruct((b, lt, d), F32)
    return pl.pallas_call(
        _rwkv_scan_kernel,
        out_shape=(y_shape, y_shape),
        grid=(b, nc),
        in_specs=[fwd, fwd, fwd, bwd, bwd, bwd, fwd_dir, fwd_dir, fwd_dir, bwd_dir, bwd_dir, bwd_dir],
        out_specs=(fwd, bwd),
        scratch_shapes=[pltpu.VMEM((2 * RW_HEADS // SCAN_G, SCAN_GW, SCAN_GW), F32)],
        compiler_params=_cparams(("arbitrary", "arbitrary")),
        name="rwkv_scan",
    )(r, v, kk, r, v, kk, lw, kd, a, lw, kd, a)


def _rwkv_out_kernel(yf_ref, yb_ref, r_ref, v_ref, g_ref, kd_ref, h_ref, mods_ref, lnw_ref, lnb_ref, rk_ref,
                     bd_ref, wo_ref, o_ref):
    def head_sum(x):
        return jnp.dot(x.astype(BF16), bd_ref[...], preferred_element_type=F32)

    y = yf_ref[0] + yb_ref[0]
    inv_n = 1.0 / RW_HEAD_DIM
    mean = head_sum(y) * inv_n
    yc = y - mean
    var = head_sum(yc * yc) * inv_n
    yn = yc * lax.rsqrt(var + RW_GN_EPS) * lnw_ref[...] + lnb_ref[...]
    r = r_ref[0].astype(F32)
    kd = kd_ref[0, 0].astype(F32) + kd_ref[1, 0].astype(F32)
    yn = yn + head_sum(r * kd * rk_ref[...]) * v_ref[0].astype(F32)
    out = jnp.dot((yn * g_ref[0].astype(F32)).astype(BF16), wo_ref[...], preferred_element_type=F32)
    o_ref[0] = h_ref[0] + _mod_row(mods_ref, 2) * out


def rwkv_readout(y_f, y_b, r, v, g, kd, h, mods, ln_w, ln_b, r_k, w_o):
    b, lt, d = h.shape
    nt = lt // TILE - 1
    hid = jnp.arange(d) // RW_HEAD_DIM
    bd = (hid[:, None] == hid[None, :]).astype(BF16)
    full = lambda shape: pl.BlockSpec(shape, lambda bi, i: (0,) * len(shape))
    tile_spec = pl.BlockSpec((1, TILE, d), lambda bi, i: (bi, i + 1, 0))
    dir_spec = pl.BlockSpec((2, 1, TILE, d), lambda bi, i: (0, bi, i + 1, 0))
    return pl.pallas_call(
        _rwkv_out_kernel,
        out_shape=jax.ShapeDtypeStruct((b, nt * TILE, d), F32),
        grid=(b, nt),
        in_specs=[tile_spec, tile_spec, tile_spec, tile_spec, tile_spec, dir_spec, tile_spec,
                  pl.BlockSpec((1, 1, 6, d), lambda bi, i: (bi, 1, 0, 0)),
                  full((1, d)), full((1, d)), full((1, d)), full((d, d)), full((d, d))],
        out_specs=pl.BlockSpec((1, TILE, d), lambda bi, i: (bi, i, 0)),
        compiler_params=_cparams(("arbitrary", "arbitrary")),
        name="rwkv_readout",
    )(y_f, y_b, r, v, g, kd, h, mods, ln_w.reshape(1, d), ln_b.reshape(1, d), r_k.reshape(1, d), bd,
      w_o.astype(BF16))


def kernel(x, c, ctx, c_ctx, ada_w, ada_b, norm_mix_g, norm_ffn_g, router_w, router_b, exp_w_in, exp_b_in,
           exp_w_out, exp_b_out, ab_w_in, na_q_g, na_k_g, na_rpb, pool_w, pool_scale, ab_w_out,
           rw_mu, rw_w_r, rw_w_k, rw_w_v, rw_w_o, rw_w0, rw_w1, rw_w2, rw_a0, rw_a1, rw_a2,
           rw_g1, rw_g2, rw_k_k, rw_k_a, rw_r_k, rw_ln_w, rw_ln_b):
    b, seq, d = x.shape
    depth = ada_w.shape[0]
    assert ctx.shape[1] == CTX_LEN == TILE and seq % TILE == 0 and d == D_MODEL
    assert depth == 2, "layer schedule below is written for [neighbourhood/pool layer, RWKV layer]"
    rows = seq // GRID_W
    assert rows >= NA_QROWS * NA_KTILES and NA_QROWS * NA_KTILES >= NA_QROWS + NA_KH - 1

    n_c = 1 + b
    n_c_pad = -(-n_c // SUBLANES) * SUBLANES
    cvec = jnp.concatenate([c_ctx[None], c, jnp.zeros((n_c_pad - n_c, d), F32)], axis=0)

    def layer_mods(layer):
        m = ada_mod(cvec, ada_w[layer], ada_b[layer])
        m_ctx = jnp.broadcast_to(m[0].reshape(1, 6, d), (b, 6, d))
        return jnp.stack([m_ctx, m[1:n_c].reshape(b, 6, d)], axis=1)

    def moe(h, layer, mods, tile0, ctx_tiles):
        return moe_layer(h, norm_ffn_g[layer], mods, router_w[layer], router_b[layer], layer, exp_w_in,
                         exp_b_in, exp_w_out, exp_b_out, tile0, ctx_tiles)

    h = jnp.concatenate([ctx, x], axis=1)

    mods = layer_mods(0)
    qk, v, u = ab_proj(h, norm_mix_g[0], mods, ab_w_in[0], na_q_g[0], na_k_g[0])
    o_na = na_attention(qk, v, _na_bias_tables(na_rpb[0], rows))
    o_pool = multiscale_pool(u, pool_w[0], pool_scale[0])
    h = out_proj_residual(o_na, o_pool, h, mods, ab_w_out[0])
    h = moe(h, 0, mods, 0, 1)

    mods = layer_mods(1)
    r, v, kk, g, lw, kd, a = rwkv_features(h, norm_mix_g[1], mods, rw_mu[0], rw_w_r[0], rw_w_k[0], rw_w_v[0],
                                           rw_w0[0], rw_w1[0], rw_w2[0], rw_a0[0], rw_a1[0], rw_a2[0],
                                           rw_g1[0], rw_g2[0], rw_k_k[0], rw_k_a[0])
    y_f, y_b = rwkv_scan(r, v, kk, lw, kd, a)
    h_lat = rwkv_readout(y_f, y_b, r, v, g, kd, h, mods, rw_ln_w[0], rw_ln_b[0], rw_r_k[0], rw_w_o[0])
    return moe(h_lat, 1, mods, 0, 0)
```

```python
import functools

import jax
import jax.numpy as jnp
import numpy as np
from jax import lax
from jax.experimental import pallas as pl
from jax.experimental.pallas import tpu as pltpu

F32 = jnp.float32
BF16 = jnp.bfloat16

D_MODEL = 1024
GRID_W = 64
CTX_LEN = 256
RMS_EPS = 1e-6
NA_HEADS = 8
NA_HEAD_DIM = 64
NA_WIDTH = NA_HEADS * NA_HEAD_DIM
NA_KH = 8
NA_KW = 16
POOL_WINDOWS = (2, 4, 8, 16)
POOL_GC = 128
POOL_WIDTH = 512
RW_HEAD_DIM = 64
RW_HEADS = D_MODEL // RW_HEAD_DIM
RW_GN_EPS = 64e-5
N_EXPERTS = 32
TOP_K = 4
D_FF = D_MODEL
SWIGLU_LIMIT = 7.0
SWIGLU_ALPHA = 1.702

TILE = 256
LANES = 128
SUBLANES = 8
MOE_TM = 512
NEG_BIG = -1e30
VMEM_LIMIT = 48 * 1024 * 1024
MOE_FFN_VMEM_LIMIT = 56 * 1024 * 1024


def _cparams(sem):
    return pltpu.CompilerParams(dimension_semantics=sem, vmem_limit_bytes=VMEM_LIMIT)


def _mod_row(mods_ref, k):
    return mods_ref[0, 0, pl.ds(k, 1), :]


def _norm_mod(x, gain, shift, scale):
    ms = jnp.mean(x * x, axis=-1, keepdims=True)
    return (x * lax.rsqrt(ms + RMS_EPS) * gain) * (1.0 + scale) + shift


def _ada_kernel(c_ref, w_ref, b_ref, o_ref):
    c = c_ref[...]
    s = c * jax.nn.sigmoid(c)
    o_ref[...] = jnp.dot(s, w_ref[...], preferred_element_type=F32,
                         precision=lax.Precision.HIGHEST) + b_ref[...]


def ada_mod(cvec, w, b):
    r, d = cvec.shape
    n = w.shape[1]
    tn = 1536
    return pl.pallas_call(
        _ada_kernel,
        out_shape=jax.ShapeDtypeStruct((r, n), F32),
        grid=(n // tn,),
        in_specs=[pl.BlockSpec((r, d), lambda j: (0, 0)),
                  pl.BlockSpec((d, tn), lambda j: (0, j)),
                  pl.BlockSpec((1, tn), lambda j: (0, j))],
        out_specs=pl.BlockSpec((r, tn), lambda j: (0, j)),
        compiler_params=_cparams(("arbitrary",)),
        name="ada_mod",
    )(cvec, w, b.reshape(1, n))


def _ab_proj_kernel(h_ref, g_ref, mods_ref, w_ref, bd_ref, qkg_ref, qk_ref, v_ref, u_ref):
    x = _norm_mod(h_ref[0], g_ref[...], _mod_row(mods_ref, 0), _mod_row(mods_ref, 1))
    p = jnp.dot(x.astype(BF16), w_ref[...], preferred_element_type=F32)
    qk = p[:, :2 * NA_WIDTH]
    ss = jnp.dot((qk * qk).astype(BF16), bd_ref[...], preferred_element_type=F32)
    qk_ref[0] = (qk * lax.rsqrt(ss + RMS_EPS) * qkg_ref[...]).astype(BF16)
    v_ref[0] = p[:, 2 * NA_WIDTH:3 * NA_WIDTH].astype(BF16)
    u_ref[0] = p[:, 3 * NA_WIDTH:]


def ab_proj(h, gain, mods, w_in, q_g, k_g):
    b, lt, d = h.shape
    nt = lt // TILE
    n_in = w_in.shape[1]
    hid = jnp.arange(2 * NA_WIDTH) // NA_HEAD_DIM
    bd = jnp.where(hid[:, None] == hid[None, :], 1.0 / NA_HEAD_DIM, 0.0).astype(BF16)
    qkg = jnp.concatenate([jnp.tile(q_g, NA_HEADS), jnp.tile(k_g, NA_HEADS)]).reshape(1, -1).astype(F32)
    return pl.pallas_call(
        _ab_proj_kernel,
        out_shape=(jax.ShapeDtypeStruct((b, lt, 2 * NA_WIDTH), BF16),
                   jax.ShapeDtypeStruct((b, lt, NA_WIDTH), BF16),
                   jax.ShapeDtypeStruct((b, lt, POOL_WIDTH), F32)),
        grid=(b, nt),
        in_specs=[pl.BlockSpec((1, TILE, d), lambda bi, i: (bi, i, 0)),
                  pl.BlockSpec((1, d), lambda bi, i: (0, 0)),
                  pl.BlockSpec((1, 1, 6, d), lambda bi, i: (bi, jnp.minimum(i, 1), 0, 0)),
                  pl.BlockSpec((d, n_in), lambda bi, i: (0, 0)),
                  pl.BlockSpec((2 * NA_WIDTH, 2 * NA_WIDTH), lambda bi, i: (0, 0)),
                  pl.BlockSpec((1, 2 * NA_WIDTH), lambda bi, i: (0, 0))],
        out_specs=(pl.BlockSpec((1, TILE, 2 * NA_WIDTH), lambda bi, i: (bi, i, 0)),
                   pl.BlockSpec((1, TILE, NA_WIDTH), lambda bi, i: (bi, i, 0)),
                   pl.BlockSpec((1, TILE, POOL_WIDTH), lambda bi, i: (bi, i, 0))),
        compiler_params=_cparams(("arbitrary", "arbitrary")),
        name="ab_proj",
    )(h, gain.reshape(1, d), mods, w_in.astype(BF16), bd, qkg)


NA_QROWS = TILE // GRID_W
NA_KTILES = 3
NA_NLOC = NA_KTILES * TILE
NA_NKEY = NA_NLOC + CTX_LEN


def _na_bias_tables(rpb, rows):
    n_r, n_c = 2 * NA_KH - 1, 2 * NA_KW - 1
    col = np.arange(GRID_W)
    c_start = np.clip(col - NA_KW // 2, 0, GRID_W - NA_KW)
    valid_c = (col[None, :] >= c_start[:, None]) & (col[None, :] < c_start[:, None] + NA_KW)
    c_rel = np.clip(col[None, :] - col[:, None] + NA_KW - 1, 0, n_c - 1)
    sel_c = (c_rel.reshape(-1)[None, :] == np.arange(n_c)[:, None]).astype(np.float32)
    by_col = jnp.einsum("hrc,cn->hrn", rpb.astype(F32), sel_c, precision=lax.Precision.HIGHEST)
    n_qt = rows // NA_QROWS
    qr, kr = np.arange(NA_QROWS), np.arange(NA_KTILES * NA_QROWS)
    tabs = []
    for jq in (0, 1, n_qt - 1):
        w0 = NA_QROWS * min(max(jq - 1, 0), n_qt - NA_KTILES)
        q_row = NA_QROWS * jq + qr
        key_row = w0 + kr
        r_start = np.clip(q_row - NA_KH // 2, 0, rows - NA_KH)
        valid_r = (key_row[None, :] >= r_start[:, None]) & (key_row[None, :] < r_start[:, None] + NA_KH)
        r_rel = np.clip(key_row[None, :] - q_row[:, None] + NA_KH - 1, 0, n_r - 1)
        sel_r = (r_rel.reshape(-1)[:, None] == np.arange(n_r)[None, :]).astype(np.float32)
        t = jnp.einsum("xr,hrn->hxn", sel_r, by_col, precision=lax.Precision.HIGHEST)
        t = t.reshape(NA_HEADS, NA_QROWS, NA_KTILES * NA_QROWS, GRID_W, GRID_W)
        t = t.transpose(0, 1, 3, 2, 4).reshape(NA_HEADS, TILE, NA_NLOC)
        valid = (valid_r[:, None, :, None] & valid_c[None, :, None, :]).reshape(TILE, NA_NLOC)
        tabs.append(jnp.where(valid[None], t, NEG_BIG))
    tabs = [jnp.full_like(tabs[0], NEG_BIG)] + tabs
    loc = jnp.stack(tabs)
    return jnp.concatenate([loc, jnp.zeros(loc.shape[:3] + (CTX_LEN,), F32)], axis=-1)


def _na_kernel(q_ref, k0_ref, k1_ref, k2_ref, kc_ref, v0_ref, v1_ref, v2_ref, vc_ref, bias_ref, o_ref):
    q = q_ref[0]
    kall = jnp.concatenate([k0_ref[0], k1_ref[0], k2_ref[0], kc_ref[0]], axis=0)
    vall = jnp.concatenate([v0_ref[0], v1_ref[0], v2_ref[0], vc_ref[0]], axis=0)
    lane = lax.broadcasted_iota(jnp.int32, q.shape, 1)
    scale = NA_HEAD_DIM ** -0.5
    outs = []
    for hh in range(2):
        in_head = (lane >= hh * NA_HEAD_DIM) & (lane < (hh + 1) * NA_HEAD_DIM)
        qh = jnp.where(in_head, q, jnp.zeros_like(q)) * jnp.asarray(scale, BF16)
        s = lax.dot_general(qh, kall, (((1,), (1,)), ((), ())), preferred_element_type=F32)
        s = s + bias_ref[0, hh]
        m = jnp.max(s, axis=-1, keepdims=True)
        p = jnp.exp(s - m)
        l = jnp.sum(p, axis=-1, keepdims=True)
        o = jnp.dot(p.astype(BF16), vall, preferred_element_type=F32)
        outs.append(o / l)
    o_ref[0] = jnp.where(lane < NA_HEAD_DIM, outs[0], outs[1]).astype(BF16)


def na_attention(qk, v, bias):
    b, lt, _ = qk.shape
    nt = lt // TILE
    n_qt = nt - 1
    assert NA_HEAD_DIM ** -0.5 == 0.125
    hp_n = NA_HEADS // 2
    kblk = NA_WIDTH // LANES

    def w0(j):
        return 1 + jnp.clip(j - 2, 0, n_qt - NA_KTILES)

    def kind(j):
        return jnp.where(j == 0, 0, jnp.where(j == 1, 1, jnp.where(j == nt - 1, 3, 2)))

    qspec = pl.BlockSpec((1, TILE, LANES), lambda j, hp, bi: (bi, j, hp))
    kspecs = [pl.BlockSpec((1, TILE, LANES), functools.partial(lambda j, hp, bi, t: (bi, w0(j) + t, kblk + hp), t=t))
              for t in range(NA_KTILES)]
    kcspec = pl.BlockSpec((1, TILE, LANES), lambda j, hp, bi: (bi, 0, kblk + hp))
    vspecs = [pl.BlockSpec((1, TILE, LANES), functools.partial(lambda j, hp, bi, t: (bi, w0(j) + t, hp), t=t))
              for t in range(NA_KTILES)]
    vcspec = pl.BlockSpec((1, TILE, LANES), lambda j, hp, bi: (bi, 0, hp))
    bspec = pl.BlockSpec((1, 2, TILE, NA_NKEY), lambda j, hp, bi: (kind(j), hp, 0, 0))
    return pl.pallas_call(
        _na_kernel,
        out_shape=jax.ShapeDtypeStruct((b, lt, NA_WIDTH), BF16),
        grid=(nt, hp_n, b),
        in_specs=[qspec] + kspecs + [kcspec] + vspecs + [vcspec, bspec],
        out_specs=pl.BlockSpec((1, TILE, LANES), lambda j, hp, bi: (bi, j, hp)),
        compiler_params=_cparams(("arbitrary", "arbitrary", "arbitrary")),
        name="na_attention",
    )(qk, qk, qk, qk, qk, v, v, v, v, bias)


POOL_HALO = 8
POOL_ROWS = TILE + 2 * POOL_HALO


def _pool_kernel(cur_ref, prev_ref, next_ref, pw_ref, ps_ref, o_ref, xp_ref, *, n_tiles):
    i = pl.program_id(1)
    has_prev = (i >= 2).astype(F32)
    has_next = ((i >= 1) & (i < n_tiles - 1)).astype(F32)
    xp_ref[0:POOL_HALO, :] = prev_ref[0] * has_prev
    xp_ref[POOL_HALO:POOL_HALO + TILE, :] = cur_ref[0]
    xp_ref[POOL_HALO + TILE:POOL_ROWS, :] = next_ref[0] * has_next
    row = lax.broadcasted_iota(jnp.int32, (TILE, POOL_GC), 0)
    tpos = jnp.where(i == 0, row, row + (i - 1) * TILE)
    seq_len = jnp.where(i == 0, CTX_LEN, (n_tiles - 1) * TILE)
    for g, w in enumerate(POOL_WINDOWS):
        cols = slice(g * POOL_GC, (g + 1) * POOL_GC)
        x = xp_ref[:, cols]
        s = x + pltpu.roll(x, 1, 0)
        half = 1
        while 2 * half < w:
            s = pltpu.roll(s, half, 0) + pltpu.roll(s, POOL_ROWS - half, 0)
            half *= 2
        s = s[POOL_HALO:POOL_HALO + TILE]
        cnt = jnp.minimum(w // 2, tpos) + jnp.minimum(w // 2, seq_len - tpos)
        pooled = s / cnt.astype(F32) - cur_ref[0, :, cols]
        y = jnp.dot(pooled.astype(BF16), pw_ref[g], preferred_element_type=F32)
        o_ref[0, :, cols] = (y * ps_ref[:, cols]).astype(BF16)


def multiscale_pool(u, pool_w, pool_scale):
    b, lt, c = u.shape
    nt = lt // TILE
    hb = TILE // POOL_HALO
    nhb = lt // POOL_HALO
    assert POOL_HALO >= max(POOL_WINDOWS) // 2 and POOL_HALO == SUBLANES
    return pl.pallas_call(
        functools.partial(_pool_kernel, n_tiles=nt),
        out_shape=jax.ShapeDtypeStruct((b, lt, c), BF16),
        grid=(b, nt),
        in_specs=[pl.BlockSpec((1, TILE, c), lambda bi, i: (bi, i, 0)),
                  pl.BlockSpec((1, POOL_HALO, c), lambda bi, i: (bi, jnp.maximum(i * hb - 1, 0), 0)),
                  pl.BlockSpec((1, POOL_HALO, c), lambda bi, i: (bi, jnp.minimum((i + 1) * hb, nhb - 1), 0)),
                  pl.BlockSpec((len(POOL_WINDOWS), POOL_GC, POOL_GC), lambda bi, i: (0, 0, 0)),
                  pl.BlockSpec((1, c), lambda bi, i: (0, 0))],
        out_specs=pl.BlockSpec((1, TILE, c), lambda bi, i: (bi, i, 0)),
        scratch_shapes=[pltpu.VMEM((POOL_ROWS, c), F32)],
        compiler_params=_cparams(("arbitrary", "arbitrary")),
        name="multiscale_pool",
    )(u, u, u, pool_w.astype(BF16), pool_scale.reshape(1, c).astype(F32))


def _out_proj_kernel(a_ref, b_ref, h_ref, mods_ref, w_ref, o_ref):
    ka = a_ref.shape[-1]
    y = jnp.dot(a_ref[0], w_ref[:ka, :], preferred_element_type=F32)
    y = y + jnp.dot(b_ref[0], w_ref[ka:, :], preferred_element_type=F32)
    o_ref[0] = h_ref[0] + _mod_row(mods_ref, 2) * y


def out_proj_residual(a, bb, h, mods, w):
    b, lt, d = h.shape
    nt = lt // TILE
    ka, kb = a.shape[-1], bb.shape[-1]
    return pl.pallas_call(
        _out_proj_kernel,
        out_shape=jax.ShapeDtypeStruct((b, lt, d), F32),
        grid=(b, nt),
        in_specs=[pl.BlockSpec((1, TILE, ka), lambda bi, i: (bi, i, 0)),
                  pl.BlockSpec((1, TILE, kb), lambda bi, i: (bi, i, 0)),
                  pl.BlockSpec((1, TILE, d), lambda bi, i: (bi, i, 0)),
                  pl.BlockSpec((1, 1, 6, d), lambda bi, i: (bi, jnp.minimum(i, 1), 0, 0)),
                  pl.BlockSpec((ka + kb, d), lambda bi, i: (0, 0))],
        out_specs=pl.BlockSpec((1, TILE, d), lambda bi, i: (bi, i, 0)),
        compiler_params=_cparams(("arbitrary", "arbitrary")),
        name="out_proj_residual",
    )(a, bb, h, mods, w.astype(BF16))


def _moe_pre_kernel(h_ref, g_ref, mods_ref, rwh_ref, rwl_ref, rb_ref, y_ref, lg_ref):
    x = _norm_mod(h_ref[0], g_ref[...], _mod_row(mods_ref, 3), _mod_row(mods_ref, 4))
    x_hi = x.astype(BF16)
    y_ref[0] = x_hi
    x_lo = (x - x_hi.astype(F32)).astype(BF16)
    w_hi = rwh_ref[...]
    lg_ref[0] = (jnp.dot(x_hi, w_hi, preferred_element_type=F32)
                 + jnp.dot(x_lo, w_hi, preferred_element_type=F32)
                 + jnp.dot(x_hi, rwl_ref[...], preferred_element_type=F32)) + rb_ref[...]


def moe_pre(h, gain, mods, router_w, router_b, tile0, ctx_tiles):
    b, lt, d = h.shape
    nt = lt // TILE - tile0
    e = router_w.shape[1]
    rw_hi = router_w.astype(BF16)
    rw_lo = (router_w - rw_hi.astype(F32)).astype(BF16)
    return pl.pallas_call(
        _moe_pre_kernel,
        out_shape=(jax.ShapeDtypeStruct((b, nt * TILE, d), BF16),
                   jax.ShapeDtypeStruct((b, nt * TILE, e), F32)),
        grid=(b, nt),
        in_specs=[pl.BlockSpec((1, TILE, d), lambda bi, i: (bi, i + tile0, 0)),
                  pl.BlockSpec((1, d), lambda bi, i: (0, 0)),
                  pl.BlockSpec((1, 1, 6, d), lambda bi, i: (bi, jnp.minimum(i + tile0 + 1 - ctx_tiles, 1), 0, 0)),
                  pl.BlockSpec((d, e), lambda bi, i: (0, 0)),
                  pl.BlockSpec((d, e), lambda bi, i: (0, 0)),
                  pl.BlockSpec((1, e), lambda bi, i: (0, 0))],
        out_specs=(pl.BlockSpec((1, TILE, d), lambda bi, i: (bi, i, 0)),
                   pl.BlockSpec((1, TILE, e), lambda bi, i: (bi, i, 0))),
        compiler_params=_cparams(("arbitrary", "arbitrary")),
        name="moe_pre",
    )(h, gain.reshape(1, d), mods, rw_hi, rw_lo, router_b.reshape(1, e))


MOE_PERM_W = 256


def _moe_ffn_kernel(te_ref, tv_ref, x_ref, win_ref, bg_ref, bl_ref, wo_ref, bo_ref, perm_ref, o_ref,
                    wg_s, wl_s, wo_s):
    t = pl.program_id(0)
    valid = tv_ref[t] > 0
    new_expert = (t == 0) | (te_ref[t] != te_ref[jnp.maximum(t - 1, 0)])

    @pl.when(valid & new_expert)
    def _():
        half = MOE_PERM_W // 2
        for cb in range(win_ref.shape[3] // MOE_PERM_W):
            w = win_ref[0, 0, :, cb * MOE_PERM_W:(cb + 1) * MOE_PERM_W].astype(BF16)
            wp = jnp.dot(w, perm_ref[...], preferred_element_type=F32).astype(BF16)
            wg_s[:, cb * half:(cb + 1) * half] = wp[:, :half]
            wl_s[:, cb * half:(cb + 1) * half] = wp[:, half:]
        wo_s[...] = wo_ref[0, 0].astype(BF16)

    @pl.when(jnp.logical_not(valid))
    def _():
        o_ref[...] = jnp.zeros_like(o_ref)

    @pl.when(valid)
    def _():
        x = x_ref[...].astype(BF16)
        hg = jnp.dot(x, wg_s[...], preferred_element_type=F32) + bg_ref[0]
        hl = jnp.dot(x, wl_s[...], preferred_element_type=F32) + bl_ref[0]
        glu = jnp.minimum(hg, SWIGLU_LIMIT)
        lin = jnp.clip(hl, -SWIGLU_LIMIT, SWIGLU_LIMIT)
        act = glu * jax.nn.sigmoid(SWIGLU_ALPHA * glu) * (lin + 1.0)
        o_ref[...] = (jnp.dot(act.astype(BF16), wo_s[...], preferred_element_type=F32)
                      + bo_ref[0]).astype(o_ref.dtype)


def moe_grouped_ffn(x_sorted, tile_expert, tile_valid, layer, w_in, bg, bl, w_out, bo):
    p, d = x_sorted.shape
    _, e, _, f2 = w_in.shape
    f = f2 // 2
    nt = p // MOE_TM
    assert f2 % MOE_PERM_W == 0 and MOE_PERM_W % (2 * LANES) == 0
    src = jnp.arange(MOE_PERM_W)
    dst = jnp.where(src % 2 == 0, src // 2, MOE_PERM_W // 2 + src // 2)
    perm = (dst[:, None] == jnp.arange(MOE_PERM_W)[None, :]).astype(BF16)
    grid_spec = pltpu.PrefetchScalarGridSpec(
        num_scalar_prefetch=2,
        grid=(nt,),
        in_specs=[pl.BlockSpec((MOE_TM, d), lambda t, te, tv: (t * tv[t], 0)),
                  pl.BlockSpec((1, 1, d, f2), lambda t, te, tv: (layer, te[t], 0, 0)),
                  pl.BlockSpec((1, 1, f), lambda t, te, tv: (te[t], 0, 0)),
                  pl.BlockSpec((1, 1, f), lambda t, te, tv: (te[t], 0, 0)),
                  pl.BlockSpec((1, 1, f, d), lambda t, te, tv: (layer, te[t], 0, 0)),
                  pl.BlockSpec((1, 1, d), lambda t, te, tv: (te[t], 0, 0)),
                  pl.BlockSpec((MOE_PERM_W, MOE_PERM_W), lambda t, te, tv: (0, 0))],
        out_specs=pl.BlockSpec((MOE_TM, d), lambda t, te, tv: (t, 0)),
        scratch_shapes=[pltpu.VMEM((d, f), BF16), pltpu.VMEM((d, f), BF16), pltpu.VMEM((f, d), BF16)],
    )
    return pl.pallas_call(
        _moe_ffn_kernel,
        out_shape=jax.ShapeDtypeStruct((p, d), BF16),
        grid_spec=grid_spec,
        compiler_params=pltpu.CompilerParams(dimension_semantics=("arbitrary",),
                                             vmem_limit_bytes=MOE_FFN_VMEM_LIMIT),
        name="moe_grouped_ffn",
    )(tile_expert, tile_valid, x_sorted, w_in, bg, bl, w_out, bo, perm)


def _moe_combine_kernel(z_ref, gt_ref, h_ref, mods_ref, o_ref):
    gt = gt_ref[0]
    acc = gt[:, 0:1] * z_ref[0, 0].astype(F32)
    for k in range(1, TOP_K):
        acc = acc + gt[:, k:k + 1] * z_ref[k, 0].astype(F32)
    o_ref[0] = h_ref[0] + _mod_row(mods_ref, 5) * acc


def moe_combine(z4, gates, h, mods, tile0, ctx_tiles):
    k, b, lp, d = z4.shape
    nt = lp // TILE
    return pl.pallas_call(
        _moe_combine_kernel,
        out_shape=jax.ShapeDtypeStruct((b, lp, d), F32),
        grid=(b, nt),
        in_specs=[pl.BlockSpec((k, 1, TILE, d), lambda bi, i: (0, bi, i, 0)),
                  pl.BlockSpec((1, TILE, k), lambda bi, i: (bi, i, 0)),
                  pl.BlockSpec((1, TILE, d), lambda bi, i: (bi, i + tile0, 0)),
                  pl.BlockSpec((1, 1, 6, d), lambda bi, i: (bi, jnp.minimum(i + tile0 + 1 - ctx_tiles, 1), 0, 0))],
        out_specs=pl.BlockSpec((1, TILE, d), lambda bi, i: (bi, i, 0)),
        compiler_params=_cparams(("arbitrary", "arbitrary")),
        name="moe_combine",
    )(z4, gates, h, mods)


DISPATCH_T = 512
DISPATCH_BLK = SUBLANES
DISPATCH_ROWS = -(-(DISPATCH_T * TOP_K + N_EXPERTS * (DISPATCH_BLK - 1)) // (2 * SUBLANES)) * (2 * SUBLANES)
DISPATCH_COPY_ROWS = (4 * DISPATCH_BLK, DISPATCH_BLK)
DISPATCH_LIST = max(DISPATCH_ROWS // DISPATCH_COPY_ROWS[0],
                    N_EXPERTS * (DISPATCH_COPY_ROWS[0] // DISPATCH_COPY_ROWS[1] - 1))


def _moe_dispatch_kernel(cnt_ref, tab_ref, ntail_ref, tail_ref, y_ref, idx_ref, off_ref, base_ref,
                         xs_ref, slot_ref, local_ref, zero_ref, sem, zero_sem):
    t = pl.program_id(0)
    last_t = pl.num_programs(0) - 1
    dt = DISPATCH_T
    idx = idx_ref[0]
    e_iota = lax.broadcasted_iota(jnp.int32, (N_EXPERTS, dt), 0)
    picked = [e_iota == idx[k:k + 1, :] for k in range(TOP_K)]
    cnt = sum(jnp.where(p, 1.0, 0.0) for p in picked).astype(BF16)
    r0 = lax.broadcasted_iota(jnp.int32, (dt, dt), 0)
    r1 = lax.broadcasted_iota(jnp.int32, (dt, dt), 1)
    earlier = jnp.where(r0 < r1, 1.0, 0.0).astype(BF16)
    rank = jnp.dot(cnt, earlier, preferred_element_type=F32)
    local_row = rank + off_ref[0].astype(F32)
    global_row = rank + base_ref[0].astype(F32)
    rows = lax.broadcasted_iota(jnp.int32, (DISPATCH_ROWS, dt), 0)
    place = None
    for k in range(TOP_K):
        pos = jnp.sum(jnp.where(picked[k], local_row, 0.0), axis=0, keepdims=True).astype(jnp.int32)
        slot_ref[0, k:k + 1, :] = jnp.sum(jnp.where(picked[k], global_row, 0.0), axis=0,
                                          keepdims=True).astype(jnp.int32)
        hit = rows == pos
        place = hit if place is None else (place | hit)
    placement = jnp.where(place, 1.0, 0.0).astype(BF16)
    buf = lax.rem(t, 2)
    local_ref[buf] = jnp.dot(placement, y_ref[...], preferred_element_type=F32)

    def seg_copy(cls, step, j):
        rows = DISPATCH_COPY_ROWS[cls]
        src_buf = jnp.where(step == t, buf, 1 - buf)
        src_blk = tab_ref[step, (2 * cls) * DISPATCH_LIST + j]
        dst_blk = tab_ref[step, (2 * cls + 1) * DISPATCH_LIST + j]
        src = local_ref.at[src_buf, pl.ds(pl.multiple_of(src_blk * DISPATCH_BLK, DISPATCH_BLK), rows), :]
        dst = xs_ref.at[pl.ds(pl.multiple_of(dst_blk * DISPATCH_BLK, DISPATCH_BLK), rows), :]
        return pltpu.make_async_copy(src, dst, sem)

    def tail_copy(e, j):
        blk = tail_ref[e] + j
        dst = xs_ref.at[pl.ds(pl.multiple_of(blk * DISPATCH_BLK, DISPATCH_BLK), DISPATCH_BLK), :]
        return pltpu.make_async_copy(zero_ref, dst, zero_sem)

    def start_all(copy, n):
        lax.fori_loop(0, n, lambda j, c: (copy(j).start(), c)[1], 0)

    def wait_all(copy, n):
        lax.fori_loop(0, n, lambda j, c: (copy(j).wait(), c)[1], 0)

    classes = range(len(DISPATCH_COPY_ROWS))

    @pl.when(t > 0)
    def _():
        for cls in classes:
            wait_all(functools.partial(seg_copy, cls, t - 1), cnt_ref[t - 1, cls])

    for cls in classes:
        start_all(functools.partial(seg_copy, cls, t), cnt_ref[t, cls])

    @pl.when(t == last_t)
    def _():
        for cls in classes:
            wait_all(functools.partial(seg_copy, cls, t), cnt_ref[t, cls])

    @pl.when(t == 0)
    def _():
        zero_ref[...] = jnp.zeros_like(zero_ref)
        for e in range(N_EXPERTS + 1):
            start_all(functools.partial(tail_copy, e), ntail_ref[e])
        for e in range(N_EXPERTS + 1):
            wait_all(functools.partial(tail_copy, e), ntail_ref[e])


def moe_dispatch(y2, idx_t, copy_cnt, copy_tab, ntail, tail_blk, loc_off, base, p_rows):
    n, d = y2.shape
    nt = n // DISPATCH_T
    grid_spec = pltpu.PrefetchScalarGridSpec(
        num_scalar_prefetch=4,
        grid=(nt,),
        in_specs=[pl.BlockSpec((DISPATCH_T, d), lambda t, *_: (t, 0)),
                  pl.BlockSpec((1, TOP_K, DISPATCH_T), lambda t, *_: (t, 0, 0)),
                  pl.BlockSpec((1, N_EXPERTS, 1), lambda t, *_: (t, 0, 0)),
                  pl.BlockSpec((1, N_EXPERTS, 1), lambda t, *_: (t, 0, 0))],
        out_specs=(pl.BlockSpec(memory_space=pl.ANY),
                   pl.BlockSpec((1, TOP_K, DISPATCH_T), lambda t, *_: (t, 0, 0))),
        scratch_shapes=[pltpu.VMEM((2, DISPATCH_ROWS, d), F32), pltpu.VMEM((DISPATCH_BLK, d), F32),
                        pltpu.SemaphoreType.DMA(()), pltpu.SemaphoreType.DMA(())],
    )
    return pl.pallas_call(
        _moe_dispatch_kernel,
        out_shape=(jax.ShapeDtypeStruct((p_rows, d), F32),
                   jax.ShapeDtypeStruct((nt, TOP_K, DISPATCH_T), jnp.int32)),
        grid_spec=grid_spec,
        compiler_params=_cparams(("arbitrary",)),
        name="moe_dispatch",
    )(copy_cnt, copy_tab, ntail, tail_blk, y2, idx_t, loc_off, base)


def moe_layer(h, gain, mods, router_w, router_b, layer, w_in, b_in, w_out, b_out, tile0, ctx_tiles):
    b, lt, d = h.shape
    y, logits = moe_pre(h, gain, mods, router_w, router_b, tile0, ctx_tiles)
    lp = y.shape[1]
    n = b * lp
    y2 = y.reshape(n, d)
    top_val, top_idx = lax.top_k(logits.reshape(n, N_EXPERTS), TOP_K)
    gates = jax.nn.softmax(top_val, axis=-1)

    assert n % DISPATCH_T == 0
    t_n = n // DISPATCH_T
    blk = DISPATCH_BLK
    experts = jnp.arange(N_EXPERTS, dtype=jnp.int32)
    idx_t = top_idx.astype(jnp.int32).reshape(t_n, DISPATCH_T, TOP_K).transpose(0, 2, 1)
    tile_cnt = jnp.sum((idx_t[..., None] == experts).astype(jnp.int32), axis=(1, 2))
    seg = (tile_cnt + blk - 1) // blk * blk
    loc_end = jnp.cumsum(seg, axis=1)
    loc_off = loc_end - seg
    tot = jnp.sum(seg, axis=0)
    padded = (tot + MOE_TM - 1) // MOE_TM * MOE_TM
    ends = jnp.cumsum(padded)
    starts = ends - padded
    base = starts[None, :] + jnp.cumsum(seg, axis=0) - seg
    q = jnp.arange(DISPATCH_LIST, dtype=jnp.int32)
    covered = jnp.zeros_like(seg)
    tabs, cnts = [], []
    for rows in DISPATCH_COPY_ROWS:
        n_c = (seg - covered) // rows
        c_end = jnp.cumsum(n_c, axis=1)
        sel = jnp.sum((c_end[:, None, :] <= q[None, :, None]).astype(jnp.int32), axis=2)[..., None] == experts
        pick = lambda a: jnp.sum(jnp.where(sel, a[:, None, :], 0), axis=2)
        inside = (q[None, :] - pick(c_end - n_c)) * rows
        tabs += [(pick(loc_off + covered) + inside) // blk, (pick(base + covered) + inside) // blk]
        cnts.append(c_end[:, -1])
        covered = covered + n_c * rows
    copy_tab = jnp.concatenate(tabs, axis=1)
    copy_cnt = jnp.stack(cnts, axis=1)
    n_tiles = -(-(n * TOP_K + t_n * N_EXPERTS * (blk - 1)) // MOE_TM) + N_EXPERTS
    tile_start = jnp.arange(n_tiles, dtype=jnp.int32) * MOE_TM
    tile_expert = jnp.minimum(jnp.sum((tile_start[:, None] >= ends[None, :]).astype(jnp.int32), axis=1),
                              N_EXPERTS - 1)
    tile_valid = (tile_start < ends[-1]).astype(jnp.int32)

    p_rows = n_tiles * MOE_TM
    gap_len = jnp.concatenate([padded - tot, p_rows - ends[-1:]]) // blk
    gap_start = jnp.concatenate([starts + tot, ends[-1:]]) // blk
    x_sorted, slot = moe_dispatch(y2, idx_t, copy_cnt, copy_tab, gap_len, gap_start,
                                  loc_off[..., None], base[..., None], p_rows)
    bg = b_in[layer, :, 0::2].reshape(N_EXPERTS, 1, D_FF)
    bl = b_in[layer, :, 1::2].reshape(N_EXPERTS, 1, D_FF)
    z = moe_grouped_ffn(x_sorted, tile_expert, tile_valid, layer, w_in, bg, bl, w_out,
                        b_out[layer].reshape(N_EXPERTS, 1, d))
    slot_kn = slot.transpose(1, 0, 2).reshape(TOP_K, n)
    z4 = z.at[slot_kn].get(mode="promise_in_bounds").reshape(TOP_K, b, lp, d)
    return moe_combine(z4, gates.reshape(b, lp, TOP_K), h, mods, tile0, ctx_tiles)


def _softplus(z):
    return jnp.maximum(z, 0.0) + jnp.log(1.0 + jnp.exp(-jnp.abs(z)))


def _rwkv_feat_kernel(h_ref, hp_ref, hn_ref, g_ref, mods_ref, mu_ref, wr_ref, wk_ref, wv_ref,
                      w1_ref, w2_ref, a1_ref, a2_ref, g1_ref, g2_ref, w0_ref, a0_ref, kk_ref, ka_ref,
                      bd_ref, r_out, v_out, kk_out, g_out, lw_out, kd_out, a_out, *, n_tiles):
    i = pl.program_id(1)
    gain, shift, scale = g_ref[...], _mod_row(mods_ref, 0), _mod_row(mods_ref, 1)
    x = _norm_mod(h_ref[0], gain, shift, scale)
    has_prev = (i >= 2).astype(F32)
    has_next = ((i >= 1) & (i < n_tiles - 1)).astype(F32)
    x_before = _norm_mod(hp_ref[0], gain, shift, scale)[SUBLANES - 1:SUBLANES] * has_prev
    x_after = _norm_mod(hn_ref[0], gain, shift, scale)[0:1] * has_next
    row = lax.broadcasted_iota(jnp.int32, x.shape, 0)
    prev = jnp.where(row == 0, x_before, pltpu.roll(x, 1, 0))
    nxt = jnp.where(row == TILE - 1, x_after, pltpu.roll(x, TILE - 1, 0))
    xx = 0.5 * (prev + nxt) - x
    xr, xw, xk, xv, xa, xg = (x + xx * mu_ref[pl.ds(j, 1), :] for j in range(6))

    def mm(a, w):
        return jnp.dot(a.astype(BF16), w, preferred_element_type=F32)

    r = mm(xr, wr_ref[...])
    k = mm(xk, wk_ref[...])
    v = mm(xv, wv_ref[...])
    g = mm(jax.nn.sigmoid(mm(xg, g1_ref[...])), g2_ref[...])
    kk = k * kk_ref[...]
    kk = kk * lax.rsqrt(mm(kk * kk, bd_ref[...]) + 1e-12)
    r_out[0] = r.astype(BF16)
    v_out[0] = v.astype(BF16)
    kk_out[0] = kk.astype(BF16)
    g_out[0] = g.astype(BF16)
    tw = jnp.tanh(mm(xw, w1_ref[...]))
    av = mm(xa, a1_ref[...])
    lane = lax.broadcasted_iota(jnp.int32, tw.shape, 1)
    half = tw.shape[1] // 2
    for d in range(2):
        in_dir = (lane >= d * half) & (lane < (d + 1) * half)
        wz = w0_ref[pl.ds(d, 1), :] + mm(jnp.where(in_dir, tw, 0.0), w2_ref[...])
        w_log = -_softplus(-wz) - 0.5
        lw_out[d, 0] = -jnp.exp(w_log)
        a = jax.nn.sigmoid(a0_ref[pl.ds(d, 1), :] + mm(jnp.where(in_dir, av, 0.0), a2_ref[...]))
        kd_out[d, 0] = (k * (1.0 + (a - 1.0) * ka_ref[...])).astype(BF16)
        a_out[d, 0] = a.astype(BF16)


def rwkv_features(h, gain, mods, mu, w_r, w_k, w_v, w0, w1, w2, a0, a1, a2, g1, g2, k_k, k_a):
    b, lt, d = h.shape
    nt = lt // TILE
    hb = TILE // SUBLANES
    nhb = lt // SUBLANES
    hid = jnp.arange(d) // RW_HEAD_DIM
    bd = (hid[:, None] == hid[None, :]).astype(BF16)
    w1c = jnp.concatenate([w1[0], w1[1]], axis=1).astype(BF16)
    a1c = jnp.concatenate([a1[0], a1[1]], axis=1).astype(BF16)
    w2c = jnp.concatenate([w2[0], w2[1]], axis=0).astype(BF16)
    a2c = jnp.concatenate([a2[0], a2[1]], axis=0).astype(BF16)
    full = lambda shape: pl.BlockSpec(shape, lambda bi, i: (0,) * len(shape))
    tile_spec = pl.BlockSpec((1, TILE, d), lambda bi, i: (bi, i, 0))
    dir_spec = pl.BlockSpec((2, 1, TILE, d), lambda bi, i: (0, bi, i, 0))
    seq_bf = jax.ShapeDtypeStruct((b, lt, d), BF16)
    return pl.pallas_call(
        functools.partial(_rwkv_feat_kernel, n_tiles=nt),
        out_shape=(seq_bf, seq_bf, seq_bf, seq_bf,
                   jax.ShapeDtypeStruct((2, b, lt, d), F32),
                   jax.ShapeDtypeStruct((2, b, lt, d), BF16),
                   jax.ShapeDtypeStruct((2, b, lt, d), BF16)),
        grid=(b, nt),
        in_specs=[tile_spec,
                  pl.BlockSpec((1, SUBLANES, d), lambda bi, i: (bi, jnp.maximum(i * hb - 1, 0), 0)),
                  pl.BlockSpec((1, SUBLANES, d), lambda bi, i: (bi, jnp.minimum((i + 1) * hb, nhb - 1), 0)),
                  full((1, d)),
                  pl.BlockSpec((1, 1, 6, d), lambda bi, i: (bi, jnp.minimum(i, 1), 0, 0)),
                  full((6, d)), full((d, d)), full((d, d)), full((d, d)),
                  full(w1c.shape), full(w2c.shape), full(a1c.shape), full(a2c.shape),
                  full(g1.shape), full(g2.shape), full((2, d)), full((2, d)), full((1, d)), full((1, d)),
                  full((d, d))],
        out_specs=(tile_spec, tile_spec, tile_spec, tile_spec, dir_spec, dir_spec, dir_spec),
        compiler_params=_cparams(("arbitrary", "arbitrary")),
        name="rwkv_features",
    )(h, h, h, gain.reshape(1, d), mods, mu, w_r.astype(BF16), w_k.astype(BF16), w_v.astype(BF16),
      w1c, w2c, a1c, a2c, g1.astype(BF16), g2.astype(BF16), w0, a0, k_k.reshape(1, d), k_a.reshape(1, d), bd)


SCAN_C = 64
SCAN_G = 4
SCAN_GW = SCAN_G * RW_HEAD_DIM


def _scan_chunk_inputs(reverse, r_ref, v_ref, kk_ref, lw_ref, kd_ref, a_ref):
    cc = SCAN_C
    ti = lax.broadcasted_iota(jnp.int32, (cc, cc), 0)
    si = lax.broadcasted_iota(jnp.int32, (cc, cc), 1)
    before_incl = (si >= ti) if reverse else (si <= ti)
    lw = lw_ref[0, 0]
    tri = jnp.where(before_incl, 1.0, 0.0).astype(BF16)
    lw_hi = lw.astype(BF16)
    lw_lo = (lw - lw_hi.astype(F32)).astype(BF16)
    cl = (jnp.dot(tri, lw_hi, preferred_element_type=F32) + jnp.dot(tri, lw_lo, preferred_element_type=F32))
    g_in = jnp.exp(cl)
    g_ex = jnp.exp(cl - lw)
    g_inv = jnp.exp(-cl)
    g_end = jnp.exp(jnp.sum(lw, axis=0, keepdims=True))
    kk = kk_ref[0].astype(F32)
    rt = (r_ref[0].astype(F32) * g_in).astype(BF16)
    at = (-kk * g_ex).astype(BF16)
    bt = (kk * a_ref[0, 0].astype(F32) * g_inv).astype(BF16)
    kt = (kd_ref[0, 0].astype(F32) * g_inv).astype(BF16)
    return rt, at, bt, kt, v_ref[0], g_end


def _rwkv_scan_kernel(rf_ref, vf_ref, kkf_ref, rb_ref, vb_ref, kkb_ref, lwf_ref, kdf_ref, af_ref,
                      lwb_ref, kdb_ref, ab_ref, yf_ref, yb_ref, h_ref):
    @pl.when(pl.program_id(1) == 0)
    def _():
        h_ref[...] = jnp.zeros_like(h_ref)

    cc, gw = SCAN_C, SCAN_GW
    ops = (_scan_chunk_inputs(False, rf_ref, vf_ref, kkf_ref, lwf_ref, kdf_ref, af_ref),
           _scan_chunk_inputs(True, rb_ref, vb_ref, kkb_ref, lwb_ref, kdb_ref, ab_ref))
    y_refs = (yf_ref, yb_ref)

    tg = lax.broadcasted_iota(jnp.int32, (cc, gw), 0)
    sg = lax.broadcasted_iota(jnp.int32, (cc, gw), 1) % cc
    m_strict = (sg < tg, sg > tg)
    m_incl = (sg <= tg, sg >= tg)
    eye_g = (sg == tg).astype(F32)
    bi = lax.broadcasted_iota(jnp.int32, (gw, gw), 0) // cc
    bj = lax.broadcasted_iota(jnp.int32, (gw, gw), 1) // cc
    blk = bi == bj

    def bdiag(x):
        return jnp.where(blk, jnp.concatenate([x] * SCAN_G, axis=0), jnp.zeros((), x.dtype))

    def mm(a, b):
        return jnp.dot(a, b, preferred_element_type=F32)

    def mm_nt(a, b):
        return lax.dot_general(a, b, (((1,), (1,)), ((), ())), preferred_element_type=F32)

    n_g = RW_HEADS // SCAN_G
    chains = [(d, g) for d in range(2) for g in range(n_g)]
    cols = [slice(g * gw, (g + 1) * gw) for _, g in chains]
    rt, at, bt, kt, v, g_end = ([ops[d][j][:, cols[i]] for i, (d, _) in enumerate(chains)] for j in range(6))
    idx = range(len(chains))
    h0 = [h_ref[i] for i in idx]
    h0b = [x.astype(BF16) for x in h0]
    v_bd = [bdiag(x) for x in v]
    mats = [mm_nt(jnp.concatenate([at[i], rt[i]], axis=0),
                  jnp.concatenate([bdiag(bt[i]), bdiag(kt[i])], axis=0)) for i in idx]
    a_ab = [jnp.where(m_strict[chains[i][0]], mats[i][:cc, :gw], 0.0) for i in idx]
    a_ak = [jnp.where(m_strict[chains[i][0]], mats[i][:cc, gw:], 0.0).astype(BF16) for i in idx]
    m_rb = [jnp.where(m_incl[chains[i][0]], mats[i][cc:, :gw], 0.0).astype(BF16) for i in idx]
    m_rk = [jnp.where(m_incl[chains[i][0]], mats[i][cc:, gw:], 0.0).astype(BF16) for i in idx]
    from_h = [mm(jnp.concatenate([at[i], rt[i]], axis=0), h0b[i]) for i in idx]
    from_v = [mm(jnp.concatenate([a_ak[i], m_rk[i]], axis=0), v_bd[i]) for i in idx]
    pq = [from_h[i][:cc] + from_v[i][:cc] for i in idx]
    pw = [x.astype(BF16) for x in a_ab]
    s_acc = [eye_g + x for x in a_ab]
    pw = [mm(x, bdiag(x)).astype(BF16) for x in pw]
    n = 2
    while n < cc:
        last = 2 * n >= cc
        lhs = [s_acc[i].astype(BF16) if last else jnp.concatenate([pw[i], s_acc[i].astype(BF16)], axis=0)
               for i in idx]
        prod = [mm(lhs[i], bdiag(pw[i])) for i in idx]
        s_acc = [s_acc[i] + (prod[i] if last else prod[i][cc:]) for i in idx]
        if not last:
            pw = [prod[i][:cc].astype(BF16) for i in idx]
        n *= 2
    ub = [mm(s_acc[i].astype(BF16), bdiag(pq[i].astype(BF16))).astype(BF16) for i in idx]
    for i in idx:
        y_refs[chains[i][0]][0, :, cols[i]] = from_h[i][cc:] + from_v[i][cc:] + mm(m_rb[i], bdiag(ub[i]))
    for i in idx:
        upd = lax.dot_general(jnp.concatenate([bt[i], kt[i]], axis=0), jnp.concatenate([ub[i], v[i]], axis=0),
                              (((0,), (0,)), ((), ())), preferred_element_type=F32)
        ge = jnp.transpose(jnp.broadcast_to(g_end[i], (gw, gw)))
        h_ref[i] = ge * (h0[i] + jnp.where(blk, upd, 0.0))


def rwkv_scan(r, v, kk, lw, kd, a):
    b, lt, d = r.shape
    nc = lt // SCAN_C
    n_ctx = CTX_LEN // SCAN_C
    assert SCAN_C == RW_HEAD_DIM and CTX_LEN % SCAN_C == 0

    def rev_chunk(c):
        return jnp.where(c < n_ctx, n_ctx - 1 - c, nc - 1 + n_ctx - c)

    fwd = pl.BlockSpec((1, SCAN_C, d), lambda bi, c: (bi, c, 0))
    bwd = pl.BlockSpec((1, SCAN_C, d), lambda bi, c: (bi, rev_chunk(c), 0))
    fwd_dir = pl.BlockSpec((1, 1, SCAN_C, d), lambda bi, c: (0, bi, c, 0))
    bwd_dir = pl.BlockSpec((1, 1, SCAN_C, d), lambda bi, c: (1, bi, rev_chunk(c), 0))
    y_shape = jax.ShapeDtypeStruct((b, lt, d), F32)
    return pl.pallas_call(
        _rwkv_scan_kernel,
        out_shape=(y_shape, y_shape),
        grid=(b, nc),
        in_specs=[fwd, fwd, fwd, bwd, bwd, bwd, fwd_dir, fwd_dir, fwd_dir, bwd_dir, bwd_dir, bwd_dir],
        out_specs=(fwd, bwd),
        scratch_shapes=[pltpu.VMEM((2 * RW_HEADS // SCAN_G, SCAN_GW, SCAN_GW), F32)],
        compiler_params=_cparams(("arbitrary", "arbitrary")),
        name="rwkv_scan",
    )(r, v, kk, r, v, kk, lw, kd, a, lw, kd, a)


def _rwkv_out_kernel(yf_ref, yb_ref, r_ref, v_ref, g_ref, kd_ref, h_ref, mods_ref, lnw_ref, lnb_ref, rk_ref,
                     bd_ref, wo_ref, o_ref):
    def head_sum(x):
        return jnp.dot(x.astype(BF16), bd_ref[...], preferred_element_type=F32)

    y = yf_ref[0] + yb_ref[0]
    inv_n = 1.0 / RW_HEAD_DIM
    mean = head_sum(y) * inv_n
    yc = y - mean
    var = head_sum(yc * yc) * inv_n
    yn = yc * lax.rsqrt(var + RW_GN_EPS) * lnw_ref[...] + lnb_ref[...]
    r = r_ref[0].astype(F32)
    kd = kd_ref[0, 0].astype(F32) + kd_ref[1, 0].astype(F32)
    yn = yn + head_sum(r * kd * rk_ref[...]) * v_ref[0].astype(F32)
    out = jnp.dot((yn * g_ref[0].astype(F32)).astype(BF16), wo_ref[...], preferred_element_type=F32)
    o_ref[0] = h_ref[0] + _mod_row(mods_ref, 2) * out


def rwkv_readout(y_f, y_b, r, v, g, kd, h, mods, ln_w, ln_b, r_k, w_o):
    b, lt, d = h.shape
    nt = lt // TILE - 1
    hid = jnp.arange(d) // RW_HEAD_DIM
    bd = (hid[:, None] == hid[None, :]).astype(BF16)
    full = lambda shape: pl.BlockSpec(shape, lambda bi, i: (0,) * len(shape))
    tile_spec = pl.BlockSpec((1, TILE, d), lambda bi, i: (bi, i + 1, 0))
    dir_spec = pl.BlockSpec((2, 1, TILE, d), lambda bi, i: (0, bi, i + 1, 0))
    return pl.pallas_call(
        _rwkv_out_kernel,
        out_shape=jax.ShapeDtypeStruct((b, nt * TILE, d), F32),
        grid=(b, nt),
        in_specs=[tile_spec, tile_spec, tile_spec, tile_spec, tile_spec, dir_spec, tile_spec,
                  pl.BlockSpec((1, 1, 6, d), lambda bi, i: (bi, 1, 0, 0)),
                  full((1, d)), full((1, d)), full((1, d)), full((d, d)), full((d, d))],
        out_specs=pl.BlockSpec((1, TILE, d), lambda bi, i: (bi, i, 0)),
        compiler_params=_cparams(("arbitrary", "arbitrary")),
        name="rwkv_readout",
    )(y_f, y_b, r, v, g, kd, h, mods, ln_w.reshape(1, d), ln_b.reshape(1, d), r_k.reshape(1, d), bd,
      w_o.astype(BF16))


def kernel(x, c, ctx, c_ctx, ada_w, ada_b, norm_mix_g, norm_ffn_g, router_w, router_b, exp_w_in, exp_b_in,
           exp_w_out, exp_b_out, ab_w_in, na_q_g, na_k_g, na_rpb, pool_w, pool_scale, ab_w_out,
           rw_mu, rw_w_r, rw_w_k, rw_w_v, rw_w_o, rw_w0, rw_w1, rw_w2, rw_a0, rw_a1, rw_a2,
           rw_g1, rw_g2, rw_k_k, rw_k_a, rw_r_k, rw_ln_w, rw_ln_b):
    b, seq, d = x.shape
    depth = ada_w.shape[0]
    assert ctx.shape[1] == CTX_LEN == TILE and seq % TILE == 0 and d == D_MODEL
    assert depth == 2, "layer schedule below is written for [neighbourhood/pool layer, RWKV layer]"
    rows = seq // GRID_W
    assert rows >= NA_QROWS * NA_KTILES and NA_QROWS * NA_KTILES >= NA_QROWS + NA_KH - 1

    n_c = 1 + b
    n_c_pad = -(-n_c // SUBLANES) * SUBLANES
    cvec = jnp.concatenate([c_ctx[None], c, jnp.zeros((n_c_pad - n_c, d), F32)], axis=0)

    def layer_mods(layer):
        m = ada_mod(cvec, ada_w[layer], ada_b[layer])
        m_ctx = jnp.broadcast_to(m[0].reshape(1, 6, d), (b, 6, d))
        return jnp.stack([m_ctx, m[1:n_c].reshape(b, 6, d)], axis=1)

    def moe(h, layer, mods, tile0, ctx_tiles):
        return moe_layer(h, norm_ffn_g[layer], mods, router_w[layer], router_b[layer], layer, exp_w_in,
                         exp_b_in, exp_w_out, exp_b_out, tile0, ctx_tiles)

    h = jnp.concatenate([ctx, x], axis=1)

    mods = layer_mods(0)
    qk, v, u = ab_proj(h, norm_mix_g[0], mods, ab_w_in[0], na_q_g[0], na_k_g[0])
    o_na = na_attention(qk, v, _na_bias_tables(na_rpb[0], rows))
    o_pool = multiscale_pool(u, pool_w[0], pool_scale[0])
    h = out_proj_residual(o_na, o_pool, h, mods, ab_w_out[0])
    h = moe(h, 0, mods, 0, 1)

    mods = layer_mods(1)
    r, v, kk, g, lw, kd, a = rwkv_features(h, norm_mix_g[1], mods, rw_mu[0], rw_w_r[0], rw_w_k[0], rw_w_v[0],
                                           rw_w0[0], rw_w1[0], rw_w2[0], rw_a0[0], rw_a1[0], rw_a2[0],
                                           rw_g1[0], rw_g2[0], rw_k_k[0], rw_k_a[0])
    y_f, y_b = rwkv_scan(r, v, kk, lw, kd, a)
    h_lat = rwkv_readout(y_f, y_b, r, v, g, kd, h, mods, rw_ln_w[0], rw_ln_b[0], rw_r_k[0], rw_w_o[0])
    return moe(h_lat, 1, mods, 0, 0)
```

```python
import functools

import jax
import jax.numpy as jnp
import numpy as np
from jax import lax
from jax.experimental import pallas as pl
from jax.experimental.pallas import tpu as pltpu

F32 = jnp.float32
BF16 = jnp.bfloat16

D_MODEL = 1024
GRID_W = 64
CTX_LEN = 256
RMS_EPS = 1e-6
NA_HEADS = 8
NA_HEAD_DIM = 64
NA_WIDTH = NA_HEADS * NA_HEAD_DIM
NA_KH = 8
NA_KW = 16
POOL_WINDOWS = (2, 4, 8, 16)
POOL_GC = 128
POOL_WIDTH = 512
RW_HEAD_DIM = 64
RW_HEADS = D_MODEL // RW_HEAD_DIM
RW_GN_EPS = 64e-5
RW_DECAY_SCALE = float(np.exp(-0.5))
N_EXPERTS = 32
TOP_K = 4
D_FF = D_MODEL
SWIGLU_LIMIT = 7.0
SWIGLU_ALPHA = 1.702

TILE = 256
LANES = 128
SUBLANES = 8
MOE_TM = 512
NEG_BIG = -1e30
VMEM_LIMIT = 48 * 1024 * 1024
MOE_FFN_VMEM_LIMIT = 56 * 1024 * 1024


def _cparams(sem):
    return pltpu.CompilerParams(dimension_semantics=sem, vmem_limit_bytes=VMEM_LIMIT)


def _mod_row(mods_ref, k):
    return mods_ref[0, 0, pl.ds(k, 1), :]


def _norm_mod(x, gain, shift, scale):
    ms = jnp.mean(x * x, axis=-1, keepdims=True)
    return (x * lax.rsqrt(ms + RMS_EPS) * gain) * (1.0 + scale) + shift


def _ada_kernel(c_ref, w_ref, b_ref, o_ref):
    c = c_ref[...]
    s = c * jax.nn.sigmoid(c)
    o_ref[...] = jnp.dot(s, w_ref[...], preferred_element_type=F32,
                         precision=lax.Precision.HIGHEST) + b_ref[...]


def ada_mod(cvec, w, b):
    r, d = cvec.shape
    n = w.shape[1]
    tn = 1536
    return pl.pallas_call(
        _ada_kernel,
        out_shape=jax.ShapeDtypeStruct((r, n), F32),
        grid=(n // tn,),
        in_specs=[pl.BlockSpec((r, d), lambda j: (0, 0)),
                  pl.BlockSpec((d, tn), lambda j: (0, j)),
                  pl.BlockSpec((1, tn), lambda j: (0, j))],
        out_specs=pl.BlockSpec((r, tn), lambda j: (0, j)),
        compiler_params=_cparams(("arbitrary",)),
        name="ada_mod",
    )(cvec, w, b.reshape(1, n))


def _ab_proj_kernel(h_ref, g_ref, mods_ref, w_ref, bd_ref, qkg_ref, qk_ref, v_ref, u_ref):
    x = _norm_mod(h_ref[0], g_ref[...], _mod_row(mods_ref, 0), _mod_row(mods_ref, 1))
    p = jnp.dot(x.astype(BF16), w_ref[...], preferred_element_type=F32)
    qk = p[:, :2 * NA_WIDTH]
    ss = jnp.dot((qk * qk).astype(BF16), bd_ref[...], preferred_element_type=F32)
    qk_ref[0] = (qk * lax.rsqrt(ss + RMS_EPS) * qkg_ref[...]).astype(BF16)
    v_ref[0] = p[:, 2 * NA_WIDTH:3 * NA_WIDTH].astype(BF16)
    u_ref[0] = p[:, 3 * NA_WIDTH:]


def ab_proj(h, gain, mods, w_in, q_g, k_g):
    b, lt, d = h.shape
    nt = lt // TILE
    n_in = w_in.shape[1]
    hid = jnp.arange(2 * NA_WIDTH) // NA_HEAD_DIM
    bd = jnp.where(hid[:, None] == hid[None, :], 1.0 / NA_HEAD_DIM, 0.0).astype(BF16)
    qkg = jnp.concatenate([jnp.tile(q_g, NA_HEADS), jnp.tile(k_g, NA_HEADS)]).reshape(1, -1).astype(F32)
    return pl.pallas_call(
        _ab_proj_kernel,
        out_shape=(jax.ShapeDtypeStruct((b, lt, 2 * NA_WIDTH), BF16),
                   jax.ShapeDtypeStruct((b, lt, NA_WIDTH), BF16),
                   jax.ShapeDtypeStruct((b, lt, POOL_WIDTH), F32)),
        grid=(b, nt),
        in_specs=[pl.BlockSpec((1, TILE, d), lambda bi, i: (bi, i, 0)),
                  pl.BlockSpec((1, d), lambda bi, i: (0, 0)),
                  pl.BlockSpec((1, 1, 6, d), lambda bi, i: (bi, jnp.minimum(i, 1), 0, 0)),
                  pl.BlockSpec((d, n_in), lambda bi, i: (0, 0)),
                  pl.BlockSpec((2 * NA_WIDTH, 2 * NA_WIDTH), lambda bi, i: (0, 0)),
                  pl.BlockSpec((1, 2 * NA_WIDTH), lambda bi, i: (0, 0))],
        out_specs=(pl.BlockSpec((1, TILE, 2 * NA_WIDTH), lambda bi, i: (bi, i, 0)),
                   pl.BlockSpec((1, TILE, NA_WIDTH), lambda bi, i: (bi, i, 0)),
                   pl.BlockSpec((1, TILE, POOL_WIDTH), lambda bi, i: (bi, i, 0))),
        compiler_params=_cparams(("arbitrary", "arbitrary")),
        name="ab_proj",
    )(h, gain.reshape(1, d), mods, w_in.astype(BF16), bd, qkg)


NA_QROWS = TILE // GRID_W
NA_KTILES = 3
NA_NLOC = NA_KTILES * TILE
NA_NKEY = NA_NLOC + CTX_LEN


def _na_bias_tables(rpb, rows):
    n_r, n_c = 2 * NA_KH - 1, 2 * NA_KW - 1
    col = np.arange(GRID_W)
    c_start = np.clip(col - NA_KW // 2, 0, GRID_W - NA_KW)
    valid_c = (col[None, :] >= c_start[:, None]) & (col[None, :] < c_start[:, None] + NA_KW)
    c_rel = np.clip(col[None, :] - col[:, None] + NA_KW - 1, 0, n_c - 1)
    sel_c = (c_rel.reshape(-1)[None, :] == np.arange(n_c)[:, None]).astype(np.float32)
    by_col = jnp.einsum("hrc,cn->hrn", rpb.astype(F32), sel_c, precision=lax.Precision.HIGHEST)
    n_qt = rows // NA_QROWS
    qr, kr = np.arange(NA_QROWS), np.arange(NA_KTILES * NA_QROWS)
    tabs = []
    for jq in (0, 1, n_qt - 1):
        w0 = NA_QROWS * min(max(jq - 1, 0), n_qt - NA_KTILES)
        q_row = NA_QROWS * jq + qr
        key_row = w0 + kr
        r_start = np.clip(q_row - NA_KH // 2, 0, rows - NA_KH)
        valid_r = (key_row[None, :] >= r_start[:, None]) & (key_row[None, :] < r_start[:, None] + NA_KH)
        r_rel = np.clip(key_row[None, :] - q_row[:, None] + NA_KH - 1, 0, n_r - 1)
        sel_r = (r_rel.reshape(-1)[:, None] == np.arange(n_r)[None, :]).astype(np.float32)
        t = jnp.einsum("xr,hrn->hxn", sel_r, by_col, precision=lax.Precision.HIGHEST)
        t = t.reshape(NA_HEADS, NA_QROWS, NA_KTILES * NA_QROWS, GRID_W, GRID_W)
        t = t.transpose(0, 1, 3, 2, 4).reshape(NA_HEADS, TILE, NA_NLOC)
        valid = (valid_r[:, None, :, None] & valid_c[None, :, None, :]).reshape(TILE, NA_NLOC)
        tabs.append(jnp.where(valid[None], t, NEG_BIG))
    tabs = [jnp.full_like(tabs[0], NEG_BIG)] + tabs
    loc = jnp.stack(tabs)
    return jnp.concatenate([loc, jnp.zeros(loc.shape[:3] + (CTX_LEN,), F32)], axis=-1)


def _na_kernel(q_ref, k0_ref, k1_ref, k2_ref, kc_ref, v0_ref, v1_ref, v2_ref, vc_ref, bias_ref, o_ref):
    q = q_ref[0]
    kall = jnp.concatenate([k0_ref[0], k1_ref[0], k2_ref[0], kc_ref[0]], axis=0)
    vall = jnp.concatenate([v0_ref[0], v1_ref[0], v2_ref[0], vc_ref[0]], axis=0)
    lane = lax.broadcasted_iota(jnp.int32, (TILE, LANES), 1)
    scale = jnp.asarray(NA_HEAD_DIM ** -0.5, BF16)
    heads = range(NA_HEADS)
    pair = [slice((h // 2) * LANES, (h // 2 + 1) * LANES) for h in heads]
    scores = []
    for h in heads:
        lo = (h % 2) * NA_HEAD_DIM
        qh = jnp.where((lane >= lo) & (lane < lo + NA_HEAD_DIM), q[:, pair[h]], jnp.zeros((), BF16)) * scale
        s = lax.dot_general(qh, kall[:, pair[h]], (((1,), (1,)), ((), ())), preferred_element_type=F32)
        scores.append(s + bias_ref[0, h])
    probs, denom = [], []
    for s in scores:
        p = jnp.exp(s - jnp.max(s, axis=-1, keepdims=True))
        probs.append(p.astype(BF16))
        denom.append(jnp.sum(p, axis=-1, keepdims=True))
    outs = [jnp.dot(probs[h], vall[:, pair[h]], preferred_element_type=F32) / denom[h] for h in heads]
    for h in range(0, NA_HEADS, 2):
        o_ref[0, :, pair[h]] = jnp.where(lane < NA_HEAD_DIM, outs[h], outs[h + 1]).astype(BF16)


def na_attention(qk, v, bias):
    b, lt, _ = qk.shape
    nt = lt // TILE
    n_qt = nt - 1
    assert NA_HEAD_DIM ** -0.5 == 0.125 and NA_HEADS % 2 == 0
    w = NA_WIDTH

    def w0(j):
        return 1 + jnp.clip(j - 2, 0, n_qt - NA_KTILES)

    def kind(j):
        return jnp.where(j == 0, 0, jnp.where(j == 1, 1, jnp.where(j == nt - 1, 3, 2)))

    qspec = pl.BlockSpec((1, TILE, w), lambda j, bi: (bi, j, 0))
    kspecs = [pl.BlockSpec((1, TILE, w), functools.partial(lambda j, bi, t: (bi, w0(j) + t, 1), t=t))
              for t in range(NA_KTILES)]
    kcspec = pl.BlockSpec((1, TILE, w), lambda j, bi: (bi, 0, 1))
    vspecs = [pl.BlockSpec((1, TILE, w), functools.partial(lambda j, bi, t: (bi, w0(j) + t, 0), t=t))
              for t in range(NA_KTILES)]
    vcspec = pl.BlockSpec((1, TILE, w), lambda j, bi: (bi, 0, 0))
    bspec = pl.BlockSpec((1, NA_HEADS, TILE, NA_NKEY), lambda j, bi: (kind(j), 0, 0, 0))
    return pl.pallas_call(
        _na_kernel,
        out_shape=jax.ShapeDtypeStruct((b, lt, w), BF16),
        grid=(nt, b),
        in_specs=[qspec] + kspecs + [kcspec] + vspecs + [vcspec, bspec],
        out_specs=pl.BlockSpec((1, TILE, w), lambda j, bi: (bi, j, 0)),
        compiler_params=_cparams(("arbitrary", "arbitrary")),
        name="na_attention",
    )(qk, qk, qk, qk, qk, v, v, v, v, bias)


POOL_HALO = 8
POOL_ROWS = TILE + 2 * POOL_HALO


def _pool_kernel(cur_ref, prev_ref, next_ref, pw_ref, ps_ref, o_ref, xp_ref, *, n_tiles):
    i = pl.program_id(1)
    has_prev = (i >= 2).astype(F32)
    has_next = ((i >= 1) & (i < n_tiles - 1)).astype(F32)
    xp_ref[0:POOL_HALO, :] = prev_ref[0] * has_prev
    xp_ref[POOL_HALO:POOL_HALO + TILE, :] = cur_ref[0]
    xp_ref[POOL_HALO + TILE:POOL_ROWS, :] = next_ref[0] * has_next
    row = lax.broadcasted_iota(jnp.int32, (TILE, POOL_GC), 0)
    tpos = jnp.where(i == 0, row, row + (i - 1) * TILE)
    seq_len = jnp.where(i == 0, CTX_LEN, (n_tiles - 1) * TILE)
    for g, w in enumerate(POOL_WINDOWS):
        cols = slice(g * POOL_GC, (g + 1) * POOL_GC)
        x = xp_ref[:, cols]
        s = x + pltpu.roll(x, 1, 0)
        half = 1
        while 2 * half < w:
            s = pltpu.roll(s, half, 0) + pltpu.roll(s, POOL_ROWS - half, 0)
            half *= 2
        s = s[POOL_HALO:POOL_HALO + TILE]
        cnt = jnp.minimum(w // 2, tpos) + jnp.minimum(w // 2, seq_len - tpos)
        pooled = s / cnt.astype(F32) - cur_ref[0, :, cols]
        y = jnp.dot(pooled.astype(BF16), pw_ref[g], preferred_element_type=F32)
        o_ref[0, :, cols] = (y * ps_ref[:, cols]).astype(BF16)


def multiscale_pool(u, pool_w, pool_scale):
    b, lt, c = u.shape
    nt = lt // TILE
    hb = TILE // POOL_HALO
    nhb = lt // POOL_HALO
    assert POOL_HALO >= max(POOL_WINDOWS) // 2 and POOL_HALO == SUBLANES
    return pl.pallas_call(
        functools.partial(_pool_kernel, n_tiles=nt),
        out_shape=jax.ShapeDtypeStruct((b, lt, c), BF16),
        grid=(b, nt),
        in_specs=[pl.BlockSpec((1, TILE, c), lambda bi, i: (bi, i, 0)),
                  pl.BlockSpec((1, POOL_HALO, c), lambda bi, i: (bi, jnp.maximum(i * hb - 1, 0), 0)),
                  pl.BlockSpec((1, POOL_HALO, c), lambda bi, i: (bi, jnp.minimum((i + 1) * hb, nhb - 1), 0)),
                  pl.BlockSpec((len(POOL_WINDOWS), POOL_GC, POOL_GC), lambda bi, i: (0, 0, 0)),
                  pl.BlockSpec((1, c), lambda bi, i: (0, 0))],
        out_specs=pl.BlockSpec((1, TILE, c), lambda bi, i: (bi, i, 0)),
        scratch_shapes=[pltpu.VMEM((POOL_ROWS, c), F32)],
        compiler_params=_cparams(("arbitrary", "arbitrary")),
        name="multiscale_pool",
    )(u, u, u, pool_w.astype(BF16), pool_scale.reshape(1, c).astype(F32))


def _out_proj_kernel(a_ref, b_ref, h_ref, mods_ref, w_ref, o_ref):
    ka = a_ref.shape[-1]
    y = jnp.dot(a_ref[0], w_ref[:ka, :], preferred_element_type=F32)
    y = y + jnp.dot(b_ref[0], w_ref[ka:, :], preferred_element_type=F32)
    o_ref[0] = h_ref[0] + _mod_row(mods_ref, 2) * y


def out_proj_residual(a, bb, h, mods, w):
    b, lt, d = h.shape
    nt = lt // TILE
    ka, kb = a.shape[-1], bb.shape[-1]
    return pl.pallas_call(
        _out_proj_kernel,
        out_shape=jax.ShapeDtypeStruct((b, lt, d), F32),
        grid=(b, nt),
        in_specs=[pl.BlockSpec((1, TILE, ka), lambda bi, i: (bi, i, 0)),
                  pl.BlockSpec((1, TILE, kb), lambda bi, i: (bi, i, 0)),
                  pl.BlockSpec((1, TILE, d), lambda bi, i: (bi, i, 0)),
                  pl.BlockSpec((1, 1, 6, d), lambda bi, i: (bi, jnp.minimum(i, 1), 0, 0)),
                  pl.BlockSpec((ka + kb, d), lambda bi, i: (0, 0))],
        out_specs=pl.BlockSpec((1, TILE, d), lambda bi, i: (bi, i, 0)),
        compiler_params=_cparams(("arbitrary", "arbitrary")),
        name="out_proj_residual",
    )(a, bb, h, mods, w.astype(BF16))


def _moe_pre_kernel(h_ref, g_ref, mods_ref, rwh_ref, rwl_ref, rb_ref, y_ref, lg_ref):
    x = _norm_mod(h_ref[0], g_ref[...], _mod_row(mods_ref, 3), _mod_row(mods_ref, 4))
    x_hi = x.astype(BF16)
    y_ref[0] = x_hi
    x_lo = (x - x_hi.astype(F32)).astype(BF16)
    w_hi, w_lo = rwh_ref[...], rwl_ref[...]
    lg_ref[0] = (jnp.dot(x_hi, w_hi, preferred_element_type=F32)
                 + jnp.dot(x_lo, w_hi, preferred_element_type=F32)
                 + jnp.dot(x_hi, w_lo, preferred_element_type=F32)
                 + jnp.dot(x_lo, w_lo, preferred_element_type=F32)) + rb_ref[...]


def moe_pre(h, gain, mods, router_w, router_b, tile0, ctx_tiles):
    b, lt, d = h.shape
    nt = lt // TILE - tile0
    e = router_w.shape[1]
    head = lax.bitcast_convert_type(lax.bitcast_convert_type(router_w, jnp.uint32) & jnp.uint32(0xFFFF0000), F32)
    rw_hi = head.astype(BF16)
    rw_lo = (router_w - head).astype(BF16)
    return pl.pallas_call(
        _moe_pre_kernel,
        out_shape=(jax.ShapeDtypeStruct((b, nt * TILE, d), BF16),
                   jax.ShapeDtypeStruct((b, nt * TILE, e), F32)),
        grid=(b, nt),
        in_specs=[pl.BlockSpec((1, TILE, d), lambda bi, i: (bi, i + tile0, 0)),
                  pl.BlockSpec((1, d), lambda bi, i: (0, 0)),
                  pl.BlockSpec((1, 1, 6, d), lambda bi, i: (bi, jnp.minimum(i + tile0 + 1 - ctx_tiles, 1), 0, 0)),
                  pl.BlockSpec((d, e), lambda bi, i: (0, 0)),
                  pl.BlockSpec((d, e), lambda bi, i: (0, 0)),
                  pl.BlockSpec((1, e), lambda bi, i: (0, 0))],
        out_specs=(pl.BlockSpec((1, TILE, d), lambda bi, i: (bi, i, 0)),
                   pl.BlockSpec((1, TILE, e), lambda bi, i: (bi, i, 0))),
        compiler_params=_cparams(("arbitrary", "arbitrary")),
        name="moe_pre",
    )(h, gain.reshape(1, d), mods, rw_hi, rw_lo, router_b.reshape(1, e))


MOE_PERM_W = 256


def _moe_ffn_kernel(te_ref, tv_ref, x_ref, win_ref, bg_ref, bl_ref, wo_ref, bo_ref, perm_ref, o_ref,
                    wg_s, wl_s, wo_s):
    t = pl.program_id(0)
    valid = tv_ref[t] > 0
    new_expert = (t == 0) | (te_ref[t] != te_ref[jnp.maximum(t - 1, 0)])

    @pl.when(valid & new_expert)
    def _():
        half = MOE_PERM_W // 2
        for cb in range(win_ref.shape[3] // MOE_PERM_W):
            w = win_ref[0, 0, :, cb * MOE_PERM_W:(cb + 1) * MOE_PERM_W].astype(BF16)
            wp = jnp.dot(w, perm_ref[...], preferred_element_type=F32).astype(BF16)
            wg_s[:, cb * half:(cb + 1) * half] = wp[:, :half]
            wl_s[:, cb * half:(cb + 1) * half] = wp[:, half:]
        wo_s[...] = wo_ref[0, 0].astype(BF16)

    @pl.when(jnp.logical_not(valid))
    def _():
        o_ref[...] = jnp.zeros_like(o_ref)

    @pl.when(valid)
    def _():
        x = x_ref[...].astype(BF16)
        hg = jnp.dot(x, wg_s[...], preferred_element_type=F32) + bg_ref[0]
        hl = jnp.dot(x, wl_s[...], preferred_element_type=F32) + bl_ref[0]
        glu = jnp.minimum(hg, SWIGLU_LIMIT)
        lin = jnp.clip(hl, -SWIGLU_LIMIT, SWIGLU_LIMIT)
        act = glu * jax.nn.sigmoid(SWIGLU_ALPHA * glu) * (lin + 1.0)
        o_ref[...] = (jnp.dot(act.astype(BF16), wo_s[...], preferred_element_type=F32)
                      + bo_ref[0]).astype(o_ref.dtype)


def moe_grouped_ffn(x_sorted, tile_expert, tile_valid, layer, w_in, bg, bl, w_out, bo):
    p, d = x_sorted.shape
    _, e, _, f2 = w_in.shape
    f = f2 // 2
    nt = p // MOE_TM
    assert f2 % MOE_PERM_W == 0 and MOE_PERM_W % (2 * LANES) == 0
    src = jnp.arange(MOE_PERM_W)
    dst = jnp.where(src % 2 == 0, src // 2, MOE_PERM_W // 2 + src // 2)
    perm = (dst[:, None] == jnp.arange(MOE_PERM_W)[None, :]).astype(BF16)
    grid_spec = pltpu.PrefetchScalarGridSpec(
        num_scalar_prefetch=2,
        grid=(nt,),
        in_specs=[pl.BlockSpec((MOE_TM, d), lambda t, te, tv: (t * tv[t], 0)),
                  pl.BlockSpec((1, 1, d, f2), lambda t, te, tv: (layer, te[t], 0, 0)),
                  pl.BlockSpec((1, 1, f), lambda t, te, tv: (te[t], 0, 0)),
                  pl.BlockSpec((1, 1, f), lambda t, te, tv: (te[t], 0, 0)),
                  pl.BlockSpec((1, 1, f, d), lambda t, te, tv: (layer, te[t], 0, 0)),
                  pl.BlockSpec((1, 1, d), lambda t, te, tv: (te[t], 0, 0)),
                  pl.BlockSpec((MOE_PERM_W, MOE_PERM_W), lambda t, te, tv: (0, 0))],
        out_specs=pl.BlockSpec((MOE_TM, d), lambda t, te, tv: (t, 0)),
        scratch_shapes=[pltpu.VMEM((d, f), BF16), pltpu.VMEM((d, f), BF16), pltpu.VMEM((f, d), BF16)],
    )
    return pl.pallas_call(
        _moe_ffn_kernel,
        out_shape=jax.ShapeDtypeStruct((p, d), BF16),
        grid_spec=grid_spec,
        compiler_params=pltpu.CompilerParams(dimension_semantics=("arbitrary",),
                                             vmem_limit_bytes=MOE_FFN_VMEM_LIMIT),
        name="moe_grouped_ffn",
    )(tile_expert, tile_valid, x_sorted, w_in, bg, bl, w_out, bo, perm)


def _moe_combine_kernel(z_ref, gt_ref, h_ref, mods_ref, o_ref):
    gt = gt_ref[0]
    acc = gt[:, 0:1] * z_ref[0, 0].astype(F32)
    for k in range(1, TOP_K):
        acc = acc + gt[:, k:k + 1] * z_ref[k, 0].astype(F32)
    o_ref[0] = h_ref[0] + _mod_row(mods_ref, 5) * acc


def moe_combine(z4, gates, h, mods, tile0, ctx_tiles):
    k, b, lp, d = z4.shape
    nt = lp // TILE
    return pl.pallas_call(
        _moe_combine_kernel,
        out_shape=jax.ShapeDtypeStruct((b, lp, d), F32),
        grid=(b, nt),
        in_specs=[pl.BlockSpec((k, 1, TILE, d), lambda bi, i: (0, bi, i, 0)),
                  pl.BlockSpec((1, TILE, k), lambda bi, i: (bi, i, 0)),
                  pl.BlockSpec((1, TILE, d), lambda bi, i: (bi, i + tile0, 0)),
                  pl.BlockSpec((1, 1, 6, d), lambda bi, i: (bi, jnp.minimum(i + tile0 + 1 - ctx_tiles, 1), 0, 0))],
        out_specs=pl.BlockSpec((1, TILE, d), lambda bi, i: (bi, i, 0)),
        compiler_params=_cparams(("arbitrary", "arbitrary")),
        name="moe_combine",
    )(z4, gates, h, mods)


DISPATCH_T = 512
DISPATCH_BLK = SUBLANES
DISPATCH_ROWS = -(-(DISPATCH_T * TOP_K + N_EXPERTS * (DISPATCH_BLK - 1)) // (2 * SUBLANES)) * (2 * SUBLANES)
DISPATCH_COPY_ROWS = (4 * DISPATCH_BLK, DISPATCH_BLK)
DISPATCH_LIST = max(DISPATCH_ROWS // DISPATCH_COPY_ROWS[0],
                    N_EXPERTS * (DISPATCH_COPY_ROWS[0] // DISPATCH_COPY_ROWS[1] - 1))


def _moe_dispatch_kernel(cnt_ref, tab_ref, ntail_ref, tail_ref, y_ref, idx_ref, off_ref, base_ref,
                         xs_ref, slot_ref, local_ref, zero_ref, sem, zero_sem):
    t = pl.program_id(0)
    last_t = pl.num_programs(0) - 1
    dt = DISPATCH_T
    idx = idx_ref[0]
    e_iota = lax.broadcasted_iota(jnp.int32, (N_EXPERTS, dt), 0)
    picked = [e_iota == idx[k:k + 1, :] for k in range(TOP_K)]
    cnt = sum(jnp.where(p, 1.0, 0.0) for p in picked).astype(BF16)
    r0 = lax.broadcasted_iota(jnp.int32, (dt, dt), 0)
    r1 = lax.broadcasted_iota(jnp.int32, (dt, dt), 1)
    earlier = jnp.where(r0 < r1, 1.0, 0.0).astype(BF16)
    rank = jnp.dot(cnt, earlier, preferred_element_type=F32)
    local_row = rank + off_ref[0].astype(F32)
    global_row = rank + base_ref[0].astype(F32)
    rows = lax.broadcasted_iota(jnp.int32, (DISPATCH_ROWS, dt), 0)
    place = None
    for k in range(TOP_K):
        pos = jnp.sum(jnp.where(picked[k], local_row, 0.0), axis=0, keepdims=True).astype(jnp.int32)
        slot_ref[0, k:k + 1, :] = jnp.sum(jnp.where(picked[k], global_row, 0.0), axis=0,
                                          keepdims=True).astype(jnp.int32)
        hit = rows == pos
        place = hit if place is None else (place | hit)
    placement = jnp.where(place, 1.0, 0.0).astype(BF16)
    buf = lax.rem(t, 2)
    local_ref[buf] = jnp.dot(placement, y_ref[...], preferred_element_type=F32)

    def seg_copy(cls, step, j):
        rows = DISPATCH_COPY_ROWS[cls]
        src_buf = jnp.where(step == t, buf, 1 - buf)
        src_blk = tab_ref[step, (2 * cls) * DISPATCH_LIST + j]
        dst_blk = tab_ref[step, (2 * cls + 1) * DISPATCH_LIST + j]
        src = local_ref.at[src_buf, pl.ds(pl.multiple_of(src_blk * DISPATCH_BLK, DISPATCH_BLK), rows), :]
        dst = xs_ref.at[pl.ds(pl.multiple_of(dst_blk * DISPATCH_BLK, DISPATCH_BLK), rows), :]
        return pltpu.make_async_copy(src, dst, sem)

    def tail_copy(e, j):
        blk = tail_ref[e] + j
        dst = xs_ref.at[pl.ds(pl.multiple_of(blk * DISPATCH_BLK, DISPATCH_BLK), DISPATCH_BLK), :]
        return pltpu.make_async_copy(zero_ref, dst, zero_sem)

    def start_all(copy, n):
        lax.fori_loop(0, n, lambda j, c: (copy(j).start(), c)[1], 0)

    def wait_all(copy, n):
        lax.fori_loop(0, n, lambda j, c: (copy(j).wait(), c)[1], 0)

    classes = range(len(DISPATCH_COPY_ROWS))

    @pl.when(t > 0)
    def _():
        for cls in classes:
            wait_all(functools.partial(seg_copy, cls, t - 1), cnt_ref[t - 1, cls])

    for cls in classes:
        start_all(functools.partial(seg_copy, cls, t), cnt_ref[t, cls])

    @pl.when(t == last_t)
    def _():
        for cls in classes:
            wait_all(functools.partial(seg_copy, cls, t), cnt_ref[t, cls])

    @pl.when(t == 0)
    def _():
        zero_ref[...] = jnp.zeros_like(zero_ref)
        for e in range(N_EXPERTS + 1):
            start_all(functools.partial(tail_copy, e), ntail_ref[e])
        for e in range(N_EXPERTS + 1):
            wait_all(functools.partial(tail_copy, e), ntail_ref[e])


def moe_dispatch(y2, idx_t, copy_cnt, copy_tab, ntail, tail_blk, loc_off, base, p_rows):
    n, d = y2.shape
    nt = n // DISPATCH_T
    grid_spec = pltpu.PrefetchScalarGridSpec(
        num_scalar_prefetch=4,
        grid=(nt,),
        in_specs=[pl.BlockSpec((DISPATCH_T, d), lambda t, *_: (t, 0)),
                  pl.BlockSpec((1, TOP_K, DISPATCH_T), lambda t, *_: (t, 0, 0)),
                  pl.BlockSpec((1, N_EXPERTS, 1), lambda t, *_: (t, 0, 0)),
                  pl.BlockSpec((1, N_EXPERTS, 1), lambda t, *_: (t, 0, 0))],
        out_specs=(pl.BlockSpec(memory_space=pl.ANY),
                   pl.BlockSpec((1, TOP_K, DISPATCH_T), lambda t, *_: (t, 0, 0))),
        scratch_shapes=[pltpu.VMEM((2, DISPATCH_ROWS, d), F32), pltpu.VMEM((DISPATCH_BLK, d), F32),
                        pltpu.SemaphoreType.DMA(()), pltpu.SemaphoreType.DMA(())],
    )
    return pl.pallas_call(
        _moe_dispatch_kernel,
        out_shape=(jax.ShapeDtypeStruct((p_rows, d), F32),
                   jax.ShapeDtypeStruct((nt, TOP_K, DISPATCH_T), jnp.int32)),
        grid_spec=grid_spec,
        compiler_params=_cparams(("arbitrary",)),
        name="moe_dispatch",
    )(copy_cnt, copy_tab, ntail, tail_blk, y2, idx_t, loc_off, base)


def moe_layer(h, gain, mods, router_w, router_b, layer, w_in, b_in, w_out, b_out, tile0, ctx_tiles):
    b, lt, d = h.shape
    y, logits = moe_pre(h, gain, mods, router_w, router_b, tile0, ctx_tiles)
    lp = y.shape[1]
    n = b * lp
    y2 = y.reshape(n, d)
    top_val, top_idx = lax.top_k(logits.reshape(n, N_EXPERTS), TOP_K)
    gates = jax.nn.softmax(top_val, axis=-1)

    assert n % DISPATCH_T == 0
    t_n = n // DISPATCH_T
    blk = DISPATCH_BLK
    experts = jnp.arange(N_EXPERTS, dtype=jnp.int32)
    idx_t = top_idx.astype(jnp.int32).reshape(t_n, DISPATCH_T, TOP_K).transpose(0, 2, 1)
    tile_cnt = jnp.sum((idx_t[..., None] == experts).astype(jnp.int32), axis=(1, 2))
    seg = (tile_cnt + blk - 1) // blk * blk
    loc_end = jnp.cumsum(seg, axis=1)
    loc_off = loc_end - seg
    tot = jnp.sum(seg, axis=0)
    padded = (tot + MOE_TM - 1) // MOE_TM * MOE_TM
    ends = jnp.cumsum(padded)
    starts = ends - padded
    base = starts[None, :] + jnp.cumsum(seg, axis=0) - seg
    q = jnp.arange(DISPATCH_LIST, dtype=jnp.int32)
    covered = jnp.zeros_like(seg)
    tabs, cnts = [], []
    for rows in DISPATCH_COPY_ROWS:
        n_c = (seg - covered) // rows
        c_end = jnp.cumsum(n_c, axis=1)
        sel = jnp.sum((c_end[:, None, :] <= q[None, :, None]).astype(jnp.int32), axis=2)[..., None] == experts
        pick = lambda a: jnp.sum(jnp.where(sel, a[:, None, :], 0), axis=2)
        inside = (q[None, :] - pick(c_end - n_c)) * rows
        tabs += [(pick(loc_off + covered) + inside) // blk, (pick(base + covered) + inside) // blk]
        cnts.append(c_end[:, -1])
        covered = covered + n_c * rows
    copy_tab = jnp.concatenate(tabs, axis=1)
    copy_cnt = jnp.stack(cnts, axis=1)
    n_tiles = -(-(n * TOP_K + t_n * N_EXPERTS * (blk - 1)) // MOE_TM) + N_EXPERTS
    tile_start = jnp.arange(n_tiles, dtype=jnp.int32) * MOE_TM
    tile_expert = jnp.minimum(jnp.sum((tile_start[:, None] >= ends[None, :]).astype(jnp.int32), axis=1),
                              N_EXPERTS - 1)
    tile_valid = (tile_start < ends[-1]).astype(jnp.int32)

    p_rows = n_tiles * MOE_TM
    gap_len = jnp.concatenate([padded - tot, p_rows - ends[-1:]]) // blk
    gap_start = jnp.concatenate([starts + tot, ends[-1:]]) // blk
    x_sorted, slot = moe_dispatch(y2, idx_t, copy_cnt, copy_tab, gap_len, gap_start,
                                  loc_off[..., None], base[..., None], p_rows)
    bg = b_in[layer, :, 0::2].reshape(N_EXPERTS, 1, D_FF)
    bl = b_in[layer, :, 1::2].reshape(N_EXPERTS, 1, D_FF)
    z = moe_grouped_ffn(x_sorted, tile_expert, tile_valid, layer, w_in, bg, bl, w_out,
                        b_out[layer].reshape(N_EXPERTS, 1, d))
    slot_kn = slot.transpose(1, 0, 2).reshape(TOP_K, n)
    z4 = z.at[slot_kn].get(mode="promise_in_bounds").reshape(TOP_K, b, lp, d)
    return moe_combine(z4, gates.reshape(b, lp, TOP_K), h, mods, tile0, ctx_tiles)


def _rwkv_feat_kernel(h_ref, hp_ref, hn_ref, g_ref, mods_ref, mu_ref, wr_ref, wk_ref, wv_ref,
                      w1_ref, w2_ref, a1_ref, a2_ref, g1_ref, g2_ref, w0_ref, a0_ref, kk_ref, ka_ref,
                      bd_ref, r_out, v_out, kk_out, g_out, lw_out, kd_out, a_out, *, n_tiles):
    i = pl.program_id(1)
    gain, shift, scale = g_ref[...], _mod_row(mods_ref, 0), _mod_row(mods_ref, 1)
    x = _norm_mod(h_ref[0], gain, shift, scale)
    has_prev = (i >= 2).astype(F32)
    has_next = ((i >= 1) & (i < n_tiles - 1)).astype(F32)
    x_before = _norm_mod(hp_ref[0], gain, shift, scale)[SUBLANES - 1:SUBLANES] * has_prev
    x_after = _norm_mod(hn_ref[0], gain, shift, scale)[0:1] * has_next
    row = lax.broadcasted_iota(jnp.int32, x.shape, 0)
    prev = jnp.where(row == 0, x_before, pltpu.roll(x, 1, 0))
    nxt = jnp.where(row == TILE - 1, x_after, pltpu.roll(x, TILE - 1, 0))
    xx = 0.5 * (prev + nxt) - x
    xr, xw, xk, xv, xa, xg = (x + xx * mu_ref[pl.ds(j, 1), :] for j in range(6))

    def mm(a, w):
        return jnp.dot(a.astype(BF16), w, preferred_element_type=F32)

    r = mm(xr, wr_ref[...])
    k = mm(xk, wk_ref[...])
    v = mm(xv, wv_ref[...])
    g = mm(jax.nn.sigmoid(mm(xg, g1_ref[...])), g2_ref[...])
    kk = k * kk_ref[...]
    kk = kk * lax.rsqrt(mm(kk * kk, bd_ref[...]) + 1e-12)
    r_out[0] = r.astype(BF16)
    v_out[0] = v.astype(BF16)
    kk_out[0] = kk.astype(BF16)
    g_out[0] = g.astype(BF16)
    tw = jnp.tanh(mm(xw, w1_ref[...]))
    av = mm(xa, a1_ref[...])
    lane = lax.broadcasted_iota(jnp.int32, tw.shape, 1)
    half = tw.shape[1] // 2
    for d in range(2):
        in_dir = (lane >= d * half) & (lane < (d + 1) * half)
        wz = w0_ref[pl.ds(d, 1), :] + mm(jnp.where(in_dir, tw, 0.0), w2_ref[...])
        lw_out[d, 0] = -RW_DECAY_SCALE * jax.nn.sigmoid(wz)
        a = jax.nn.sigmoid(a0_ref[pl.ds(d, 1), :] + mm(jnp.where(in_dir, av, 0.0), a2_ref[...]))
        kd_out[d, 0] = (k * (1.0 + (a - 1.0) * ka_ref[...])).astype(BF16)
        a_out[d, 0] = a.astype(BF16)


def rwkv_features(h, gain, mods, mu, w_r, w_k, w_v, w0, w1, w2, a0, a1, a2, g1, g2, k_k, k_a):
    b, lt, d = h.shape
    nt = lt // TILE
    hb = TILE // SUBLANES
    nhb = lt // SUBLANES
    hid = jnp.arange(d) // RW_HEAD_DIM
    bd = (hid[:, None] == hid[None, :]).astype(BF16)
    w1c = jnp.concatenate([w1[0], w1[1]], axis=1).astype(BF16)
    a1c = jnp.concatenate([a1[0], a1[1]], axis=1).astype(BF16)
    w2c = jnp.concatenate([w2[0], w2[1]], axis=0).astype(BF16)
    a2c = jnp.concatenate([a2[0], a2[1]], axis=0).astype(BF16)
    full = lambda shape: pl.BlockSpec(shape, lambda bi, i: (0,) * len(shape))
    tile_spec = pl.BlockSpec((1, TILE, d), lambda bi, i: (bi, i, 0))
    dir_spec = pl.BlockSpec((2, 1, TILE, d), lambda bi, i: (0, bi, i, 0))
    seq_bf = jax.ShapeDtypeStruct((b, lt, d), BF16)
    return pl.pallas_call(
        functools.partial(_rwkv_feat_kernel, n_tiles=nt),
        out_shape=(seq_bf, seq_bf, seq_bf, seq_bf,
                   jax.ShapeDtypeStruct((2, b, lt, d), F32),
                   jax.ShapeDtypeStruct((2, b, lt, d), BF16),
                   jax.ShapeDtypeStruct((2, b, lt, d), BF16)),
        grid=(b, nt),
        in_specs=[tile_spec,
                  pl.BlockSpec((1, SUBLANES, d), lambda bi, i: (bi, jnp.maximum(i * hb - 1, 0), 0)),
                  pl.BlockSpec((1, SUBLANES, d), lambda bi, i: (bi, jnp.minimum((i + 1) * hb, nhb - 1), 0)),
                  full((1, d)),
                  pl.BlockSpec((1, 1, 6, d), lambda bi, i: (bi, jnp.minimum(i, 1), 0, 0)),
                  full((6, d)), full((d, d)), full((d, d)), full((d, d)),
                  full(w1c.shape), full(w2c.shape), full(a1c.shape), full(a2c.shape),
                  full(g1.shape), full(g2.shape), full((2, d)), full((2, d)), full((1, d)), full((1, d)),
                  full((d, d))],
        out_specs=(tile_spec, tile_spec, tile_spec, tile_spec, dir_spec, dir_spec, dir_spec),
        compiler_params=_cparams(("arbitrary", "arbitrary")),
        name="rwkv_features",
    )(h, h, h, gain.reshape(1, d), mods, mu, w_r.astype(BF16), w_k.astype(BF16), w_v.astype(BF16),
      w1c, w2c, a1c, a2c, g1.astype(BF16), g2.astype(BF16), w0, a0, k_k.reshape(1, d), k_a.reshape(1, d), bd)


SCAN_C = 64
SCAN_G = 4
SCAN_GW = SCAN_G * RW_HEAD_DIM


def _scan_chunk_inputs(reverse, r_ref, v_ref, kk_ref, lw_ref, kd_ref, a_ref):
    cc = SCAN_C
    ti = lax.broadcasted_iota(jnp.int32, (cc, cc), 0)
    si = lax.broadcasted_iota(jnp.int32, (cc, cc), 1)
    before_incl = (si >= ti) if reverse else (si <= ti)
    lw = lw_ref[0, 0]
    tri = jnp.where(before_incl, 1.0, 0.0).astype(BF16)
    lw_hi = lw.astype(BF16)
    lw_lo = (lw - lw_hi.astype(F32)).astype(BF16)
    cl = (jnp.dot(tri, lw_hi, preferred_element_type=F32) + jnp.dot(tri, lw_lo, preferred_element_type=F32))
    g_in = jnp.exp(cl)
    g_ex = jnp.exp(cl - lw)
    g_inv = jnp.exp(-cl)
    g_end = jnp.exp(jnp.sum(lw, axis=0, keepdims=True))
    kk = kk_ref[0].astype(F32)
    rt = (r_ref[0].astype(F32) * g_in).astype(BF16)
    at = (-kk * g_ex).astype(BF16)
    bt = (kk * a_ref[0, 0].astype(F32) * g_inv).astype(BF16)
    kt = (kd_ref[0, 0].astype(F32) * g_inv).astype(BF16)
    return rt, at, bt, kt, v_ref[0], g_end


def _rwkv_scan_kernel(rf_ref, vf_ref, kkf_ref, rb_ref, vb_ref, kkb_ref, lwf_ref, kdf_ref, af_ref,
                      lwb_ref, kdb_ref, ab_ref, yf_ref, yb_ref, h_ref):
    @pl.when(pl.program_id(1) == 0)
    def _():
        h_ref[...] = jnp.zeros_like(h_ref)

    cc, gw = SCAN_C, SCAN_GW
    ops = (_scan_chunk_inputs(False, rf_ref, vf_ref, kkf_ref, lwf_ref, kdf_ref, af_ref),
           _scan_chunk_inputs(True, rb_ref, vb_ref, kkb_ref, lwb_ref, kdb_ref, ab_ref))
    y_refs = (yf_ref, yb_ref)

    tg = lax.broadcasted_iota(jnp.int32, (cc, gw), 0)
    sg = lax.broadcasted_iota(jnp.int32, (cc, gw), 1) % cc
    m_strict = (sg < tg, sg > tg)
    m_incl = (sg <= tg, sg >= tg)
    eye_g = (sg == tg).astype(F32)
    bi = lax.broadcasted_iota(jnp.int32, (gw, gw), 0) // cc
    bj = lax.broadcasted_iota(jnp.int32, (gw, gw), 1) // cc
    blk = bi == bj

    def bdiag(x):
        return jnp.where(blk, jnp.concatenate([x] * SCAN_G, axis=0), jnp.zeros((), x.dtype))

    def mm(a, b):
        return jnp.dot(a, b, preferred_element_type=F32)

    def mm_nt(a, b):
        return lax.dot_general(a, b, (((1,), (1,)), ((), ())), preferred_element_type=F32)

    n_g = RW_HEADS // SCAN_G
    chains = [(d, g) for d in range(2) for g in range(n_g)]
    cols = [slice(g * gw, (g + 1) * gw) for _, g in chains]
    rt, at, bt, kt, v, g_end = ([ops[d][j][:, cols[i]] for i, (d, _) in enumerate(chains)] for j in range(6))
    idx = range(len(chains))
    h0 = [h_ref[i] for i in idx]
    h0b = [x.astype(BF16) for x in h0]
    v_bd = [bdiag(x) for x in v]
    mats = [mm_nt(jnp.concatenate([at[i], rt[i]], axis=0),
                  jnp.concatenate([bdiag(bt[i]), bdiag(kt[i])], axis=0)) for i in idx]
    a_ab = [jnp.where(m_strict[chains[i][0]], mats[i][:cc, :gw], 0.0) for i in idx]
    a_ak = [jnp.where(m_strict[chains[i][0]], mats[i][:cc, gw:], 0.0).astype(BF16) for i in idx]
    m_rb = [jnp.where(m_incl[chains[i][0]], mats[i][cc:, :gw], 0.0).astype(BF16) for i in idx]
    m_rk = [jnp.where(m_incl[chains[i][0]], mats[i][cc:, gw:], 0.0).astype(BF16) for i in idx]
    from_h = [mm(jnp.concatenate([at[i], rt[i]], axis=0), h0b[i]) for i in idx]
    from_v = [mm(jnp.concatenate([a_ak[i], m_rk[i]], axis=0), v_bd[i]) for i in idx]
    pq = [from_h[i][:cc] + from_v[i][:cc] for i in idx]
    pw = [x.astype(BF16) for x in a_ab]
    s_acc = [eye_g + x for x in a_ab]
    pw = [mm(x, bdiag(x)).astype(BF16) for x in pw]
    n = 2
    while n < cc:
        last = 2 * n >= cc
        lhs = [s_acc[i].astype(BF16) if last else jnp.concatenate([pw[i], s_acc[i].astype(BF16)], axis=0)
               for i in idx]
        prod = [mm(lhs[i], bdiag(pw[i])) for i in idx]
        s_acc = [s_acc[i] + (prod[i] if last else prod[i][cc:]) for i in idx]
        if not last:
            pw = [prod[i][:cc].astype(BF16) for i in idx]
        n *= 2
    ub = [mm(s_acc[i].astype(BF16), bdiag(pq[i].astype(BF16))).astype(BF16) for i in idx]
    for i in idx:
        y_refs[chains[i][0]][0, :, cols[i]] = from_h[i][cc:] + from_v[i][cc:] + mm(m_rb[i], bdiag(ub[i]))
    for i in idx:
        upd = lax.dot_general(jnp.concatenate([bt[i], kt[i]], axis=0), jnp.concatenate([ub[i], v[i]], axis=0),
                              (((0,), (0,)), ((), ())), preferred_element_type=F32)
        ge = jnp.transpose(jnp.broadcast_to(g_end[i], (gw, gw)))
        h_ref[i] = ge * (h0[i] + jnp.where(blk, upd, 0.0))


def rwkv_scan(r, v, kk, lw, kd, a):
    b, lt, d = r.shape
    nc = lt // SCAN_C
    n_ctx = CTX_LEN // SCAN_C
    assert SCAN_C == RW_HEAD_DIM and CTX_LEN % SCAN_C == 0

    def rev_chunk(c):
        return jnp.where(c < n_ctx, n_ctx - 1 - c, nc - 1 + n_ctx - c)

    fwd = pl.BlockSpec((1, SCAN_C, d), lambda bi, c: (bi, c, 0))
    bwd = pl.BlockSpec((1, SCAN_C, d), lambda bi, c: (bi, rev_chunk(c), 0))
    fwd_dir = pl.BlockSpec((1, 1, SCAN_C, d), lambda bi, c: (0, bi, c, 0))
    bwd_dir = pl.BlockSpec((1, 1, SCAN_C, d), lambda bi, c: (1, bi, rev_chunk(c), 0))
    y_shape = jax.ShapeDtypeStruct((b, lt, d), F32)
    return pl.pallas_call(
        _rwkv_scan_kernel,
        out_shape=(y_shape, y_shape),
        grid=(b, nc),
        in_specs=[fwd, fwd, fwd, bwd, bwd, bwd, fwd_dir, fwd_dir, fwd_dir, bwd_dir, bwd_dir, bwd_dir],
        out_specs=(fwd, bwd),
        scratch_shapes=[pltpu.VMEM((2 * RW_HEADS // SCAN_G, SCAN_GW, SCAN_GW), F32)],
        compiler_params=_cparams(("arbitrary", "arbitrary")),
        name="rwkv_scan",
    )(r, v, kk, r, v, kk, lw, kd, a, lw, kd, a)


def _rwkv_out_kernel(yf_ref, yb_ref, r_ref, v_ref, g_ref, kd_ref, h_ref, mods_ref, lnw_ref, lnb_ref, rk_ref,
                     bd_ref, wo_ref, o_ref):
    def head_sum(x):
        return jnp.dot(x.astype(BF16), bd_ref[...], preferred_element_type=F32)

    y = yf_ref[0] + yb_ref[0]
    inv_n = 1.0 / RW_HEAD_DIM
    mean = head_sum(y) * inv_n
    yc = y - mean
    var = head_sum(yc * yc) * inv_n
    yn = yc * lax.rsqrt(var + RW_GN_EPS) * lnw_ref[...] + lnb_ref[...]
    r = r_ref[0].astype(F32)
    kd = kd_ref[0, 0].astype(F32) + kd_ref[1, 0].astype(F32)
    yn = yn + head_sum(r * kd * rk_ref[...]) * v_ref[0].astype(F32)
    out = jnp.dot((yn * g_ref[0].astype(F32)).astype(BF16), wo_ref[...], preferred_element_type=F32)
    o_ref[0] = h_ref[0] + _mod_row(mods_ref, 2) * out


def rwkv_readout(y_f, y_b, r, v, g, kd, h, mods, ln_w, ln_b, r_k, w_o):
    b, lt, d = h.shape
    nt = lt // TILE - 1
    hid = jnp.arange(d) // RW_HEAD_DIM
    bd = (hid[:, None] == hid[None, :]).astype(BF16)
    full = lambda shape: pl.BlockSpec(shape, lambda bi, i: (0,) * len(shape))
    tile_spec = pl.BlockSpec((1, TILE, d), lambda bi, i: (bi, i + 1, 0))
    dir_spec = pl.BlockSpec((2, 1, TILE, d), lambda bi, i: (0, bi, i + 1, 0))
    return pl.pallas_call(
        _rwkv_out_kernel,
        out_shape=jax.ShapeDtypeStruct((b, nt * TILE, d), F32),
        grid=(b, nt),
        in_specs=[tile_spec, tile_spec, tile_spec, tile_spec, tile_spec, dir_spec, tile_spec,
                  pl.BlockSpec((1, 1, 6, d), lambda bi, i: (bi, 1, 0, 0)),
                  full((1, d)), full((1, d)), full((1, d)), full((d, d)), full((d, d))],
        out_specs=pl.BlockSpec((1, TILE, d), lambda bi, i: (bi, i, 0)),
        compiler_params=_cparams(("arbitrary", "arbitrary")),
        name="rwkv_readout",
    )(y_f, y_b, r, v, g, kd, h, mods, ln_w.reshape(1, d), ln_b.reshape(1, d), r_k.reshape(1, d), bd,
      w_o.astype(BF16))


def kernel(x, c, ctx, c_ctx, ada_w, ada_b, norm_mix_g, norm_ffn_g, router_w, router_b, exp_w_in, exp_b_in,
           exp_w_out, exp_b_out, ab_w_in, na_q_g, na_k_g, na_rpb, pool_w, pool_scale, ab_w_out,
           rw_mu, rw_w_r, rw_w_k, rw_w_v, rw_w_o, rw_w0, rw_w1, rw_w2, rw_a0, rw_a1, rw_a2,
           rw_g1, rw_g2, rw_k_k, rw_k_a, rw_r_k, rw_ln_w, rw_ln_b):
    b, seq, d = x.shape
    depth = ada_w.shape[0]
    assert ctx.shape[1] == CTX_LEN == TILE and seq % TILE == 0 and d == D_MODEL
    assert depth == 2, "layer schedule below is written for [neighbourhood/pool layer, RWKV layer]"
    rows = seq // GRID_W
    assert rows >= NA_QROWS * NA_KTILES and NA_QROWS * NA_KTILES >= NA_QROWS + NA_KH - 1

    n_c = 1 + b
    n_c_pad = -(-n_c // SUBLANES) * SUBLANES
    cvec = jnp.concatenate([c_ctx[None], c, jnp.zeros((n_c_pad - n_c, d), F32)], axis=0)

    def layer_mods(layer):
        m = ada_mod(cvec, ada_w[layer], ada_b[layer])
        m_ctx = jnp.broadcast_to(m[0].reshape(1, 6, d), (b, 6, d))
        return jnp.stack([m_ctx, m[1:n_c].reshape(b, 6, d)], axis=1)

    def moe(h, layer, mods, tile0, ctx_tiles):
        return moe_layer(h, norm_ffn_g[layer], mods, router_w[layer], router_b[layer], layer, exp_w_in,
                         exp_b_in, exp_w_out, exp_b_out, tile0, ctx_tiles)

    h = jnp.concatenate([ctx, x], axis=1)

    mods = layer_mods(0)
    qk, v, u = ab_proj(h, norm_mix_g[0], mods, ab_w_in[0], na_q_g[0], na_k_g[0])
    o_na = na_attention(qk, v, _na_bias_tables(na_rpb[0], rows))
    o_pool = multiscale_pool(u, pool_w[0], pool_scale[0])
    h = out_proj_residual(o_na, o_pool, h, mods, ab_w_out[0])
    h = moe(h, 0, mods, 0, 1)

    mods = layer_mods(1)
    r, v, kk, g, lw, kd, a = rwkv_features(h, norm_mix_g[1], mods, rw_mu[0], rw_w_r[0], rw_w_k[0], rw_w_v[0],
                                           rw_w0[0], rw_w1[0], rw_w2[0], rw_a0[0], rw_a1[0], rw_a2[0],
                                           rw_g1[0], rw_g2[0], rw_k_k[0], rw_k_a[0])
    y_f, y_b = rwkv_scan(r, v, kk, lw, kd, a)
    h_lat = rwkv_readout(y_f, y_b, r, v, g, kd, h, mods, rw_ln_w[0], rw_ln_b[0], rw_r_k[0], rw_w_o[0])
    return moe(h_lat, 1, mods, 0, 0)
```

```python
import functools

import jax
import jax.numpy as jnp
import numpy as np
from jax import lax
from jax.experimental import pallas as pl
from jax.experimental.pallas import tpu as pltpu

F32 = jnp.float32
BF16 = jnp.bfloat16

D_MODEL = 1024
GRID_W = 64
CTX_LEN = 256
RMS_EPS = 1e-6
NA_HEADS = 8
NA_HEAD_DIM = 64
NA_WIDTH = NA_HEADS * NA_HEAD_DIM
NA_KH = 8
NA_KW = 16
POOL_WINDOWS = (2, 4, 8, 16)
POOL_GC = 128
POOL_WIDTH = 512
RW_HEAD_DIM = 64
RW_HEADS = D_MODEL // RW_HEAD_DIM
RW_GN_EPS = 64e-5
RW_DECAY_SCALE = float(np.exp(-0.5))
N_EXPERTS = 32
TOP_K = 4
D_FF = D_MODEL
SWIGLU_LIMIT = 7.0
SWIGLU_ALPHA = 1.702

TILE = 256
LANES = 128
SUBLANES = 8
MOE_TM = 512
NEG_BIG = -1e30
VMEM_LIMIT = 48 * 1024 * 1024
MOE_FFN_VMEM_LIMIT = 56 * 1024 * 1024


def _cparams(sem):
    return pltpu.CompilerParams(dimension_semantics=sem, vmem_limit_bytes=VMEM_LIMIT)


def _mod_row(mods_ref, k):
    return mods_ref[0, 0, pl.ds(k, 1), :]


def _norm_mod(x, gain, shift, scale):
    ms = jnp.mean(x * x, axis=-1, keepdims=True)
    return (x * lax.rsqrt(ms + RMS_EPS) * gain) * (1.0 + scale) + shift


def _ada_kernel(c_ref, w_ref, b_ref, o_ref):
    c = c_ref[...]
    s = c * jax.nn.sigmoid(c)
    o_ref[...] = jnp.dot(s, w_ref[...], preferred_element_type=F32,
                         precision=lax.Precision.HIGHEST) + b_ref[...]


def ada_mod(cvec, w, b):
    r, d = cvec.shape
    n = w.shape[1]
    tn = 1536
    return pl.pallas_call(
        _ada_kernel,
        out_shape=jax.ShapeDtypeStruct((r, n), F32),
        grid=(n // tn,),
        in_specs=[pl.BlockSpec((r, d), lambda j: (0, 0)),
                  pl.BlockSpec((d, tn), lambda j: (0, j)),
                  pl.BlockSpec((1, tn), lambda j: (0, j))],
        out_specs=pl.BlockSpec((r, tn), lambda j: (0, j)),
        compiler_params=_cparams(("arbitrary",)),
        name="ada_mod",
    )(cvec, w, b.reshape(1, n))


def _ab_proj_kernel(h_ref, g_ref, mods_ref, w_ref, bd_ref, qkg_ref, qk_ref, v_ref, u_ref):
    x = _norm_mod(h_ref[0], g_ref[...], _mod_row(mods_ref, 0), _mod_row(mods_ref, 1))
    p = jnp.dot(x.astype(BF16), w_ref[...], preferred_element_type=F32)
    qk = p[:, :2 * NA_WIDTH]
    ss = jnp.dot((qk * qk).astype(BF16), bd_ref[...], preferred_element_type=F32)
    qk_ref[0] = (qk * lax.rsqrt(ss + RMS_EPS) * qkg_ref[...]).astype(BF16)
    v_ref[0] = p[:, 2 * NA_WIDTH:3 * NA_WIDTH].astype(BF16)
    u_ref[0] = p[:, 3 * NA_WIDTH:]


def ab_proj(h, gain, mods, w_in, q_g, k_g):
    b, lt, d = h.shape
    nt = lt // TILE
    n_in = w_in.shape[1]
    hid = jnp.arange(2 * NA_WIDTH) // NA_HEAD_DIM
    bd = jnp.where(hid[:, None] == hid[None, :], 1.0 / NA_HEAD_DIM, 0.0).astype(BF16)
    qkg = jnp.concatenate([jnp.tile(q_g, NA_HEADS), jnp.tile(k_g, NA_HEADS)]).reshape(1, -1).astype(F32)
    return pl.pallas_call(
        _ab_proj_kernel,
        out_shape=(jax.ShapeDtypeStruct((b, lt, 2 * NA_WIDTH), BF16),
                   jax.ShapeDtypeStruct((b, lt, NA_WIDTH), BF16),
                   jax.ShapeDtypeStruct((b, lt, POOL_WIDTH), F32)),
        grid=(b, nt),
        in_specs=[pl.BlockSpec((1, TILE, d), lambda bi, i: (bi, i, 0)),
                  pl.BlockSpec((1, d), lambda bi, i: (0, 0)),
                  pl.BlockSpec((1, 1, 6, d), lambda bi, i: (bi, jnp.minimum(i, 1), 0, 0)),
                  pl.BlockSpec((d, n_in), lambda bi, i: (0, 0)),
                  pl.BlockSpec((2 * NA_WIDTH, 2 * NA_WIDTH), lambda bi, i: (0, 0)),
                  pl.BlockSpec((1, 2 * NA_WIDTH), lambda bi, i: (0, 0))],
        out_specs=(pl.BlockSpec((1, TILE, 2 * NA_WIDTH), lambda bi, i: (bi, i, 0)),
                   pl.BlockSpec((1, TILE, NA_WIDTH), lambda bi, i: (bi, i, 0)),
                   pl.BlockSpec((1, TILE, POOL_WIDTH), lambda bi, i: (bi, i, 0))),
        compiler_params=_cparams(("arbitrary", "arbitrary")),
        name="ab_proj",
    )(h, gain.reshape(1, d), mods, w_in.astype(BF16), bd, qkg)


NA_QROWS = TILE // GRID_W
NA_KTILES = 3
NA_NLOC = NA_KTILES * TILE
NA_NKEY = NA_NLOC + CTX_LEN


def _na_bias_tables(rpb, rows):
    n_r, n_c = 2 * NA_KH - 1, 2 * NA_KW - 1
    col = np.arange(GRID_W)
    c_start = np.clip(col - NA_KW // 2, 0, GRID_W - NA_KW)
    valid_c = (col[None, :] >= c_start[:, None]) & (col[None, :] < c_start[:, None] + NA_KW)
    c_rel = np.clip(col[None, :] - col[:, None] + NA_KW - 1, 0, n_c - 1)
    sel_c = (c_rel.reshape(-1)[None, :] == np.arange(n_c)[:, None]).astype(np.float32)
    by_col = jnp.einsum("hrc,cn->hrn", rpb.astype(F32), sel_c, precision=lax.Precision.HIGHEST)
    n_qt = rows // NA_QROWS
    qr, kr = np.arange(NA_QROWS), np.arange(NA_KTILES * NA_QROWS)
    tabs = []
    for jq in (0, 1, n_qt - 1):
        w0 = NA_QROWS * min(max(jq - 1, 0), n_qt - NA_KTILES)
        q_row = NA_QROWS * jq + qr
        key_row = w0 + kr
        r_start = np.clip(q_row - NA_KH // 2, 0, rows - NA_KH)
        valid_r = (key_row[None, :] >= r_start[:, None]) & (key_row[None, :] < r_start[:, None] + NA_KH)
        r_rel = np.clip(key_row[None, :] - q_row[:, None] + NA_KH - 1, 0, n_r - 1)
        sel_r = (r_rel.reshape(-1)[:, None] == np.arange(n_r)[None, :]).astype(np.float32)
        t = jnp.einsum("xr,hrn->hxn", sel_r, by_col, precision=lax.Precision.HIGHEST)
        t = t.reshape(NA_HEADS, NA_QROWS, NA_KTILES * NA_QROWS, GRID_W, GRID_W)
        t = t.transpose(0, 1, 3, 2, 4).reshape(NA_HEADS, TILE, NA_NLOC)
        valid = (valid_r[:, None, :, None] & valid_c[None, :, None, :]).reshape(TILE, NA_NLOC)
        tabs.append(jnp.where(valid[None], t, NEG_BIG))
    tabs = [jnp.full_like(tabs[0], NEG_BIG)] + tabs
    loc = jnp.stack(tabs)
    return jnp.concatenate([loc, jnp.zeros(loc.shape[:3] + (CTX_LEN,), F32)], axis=-1)


def _na_kernel(q_ref, k0_ref, k1_ref, k2_ref, kc_ref, v0_ref, v1_ref, v2_ref, vc_ref, bias_ref, o_ref):
    q = q_ref[0]
    kall = jnp.concatenate([k0_ref[0], k1_ref[0], k2_ref[0], kc_ref[0]], axis=0)
    vall = jnp.concatenate([v0_ref[0], v1_ref[0], v2_ref[0], vc_ref[0]], axis=0)
    lane = lax.broadcasted_iota(jnp.int32, (TILE, LANES), 1)
    scale = jnp.asarray(NA_HEAD_DIM ** -0.5, BF16)
    heads = range(NA_HEADS)
    pair = [slice((h // 2) * LANES, (h // 2 + 1) * LANES) for h in heads]
    scores = []
    for h in heads:
        lo = (h % 2) * NA_HEAD_DIM
        qh = jnp.where((lane >= lo) & (lane < lo + NA_HEAD_DIM), q[:, pair[h]], jnp.zeros((), BF16)) * scale
        s = lax.dot_general(qh, kall[:, pair[h]], (((1,), (1,)), ((), ())), preferred_element_type=F32)
        scores.append(s + bias_ref[0, h])
    probs, denom = [], []
    for s in scores:
        p = jnp.exp(s - jnp.max(s, axis=-1, keepdims=True))
        probs.append(p.astype(BF16))
        denom.append(jnp.sum(p, axis=-1, keepdims=True))
    outs = [jnp.dot(probs[h], vall[:, pair[h]], preferred_element_type=F32) / denom[h] for h in heads]
    for h in range(0, NA_HEADS, 2):
        o_ref[0, :, pair[h]] = jnp.where(lane < NA_HEAD_DIM, outs[h], outs[h + 1]).astype(BF16)


def na_attention(qk, v, bias):
    b, lt, _ = qk.shape
    nt = lt // TILE
    n_qt = nt - 1
    assert NA_HEAD_DIM ** -0.5 == 0.125 and NA_HEADS % 2 == 0
    w = NA_WIDTH

    def w0(j):
        return 1 + jnp.clip(j - 2, 0, n_qt - NA_KTILES)

    def kind(j):
        return jnp.where(j == 0, 0, jnp.where(j == 1, 1, jnp.where(j == nt - 1, 3, 2)))

    qspec = pl.BlockSpec((1, TILE, w), lambda j, bi: (bi, j, 0))
    kspecs = [pl.BlockSpec((1, TILE, w), functools.partial(lambda j, bi, t: (bi, w0(j) + t, 1), t=t))
              for t in range(NA_KTILES)]
    kcspec = pl.BlockSpec((1, TILE, w), lambda j, bi: (bi, 0, 1))
    vspecs = [pl.BlockSpec((1, TILE, w), functools.partial(lambda j, bi, t: (bi, w0(j) + t, 0), t=t))
              for t in range(NA_KTILES)]
    vcspec = pl.BlockSpec((1, TILE, w), lambda j, bi: (bi, 0, 0))
    bspec = pl.BlockSpec((1, NA_HEADS, TILE, NA_NKEY), lambda j, bi: (kind(j), 0, 0, 0))
    return pl.pallas_call(
        _na_kernel,
        out_shape=jax.ShapeDtypeStruct((b, lt, w), BF16),
        grid=(nt, b),
        in_specs=[qspec] + kspecs + [kcspec] + vspecs + [vcspec, bspec],
        out_specs=pl.BlockSpec((1, TILE, w), lambda j, bi: (bi, j, 0)),
        compiler_params=_cparams(("arbitrary", "arbitrary")),
        name="na_attention",
    )(qk, qk, qk, qk, qk, v, v, v, v, bias)


POOL_HALO = 8
POOL_ROWS = TILE + 2 * POOL_HALO


def _pool_kernel(cur_ref, prev_ref, next_ref, pw_ref, ps_ref, o_ref, xp_ref, *, n_tiles):
    i = pl.program_id(1)
    has_prev = (i >= 2).astype(F32)
    has_next = ((i >= 1) & (i < n_tiles - 1)).astype(F32)
    xp_ref[0:POOL_HALO, :] = prev_ref[0] * has_prev
    xp_ref[POOL_HALO:POOL_HALO + TILE, :] = cur_ref[0]
    xp_ref[POOL_HALO + TILE:POOL_ROWS, :] = next_ref[0] * has_next
    row = lax.broadcasted_iota(jnp.int32, (TILE, POOL_GC), 0)
    tpos = jnp.where(i == 0, row, row + (i - 1) * TILE)
    seq_len = jnp.where(i == 0, CTX_LEN, (n_tiles - 1) * TILE)
    for g, w in enumerate(POOL_WINDOWS):
        cols = slice(g * POOL_GC, (g + 1) * POOL_GC)
        x = xp_ref[:, cols]
        s = x + pltpu.roll(x, 1, 0)
        half = 1
        while 2 * half < w:
            s = pltpu.roll(s, half, 0) + pltpu.roll(s, POOL_ROWS - half, 0)
            half *= 2
        s = s[POOL_HALO:POOL_HALO + TILE]
        cnt = jnp.minimum(w // 2, tpos) + jnp.minimum(w // 2, seq_len - tpos)
        pooled = s / cnt.astype(F32) - cur_ref[0, :, cols]
        y = jnp.dot(pooled.astype(BF16), pw_ref[g], preferred_element_type=F32)
        o_ref[0, :, cols] = (y * ps_ref[:, cols]).astype(BF16)


def multiscale_pool(u, pool_w, pool_scale):
    b, lt, c = u.shape
    nt = lt // TILE
    hb = TILE // POOL_HALO
    nhb = lt // POOL_HALO
    assert POOL_HALO >= max(POOL_WINDOWS) // 2 and POOL_HALO == SUBLANES
    return pl.pallas_call(
        functools.partial(_pool_kernel, n_tiles=nt),
        out_shape=jax.ShapeDtypeStruct((b, lt, c), BF16),
        grid=(b, nt),
        in_specs=[pl.BlockSpec((1, TILE, c), lambda bi, i: (bi, i, 0)),
                  pl.BlockSpec((1, POOL_HALO, c), lambda bi, i: (bi, jnp.maximum(i * hb - 1, 0), 0)),
                  pl.BlockSpec((1, POOL_HALO, c), lambda bi, i: (bi, jnp.minimum((i + 1) * hb, nhb - 1), 0)),
                  pl.BlockSpec((len(POOL_WINDOWS), POOL_GC, POOL_GC), lambda bi, i: (0, 0, 0)),
                  pl.BlockSpec((1, c), lambda bi, i: (0, 0))],
        out_specs=pl.BlockSpec((1, TILE, c), lambda bi, i: (bi, i, 0)),
        scratch_shapes=[pltpu.VMEM((POOL_ROWS, c), F32)],
        compiler_params=_cparams(("arbitrary", "arbitrary")),
        name="multiscale_pool",
    )(u, u, u, pool_w.astype(BF16), pool_scale.reshape(1, c).astype(F32))


def _router_tail(h_new, g_ref, mods_ref, rwh_ref, rwl_ref, rb_ref, y_ref, lg_ref):
    x = _norm_mod(h_new, g_ref[...], _mod_row(mods_ref, 3), _mod_row(mods_ref, 4))
    x_hi = x.astype(BF16)
    y_ref[0] = x_hi
    x_lo = (x - x_hi.astype(F32)).astype(BF16)
    w_hi, w_lo = rwh_ref[...], rwl_ref[...]
    lg_ref[0] = (jnp.dot(x_hi, w_hi, preferred_element_type=F32)
                 + jnp.dot(x_lo, w_hi, preferred_element_type=F32)
                 + jnp.dot(x_hi, w_lo, preferred_element_type=F32)
                 + jnp.dot(x_lo, w_lo, preferred_element_type=F32)) + rb_ref[...]


def _router_operands(gain, router_w, router_b):
    d, e = router_w.shape
    head = lax.bitcast_convert_type(lax.bitcast_convert_type(router_w, jnp.uint32) & jnp.uint32(0xFFFF0000), F32)
    full = lambda shape: pl.BlockSpec(shape, lambda bi, i: (0,) * len(shape))
    return ((gain.reshape(1, d), head.astype(BF16), (router_w - head).astype(BF16), router_b.reshape(1, e)),
            [full((1, d)), full((d, e)), full((d, e)), full((1, e))])


def _out_proj_kernel(a_ref, b_ref, h_ref, mods_ref, w_ref, g_ref, rwh_ref, rwl_ref, rb_ref,
                     o_ref, y_ref, lg_ref):
    ka = a_ref.shape[-1]
    y = jnp.dot(a_ref[0], w_ref[:ka, :], preferred_element_type=F32)
    y = y + jnp.dot(b_ref[0], w_ref[ka:, :], preferred_element_type=F32)
    h_new = h_ref[0] + _mod_row(mods_ref, 2) * y
    o_ref[0] = h_new
    _router_tail(h_new, g_ref, mods_ref, rwh_ref, rwl_ref, rb_ref, y_ref, lg_ref)


def out_proj_residual(a, bb, h, mods, w, ffn_gain, router_w, router_b):
    b, lt, d = h.shape
    nt = lt // TILE
    ka, kb = a.shape[-1], bb.shape[-1]
    e = router_w.shape[1]
    r_ops, r_specs = _router_operands(ffn_gain, router_w, router_b)
    tile = lambda width: pl.BlockSpec((1, TILE, width), lambda bi, i: (bi, i, 0))
    return pl.pallas_call(
        _out_proj_kernel,
        out_shape=(jax.ShapeDtypeStruct((b, lt, d), F32), jax.ShapeDtypeStruct((b, lt, d), BF16),
                   jax.ShapeDtypeStruct((b, lt, e), F32)),
        grid=(b, nt),
        in_specs=[tile(ka), tile(kb), tile(d),
                  pl.BlockSpec((1, 1, 6, d), lambda bi, i: (bi, jnp.minimum(i, 1), 0, 0)),
                  pl.BlockSpec((ka + kb, d), lambda bi, i: (0, 0))] + r_specs,
        out_specs=(tile(d), tile(d), tile(e)),
        compiler_params=_cparams(("arbitrary", "arbitrary")),
        name="out_proj_residual",
    )(a, bb, h, mods, w.astype(BF16), *r_ops)


MOE_PERM_W = 256


def _moe_ffn_kernel(te_ref, tv_ref, x_ref, win_ref, bg_ref, bl_ref, wo_ref, bo_ref, perm_ref, o_ref,
                    wg_s, wl_s, wo_s):
    t = pl.program_id(0)
    valid = tv_ref[t] > 0
    new_expert = (t == 0) | (te_ref[t] != te_ref[jnp.maximum(t - 1, 0)])

    @pl.when(valid & new_expert)
    def _():
        half = MOE_PERM_W // 2
        for cb in range(win_ref.shape[3] // MOE_PERM_W):
            w = win_ref[0, 0, :, cb * MOE_PERM_W:(cb + 1) * MOE_PERM_W].astype(BF16)
            wp = jnp.dot(w, perm_ref[...], preferred_element_type=F32).astype(BF16)
            wg_s[:, cb * half:(cb + 1) * half] = wp[:, :half]
            wl_s[:, cb * half:(cb + 1) * half] = wp[:, half:]
        wo_s[...] = wo_ref[0, 0].astype(BF16)

    @pl.when(jnp.logical_not(valid))
    def _():
        o_ref[...] = jnp.zeros_like(o_ref)

    @pl.when(valid)
    def _():
        x = x_ref[...].astype(BF16)
        hg = jnp.dot(x, wg_s[...], preferred_element_type=F32) + bg_ref[0]
        hl = jnp.dot(x, wl_s[...], preferred_element_type=F32) + bl_ref[0]
        glu = jnp.minimum(hg, SWIGLU_LIMIT)
        lin = jnp.clip(hl, -SWIGLU_LIMIT, SWIGLU_LIMIT)
        act = glu * jax.nn.sigmoid(SWIGLU_ALPHA * glu) * (lin + 1.0)
        o_ref[...] = (jnp.dot(act.astype(BF16), wo_s[...], preferred_element_type=F32)
                      + bo_ref[0]).astype(o_ref.dtype)


def moe_grouped_ffn(x_sorted, tile_expert, tile_valid, layer, w_in, bg, bl, w_out, bo):
    p, d = x_sorted.shape
    _, e, _, f2 = w_in.shape
    f = f2 // 2
    nt = p // MOE_TM
    assert f2 % MOE_PERM_W == 0 and MOE_PERM_W % (2 * LANES) == 0
    src = jnp.arange(MOE_PERM_W)
    dst = jnp.where(src % 2 == 0, src // 2, MOE_PERM_W // 2 + src // 2)
    perm = (dst[:, None] == jnp.arange(MOE_PERM_W)[None, :]).astype(BF16)
    grid_spec = pltpu.PrefetchScalarGridSpec(
        num_scalar_prefetch=2,
        grid=(nt,),
        in_specs=[pl.BlockSpec((MOE_TM, d), lambda t, te, tv: (t * tv[t], 0)),
                  pl.BlockSpec((1, 1, d, f2), lambda t, te, tv: (layer, te[t], 0, 0)),
                  pl.BlockSpec((1, 1, f), lambda t, te, tv: (te[t], 0, 0)),
                  pl.BlockSpec((1, 1, f), lambda t, te, tv: (te[t], 0, 0)),
                  pl.BlockSpec((1, 1, f, d), lambda t, te, tv: (layer, te[t], 0, 0)),
                  pl.BlockSpec((1, 1, d), lambda t, te, tv: (te[t], 0, 0)),
                  pl.BlockSpec((MOE_PERM_W, MOE_PERM_W), lambda t, te, tv: (0, 0))],
        out_specs=pl.BlockSpec((MOE_TM, d), lambda t, te, tv: (t, 0)),
        scratch_shapes=[pltpu.VMEM((d, f), BF16), pltpu.VMEM((d, f), BF16), pltpu.VMEM((f, d), BF16)],
    )
    return pl.pallas_call(
        _moe_ffn_kernel,
        out_shape=jax.ShapeDtypeStruct((p, d), BF16),
        grid_spec=grid_spec,
        compiler_params=pltpu.CompilerParams(dimension_semantics=("arbitrary",),
                                             vmem_limit_bytes=MOE_FFN_VMEM_LIMIT),
        name="moe_grouped_ffn",
    )(tile_expert, tile_valid, x_sorted, w_in, bg, bl, w_out, bo, perm)


def _moe_combine_kernel(z_ref, gt_ref, h_ref, mods_ref, o_ref):
    gt = gt_ref[0]
    acc = gt[:, 0:1] * z_ref[0, 0].astype(F32)
    for k in range(1, TOP_K):
        acc = acc + gt[:, k:k + 1] * z_ref[k, 0].astype(F32)
    o_ref[0] = h_ref[0] + _mod_row(mods_ref, 5) * acc


def moe_combine(z4, gates, h, mods, ctx_tiles):
    k, b, lp, d = z4.shape
    nt = lp // TILE
    return pl.pallas_call(
        _moe_combine_kernel,
        out_shape=jax.ShapeDtypeStruct((b, lp, d), F32),
        grid=(b, nt),
        in_specs=[pl.BlockSpec((k, 1, TILE, d), lambda bi, i: (0, bi, i, 0)),
                  pl.BlockSpec((1, TILE, k), lambda bi, i: (bi, i, 0)),
                  pl.BlockSpec((1, TILE, d), lambda bi, i: (bi, i, 0)),
                  pl.BlockSpec((1, 1, 6, d), lambda bi, i: (bi, jnp.minimum(i + 1 - ctx_tiles, 1), 0, 0))],
        out_specs=pl.BlockSpec((1, TILE, d), lambda bi, i: (bi, i, 0)),
        compiler_params=_cparams(("arbitrary", "arbitrary")),
        name="moe_combine",
    )(z4, gates, h, mods)


DISPATCH_T = 512
DISPATCH_BLK = SUBLANES
DISPATCH_ROWS = -(-(DISPATCH_T * TOP_K + N_EXPERTS * (DISPATCH_BLK - 1)) // (2 * SUBLANES)) * (2 * SUBLANES)
DISPATCH_COPY_ROWS = (4 * DISPATCH_BLK, DISPATCH_BLK)
DISPATCH_LIST = max(DISPATCH_ROWS // DISPATCH_COPY_ROWS[0],
                    N_EXPERTS * (DISPATCH_COPY_ROWS[0] // DISPATCH_COPY_ROWS[1] - 1))


def _moe_dispatch_kernel(cnt_ref, tab_ref, ntail_ref, tail_ref, y_ref, idx_ref, off_ref, base_ref,
                         xs_ref, slot_ref, local_ref, zero_ref, sem, zero_sem):
    t = pl.program_id(0)
    last_t = pl.num_programs(0) - 1
    dt = DISPATCH_T
    idx = idx_ref[0]
    e_iota = lax.broadcasted_iota(jnp.int32, (N_EXPERTS, dt), 0)
    picked = [e_iota == idx[k:k + 1, :] for k in range(TOP_K)]
    cnt = sum(jnp.where(p, 1.0, 0.0) for p in picked).astype(BF16)
    r0 = lax.broadcasted_iota(jnp.int32, (dt, dt), 0)
    r1 = lax.broadcasted_iota(jnp.int32, (dt, dt), 1)
    earlier = jnp.where(r0 < r1, 1.0, 0.0).astype(BF16)
    rank = jnp.dot(cnt, earlier, preferred_element_type=F32)
    local_row = rank + off_ref[0].astype(F32)
    global_row = rank + base_ref[0].astype(F32)
    rows = lax.broadcasted_iota(jnp.int32, (DISPATCH_ROWS, dt), 0)
    place = None
    for k in range(TOP_K):
        pos = jnp.sum(jnp.where(picked[k], local_row, 0.0), axis=0, keepdims=True).astype(jnp.int32)
        slot_ref[0, k:k + 1, :] = jnp.sum(jnp.where(picked[k], global_row, 0.0), axis=0,
                                          keepdims=True).astype(jnp.int32)
        hit = rows == pos
        place = hit if place is None else (place | hit)
    placement = jnp.where(place, 1.0, 0.0).astype(BF16)
    buf = lax.rem(t, 2)
    local_ref[buf] = jnp.dot(placement, y_ref[...], preferred_element_type=F32)

    def seg_copy(cls, step, j):
        rows = DISPATCH_COPY_ROWS[cls]
        src_buf = jnp.where(step == t, buf, 1 - buf)
        src_blk = tab_ref[step, (2 * cls) * DISPATCH_LIST + j]
        dst_blk = tab_ref[step, (2 * cls + 1) * DISPATCH_LIST + j]
        src = local_ref.at[src_buf, pl.ds(pl.multiple_of(src_blk * DISPATCH_BLK, DISPATCH_BLK), rows), :]
        dst = xs_ref.at[pl.ds(pl.multiple_of(dst_blk * DISPATCH_BLK, DISPATCH_BLK), rows), :]
        return pltpu.make_async_copy(src, dst, sem)

    def tail_copy(e, j):
        blk = tail_ref[e] + j
        dst = xs_ref.at[pl.ds(pl.multiple_of(blk * DISPATCH_BLK, DISPATCH_BLK), DISPATCH_BLK), :]
        return pltpu.make_async_copy(zero_ref, dst, zero_sem)

    def start_all(copy, n):
        lax.fori_loop(0, n, lambda j, c: (copy(j).start(), c)[1], 0)

    def wait_all(copy, n):
        lax.fori_loop(0, n, lambda j, c: (copy(j).wait(), c)[1], 0)

    classes = range(len(DISPATCH_COPY_ROWS))

    @pl.when(t > 0)
    def _():
        for cls in classes:
            wait_all(functools.partial(seg_copy, cls, t - 1), cnt_ref[t - 1, cls])

    for cls in classes:
        start_all(functools.partial(seg_copy, cls, t), cnt_ref[t, cls])

    @pl.when(t == last_t)
    def _():
        for cls in classes:
            wait_all(functools.partial(seg_copy, cls, t), cnt_ref[t, cls])

    @pl.when(t == 0)
    def _():
        zero_ref[...] = jnp.zeros_like(zero_ref)
        for e in range(N_EXPERTS + 1):
            start_all(functools.partial(tail_copy, e), ntail_ref[e])
        for e in range(N_EXPERTS + 1):
            wait_all(functools.partial(tail_copy, e), ntail_ref[e])


def moe_dispatch(y2, idx_t, copy_cnt, copy_tab, ntail, tail_blk, loc_off, base, p_rows):
    n, d = y2.shape
    nt = n // DISPATCH_T
    grid_spec = pltpu.PrefetchScalarGridSpec(
        num_scalar_prefetch=4,
        grid=(nt,),
        in_specs=[pl.BlockSpec((DISPATCH_T, d), lambda t, *_: (t, 0)),
                  pl.BlockSpec((1, TOP_K, DISPATCH_T), lambda t, *_: (t, 0, 0)),
                  pl.BlockSpec((1, N_EXPERTS, 1), lambda t, *_: (t, 0, 0)),
                  pl.BlockSpec((1, N_EXPERTS, 1), lambda t, *_: (t, 0, 0))],
        out_specs=(pl.BlockSpec(memory_space=pl.ANY),
                   pl.BlockSpec((1, TOP_K, DISPATCH_T), lambda t, *_: (t, 0, 0))),
        scratch_shapes=[pltpu.VMEM((2, DISPATCH_ROWS, d), F32), pltpu.VMEM((DISPATCH_BLK, d), F32),
                        pltpu.SemaphoreType.DMA(()), pltpu.SemaphoreType.DMA(())],
    )
    return pl.pallas_call(
        _moe_dispatch_kernel,
        out_shape=(jax.ShapeDtypeStruct((p_rows, d), F32),
                   jax.ShapeDtypeStruct((nt, TOP_K, DISPATCH_T), jnp.int32)),
        grid_spec=grid_spec,
        compiler_params=_cparams(("arbitrary",)),
        name="moe_dispatch",
    )(copy_cnt, copy_tab, ntail, tail_blk, y2, idx_t, loc_off, base)


def moe_layer(h, y, logits, mods, layer, w_in, b_in, w_out, b_out, ctx_tiles):
    b, lp, d = h.shape
    n = b * lp
    y2 = y.reshape(n, d)
    top_val, top_idx = lax.top_k(logits.reshape(n, N_EXPERTS), TOP_K)
    gates = jax.nn.softmax(top_val, axis=-1)

    assert n % DISPATCH_T == 0
    t_n = n // DISPATCH_T
    blk = DISPATCH_BLK
    experts = jnp.arange(N_EXPERTS, dtype=jnp.int32)
    idx_t = top_idx.astype(jnp.int32).reshape(t_n, DISPATCH_T, TOP_K).transpose(0, 2, 1)
    tile_cnt = jnp.sum((idx_t[..., None] == experts).astype(jnp.int32), axis=(1, 2))
    seg = (tile_cnt + blk - 1) // blk * blk
    loc_end = jnp.cumsum(seg, axis=1)
    loc_off = loc_end - seg
    tot = jnp.sum(seg, axis=0)
    padded = (tot + MOE_TM - 1) // MOE_TM * MOE_TM
    ends = jnp.cumsum(padded)
    starts = ends - padded
    base = starts[None, :] + jnp.cumsum(seg, axis=0) - seg
    q = jnp.arange(DISPATCH_LIST, dtype=jnp.int32)
    covered = jnp.zeros_like(seg)
    tabs, cnts = [], []
    for rows in DISPATCH_COPY_ROWS:
        n_c = (seg - covered) // rows
        c_end = jnp.cumsum(n_c, axis=1)
        sel = jnp.sum((c_end[:, None, :] <= q[None, :, None]).astype(jnp.int32), axis=2)[..., None] == experts
        pick = lambda a: jnp.sum(jnp.where(sel, a[:, None, :], 0), axis=2)
        inside = (q[None, :] - pick(c_end - n_c)) * rows
        tabs += [(pick(loc_off + covered) + inside) // blk, (pick(base + covered) + inside) // blk]
        cnts.append(c_end[:, -1])
        covered = covered + n_c * rows
    copy_tab = jnp.concatenate(tabs, axis=1)
    copy_cnt = jnp.stack(cnts, axis=1)
    n_tiles = -(-(n * TOP_K + t_n * N_EXPERTS * (blk - 1)) // MOE_TM) + N_EXPERTS
    tile_start = jnp.arange(n_tiles, dtype=jnp.int32) * MOE_TM
    tile_expert = jnp.minimum(jnp.sum((tile_start[:, None] >= ends[None, :]).astype(jnp.int32), axis=1),
                              N_EXPERTS - 1)
    tile_valid = (tile_start < ends[-1]).astype(jnp.int32)

    p_rows = n_tiles * MOE_TM
    gap_len = jnp.concatenate([padded - tot, p_rows - ends[-1:]]) // blk
    gap_start = jnp.concatenate([starts + tot, ends[-1:]]) // blk
    x_sorted, slot = moe_dispatch(y2, idx_t, copy_cnt, copy_tab, gap_len, gap_start,
                                  loc_off[..., None], base[..., None], p_rows)
    bg = b_in[layer, :, 0::2].reshape(N_EXPERTS, 1, D_FF)
    bl = b_in[layer, :, 1::2].reshape(N_EXPERTS, 1, D_FF)
    z = moe_grouped_ffn(x_sorted, tile_expert, tile_valid, layer, w_in, bg, bl, w_out,
                        b_out[layer].reshape(N_EXPERTS, 1, d))
    slot_kn = slot.transpose(1, 0, 2).reshape(TOP_K, n)
    z4 = z.at[slot_kn].get(mode="promise_in_bounds").reshape(TOP_K, b, lp, d)
    return moe_combine(z4, gates.reshape(b, lp, TOP_K), h, mods, ctx_tiles)


def _rwkv_feat_kernel(h_ref, hp_ref, hn_ref, g_ref, mods_ref, mu_ref, wr_ref, wk_ref, wv_ref,
                      w1_ref, w2_ref, a1_ref, a2_ref, g1_ref, g2_ref, w0_ref, a0_ref, kk_ref, ka_ref,
                      bd_ref, r_out, v_out, kk_out, g_out, lw_out, kd_out, a_out, *, n_tiles):
    i = pl.program_id(1)
    gain, shift, scale = g_ref[...], _mod_row(mods_ref, 0), _mod_row(mods_ref, 1)
    x = _norm_mod(h_ref[0], gain, shift, scale)
    has_prev = (i >= 2).astype(F32)
    has_next = ((i >= 1) & (i < n_tiles - 1)).astype(F32)
    x_before = _norm_mod(hp_ref[0], gain, shift, scale)[SUBLANES - 1:SUBLANES] * has_prev
    x_after = _norm_mod(hn_ref[0], gain, shift, scale)[0:1] * has_next
    row = lax.broadcasted_iota(jnp.int32, x.shape, 0)
    prev = jnp.where(row == 0, x_before, pltpu.roll(x, 1, 0))
    nxt = jnp.where(row == TILE - 1, x_after, pltpu.roll(x, TILE - 1, 0))
    xx = 0.5 * (prev + nxt) - x
    xr, xw, xk, xv, xa, xg = (x + xx * mu_ref[pl.ds(j, 1), :] for j in range(6))

    def mm(a, w):
        return jnp.dot(a.astype(BF16), w, preferred_element_type=F32)

    r = mm(xr, wr_ref[...])
    k = mm(xk, wk_ref[...])
    v = mm(xv, wv_ref[...])
    g = mm(jax.nn.sigmoid(mm(xg, g1_ref[...])), g2_ref[...])
    kk = k * kk_ref[...]
    kk = kk * lax.rsqrt(mm(kk * kk, bd_ref[...]) + 1e-12)
    r_out[0] = r.astype(BF16)
    v_out[0] = v.astype(BF16)
    kk_out[0] = kk.astype(BF16)
    g_out[0] = g.astype(BF16)
    tw = jnp.tanh(mm(xw, w1_ref[...]))
    av = mm(xa, a1_ref[...])
    lane = lax.broadcasted_iota(jnp.int32, tw.shape, 1)
    half = tw.shape[1] // 2
    for d in range(2):
        in_dir = (lane >= d * half) & (lane < (d + 1) * half)
        wz = w0_ref[pl.ds(d, 1), :] + mm(jnp.where(in_dir, tw, 0.0), w2_ref[...])
        lw_out[d, 0] = -RW_DECAY_SCALE * jax.nn.sigmoid(wz)
        a = jax.nn.sigmoid(a0_ref[pl.ds(d, 1), :] + mm(jnp.where(in_dir, av, 0.0), a2_ref[...]))
        kd_out[d, 0] = (k * (1.0 + (a - 1.0) * ka_ref[...])).astype(BF16)
        a_out[d, 0] = a.astype(BF16)


def rwkv_features(h, gain, mods, mu, w_r, w_k, w_v, w0, w1, w2, a0, a1, a2, g1, g2, k_k, k_a):
    b, lt, d = h.shape
    nt = lt // TILE
    hb = TILE // SUBLANES
    nhb = lt // SUBLANES
    hid = jnp.arange(d) // RW_HEAD_DIM
    bd = (hid[:, None] == hid[None, :]).astype(BF16)
    w1c = jnp.concatenate([w1[0], w1[1]], axis=1).astype(BF16)
    a1c = jnp.concatenate([a1[0], a1[1]], axis=1).astype(BF16)
    w2c = jnp.concatenate([w2[0], w2[1]], axis=0).astype(BF16)
    a2c = jnp.concatenate([a2[0], a2[1]], axis=0).astype(BF16)
    full = lambda shape: pl.BlockSpec(shape, lambda bi, i: (0,) * len(shape))
    tile_spec = pl.BlockSpec((1, TILE, d), lambda bi, i: (bi, i, 0))
    dir_spec = pl.BlockSpec((2, 1, TILE, d), lambda bi, i: (0, bi, i, 0))
    seq_bf = jax.ShapeDtypeStruct((b, lt, d), BF16)
    return pl.pallas_call(
        functools.partial(_rwkv_feat_kernel, n_tiles=nt),
        out_shape=(seq_bf, seq_bf, seq_bf, seq_bf,
                   jax.ShapeDtypeStruct((2, b, lt, d), F32),
                   jax.ShapeDtypeStruct((2, b, lt, d), BF16),
                   jax.ShapeDtypeStruct((2, b, lt, d), BF16)),
        grid=(b, nt),
        in_specs=[tile_spec,
                  pl.BlockSpec((1, SUBLANES, d), lambda bi, i: (bi, jnp.maximum(i * hb - 1, 0), 0)),
                  pl.BlockSpec((1, SUBLANES, d), lambda bi, i: (bi, jnp.minimum((i + 1) * hb, nhb - 1), 0)),
                  full((1, d)),
                  pl.BlockSpec((1, 1, 6, d), lambda bi, i: (bi, jnp.minimum(i, 1), 0, 0)),
                  full((6, d)), full((d, d)), full((d, d)), full((d, d)),
                  full(w1c.shape), full(w2c.shape), full(a1c.shape), full(a2c.shape),
                  full(g1.shape), full(g2.shape), full((2, d)), full((2, d)), full((1, d)), full((1, d)),
                  full((d, d))],
        out_specs=(tile_spec, tile_spec, tile_spec, tile_spec, dir_spec, dir_spec, dir_spec),
        compiler_params=_cparams(("arbitrary", "arbitrary")),
        name="rwkv_features",
    )(h, h, h, gain.reshape(1, d), mods, mu, w_r.astype(BF16), w_k.astype(BF16), w_v.astype(BF16),
      w1c, w2c, a1c, a2c, g1.astype(BF16), g2.astype(BF16), w0, a0, k_k.reshape(1, d), k_a.reshape(1, d), bd)


SCAN_C = 64
SCAN_G = 4
SCAN_GW = SCAN_G * RW_HEAD_DIM


def _scan_chunk_inputs(reverse, r_ref, v_ref, kk_ref, lw_ref, kd_ref, a_ref):
    cc = SCAN_C
    ti = lax.broadcasted_iota(jnp.int32, (cc, cc), 0)
    si = lax.broadcasted_iota(jnp.int32, (cc, cc), 1)
    before_incl = (si >= ti) if reverse else (si <= ti)
    lw = lw_ref[0, 0]
    tri = jnp.where(before_incl, 1.0, 0.0).astype(BF16)
    lw_hi = lw.astype(BF16)
    lw_lo = (lw - lw_hi.astype(F32)).astype(BF16)
    cl = (jnp.dot(tri, lw_hi, preferred_element_type=F32) + jnp.dot(tri, lw_lo, preferred_element_type=F32))
    g_in = jnp.exp(cl)
    g_ex = jnp.exp(cl - lw)
    g_inv = jnp.exp(-cl)
    g_end = jnp.exp(jnp.sum(lw, axis=0, keepdims=True))
    kk = kk_ref[0].astype(F32)
    rt = (r_ref[0].astype(F32) * g_in).astype(BF16)
    at = (-kk * g_ex).astype(BF16)
    bt = (kk * a_ref[0, 0].astype(F32) * g_inv).astype(BF16)
    kt = (kd_ref[0, 0].astype(F32) * g_inv).astype(BF16)
    return rt, at, bt, kt, v_ref[0], g_end


def _rwkv_scan_kernel(rf_ref, vf_ref, kkf_ref, rb_ref, vb_ref, kkb_ref, lwf_ref, kdf_ref, af_ref,
                      lwb_ref, kdb_ref, ab_ref, yf_ref, yb_ref, h_ref):
    @pl.when(pl.program_id(1) == 0)
    def _():
        h_ref[...] = jnp.zeros_like(h_ref)

    cc, gw = SCAN_C, SCAN_GW
    ops = (_scan_chunk_inputs(False, rf_ref, vf_ref, kkf_ref, lwf_ref, kdf_ref, af_ref),
           _scan_chunk_inputs(True, rb_ref, vb_ref, kkb_ref, lwb_ref, kdb_ref, ab_ref))
    y_refs = (yf_ref, yb_ref)

    tg = lax.broadcasted_iota(jnp.int32, (cc, gw), 0)
    sg = lax.broadcasted_iota(jnp.int32, (cc, gw), 1) % cc
    m_strict = (sg < tg, sg > tg)
    m_incl = (sg <= tg, sg >= tg)
    eye_g = (sg == tg).astype(F32)
    bi = lax.broadcasted_iota(jnp.int32, (gw, gw), 0) // cc
    bj = lax.broadcasted_iota(jnp.int32, (gw, gw), 1) // cc
    blk = bi == bj

    def bdiag(x):
        return jnp.where(blk, jnp.concatenate([x] * SCAN_G, axis=0), jnp.zeros((), x.dtype))

    def mm(a, b):
        return jnp.dot(a, b, preferred_element_type=F32)

    def mm_nt(a, b):
        return lax.dot_general(a, b, (((1,), (1,)), ((), ())), preferred_element_type=F32)

    n_g = RW_HEADS // SCAN_G
    chains = [(d, g) for d in range(2) for g in range(n_g)]
    cols = [slice(g * gw, (g + 1) * gw) for _, g in chains]
    rt, at, bt, kt, v, g_end = ([ops[d][j][:, cols[i]] for i, (d, _) in enumerate(chains)] for j in range(6))
    idx = range(len(chains))
    h0 = [h_ref[i] for i in idx]
    h0b = [x.astype(BF16) for x in h0]
    v_bd = [bdiag(x) for x in v]
    mats = [mm_nt(jnp.concatenate([at[i], rt[i]], axis=0),
                  jnp.concatenate([bdiag(bt[i]), bdiag(kt[i])], axis=0)) for i in idx]
    a_ab = [jnp.where(m_strict[chains[i][0]], mats[i][:cc, :gw], 0.0) for i in idx]
    a_ak = [jnp.where(m_strict[chains[i][0]], mats[i][:cc, gw:], 0.0).astype(BF16) for i in idx]
    m_rb = [jnp.where(m_incl[chains[i][0]], mats[i][cc:, :gw], 0.0).astype(BF16) for i in idx]
    m_rk = [jnp.where(m_incl[chains[i][0]], mats[i][cc:, gw:], 0.0).astype(BF16) for i in idx]
    from_h = [mm(jnp.concatenate([at[i], rt[i]], axis=0), h0b[i]) for i in idx]
    from_v = [mm(jnp.concatenate([a_ak[i], m_rk[i]], axis=0), v_bd[i]) for i in idx]
    pq = [from_h[i][:cc] + from_v[i][:cc] for i in idx]
    pw = [x.astype(BF16) for x in a_ab]
    s_acc = [eye_g + x for x in a_ab]
    pw = [mm(x, bdiag(x)).astype(BF16) for x in pw]
    n = 2
    while n < cc:
        last = 2 * n >= cc
        lhs = [s_acc[i].astype(BF16) if last else jnp.concatenate([pw[i], s_acc[i].astype(BF16)], axis=0)
               for i in idx]
        prod = [mm(lhs[i], bdiag(pw[i])) for i in idx]
        s_acc = [s_acc[i] + (prod[i] if last else prod[i][cc:]) for i in idx]
        if not last:
            pw = [prod[i][:cc].astype(BF16) for i in idx]
        n *= 2
    ub = [mm(s_acc[i].astype(BF16), bdiag(pq[i].astype(BF16))).astype(BF16) for i in idx]
    for i in idx:
        y_refs[chains[i][0]][0, :, cols[i]] = from_h[i][cc:] + from_v[i][cc:] + mm(m_rb[i], bdiag(ub[i]))
    for i in idx:
        upd = lax.dot_general(jnp.concatenate([bt[i], kt[i]], axis=0), jnp.concatenate([ub[i], v[i]], axis=0),
                              (((0,), (0,)), ((), ())), preferred_element_type=F32)
        ge = jnp.transpose(jnp.broadcast_to(g_end[i], (gw, gw)))
        h_ref[i] = ge * (h0[i] + jnp.where(blk, upd, 0.0))


def rwkv_scan(r, v, kk, lw, kd, a):
    b, lt, d = r.shape
    nc = lt // SCAN_C
    n_ctx = CTX_LEN // SCAN_C
    assert SCAN_C == RW_HEAD_DIM and CTX_LEN % SCAN_C == 0

    def rev_chunk(c):
        return jnp.where(c < n_ctx, n_ctx - 1 - c, nc - 1 + n_ctx - c)

    fwd = pl.BlockSpec((1, SCAN_C, d), lambda bi, c: (bi, c, 0))
    bwd = pl.BlockSpec((1, SCAN_C, d), lambda bi, c: (bi, rev_chunk(c), 0))
    fwd_dir = pl.BlockSpec((1, 1, SCAN_C, d), lambda bi, c: (0, bi, c, 0))
    bwd_dir = pl.BlockSpec((1, 1, SCAN_C, d), lambda bi, c: (1, bi, rev_chunk(c), 0))
    y_shape = jax.ShapeDtypeStruct((b, lt, d), F32)
    return pl.pallas_call(
        _rwkv_scan_kernel,
        out_shape=(y_shape, y_shape),
        grid=(b, nc),
        in_specs=[fwd, fwd, fwd, bwd, bwd, bwd, fwd_dir, fwd_dir, fwd_dir, bwd_dir, bwd_dir, bwd_dir],
        out_specs=(fwd, bwd),
        scratch_shapes=[pltpu.VMEM((2 * RW_HEADS // SCAN_G, SCAN_GW, SCAN_GW), F32)],
        compiler_params=_cparams(("arbitrary", "arbitrary")),
        name="rwkv_scan",
    )(r, v, kk, r, v, kk, lw, kd, a, lw, kd, a)


def _rwkv_out_kernel(yf_ref, yb_ref, r_ref, v_ref, g_ref, kd_ref, h_ref, mods_ref, lnw_ref, lnb_ref, rk_ref,
                     bd_ref, wo_ref, gffn_ref, rwh_ref, rwl_ref, rb_ref, o_ref, y_ref, lg_ref):
    def head_sum(x):
        return jnp.dot(x.astype(BF16), bd_ref[...], preferred_element_type=F32)

    y = yf_ref[0] + yb_ref[0]
    inv_n = 1.0 / RW_HEAD_DIM
    mean = head_sum(y) * inv_n
    yc = y - mean
    var = head_sum(yc * yc) * inv_n
    yn = yc * lax.rsqrt(var + RW_GN_EPS) * lnw_ref[...] + lnb_ref[...]
    r = r_ref[0].astype(F32)
    kd = kd_ref[0, 0].astype(F32) + kd_ref[1, 0].astype(F32)
    yn = yn + head_sum(r * kd * rk_ref[...]) * v_ref[0].astype(F32)
    out = jnp.dot((yn * g_ref[0].astype(F32)).astype(BF16), wo_ref[...], preferred_element_type=F32)
    h_new = h_ref[0] + _mod_row(mods_ref, 2) * out
    o_ref[0] = h_new
    _router_tail(h_new, gffn_ref, mods_ref, rwh_ref, rwl_ref, rb_ref, y_ref, lg_ref)


def rwkv_readout(y_f, y_b, r, v, g, kd, h, mods, ln_w, ln_b, r_k, w_o, ffn_gain, router_w, router_b):
    b, lt, d = h.shape
    nt = lt // TILE - 1
    e = router_w.shape[1]
    r_ops, r_specs = _router_operands(ffn_gain, router_w, router_b)
    out_tile = lambda width: pl.BlockSpec((1, TILE, width), lambda bi, i: (bi, i, 0))
    hid = jnp.arange(d) // RW_HEAD_DIM
    bd = (hid[:, None] == hid[None, :]).astype(BF16)
    full = lambda shape: pl.BlockSpec(shape, lambda bi, i: (0,) * len(shape))
    tile_spec = pl.BlockSpec((1, TILE, d), lambda bi, i: (bi, i + 1, 0))
    dir_spec = pl.BlockSpec((2, 1, TILE, d), lambda bi, i: (0, bi, i + 1, 0))
    return pl.pallas_call(
        _rwkv_out_kernel,
        out_shape=(jax.ShapeDtypeStruct((b, nt * TILE, d), F32), jax.ShapeDtypeStruct((b, nt * TILE, d), BF16),
                   jax.ShapeDtypeStruct((b, nt * TILE, e), F32)),
        grid=(b, nt),
        in_specs=[tile_spec, tile_spec, tile_spec, tile_spec, tile_spec, dir_spec, tile_spec,
                  pl.BlockSpec((1, 1, 6, d), lambda bi, i: (bi, 1, 0, 0)),
                  full((1, d)), full((1, d)), full((1, d)), full((d, d)), full((d, d))] + r_specs,
        out_specs=(out_tile(d), out_tile(d), out_tile(e)),
        compiler_params=_cparams(("arbitrary", "arbitrary")),
        name="rwkv_readout",
    )(y_f, y_b, r, v, g, kd, h, mods, ln_w.reshape(1, d), ln_b.reshape(1, d), r_k.reshape(1, d), bd,
      w_o.astype(BF16), *r_ops)


def kernel(x, c, ctx, c_ctx, ada_w, ada_b, norm_mix_g, norm_ffn_g, router_w, router_b, exp_w_in, exp_b_in,
           exp_w_out, exp_b_out, ab_w_in, na_q_g, na_k_g, na_rpb, pool_w, pool_scale, ab_w_out,
           rw_mu, rw_w_r, rw_w_k, rw_w_v, rw_w_o, rw_w0, rw_w1, rw_w2, rw_a0, rw_a1, rw_a2,
           rw_g1, rw_g2, rw_k_k, rw_k_a, rw_r_k, rw_ln_w, rw_ln_b):
    b, seq, d = x.shape
    depth = ada_w.shape[0]
    assert ctx.shape[1] == CTX_LEN == TILE and seq % TILE == 0 and d == D_MODEL
    assert depth == 2, "layer schedule below is written for [neighbourhood/pool layer, RWKV layer]"
    rows = seq // GRID_W
    assert rows >= NA_QROWS * NA_KTILES and NA_QROWS * NA_KTILES >= NA_QROWS + NA_KH - 1

    n_c = 1 + b
    n_c_pad = -(-n_c // SUBLANES) * SUBLANES
    cvec = jnp.concatenate([c_ctx[None], c, jnp.zeros((n_c_pad - n_c, d), F32)], axis=0)

    def layer_mods(layer):
        m = ada_mod(cvec, ada_w[layer], ada_b[layer])
        m_ctx = jnp.broadcast_to(m[0].reshape(1, 6, d), (b, 6, d))
        return jnp.stack([m_ctx, m[1:n_c].reshape(b, 6, d)], axis=1)

    def moe(h, y, logits, layer, mods, ctx_tiles):
        return moe_layer(h, y, logits, mods, layer, exp_w_in, exp_b_in, exp_w_out, exp_b_out, ctx_tiles)

    h = jnp.concatenate([ctx, x], axis=1)

    mods = layer_mods(0)
    qk, v, u = ab_proj(h, norm_mix_g[0], mods, ab_w_in[0], na_q_g[0], na_k_g[0])
    o_na = na_attention(qk, v, _na_bias_tables(na_rpb[0], rows))
    o_pool = multiscale_pool(u, pool_w[0], pool_scale[0])
    h, y, logits = out_proj_residual(o_na, o_pool, h, mods, ab_w_out[0], norm_ffn_g[0], router_w[0], router_b[0])
    h = moe(h, y, logits, 0, mods, 1)

    mods = layer_mods(1)
    r, v, kk, g, lw, kd, a = rwkv_features(h, norm_mix_g[1], mods, rw_mu[0], rw_w_r[0], rw_w_k[0], rw_w_v[0],
                                           rw_w0[0], rw_w1[0], rw_w2[0], rw_a0[0], rw_a1[0], rw_a2[0],
                                           rw_g1[0], rw_g2[0], rw_k_k[0], rw_k_a[0])
    y_f, y_b = rwkv_scan(r, v, kk, lw, kd, a)
    h_lat, y, logits = rwkv_readout(y_f, y_b, r, v, g, kd, h, mods, rw_ln_w[0], rw_ln_b[0], rw_r_k[0], rw_w_o[0],
                                    norm_ffn_g[1], router_w[1], router_b[1])
    return moe(h_lat, y, logits, 1, mods, 0)
```

```python
import functools

import jax
import jax.numpy as jnp
import numpy as np
from jax import lax
from jax.experimental import pallas as pl
from jax.experimental.pallas import tpu as pltpu

F32 = jnp.float32
BF16 = jnp.bfloat16

D_MODEL = 1024
GRID_W = 64
CTX_LEN = 256
RMS_EPS = 1e-6
NA_HEADS = 8
NA_HEAD_DIM = 64
NA_WIDTH = NA_HEADS * NA_HEAD_DIM
NA_KH = 8
NA_KW = 16
POOL_WINDOWS = (2, 4, 8, 16)
POOL_GC = 128
POOL_WIDTH = 512
RW_HEAD_DIM = 64
RW_HEADS = D_MODEL // RW_HEAD_DIM
RW_GN_EPS = 64e-5
RW_DECAY_SCALE = float(np.exp(-0.5))
N_EXPERTS = 32
TOP_K = 4
D_FF = D_MODEL
SWIGLU_LIMIT = 7.0
SWIGLU_ALPHA = 1.702

TILE = 256
LANES = 128
SUBLANES = 8
MOE_TM = 512
NEG_BIG = -1e30
VMEM_LIMIT = 48 * 1024 * 1024
MOE_FFN_VMEM_LIMIT = 56 * 1024 * 1024


def _cparams(sem):
    return pltpu.CompilerParams(dimension_semantics=sem, vmem_limit_bytes=VMEM_LIMIT)


def _mod_row(mods_ref, k):
    return mods_ref[0, 0, pl.ds(k, 1), :]


def _norm_mod(x, gain, shift, scale):
    ms = jnp.mean(x * x, axis=-1, keepdims=True)
    return (x * lax.rsqrt(ms + RMS_EPS) * gain) * (1.0 + scale) + shift


def _ada_kernel(c_ref, w_ref, b_ref, o_ref):
    c = c_ref[...]
    s = c * jax.nn.sigmoid(c)
    o_ref[...] = jnp.dot(s, w_ref[...], preferred_element_type=F32,
                         precision=lax.Precision.HIGHEST) + b_ref[...]


def ada_mod(cvec, w, b):
    r, d = cvec.shape
    n = w.shape[1]
    tn = 1536
    return pl.pallas_call(
        _ada_kernel,
        out_shape=jax.ShapeDtypeStruct((r, n), F32),
        grid=(n // tn,),
        in_specs=[pl.BlockSpec((r, d), lambda j: (0, 0)),
                  pl.BlockSpec((d, tn), lambda j: (0, j)),
                  pl.BlockSpec((1, tn), lambda j: (0, j))],
        out_specs=pl.BlockSpec((r, tn), lambda j: (0, j)),
        compiler_params=_cparams(("arbitrary",)),
        name="ada_mod",
    )(cvec, w, b.reshape(1, n))


def _ab_proj_kernel(h_ref, g_ref, mods_ref, w_ref, bd_ref, qkg_ref, qk_ref, v_ref, u_ref):
    x = _norm_mod(h_ref[0], g_ref[...], _mod_row(mods_ref, 0), _mod_row(mods_ref, 1))
    p = jnp.dot(x.astype(BF16), w_ref[...], preferred_element_type=F32)
    qk = p[:, :2 * NA_WIDTH]
    ss = jnp.dot((qk * qk).astype(BF16), bd_ref[...], preferred_element_type=F32)
    qk_ref[0] = (qk * lax.rsqrt(ss + RMS_EPS) * qkg_ref[...]).astype(BF16)
    v_ref[0] = p[:, 2 * NA_WIDTH:3 * NA_WIDTH].astype(BF16)
    u_ref[0] = p[:, 3 * NA_WIDTH:]


def ab_proj(h, gain, mods, w_in, q_g, k_g):
    b, lt, d = h.shape
    nt = lt // TILE
    n_in = w_in.shape[1]
    hid = jnp.arange(2 * NA_WIDTH) // NA_HEAD_DIM
    bd = jnp.where(hid[:, None] == hid[None, :], 1.0 / NA_HEAD_DIM, 0.0).astype(BF16)
    qkg = jnp.concatenate([jnp.tile(q_g, NA_HEADS), jnp.tile(k_g, NA_HEADS)]).reshape(1, -1).astype(F32)
    return pl.pallas_call(
        _ab_proj_kernel,
        out_shape=(jax.ShapeDtypeStruct((b, lt, 2 * NA_WIDTH), BF16),
                   jax.ShapeDtypeStruct((b, lt, NA_WIDTH), BF16),
                   jax.ShapeDtypeStruct((b, lt, POOL_WIDTH), F32)),
        grid=(b, nt),
        in_specs=[pl.BlockSpec((1, TILE, d), lambda bi, i: (bi, i, 0)),
                  pl.BlockSpec((1, d), lambda bi, i: (0, 0)),
                  pl.BlockSpec((1, 1, 6, d), lambda bi, i: (bi, jnp.minimum(i, 1), 0, 0)),
                  pl.BlockSpec((d, n_in), lambda bi, i: (0, 0)),
                  pl.BlockSpec((2 * NA_WIDTH, 2 * NA_WIDTH), lambda bi, i: (0, 0)),
                  pl.BlockSpec((1, 2 * NA_WIDTH), lambda bi, i: (0, 0))],
        out_specs=(pl.BlockSpec((1, TILE, 2 * NA_WIDTH), lambda bi, i: (bi, i, 0)),
                   pl.BlockSpec((1, TILE, NA_WIDTH), lambda bi, i: (bi, i, 0)),
                   pl.BlockSpec((1, TILE, POOL_WIDTH), lambda bi, i: (bi, i, 0))),
        compiler_params=_cparams(("arbitrary", "arbitrary")),
        name="ab_proj",
    )(h, gain.reshape(1, d), mods, w_in.astype(BF16), bd, qkg)


NA_QROWS = TILE // GRID_W
NA_KTILES = 3
NA_NLOC = NA_KTILES * TILE
NA_NKEY = NA_NLOC + CTX_LEN


def _na_bias_tables(rpb, rows):
    n_r, n_c = 2 * NA_KH - 1, 2 * NA_KW - 1
    col = np.arange(GRID_W)
    c_start = np.clip(col - NA_KW // 2, 0, GRID_W - NA_KW)
    valid_c = (col[None, :] >= c_start[:, None]) & (col[None, :] < c_start[:, None] + NA_KW)
    c_rel = np.clip(col[None, :] - col[:, None] + NA_KW - 1, 0, n_c - 1)
    sel_c = (c_rel.reshape(-1)[None, :] == np.arange(n_c)[:, None]).astype(np.float32)
    by_col = jnp.einsum("hrc,cn->hrn", rpb.astype(F32), sel_c, precision=lax.Precision.HIGHEST)
    n_qt = rows // NA_QROWS
    qr, kr = np.arange(NA_QROWS), np.arange(NA_KTILES * NA_QROWS)
    tabs = []
    for jq in (0, 1, n_qt - 1):
        w0 = NA_QROWS * min(max(jq - 1, 0), n_qt - NA_KTILES)
        q_row = NA_QROWS * jq + qr
        key_row = w0 + kr
        r_start = np.clip(q_row - NA_KH // 2, 0, rows - NA_KH)
        valid_r = (key_row[None, :] >= r_start[:, None]) & (key_row[None, :] < r_start[:, None] + NA_KH)
        r_rel = np.clip(key_row[None, :] - q_row[:, None] + NA_KH - 1, 0, n_r - 1)
        sel_r = (r_rel.reshape(-1)[:, None] == np.arange(n_r)[None, :]).astype(np.float32)
        t = jnp.einsum("xr,hrn->hxn", sel_r, by_col, precision=lax.Precision.HIGHEST)
        t = t.reshape(NA_HEADS, NA_QROWS, NA_KTILES * NA_QROWS, GRID_W, GRID_W)
        t = t.transpose(0, 1, 3, 2, 4).reshape(NA_HEADS, TILE, NA_NLOC)
        valid = (valid_r[:, None, :, None] & valid_c[None, :, None, :]).reshape(TILE, NA_NLOC)
        tabs.append(jnp.where(valid[None], t, NEG_BIG))
    tabs = [jnp.full_like(tabs[0], NEG_BIG)] + tabs
    loc = jnp.stack(tabs)
    return jnp.concatenate([loc, jnp.zeros(loc.shape[:3] + (CTX_LEN,), F32)], axis=-1)


def _na_kernel(q_ref, k0_ref, k1_ref, k2_ref, kc_ref, v0_ref, v1_ref, v2_ref, vc_ref, bias_ref, o_ref):
    q = q_ref[0]
    kall = jnp.concatenate([k0_ref[0], k1_ref[0], k2_ref[0], kc_ref[0]], axis=0)
    vall = jnp.concatenate([v0_ref[0], v1_ref[0], v2_ref[0], vc_ref[0]], axis=0)
    lane = lax.broadcasted_iota(jnp.int32, (TILE, LANES), 1)
    scale = jnp.asarray(NA_HEAD_DIM ** -0.5, BF16)
    heads = range(NA_HEADS)
    pair = [slice((h // 2) * LANES, (h // 2 + 1) * LANES) for h in heads]
    scores = []
    for h in heads:
        lo = (h % 2) * NA_HEAD_DIM
        qh = jnp.where((lane >= lo) & (lane < lo + NA_HEAD_DIM), q[:, pair[h]], jnp.zeros((), BF16)) * scale
        s = lax.dot_general(qh, kall[:, pair[h]], (((1,), (1,)), ((), ())), preferred_element_type=F32)
        scores.append(s + bias_ref[0, h])
    probs, denom = [], []
    for s in scores:
        p = jnp.exp(s - jnp.max(s, axis=-1, keepdims=True))
        probs.append(p.astype(BF16))
        denom.append(jnp.sum(p, axis=-1, keepdims=True))
    outs = [jnp.dot(probs[h], vall[:, pair[h]], preferred_element_type=F32) / denom[h] for h in heads]
    for h in range(0, NA_HEADS, 2):
        o_ref[0, :, pair[h]] = jnp.where(lane < NA_HEAD_DIM, outs[h], outs[h + 1]).astype(BF16)


def na_attention(qk, v, bias):
    b, lt, _ = qk.shape
    nt = lt // TILE
    n_qt = nt - 1
    assert NA_HEAD_DIM ** -0.5 == 0.125 and NA_HEADS % 2 == 0
    w = NA_WIDTH

    def w0(j):
        return 1 + jnp.clip(j - 2, 0, n_qt - NA_KTILES)

    def kind(j):
        return jnp.where(j == 0, 0, jnp.where(j == 1, 1, jnp.where(j == nt - 1, 3, 2)))

    qspec = pl.BlockSpec((1, TILE, w), lambda j, bi: (bi, j, 0))
    kspecs = [pl.BlockSpec((1, TILE, w), functools.partial(lambda j, bi, t: (bi, w0(j) + t, 1), t=t))
              for t in range(NA_KTILES)]
    kcspec = pl.BlockSpec((1, TILE, w), lambda j, bi: (bi, 0, 1))
    vspecs = [pl.BlockSpec((1, TILE, w), functools.partial(lambda j, bi, t: (bi, w0(j) + t, 0), t=t))
              for t in range(NA_KTILES)]
    vcspec = pl.BlockSpec((1, TILE, w), lambda j, bi: (bi, 0, 0))
    bspec = pl.BlockSpec((1, NA_HEADS, TILE, NA_NKEY), lambda j, bi: (kind(j), 0, 0, 0))
    return pl.pallas_call(
        _na_kernel,
        out_shape=jax.ShapeDtypeStruct((b, lt, w), BF16),
        grid=(nt, b),
        in_specs=[qspec] + kspecs + [kcspec] + vspecs + [vcspec, bspec],
        out_specs=pl.BlockSpec((1, TILE, w), lambda j, bi: (bi, j, 0)),
        compiler_params=_cparams(("arbitrary", "arbitrary")),
        name="na_attention",
    )(qk, qk, qk, qk, qk, v, v, v, v, bias)


POOL_HALO = 8
POOL_ROWS = TILE + 2 * POOL_HALO


def _pool_kernel(cur_ref, prev_ref, next_ref, pw_ref, ps_ref, o_ref, xp_ref, *, n_tiles):
    i = pl.program_id(1)
    has_prev = (i >= 2).astype(F32)
    has_next = ((i >= 1) & (i < n_tiles - 1)).astype(F32)
    xp_ref[0:POOL_HALO, :] = prev_ref[0] * has_prev
    xp_ref[POOL_HALO:POOL_HALO + TILE, :] = cur_ref[0]
    xp_ref[POOL_HALO + TILE:POOL_ROWS, :] = next_ref[0] * has_next
    row = lax.broadcasted_iota(jnp.int32, (TILE, POOL_GC), 0)
    tpos = jnp.where(i == 0, row, row + (i - 1) * TILE)
    seq_len = jnp.where(i == 0, CTX_LEN, (n_tiles - 1) * TILE)
    for g, w in enumerate(POOL_WINDOWS):
        cols = slice(g * POOL_GC, (g + 1) * POOL_GC)
        x = xp_ref[:, cols]
        s = x + pltpu.roll(x, 1, 0)
        half = 1
        while 2 * half < w:
            s = pltpu.roll(s, half, 0) + pltpu.roll(s, POOL_ROWS - half, 0)
            half *= 2
        s = s[POOL_HALO:POOL_HALO + TILE]
        cnt = jnp.minimum(w // 2, tpos) + jnp.minimum(w // 2, seq_len - tpos)
        pooled = s / cnt.astype(F32) - cur_ref[0, :, cols]
        y = jnp.dot(pooled.astype(BF16), pw_ref[g], preferred_element_type=F32)
        o_ref[0, :, cols] = (y * ps_ref[:, cols]).astype(BF16)


def multiscale_pool(u, pool_w, pool_scale):
    b, lt, c = u.shape
    nt = lt // TILE
    hb = TILE // POOL_HALO
    nhb = lt // POOL_HALO
    assert POOL_HALO >= max(POOL_WINDOWS) // 2 and POOL_HALO == SUBLANES
    return pl.pallas_call(
        functools.partial(_pool_kernel, n_tiles=nt),
        out_shape=jax.ShapeDtypeStruct((b, lt, c), BF16),
        grid=(b, nt),
        in_specs=[pl.BlockSpec((1, TILE, c), lambda bi, i: (bi, i, 0)),
                  pl.BlockSpec((1, POOL_HALO, c), lambda bi, i: (bi, jnp.maximum(i * hb - 1, 0), 0)),
                  pl.BlockSpec((1, POOL_HALO, c), lambda bi, i: (bi, jnp.minimum((i + 1) * hb, nhb - 1), 0)),
                  pl.BlockSpec((len(POOL_WINDOWS), POOL_GC, POOL_GC), lambda bi, i: (0, 0, 0)),
                  pl.BlockSpec((1, c), lambda bi, i: (0, 0))],
        out_specs=pl.BlockSpec((1, TILE, c), lambda bi, i: (bi, i, 0)),
        scratch_shapes=[pltpu.VMEM((POOL_ROWS, c), F32)],
        compiler_params=_cparams(("arbitrary", "arbitrary")),
        name="multiscale_pool",
    )(u, u, u, pool_w.astype(BF16), pool_scale.reshape(1, c).astype(F32))


def _router_tail(h_new, g_ref, mods_ref, rwh_ref, rwl_ref, rb_ref, y_ref, idx_ref, gate_ref):
    x = _norm_mod(h_new, g_ref[...], _mod_row(mods_ref, 3), _mod_row(mods_ref, 4))
    x_hi = x.astype(BF16)
    y_ref[0] = x_hi
    x_lo = (x - x_hi.astype(F32)).astype(BF16)
    w_hi, w_lo = rwh_ref[...], rwl_ref[...]
    logits = (jnp.dot(x_hi, w_hi, preferred_element_type=F32)
              + jnp.dot(x_lo, w_hi, preferred_element_type=F32)
              + jnp.dot(x_hi, w_lo, preferred_element_type=F32)
              + jnp.dot(x_lo, w_lo, preferred_element_type=F32)) + rb_ref[...]
    lane = lax.broadcasted_iota(jnp.int32, logits.shape, 1)
    vals, ids = [], []
    cur = logits
    for _ in range(TOP_K):
        m = jnp.max(cur, axis=-1, keepdims=True)
        i = jnp.min(jnp.where(cur == m, lane, N_EXPERTS), axis=-1, keepdims=True)
        vals.append(m)
        ids.append(i)
        cur = jnp.where(lane == i, -jnp.inf, cur)
    es = [jnp.exp(v - vals[0]) for v in vals]
    total = functools.reduce(lambda p, q: p + q, es)
    out_lane = lax.broadcasted_iota(jnp.int32, (logits.shape[0], TOP_K), 1)
    idx = jnp.zeros(out_lane.shape, jnp.int32)
    gate = jnp.zeros(out_lane.shape, F32)
    for k in range(TOP_K):
        idx = jnp.where(out_lane == k, ids[k], idx)
        gate = jnp.where(out_lane == k, es[k] / total, gate)
    idx_ref[0] = idx
    gate_ref[0] = gate


def _router_operands(gain, router_w, router_b):
    d, e = router_w.shape
    head = lax.bitcast_convert_type(lax.bitcast_convert_type(router_w, jnp.uint32) & jnp.uint32(0xFFFF0000), F32)
    full = lambda shape: pl.BlockSpec(shape, lambda bi, i: (0,) * len(shape))
    return ((gain.reshape(1, d), head.astype(BF16), (router_w - head).astype(BF16), router_b.reshape(1, e)),
            [full((1, d)), full((d, e)), full((d, e)), full((1, e))])


def _out_proj_kernel(a_ref, b_ref, h_ref, mods_ref, w_ref, g_ref, rwh_ref, rwl_ref, rb_ref,
                     o_ref, y_ref, idx_ref, gate_ref):
    ka = a_ref.shape[-1]
    y = jnp.dot(a_ref[0], w_ref[:ka, :], preferred_element_type=F32)
    y = y + jnp.dot(b_ref[0], w_ref[ka:, :], preferred_element_type=F32)
    h_new = h_ref[0] + _mod_row(mods_ref, 2) * y
    o_ref[0] = h_new
    _router_tail(h_new, g_ref, mods_ref, rwh_ref, rwl_ref, rb_ref, y_ref, idx_ref, gate_ref)


def out_proj_residual(a, bb, h, mods, w, ffn_gain, router_w, router_b):
    b, lt, d = h.shape
    nt = lt // TILE
    ka, kb = a.shape[-1], bb.shape[-1]
    r_ops, r_specs = _router_operands(ffn_gain, router_w, router_b)
    tile = lambda width: pl.BlockSpec((1, TILE, width), lambda bi, i: (bi, i, 0))
    return pl.pallas_call(
        _out_proj_kernel,
        out_shape=(jax.ShapeDtypeStruct((b, lt, d), F32), jax.ShapeDtypeStruct((b, lt, d), BF16),
                   jax.ShapeDtypeStruct((b, lt, TOP_K), jnp.int32), jax.ShapeDtypeStruct((b, lt, TOP_K), F32)),
        grid=(b, nt),
        in_specs=[tile(ka), tile(kb), tile(d),
                  pl.BlockSpec((1, 1, 6, d), lambda bi, i: (bi, jnp.minimum(i, 1), 0, 0)),
                  pl.BlockSpec((ka + kb, d), lambda bi, i: (0, 0))] + r_specs,
        out_specs=(tile(d), tile(d), tile(TOP_K), tile(TOP_K)),
        compiler_params=_cparams(("arbitrary", "arbitrary")),
        name="out_proj_residual",
    )(a, bb, h, mods, w.astype(BF16), *r_ops)


MOE_PERM_W = 256


def _moe_ffn_kernel(te_ref, tv_ref, x_ref, win_ref, bg_ref, bl_ref, wo_ref, bo_ref, perm_ref, o_ref,
                    wg_s, wl_s, wo_s):
    t = pl.program_id(0)
    valid = tv_ref[t] > 0
    new_expert = (t == 0) | (te_ref[t] != te_ref[jnp.maximum(t - 1, 0)])

    @pl.when(valid & new_expert)
    def _():
        half = MOE_PERM_W // 2
        for cb in range(win_ref.shape[3] // MOE_PERM_W):
            w = win_ref[0, 0, :, cb * MOE_PERM_W:(cb + 1) * MOE_PERM_W].astype(BF16)
            wp = jnp.dot(w, perm_ref[...], preferred_element_type=F32).astype(BF16)
            wg_s[:, cb * half:(cb + 1) * half] = wp[:, :half]
            wl_s[:, cb * half:(cb + 1) * half] = wp[:, half:]
        wo_s[...] = wo_ref[0, 0].astype(BF16)

    @pl.when(jnp.logical_not(valid))
    def _():
        o_ref[...] = jnp.zeros_like(o_ref)

    @pl.when(valid)
    def _():
        x = x_ref[...].astype(BF16)
        hg = jnp.dot(x, wg_s[...], preferred_element_type=F32) + bg_ref[0]
        hl = jnp.dot(x, wl_s[...], preferred_element_type=F32) + bl_ref[0]
        glu = jnp.minimum(hg, SWIGLU_LIMIT)
        lin = jnp.clip(hl, -SWIGLU_LIMIT, SWIGLU_LIMIT)
        act = glu * jax.nn.sigmoid(SWIGLU_ALPHA * glu) * (lin + 1.0)
        o_ref[...] = (jnp.dot(act.astype(BF16), wo_s[...], preferred_element_type=F32)
                      + bo_ref[0]).astype(o_ref.dtype)


def moe_grouped_ffn(x_sorted, tile_expert, tile_valid, layer, w_in, bg, bl, w_out, bo):
    p, d = x_sorted.shape
    _, e, _, f2 = w_in.shape
    f = f2 // 2
    nt = p // MOE_TM
    assert f2 % MOE_PERM_W == 0 and MOE_PERM_W % (2 * LANES) == 0
    src = jnp.arange(MOE_PERM_W)
    dst = jnp.where(src % 2 == 0, src // 2, MOE_PERM_W // 2 + src // 2)
    perm = (dst[:, None] == jnp.arange(MOE_PERM_W)[None, :]).astype(BF16)
    grid_spec = pltpu.PrefetchScalarGridSpec(
        num_scalar_prefetch=2,
        grid=(nt,),
        in_specs=[pl.BlockSpec((MOE_TM, d), lambda t, te, tv: (t * tv[t], 0)),
                  pl.BlockSpec((1, 1, d, f2), lambda t, te, tv: (layer, te[t], 0, 0)),
                  pl.BlockSpec((1, 1, f), lambda t, te, tv: (te[t], 0, 0)),
                  pl.BlockSpec((1, 1, f), lambda t, te, tv: (te[t], 0, 0)),
                  pl.BlockSpec((1, 1, f, d), lambda t, te, tv: (layer, te[t], 0, 0)),
                  pl.BlockSpec((1, 1, d), lambda t, te, tv: (te[t], 0, 0)),
                  pl.BlockSpec((MOE_PERM_W, MOE_PERM_W), lambda t, te, tv: (0, 0))],
        out_specs=pl.BlockSpec((MOE_TM, d), lambda t, te, tv: (t, 0)),
        scratch_shapes=[pltpu.VMEM((d, f), BF16), pltpu.VMEM((d, f), BF16), pltpu.VMEM((f, d), BF16)],
    )
    return pl.pallas_call(
        _moe_ffn_kernel,
        out_shape=jax.ShapeDtypeStruct((p, d), BF16),
        grid_spec=grid_spec,
        compiler_params=pltpu.CompilerParams(dimension_semantics=("arbitrary",),
                                             vmem_limit_bytes=MOE_FFN_VMEM_LIMIT),
        name="moe_grouped_ffn",
    )(tile_expert, tile_valid, x_sorted, w_in, bg, bl, w_out, bo, perm)


def _moe_combine_kernel(z_ref, gt_ref, h_ref, mods_ref, o_ref):
    gt = gt_ref[0]
    acc = gt[:, 0:1] * z_ref[0, 0].astype(F32)
    for k in range(1, TOP_K):
        acc = acc + gt[:, k:k + 1] * z_ref[k, 0].astype(F32)
    o_ref[0] = h_ref[0] + _mod_row(mods_ref, 5) * acc


def moe_combine(z4, gates, h, mods, ctx_tiles):
    k, b, lp, d = z4.shape
    nt = lp // TILE
    return pl.pallas_call(
        _moe_combine_kernel,
        out_shape=jax.ShapeDtypeStruct((b, lp, d), F32),
        grid=(b, nt),
        in_specs=[pl.BlockSpec((k, 1, TILE, d), lambda bi, i: (0, bi, i, 0)),
                  pl.BlockSpec((1, TILE, k), lambda bi, i: (bi, i, 0)),
                  pl.BlockSpec((1, TILE, d), lambda bi, i: (bi, i, 0)),
                  pl.BlockSpec((1, 1, 6, d), lambda bi, i: (bi, jnp.minimum(i + 1 - ctx_tiles, 1), 0, 0))],
        out_specs=pl.BlockSpec((1, TILE, d), lambda bi, i: (bi, i, 0)),
        compiler_params=_cparams(("arbitrary", "arbitrary")),
        name="moe_combine",
    )(z4, gates, h, mods)


DISPATCH_T = 512
DISPATCH_BLK = SUBLANES
DISPATCH_ROWS = -(-(DISPATCH_T * TOP_K + N_EXPERTS * (DISPATCH_BLK - 1)) // (2 * SUBLANES)) * (2 * SUBLANES)
DISPATCH_COPY_ROWS = (4 * DISPATCH_BLK, DISPATCH_BLK)
DISPATCH_LIST = max(DISPATCH_ROWS // DISPATCH_COPY_ROWS[0],
                    N_EXPERTS * (DISPATCH_COPY_ROWS[0] // DISPATCH_COPY_ROWS[1] - 1))


def _moe_dispatch_kernel(cnt_ref, tab_ref, ntail_ref, tail_ref, y_ref, idx_ref, off_ref, base_ref,
                         xs_ref, slot_ref, local_ref, zero_ref, sem, zero_sem):
    t = pl.program_id(0)
    last_t = pl.num_programs(0) - 1
    dt = DISPATCH_T
    idx = idx_ref[0]
    e_iota = lax.broadcasted_iota(jnp.int32, (N_EXPERTS, dt), 0)
    picked = [e_iota == idx[k:k + 1, :] for k in range(TOP_K)]
    cnt = sum(jnp.where(p, 1.0, 0.0) for p in picked).astype(BF16)
    r0 = lax.broadcasted_iota(jnp.int32, (dt, dt), 0)
    r1 = lax.broadcasted_iota(jnp.int32, (dt, dt), 1)
    earlier = jnp.where(r0 < r1, 1.0, 0.0).astype(BF16)
    rank = jnp.dot(cnt, earlier, preferred_element_type=F32)
    local_row = rank + off_ref[0].astype(F32)
    global_row = rank + base_ref[0].astype(F32)
    rows = lax.broadcasted_iota(jnp.int32, (DISPATCH_ROWS, dt), 0)
    place = None
    for k in range(TOP_K):
        pos = jnp.sum(jnp.where(picked[k], local_row, 0.0), axis=0, keepdims=True).astype(jnp.int32)
        slot_ref[0, k:k + 1, :] = jnp.sum(jnp.where(picked[k], global_row, 0.0), axis=0,
                                          keepdims=True).astype(jnp.int32)
        hit = rows == pos
        place = hit if place is None else (place | hit)
    placement = jnp.where(place, 1.0, 0.0).astype(BF16)
    buf = lax.rem(t, 2)
    local_ref[buf] = jnp.dot(placement, y_ref[...], preferred_element_type=F32)

    def seg_copy(cls, step, j):
        rows = DISPATCH_COPY_ROWS[cls]
        src_buf = jnp.where(step == t, buf, 1 - buf)
        src_blk = tab_ref[step, (2 * cls) * DISPATCH_LIST + j]
        dst_blk = tab_ref[step, (2 * cls + 1) * DISPATCH_LIST + j]
        src = local_ref.at[src_buf, pl.ds(pl.multiple_of(src_blk * DISPATCH_BLK, DISPATCH_BLK), rows), :]
        dst = xs_ref.at[pl.ds(pl.multiple_of(dst_blk * DISPATCH_BLK, DISPATCH_BLK), rows), :]
        return pltpu.make_async_copy(src, dst, sem)

    def tail_copy(e, j):
        blk = tail_ref[e] + j
        dst = xs_ref.at[pl.ds(pl.multiple_of(blk * DISPATCH_BLK, DISPATCH_BLK), DISPATCH_BLK), :]
        return pltpu.make_async_copy(zero_ref, dst, zero_sem)

    def start_all(copy, n):
        lax.fori_loop(0, n, lambda j, c: (copy(j).start(), c)[1], 0)

    def wait_all(copy, n):
        lax.fori_loop(0, n, lambda j, c: (copy(j).wait(), c)[1], 0)

    classes = range(len(DISPATCH_COPY_ROWS))

    @pl.when(t > 0)
    def _():
        for cls in classes:
            wait_all(functools.partial(seg_copy, cls, t - 1), cnt_ref[t - 1, cls])

    for cls in classes:
        start_all(functools.partial(seg_copy, cls, t), cnt_ref[t, cls])

    @pl.when(t == last_t)
    def _():
        for cls in classes:
            wait_all(functools.partial(seg_copy, cls, t), cnt_ref[t, cls])

    @pl.when(t == 0)
    def _():
        zero_ref[...] = jnp.zeros_like(zero_ref)
        for e in range(N_EXPERTS + 1):
            start_all(functools.partial(tail_copy, e), ntail_ref[e])
        for e in range(N_EXPERTS + 1):
            wait_all(functools.partial(tail_copy, e), ntail_ref[e])


def moe_dispatch(y2, idx_t, copy_cnt, copy_tab, ntail, tail_blk, loc_off, base, p_rows):
    n, d = y2.shape
    nt = n // DISPATCH_T
    grid_spec = pltpu.PrefetchScalarGridSpec(
        num_scalar_prefetch=4,
        grid=(nt,),
        in_specs=[pl.BlockSpec((DISPATCH_T, d), lambda t, *_: (t, 0)),
                  pl.BlockSpec((1, TOP_K, DISPATCH_T), lambda t, *_: (t, 0, 0)),
                  pl.BlockSpec((1, N_EXPERTS, 1), lambda t, *_: (t, 0, 0)),
                  pl.BlockSpec((1, N_EXPERTS, 1), lambda t, *_: (t, 0, 0))],
        out_specs=(pl.BlockSpec(memory_space=pl.ANY),
                   pl.BlockSpec((1, TOP_K, DISPATCH_T), lambda t, *_: (t, 0, 0))),
        scratch_shapes=[pltpu.VMEM((2, DISPATCH_ROWS, d), F32), pltpu.VMEM((DISPATCH_BLK, d), F32),
                        pltpu.SemaphoreType.DMA(()), pltpu.SemaphoreType.DMA(())],
    )
    return pl.pallas_call(
        _moe_dispatch_kernel,
        out_shape=(jax.ShapeDtypeStruct((p_rows, d), F32),
                   jax.ShapeDtypeStruct((nt, TOP_K, DISPATCH_T), jnp.int32)),
        grid_spec=grid_spec,
        compiler_params=_cparams(("arbitrary",)),
        name="moe_dispatch",
    )(copy_cnt, copy_tab, ntail, tail_blk, y2, idx_t, loc_off, base)


def moe_layer(h, y, top_idx, gates, mods, layer, w_in, b_in, w_out, b_out, ctx_tiles):
    b, lp, d = h.shape
    n = b * lp
    y2 = y.reshape(n, d)

    assert n % DISPATCH_T == 0
    t_n = n // DISPATCH_T
    blk = DISPATCH_BLK
    experts = jnp.arange(N_EXPERTS, dtype=jnp.int32)
    idx_t = top_idx.reshape(t_n, DISPATCH_T, TOP_K).transpose(0, 2, 1)
    tile_cnt = jnp.sum((idx_t[..., None] == experts).astype(jnp.int32), axis=(1, 2))
    seg = (tile_cnt + blk - 1) // blk * blk
    loc_end = jnp.cumsum(seg, axis=1)
    loc_off = loc_end - seg
    tot = jnp.sum(seg, axis=0)
    padded = (tot + MOE_TM - 1) // MOE_TM * MOE_TM
    ends = jnp.cumsum(padded)
    starts = ends - padded
    base = starts[None, :] + jnp.cumsum(seg, axis=0) - seg
    q = jnp.arange(DISPATCH_LIST, dtype=jnp.int32)
    covered = jnp.zeros_like(seg)
    tabs, cnts = [], []
    for rows in DISPATCH_COPY_ROWS:
        n_c = (seg - covered) // rows
        c_end = jnp.cumsum(n_c, axis=1)
        sel = jnp.sum((c_end[:, None, :] <= q[None, :, None]).astype(jnp.int32), axis=2)[..., None] == experts
        pick = lambda a: jnp.sum(jnp.where(sel, a[:, None, :], 0), axis=2)
        inside = (q[None, :] - pick(c_end - n_c)) * rows
        tabs += [(pick(loc_off + covered) + inside) // blk, (pick(base + covered) + inside) // blk]
        cnts.append(c_end[:, -1])
        covered = covered + n_c * rows
    copy_tab = jnp.concatenate(tabs, axis=1)
    copy_cnt = jnp.stack(cnts, axis=1)
    n_tiles = -(-(n * TOP_K + t_n * N_EXPERTS * (blk - 1)) // MOE_TM) + N_EXPERTS
    tile_start = jnp.arange(n_tiles, dtype=jnp.int32) * MOE_TM
    tile_expert = jnp.minimum(jnp.sum((tile_start[:, None] >= ends[None, :]).astype(jnp.int32), axis=1),
                              N_EXPERTS - 1)
    tile_valid = (tile_start < ends[-1]).astype(jnp.int32)

    p_rows = n_tiles * MOE_TM
    gap_len = jnp.concatenate([padded - tot, p_rows - ends[-1:]]) // blk
    gap_start = jnp.concatenate([starts + tot, ends[-1:]]) // blk
    x_sorted, slot = moe_dispatch(y2, idx_t, copy_cnt, copy_tab, gap_len, gap_start,
                                  loc_off[..., None], base[..., None], p_rows)
    bg = b_in[layer, :, 0::2].reshape(N_EXPERTS, 1, D_FF)
    bl = b_in[layer, :, 1::2].reshape(N_EXPERTS, 1, D_FF)
    z = moe_grouped_ffn(x_sorted, tile_expert, tile_valid, layer, w_in, bg, bl, w_out,
                        b_out[layer].reshape(N_EXPERTS, 1, d))
    slot_kn = slot.transpose(1, 0, 2).reshape(TOP_K, n)
    z4 = z.at[slot_kn].get(mode="promise_in_bounds").reshape(TOP_K, b, lp, d)
    return moe_combine(z4, gates, h, mods, ctx_tiles)


def _rwkv_feat_kernel(h_ref, hp_ref, hn_ref, g_ref, mods_ref, mu_ref, wr_ref, wk_ref, wv_ref,
                      w1_ref, w2_ref, a1_ref, a2_ref, g1_ref, g2_ref, w0_ref, a0_ref, kk_ref, ka_ref,
                      bd_ref, r_out, v_out, kk_out, g_out, lw_out, kd_out, a_out, *, n_tiles):
    i = pl.program_id(1)
    gain, shift, scale = g_ref[...], _mod_row(mods_ref, 0), _mod_row(mods_ref, 1)
    x = _norm_mod(h_ref[0], gain, shift, scale)
    has_prev = (i >= 2).astype(F32)
    has_next = ((i >= 1) & (i < n_tiles - 1)).astype(F32)
    x_before = _norm_mod(hp_ref[0], gain, shift, scale)[SUBLANES - 1:SUBLANES] * has_prev
    x_after = _norm_mod(hn_ref[0], gain, shift, scale)[0:1] * has_next
    row = lax.broadcasted_iota(jnp.int32, x.shape, 0)
    prev = jnp.where(row == 0, x_before, pltpu.roll(x, 1, 0))
    nxt = jnp.where(row == TILE - 1, x_after, pltpu.roll(x, TILE - 1, 0))
    xx = 0.5 * (prev + nxt) - x
    xr, xw, xk, xv, xa, xg = (x + xx * mu_ref[pl.ds(j, 1), :] for j in range(6))

    def mm(a, w):
        return jnp.dot(a.astype(BF16), w, preferred_element_type=F32)

    r = mm(xr, wr_ref[...])
    k = mm(xk, wk_ref[...])
    v = mm(xv, wv_ref[...])
    g = mm(jax.nn.sigmoid(mm(xg, g1_ref[...])), g2_ref[...])
    kk = k * kk_ref[...]
    kk = kk * lax.rsqrt(mm(kk * kk, bd_ref[...]) + 1e-12)
    r_out[0] = r.astype(BF16)
    v_out[0] = v.astype(BF16)
    kk_out[0] = kk.astype(BF16)
    g_out[0] = g.astype(BF16)
    tw = jnp.tanh(mm(xw, w1_ref[...]))
    av = mm(xa, a1_ref[...])
    lane = lax.broadcasted_iota(jnp.int32, tw.shape, 1)
    half = tw.shape[1] // 2
    for d in range(2):
        in_dir = (lane >= d * half) & (lane < (d + 1) * half)
        wz = w0_ref[pl.ds(d, 1), :] + mm(jnp.where(in_dir, tw, 0.0), w2_ref[...])
        lw_out[d, 0] = -RW_DECAY_SCALE * jax.nn.sigmoid(wz)
        a = jax.nn.sigmoid(a0_ref[pl.ds(d, 1), :] + mm(jnp.where(in_dir, av, 0.0), a2_ref[...]))
        kd_out[d, 0] = (k * (1.0 + (a - 1.0) * ka_ref[...])).astype(BF16)
        a_out[d, 0] = a.astype(BF16)


def rwkv_features(h, gain, mods, mu, w_r, w_k, w_v, w0, w1, w2, a0, a1, a2, g1, g2, k_k, k_a):
    b, lt, d = h.shape
    nt = lt // TILE
    hb = TILE // SUBLANES
    nhb = lt // SUBLANES
    hid = jnp.arange(d) // RW_HEAD_DIM
    bd = (hid[:, None] == hid[None, :]).astype(BF16)
    w1c = jnp.concatenate([w1[0], w1[1]], axis=1).astype(BF16)
    a1c = jnp.concatenate([a1[0], a1[1]], axis=1).astype(BF16)
    w2c = jnp.concatenate([w2[0], w2[1]], axis=0).astype(BF16)
    a2c = jnp.concatenate([a2[0], a2[1]], axis=0).astype(BF16)
    full = lambda shape: pl.BlockSpec(shape, lambda bi, i: (0,) * len(shape))
    tile_spec = pl.BlockSpec((1, TILE, d), lambda bi, i: (bi, i, 0))
    dir_spec = pl.BlockSpec((2, 1, TILE, d), lambda bi, i: (0, bi, i, 0))
    seq_bf = jax.ShapeDtypeStruct((b, lt, d), BF16)
    return pl.pallas_call(
        functools.partial(_rwkv_feat_kernel, n_tiles=nt),
        out_shape=(seq_bf, seq_bf, seq_bf, seq_bf,
                   jax.ShapeDtypeStruct((2, b, lt, d), F32),
                   jax.ShapeDtypeStruct((2, b, lt, d), BF16),
                   jax.ShapeDtypeStruct((2, b, lt, d), BF16)),
        grid=(b, nt),
        in_specs=[tile_spec,
                  pl.BlockSpec((1, SUBLANES, d), lambda bi, i: (bi, jnp.maximum(i * hb - 1, 0), 0)),
                  pl.BlockSpec((1, SUBLANES, d), lambda bi, i: (bi, jnp.minimum((i + 1) * hb, nhb - 1), 0)),
                  full((1, d)),
                  pl.BlockSpec((1, 1, 6, d), lambda bi, i: (bi, jnp.minimum(i, 1), 0, 0)),
                  full((6, d)), full((d, d)), full((d, d)), full((d, d)),
                  full(w1c.shape), full(w2c.shape), full(a1c.shape), full(a2c.shape),
                  full(g1.shape), full(g2.shape), full((2, d)), full((2, d)), full((1, d)), full((1, d)),
                  full((d, d))],
        out_specs=(tile_spec, tile_spec, tile_spec, tile_spec, dir_spec, dir_spec, dir_spec),
        compiler_params=_cparams(("arbitrary", "arbitrary")),
        name="rwkv_features",
    )(h, h, h, gain.reshape(1, d), mods, mu, w_r.astype(BF16), w_k.astype(BF16), w_v.astype(BF16),
      w1c, w2c, a1c, a2c, g1.astype(BF16), g2.astype(BF16), w0, a0, k_k.reshape(1, d), k_a.reshape(1, d), bd)


SCAN_C = 64
SCAN_G = 4
SCAN_GW = SCAN_G * RW_HEAD_DIM


def _scan_chunk_inputs(reverse, r_ref, v_ref, kk_ref, lw_ref, kd_ref, a_ref):
    cc = SCAN_C
    ti = lax.broadcasted_iota(jnp.int32, (cc, cc), 0)
    si = lax.broadcasted_iota(jnp.int32, (cc, cc), 1)
    before_incl = (si >= ti) if reverse else (si <= ti)
    lw = lw_ref[0, 0]
    tri = jnp.where(before_incl, 1.0, 0.0).astype(BF16)
    lw_hi = lw.astype(BF16)
    lw_lo = (lw - lw_hi.astype(F32)).astype(BF16)
    cl = (jnp.dot(tri, lw_hi, preferred_element_type=F32) + jnp.dot(tri, lw_lo, preferred_element_type=F32))
    g_in = jnp.exp(cl)
    g_ex = jnp.exp(cl - lw)
    g_inv = jnp.exp(-cl)
    g_end = jnp.exp(jnp.sum(lw, axis=0, keepdims=True))
    kk = kk_ref[0].astype(F32)
    rt = (r_ref[0].astype(F32) * g_in).astype(BF16)
    at = (-kk * g_ex).astype(BF16)
    bt = (kk * a_ref[0, 0].astype(F32) * g_inv).astype(BF16)
    kt = (kd_ref[0, 0].astype(F32) * g_inv).astype(BF16)
    return rt, at, bt, kt, v_ref[0], g_end


def _rwkv_scan_kernel(rf_ref, vf_ref, kkf_ref, rb_ref, vb_ref, kkb_ref, lwf_ref, kdf_ref, af_ref,
                      lwb_ref, kdb_ref, ab_ref, yf_ref, yb_ref, h_ref):
    @pl.when(pl.program_id(1) == 0)
    def _():
        h_ref[...] = jnp.zeros_like(h_ref)

    cc, gw = SCAN_C, SCAN_GW
    ops = (_scan_chunk_inputs(False, rf_ref, vf_ref, kkf_ref, lwf_ref, kdf_ref, af_ref),
           _scan_chunk_inputs(True, rb_ref, vb_ref, kkb_ref, lwb_ref, kdb_ref, ab_ref))
    y_refs = (yf_ref, yb_ref)

    tg = lax.broadcasted_iota(jnp.int32, (cc, gw), 0)
    sg = lax.broadcasted_iota(jnp.int32, (cc, gw), 1) % cc
    m_strict = (sg < tg, sg > tg)
    m_incl = (sg <= tg, sg >= tg)
    eye_g = (sg == tg).astype(F32)
    bi = lax.broadcasted_iota(jnp.int32, (gw, gw), 0) // cc
    bj = lax.broadcasted_iota(jnp.int32, (gw, gw), 1) // cc
    blk = bi == bj

    def bdiag(x):
        return jnp.where(blk, jnp.concatenate([x] * SCAN_G, axis=0), jnp.zeros((), x.dtype))

    def mm(a, b):
        return jnp.dot(a, b, preferred_element_type=F32)

    def mm_nt(a, b):
        return lax.dot_general(a, b, (((1,), (1,)), ((), ())), preferred_element_type=F32)

    n_g = RW_HEADS // SCAN_G
    chains = [(d, g) for d in range(2) for g in range(n_g)]
    cols = [slice(g * gw, (g + 1) * gw) for _, g in chains]
    rt, at, bt, kt, v, g_end = ([ops[d][j][:, cols[i]] for i, (d, _) in enumerate(chains)] for j in range(6))
    idx = range(len(chains))
    h0 = [h_ref[i] for i in idx]
    h0b = [x.astype(BF16) for x in h0]
    v_bd = [bdiag(x) for x in v]
    mats = [mm_nt(jnp.concatenate([at[i], rt[i]], axis=0),
                  jnp.concatenate([bdiag(bt[i]), bdiag(kt[i])], axis=0)) for i in idx]
    a_ab = [jnp.where(m_strict[chains[i][0]], mats[i][:cc, :gw], 0.0) for i in idx]
    a_ak = [jnp.where(m_strict[chains[i][0]], mats[i][:cc, gw:], 0.0).astype(BF16) for i in idx]
    m_rb = [jnp.where(m_incl[chains[i][0]], mats[i][cc:, :gw], 0.0).astype(BF16) for i in idx]
    m_rk = [jnp.where(m_incl[chains[i][0]], mats[i][cc:, gw:], 0.0).astype(BF16) for i in idx]
    from_h = [mm(jnp.concatenate([at[i], rt[i]], axis=0), h0b[i]) for i in idx]
    from_v = [mm(jnp.concatenate([a_ak[i], m_rk[i]], axis=0), v_bd[i]) for i in idx]
    pq = [from_h[i][:cc] + from_v[i][:cc] for i in idx]
    pw = [x.astype(BF16) for x in a_ab]
    s_acc = [eye_g + x for x in a_ab]
    pw = [mm(x, bdiag(x)).astype(BF16) for x in pw]
    n = 2
    while n < cc:
        last = 2 * n >= cc
        lhs = [s_acc[i].astype(BF16) if last else jnp.concatenate([pw[i], s_acc[i].astype(BF16)], axis=0)
               for i in idx]
        prod = [mm(lhs[i], bdiag(pw[i])) for i in idx]
        s_acc = [s_acc[i] + (prod[i] if last else prod[i][cc:]) for i in idx]
        if not last:
            pw = [prod[i][:cc].astype(BF16) for i in idx]
        n *= 2
    ub = [mm(s_acc[i].astype(BF16), bdiag(pq[i].astype(BF16))).astype(BF16) for i in idx]
    for i in idx:
        y_refs[chains[i][0]][0, :, cols[i]] = from_h[i][cc:] + from_v[i][cc:] + mm(m_rb[i], bdiag(ub[i]))
    for i in idx:
        upd = lax.dot_general(jnp.concatenate([bt[i], kt[i]], axis=0), jnp.concatenate([ub[i], v[i]], axis=0),
                              (((0,), (0,)), ((), ())), preferred_element_type=F32)
        ge = jnp.transpose(jnp.broadcast_to(g_end[i], (gw, gw)))
        h_ref[i] = ge * (h0[i] + jnp.where(blk, upd, 0.0))


def rwkv_scan(r, v, kk, lw, kd, a):
    b, lt, d = r.shape
    nc = lt // SCAN_C
    n_ctx = CTX_LEN // SCAN_C
    assert SCAN_C == RW_HEAD_DIM and CTX_LEN % SCAN_C == 0

    def rev_chunk(c):
        return jnp.where(c < n_ctx, n_ctx - 1 - c, nc - 1 + n_ctx - c)

    fwd = pl.BlockSpec((1, SCAN_C, d), lambda bi, c: (bi, c, 0))
    bwd = pl.BlockSpec((1, SCAN_C, d), lambda bi, c: (bi, rev_chunk(c), 0))
    fwd_dir = pl.BlockSpec((1, 1, SCAN_C, d), lambda bi, c: (0, bi, c, 0))
    bwd_dir = pl.BlockSpec((1, 1, SCAN_C, d), lambda bi, c: (1, bi, rev_chunk(c), 0))
    y_shape = jax.ShapeDtypeStruct((b, lt, d), F32)
    return pl.pallas_call(
        _rwkv_scan_kernel,
        out_shape=(y_shape, y_shape),
        grid=(b, nc),
        in_specs=[fwd, fwd, fwd, bwd, bwd, bwd, fwd_dir, fwd_dir, fwd_dir, bwd_dir, bwd_dir, bwd_dir],
        out_specs=(fwd, bwd),
        scratch_shapes=[pltpu.VMEM((2 * RW_HEADS // SCAN_G, SCAN_GW, SCAN_GW), F32)],
        compiler_params=_cparams(("arbitrary", "arbitrary")),
        name="rwkv_scan",
    )(r, v, kk, r, v, kk, lw, kd, a, lw, kd, a)


def _rwkv_out_kernel(yf_ref, yb_ref, r_ref, v_ref, g_ref, kd_ref, h_ref, mods_ref, lnw_ref, lnb_ref, rk_ref,
                     bd_ref, wo_ref, gffn_ref, rwh_ref, rwl_ref, rb_ref, o_ref, y_ref, idx_ref, gate_ref):
    def head_sum(x):
        return jnp.dot(x.astype(BF16), bd_ref[...], preferred_element_type=F32)

    y = yf_ref[0] + yb_ref[0]
    inv_n = 1.0 / RW_HEAD_DIM
    mean = head_sum(y) * inv_n
    yc = y - mean
    var = head_sum(yc * yc) * inv_n
    yn = yc * lax.rsqrt(var + RW_GN_EPS) * lnw_ref[...] + lnb_ref[...]
    r = r_ref[0].astype(F32)
    kd = kd_ref[0, 0].astype(F32) + kd_ref[1, 0].astype(F32)
    yn = yn + head_sum(r * kd * rk_ref[...]) * v_ref[0].astype(F32)
    out = jnp.dot((yn * g_ref[0].astype(F32)).astype(BF16), wo_ref[...], preferred_element_type=F32)
    h_new = h_ref[0] + _mod_row(mods_ref, 2) * out
    o_ref[0] = h_new
    _router_tail(h_new, gffn_ref, mods_ref, rwh_ref, rwl_ref, rb_ref, y_ref, idx_ref, gate_ref)


def rwkv_readout(y_f, y_b, r, v, g, kd, h, mods, ln_w, ln_b, r_k, w_o, ffn_gain, router_w, router_b):
    b, lt, d = h.shape
    nt = lt // TILE - 1
    r_ops, r_specs = _router_operands(ffn_gain, router_w, router_b)
    out_tile = lambda width: pl.BlockSpec((1, TILE, width), lambda bi, i: (bi, i, 0))
    hid = jnp.arange(d) // RW_HEAD_DIM
    bd = (hid[:, None] == hid[None, :]).astype(BF16)
    full = lambda shape: pl.BlockSpec(shape, lambda bi, i: (0,) * len(shape))
    tile_spec = pl.BlockSpec((1, TILE, d), lambda bi, i: (bi, i + 1, 0))
    dir_spec = pl.BlockSpec((2, 1, TILE, d), lambda bi, i: (0, bi, i + 1, 0))
    return pl.pallas_call(
        _rwkv_out_kernel,
        out_shape=(jax.ShapeDtypeStruct((b, nt * TILE, d), F32), jax.ShapeDtypeStruct((b, nt * TILE, d), BF16),
                   jax.ShapeDtypeStruct((b, nt * TILE, TOP_K), jnp.int32),
                   jax.ShapeDtypeStruct((b, nt * TILE, TOP_K), F32)),
        grid=(b, nt),
        in_specs=[tile_spec, tile_spec, tile_spec, tile_spec, tile_spec, dir_spec, tile_spec,
                  pl.BlockSpec((1, 1, 6, d), lambda bi, i: (bi, 1, 0, 0)),
                  full((1, d)), full((1, d)), full((1, d)), full((d, d)), full((d, d))] + r_specs,
        out_specs=(out_tile(d), out_tile(d), out_tile(TOP_K), out_tile(TOP_K)),
        compiler_params=_cparams(("arbitrary", "arbitrary")),
        name="rwkv_readout",
    )(y_f, y_b, r, v, g, kd, h, mods, ln_w.reshape(1, d), ln_b.reshape(1, d), r_k.reshape(1, d), bd,
      w_o.astype(BF16), *r_ops)


def kernel(x, c, ctx, c_ctx, ada_w, ada_b, norm_mix_g, norm_ffn_g, router_w, router_b, exp_w_in, exp_b_in,
           exp_w_out, exp_b_out, ab_w_in, na_q_g, na_k_g, na_rpb, pool_w, pool_scale, ab_w_out,
           rw_mu, rw_w_r, rw_w_k, rw_w_v, rw_w_o, rw_w0, rw_w1, rw_w2, rw_a0, rw_a1, rw_a2,
           rw_g1, rw_g2, rw_k_k, rw_k_a, rw_r_k, rw_ln_w, rw_ln_b):
    b, seq, d = x.shape
    depth = ada_w.shape[0]
    assert ctx.shape[1] == CTX_LEN == TILE and seq % TILE == 0 and d == D_MODEL
    assert depth == 2, "layer schedule below is written for [neighbourhood/pool layer, RWKV layer]"
    rows = seq // GRID_W
    assert rows >= NA_QROWS * NA_KTILES and NA_QROWS * NA_KTILES >= NA_QROWS + NA_KH - 1

    n_c = 1 + b
    n_c_pad = -(-n_c // SUBLANES) * SUBLANES
    cvec = jnp.concatenate([c_ctx[None], c, jnp.zeros((n_c_pad - n_c, d), F32)], axis=0)

    def layer_mods(layer):
        m = ada_mod(cvec, ada_w[layer], ada_b[layer])
        m_ctx = jnp.broadcast_to(m[0].reshape(1, 6, d), (b, 6, d))
        return jnp.stack([m_ctx, m[1:n_c].reshape(b, 6, d)], axis=1)

    def moe(h, routed, layer, mods, ctx_tiles):
        return moe_layer(h, *routed, mods, layer, exp_w_in, exp_b_in, exp_w_out, exp_b_out, ctx_tiles)

    h = jnp.concatenate([ctx, x], axis=1)

    mods = layer_mods(0)
    qk, v, u = ab_proj(h, norm_mix_g[0], mods, ab_w_in[0], na_q_g[0], na_k_g[0])
    o_na = na_attention(qk, v, _na_bias_tables(na_rpb[0], rows))
    o_pool = multiscale_pool(u, pool_w[0], pool_scale[0])
    h, *routed = out_proj_residual(o_na, o_pool, h, mods, ab_w_out[0], norm_ffn_g[0], router_w[0], router_b[0])
    h = moe(h, routed, 0, mods, 1)

    mods = layer_mods(1)
    r, v, kk, g, lw, kd, a = rwkv_features(h, norm_mix_g[1], mods, rw_mu[0], rw_w_r[0], rw_w_k[0], rw_w_v[0],
                                           rw_w0[0], rw_w1[0], rw_w2[0], rw_a0[0], rw_a1[0], rw_a2[0],
                                           rw_g1[0], rw_g2[0], rw_k_k[0], rw_k_a[0])
    y_f, y_b = rwkv_scan(r, v, kk, lw, kd, a)
    h_lat, *routed = rwkv_readout(y_f, y_b, r, v, g, kd, h, mods, rw_ln_w[0], rw_ln_b[0], rw_r_k[0], rw_w_o[0],
                                  norm_ffn_g[1], router_w[1], router_b[1])
    return moe(h_lat, routed, 1, mods, 0)
```

```python
import functools

import jax
import jax.numpy as jnp
import numpy as np
from jax import lax
from jax.experimental import pallas as pl
from jax.experimental.pallas import tpu as pltpu

F32 = jnp.float32
BF16 = jnp.bfloat16

D_MODEL = 1024
GRID_W = 64
CTX_LEN = 256
RMS_EPS = 1e-6
NA_HEADS = 8
NA_HEAD_DIM = 64
NA_WIDTH = NA_HEADS * NA_HEAD_DIM
NA_KH = 8
NA_KW = 16
POOL_WINDOWS = (2, 4, 8, 16)
POOL_GC = 128
POOL_WIDTH = 512
RW_HEAD_DIM = 64
RW_HEADS = D_MODEL // RW_HEAD_DIM
RW_GN_EPS = 64e-5
RW_DECAY_SCALE = float(np.exp(-0.5))
N_EXPERTS = 32
TOP_K = 4
D_FF = D_MODEL
SWIGLU_LIMIT = 7.0
SWIGLU_ALPHA = 1.702

TILE = 256
LANES = 128
SUBLANES = 8
MOE_TM = 512
NEG_BIG = -1e30
VMEM_LIMIT = 48 * 1024 * 1024
MOE_FFN_VMEM_LIMIT = 56 * 1024 * 1024


def _cparams(sem):
    return pltpu.CompilerParams(dimension_semantics=sem, vmem_limit_bytes=VMEM_LIMIT)


def _mod_row(mods_ref, k):
    return mods_ref[0, 0, pl.ds(k, 1), :]


def _norm_mod(x, gain, shift, scale):
    ms = jnp.mean(x * x, axis=-1, keepdims=True)
    return (x * lax.rsqrt(ms + RMS_EPS) * gain) * (1.0 + scale) + shift


def _ada_kernel(c_ref, w_ref, b_ref, o_ref):
    c = c_ref[...]
    s = c * jax.nn.sigmoid(c)
    o_ref[...] = jnp.dot(s, w_ref[...], preferred_element_type=F32,
                         precision=lax.Precision.HIGHEST) + b_ref[...]


def ada_mod(cvec, w, b):
    r, d = cvec.shape
    n = w.shape[1]
    tn = 1536
    return pl.pallas_call(
        _ada_kernel,
        out_shape=jax.ShapeDtypeStruct((r, n), F32),
        grid=(n // tn,),
        in_specs=[pl.BlockSpec((r, d), lambda j: (0, 0)),
                  pl.BlockSpec((d, tn), lambda j: (0, j)),
                  pl.BlockSpec((1, tn), lambda j: (0, j))],
        out_specs=pl.BlockSpec((r, tn), lambda j: (0, j)),
        compiler_params=_cparams(("arbitrary",)),
        name="ada_mod",
    )(cvec, w, b.reshape(1, n))


def _ab_proj_kernel(h_ref, g_ref, mods_ref, w_ref, bd_ref, qkg_ref, qk_ref, v_ref, u_ref):
    x = _norm_mod(h_ref[0], g_ref[...], _mod_row(mods_ref, 0), _mod_row(mods_ref, 1))
    p = jnp.dot(x.astype(BF16), w_ref[...], preferred_element_type=F32)
    qk = p[:, :2 * NA_WIDTH]
    ss = jnp.dot((qk * qk).astype(BF16), bd_ref[...], preferred_element_type=F32)
    qk_ref[0] = (qk * lax.rsqrt(ss + RMS_EPS) * qkg_ref[...]).astype(BF16)
    v_ref[0] = p[:, 2 * NA_WIDTH:3 * NA_WIDTH].astype(BF16)
    u_ref[0] = p[:, 3 * NA_WIDTH:]


def ab_proj(h, gain, mods, w_in, q_g, k_g):
    b, lt, d = h.shape
    nt = lt // TILE
    n_in = w_in.shape[1]
    hid = jnp.arange(2 * NA_WIDTH) // NA_HEAD_DIM
    bd = jnp.where(hid[:, None] == hid[None, :], 1.0 / NA_HEAD_DIM, 0.0).astype(BF16)
    qkg = jnp.concatenate([jnp.tile(q_g, NA_HEADS), jnp.tile(k_g, NA_HEADS)]).reshape(1, -1).astype(F32)
    return pl.pallas_call(
        _ab_proj_kernel,
        out_shape=(jax.ShapeDtypeStruct((b, lt, 2 * NA_WIDTH), BF16),
                   jax.ShapeDtypeStruct((b, lt, NA_WIDTH), BF16),
                   jax.ShapeDtypeStruct((b, lt, POOL_WIDTH), F32)),
        grid=(b, nt),
        in_specs=[pl.BlockSpec((1, TILE, d), lambda bi, i: (bi, i, 0)),
                  pl.BlockSpec((1, d), lambda bi, i: (0, 0)),
                  pl.BlockSpec((1, 1, 6, d), lambda bi, i: (bi, jnp.minimum(i, 1), 0, 0)),
                  pl.BlockSpec((d, n_in), lambda bi, i: (0, 0)),
                  pl.BlockSpec((2 * NA_WIDTH, 2 * NA_WIDTH), lambda bi, i: (0, 0)),
                  pl.BlockSpec((1, 2 * NA_WIDTH), lambda bi, i: (0, 0))],
        out_specs=(pl.BlockSpec((1, TILE, 2 * NA_WIDTH), lambda bi, i: (bi, i, 0)),
                   pl.BlockSpec((1, TILE, NA_WIDTH), lambda bi, i: (bi, i, 0)),
                   pl.BlockSpec((1, TILE, POOL_WIDTH), lambda bi, i: (bi, i, 0))),
        compiler_params=_cparams(("arbitrary", "arbitrary")),
        name="ab_proj",
    )(h, gain.reshape(1, d), mods, w_in.astype(BF16), bd, qkg)


NA_QROWS = TILE // GRID_W
NA_KTILES = 3
NA_NLOC = NA_KTILES * TILE
NA_NKEY = NA_NLOC + CTX_LEN


def _na_bias_tables(rpb, rows):
    n_r, n_c = 2 * NA_KH - 1, 2 * NA_KW - 1
    col = np.arange(GRID_W)
    c_start = np.clip(col - NA_KW // 2, 0, GRID_W - NA_KW)
    valid_c = (col[None, :] >= c_start[:, None]) & (col[None, :] < c_start[:, None] + NA_KW)
    c_rel = np.clip(col[None, :] - col[:, None] + NA_KW - 1, 0, n_c - 1)
    sel_c = (c_rel.reshape(-1)[None, :] == np.arange(n_c)[:, None]).astype(np.float32)
    by_col = jnp.einsum("hrc,cn->hrn", rpb.astype(F32), sel_c, precision=lax.Precision.HIGHEST)
    n_qt = rows // NA_QROWS
    qr, kr = np.arange(NA_QROWS), np.arange(NA_KTILES * NA_QROWS)
    tabs = []
    for jq in (0, 1, n_qt - 1):
        w0 = NA_QROWS * min(max(jq - 1, 0), n_qt - NA_KTILES)
        q_row = NA_QROWS * jq + qr
        key_row = w0 + kr
        r_start = np.clip(q_row - NA_KH // 2, 0, rows - NA_KH)
        valid_r = (key_row[None, :] >= r_start[:, None]) & (key_row[None, :] < r_start[:, None] + NA_KH)
        r_rel = np.clip(key_row[None, :] - q_row[:, None] + NA_KH - 1, 0, n_r - 1)
        sel_r = (r_rel.reshape(-1)[:, None] == np.arange(n_r)[None, :]).astype(np.float32)
        t = jnp.einsum("xr,hrn->hxn", sel_r, by_col, precision=lax.Precision.HIGHEST)
        t = t.reshape(NA_HEADS, NA_QROWS, NA_KTILES * NA_QROWS, GRID_W, GRID_W)
        t = t.transpose(0, 1, 3, 2, 4).reshape(NA_HEADS, TILE, NA_NLOC)
        valid = (valid_r[:, None, :, None] & valid_c[None, :, None, :]).reshape(TILE, NA_NLOC)
        tabs.append(jnp.where(valid[None], t, NEG_BIG))
    tabs = [jnp.full_like(tabs[0], NEG_BIG)] + tabs
    loc = jnp.stack(tabs)
    return jnp.concatenate([loc, jnp.zeros(loc.shape[:3] + (CTX_LEN,), F32)], axis=-1)


def _na_kernel(q_ref, k0_ref, k1_ref, k2_ref, kc_ref, v0_ref, v1_ref, v2_ref, vc_ref, bias_ref, o_ref):
    q = q_ref[0]
    kall = jnp.concatenate([k0_ref[0], k1_ref[0], k2_ref[0], kc_ref[0]], axis=0)
    vall = jnp.concatenate([v0_ref[0], v1_ref[0], v2_ref[0], vc_ref[0]], axis=0)
    lane = lax.broadcasted_iota(jnp.int32, (TILE, LANES), 1)
    scale = jnp.asarray(NA_HEAD_DIM ** -0.5, BF16)
    heads = range(NA_HEADS)
    pair = [slice((h // 2) * LANES, (h // 2 + 1) * LANES) for h in heads]
    scores = []
    for h in heads:
        lo = (h % 2) * NA_HEAD_DIM
        qh = jnp.where((lane >= lo) & (lane < lo + NA_HEAD_DIM), q[:, pair[h]], jnp.zeros((), BF16)) * scale
        s = lax.dot_general(qh, kall[:, pair[h]], (((1,), (1,)), ((), ())), preferred_element_type=F32)
        scores.append(s + bias_ref[0, h])
    probs, denom = [], []
    for s in scores:
        p = jnp.exp(s - jnp.max(s, axis=-1, keepdims=True))
        probs.append(p.astype(BF16))
        denom.append(jnp.sum(p, axis=-1, keepdims=True))
    outs = [jnp.dot(probs[h], vall[:, pair[h]], preferred_element_type=F32) / denom[h] for h in heads]
    for h in range(0, NA_HEADS, 2):
        o_ref[0, :, pair[h]] = jnp.where(lane < NA_HEAD_DIM, outs[h], outs[h + 1]).astype(BF16)


def na_attention(qk, v, bias):
    b, lt, _ = qk.shape
    nt = lt // TILE
    n_qt = nt - 1
    assert NA_HEAD_DIM ** -0.5 == 0.125 and NA_HEADS % 2 == 0
    w = NA_WIDTH

    def w0(j):
        return 1 + jnp.clip(j - 2, 0, n_qt - NA_KTILES)

    def kind(j):
        return jnp.where(j == 0, 0, jnp.where(j == 1, 1, jnp.where(j == nt - 1, 3, 2)))

    qspec = pl.BlockSpec((1, TILE, w), lambda j, bi: (bi, j, 0))
    kspecs = [pl.BlockSpec((1, TILE, w), functools.partial(lambda j, bi, t: (bi, w0(j) + t, 1), t=t))
              for t in range(NA_KTILES)]
    kcspec = pl.BlockSpec((1, TILE, w), lambda j, bi: (bi, 0, 1))
    vspecs = [pl.BlockSpec((1, TILE, w), functools.partial(lambda j, bi, t: (bi, w0(j) + t, 0), t=t))
              for t in range(NA_KTILES)]
    vcspec = pl.BlockSpec((1, TILE, w), lambda j, bi: (bi, 0, 0))
    bspec = pl.BlockSpec((1, NA_HEADS, TILE, NA_NKEY), lambda j, bi: (kind(j), 0, 0, 0))
    return pl.pallas_call(
        _na_kernel,
        out_shape=jax.ShapeDtypeStruct((b, lt, w), BF16),
        grid=(nt, b),
        in_specs=[qspec] + kspecs + [kcspec] + vspecs + [vcspec, bspec],
        out_specs=pl.BlockSpec((1, TILE, w), lambda j, bi: (bi, j, 0)),
        compiler_params=_cparams(("arbitrary", "arbitrary")),
        name="na_attention",
    )(qk, qk, qk, qk, qk, v, v, v, v, bias)


POOL_HALO = 8
POOL_ROWS = TILE + 2 * POOL_HALO


def _pool_kernel(cur_ref, prev_ref, next_ref, pw_ref, ps_ref, o_ref, xp_ref, *, n_tiles):
    i = pl.program_id(1)
    has_prev = (i >= 2).astype(F32)
    has_next = ((i >= 1) & (i < n_tiles - 1)).astype(F32)
    xp_ref[0:POOL_HALO, :] = prev_ref[0] * has_prev
    xp_ref[POOL_HALO:POOL_HALO + TILE, :] = cur_ref[0]
    xp_ref[POOL_HALO + TILE:POOL_ROWS, :] = next_ref[0] * has_next
    row = lax.broadcasted_iota(jnp.int32, (TILE, POOL_GC), 0)
    tpos = jnp.where(i == 0, row, row + (i - 1) * TILE)
    seq_len = jnp.where(i == 0, CTX_LEN, (n_tiles - 1) * TILE)
    for g, w in enumerate(POOL_WINDOWS):
        cols = slice(g * POOL_GC, (g + 1) * POOL_GC)
        x = xp_ref[:, cols]
        s = x + pltpu.roll(x, 1, 0)
        half = 1
        while 2 * half < w:
            s = pltpu.roll(s, half, 0) + pltpu.roll(s, POOL_ROWS - half, 0)
            half *= 2
        s = s[POOL_HALO:POOL_HALO + TILE]
        cnt = jnp.minimum(w // 2, tpos) + jnp.minimum(w // 2, seq_len - tpos)
        pooled = s / cnt.astype(F32) - cur_ref[0, :, cols]
        y = jnp.dot(pooled.astype(BF16), pw_ref[g], preferred_element_type=F32)
        o_ref[0, :, cols] = (y * ps_ref[:, cols]).astype(BF16)


def multiscale_pool(u, pool_w, pool_scale):
    b, lt, c = u.shape
    nt = lt // TILE
    hb = TILE // POOL_HALO
    nhb = lt // POOL_HALO
    assert POOL_HALO >= max(POOL_WINDOWS) // 2 and POOL_HALO == SUBLANES
    return pl.pallas_call(
        functools.partial(_pool_kernel, n_tiles=nt),
        out_shape=jax.ShapeDtypeStruct((b, lt, c), BF16),
        grid=(b, nt),
        in_specs=[pl.BlockSpec((1, TILE, c), lambda bi, i: (bi, i, 0)),
                  pl.BlockSpec((1, POOL_HALO, c), lambda bi, i: (bi, jnp.maximum(i * hb - 1, 0), 0)),
                  pl.BlockSpec((1, POOL_HALO, c), lambda bi, i: (bi, jnp.minimum((i + 1) * hb, nhb - 1), 0)),
                  pl.BlockSpec((len(POOL_WINDOWS), POOL_GC, POOL_GC), lambda bi, i: (0, 0, 0)),
                  pl.BlockSpec((1, c), lambda bi, i: (0, 0))],
        out_specs=pl.BlockSpec((1, TILE, c), lambda bi, i: (bi, i, 0)),
        scratch_shapes=[pltpu.VMEM((POOL_ROWS, c), F32)],
        compiler_params=_cparams(("arbitrary", "arbitrary")),
        name="multiscale_pool",
    )(u, u, u, pool_w.astype(BF16), pool_scale.reshape(1, c).astype(F32))


def _router_tail(h_new, g_ref, mods_ref, rwh_ref, rwl_ref, rb_ref, y_ref, lg_ref):
    x = _norm_mod(h_new, g_ref[...], _mod_row(mods_ref, 3), _mod_row(mods_ref, 4))
    x_hi = x.astype(BF16)
    y_ref[0] = x_hi
    x_lo = (x - x_hi.astype(F32)).astype(BF16)
    w_hi, w_lo = rwh_ref[...], rwl_ref[...]
    lg_ref[0] = (jnp.dot(x_hi, w_hi, preferred_element_type=F32)
                 + jnp.dot(x_lo, w_hi, preferred_element_type=F32)
                 + jnp.dot(x_hi, w_lo, preferred_element_type=F32)
                 + jnp.dot(x_lo, w_lo, preferred_element_type=F32)) + rb_ref[...]


def _router_operands(gain, router_w, router_b):
    d, e = router_w.shape
    head = lax.bitcast_convert_type(lax.bitcast_convert_type(router_w, jnp.uint32) & jnp.uint32(0xFFFF0000), F32)
    full = lambda shape: pl.BlockSpec(shape, lambda bi, i: (0,) * len(shape))
    return ((gain.reshape(1, d), head.astype(BF16), (router_w - head).astype(BF16), router_b.reshape(1, e)),
            [full((1, d)), full((d, e)), full((d, e)), full((1, e))])


def _out_proj_kernel(a_ref, b_ref, h_ref, mods_ref, w_ref, g_ref, rwh_ref, rwl_ref, rb_ref,
                     o_ref, y_ref, lg_ref):
    ka = a_ref.shape[-1]
    y = jnp.dot(a_ref[0], w_ref[:ka, :], preferred_element_type=F32)
    y = y + jnp.dot(b_ref[0], w_ref[ka:, :], preferred_element_type=F32)
    h_new = h_ref[0] + _mod_row(mods_ref, 2) * y
    o_ref[0] = h_new
    _router_tail(h_new, g_ref, mods_ref, rwh_ref, rwl_ref, rb_ref, y_ref, lg_ref)


def out_proj_residual(a, bb, h, mods, w, ffn_gain, router_w, router_b):
    b, lt, d = h.shape
    nt = lt // TILE
    ka, kb = a.shape[-1], bb.shape[-1]
    e = router_w.shape[1]
    r_ops, r_specs = _router_operands(ffn_gain, router_w, router_b)
    tile = lambda width: pl.BlockSpec((1, TILE, width), lambda bi, i: (bi, i, 0))
    return pl.pallas_call(
        _out_proj_kernel,
        out_shape=(jax.ShapeDtypeStruct((b, lt, d), F32), jax.ShapeDtypeStruct((b, lt, d), BF16),
                   jax.ShapeDtypeStruct((b, lt, e), F32)),
        grid=(b, nt),
        in_specs=[tile(ka), tile(kb), tile(d),
                  pl.BlockSpec((1, 1, 6, d), lambda bi, i: (bi, jnp.minimum(i, 1), 0, 0)),
                  pl.BlockSpec((ka + kb, d), lambda bi, i: (0, 0))] + r_specs,
        out_specs=(tile(d), tile(d), tile(e)),
        compiler_params=_cparams(("arbitrary", "arbitrary")),
        name="out_proj_residual",
    )(a, bb, h, mods, w.astype(BF16), *r_ops)


MOE_PERM_W = 256


def _moe_ffn_kernel(te_ref, tv_ref, x_ref, win_ref, bg_ref, bl_ref, wo_ref, bo_ref, perm_ref, o_ref,
                    wg_s, wl_s, wo_s):
    t = pl.program_id(0)
    valid = tv_ref[t] > 0
    new_expert = (t == 0) | (te_ref[t] != te_ref[jnp.maximum(t - 1, 0)])

    @pl.when(valid & new_expert)
    def _():
        half = MOE_PERM_W // 2
        for cb in range(win_ref.shape[3] // MOE_PERM_W):
            w = win_ref[0, 0, :, cb * MOE_PERM_W:(cb + 1) * MOE_PERM_W].astype(BF16)
            wp = jnp.dot(w, perm_ref[...], preferred_element_type=F32).astype(BF16)
            wg_s[:, cb * half:(cb + 1) * half] = wp[:, :half]
            wl_s[:, cb * half:(cb + 1) * half] = wp[:, half:]
        wo_s[...] = wo_ref[0, 0].astype(BF16)

    @pl.when(jnp.logical_not(valid))
    def _():
        o_ref[...] = jnp.zeros_like(o_ref)

    @pl.when(valid)
    def _():
        x = x_ref[...].astype(BF16)
        hg = jnp.dot(x, wg_s[...], preferred_element_type=F32) + bg_ref[0]
        hl = jnp.dot(x, wl_s[...], preferred_element_type=F32) + bl_ref[0]
        glu = jnp.minimum(hg, SWIGLU_LIMIT)
        lin = jnp.clip(hl, -SWIGLU_LIMIT, SWIGLU_LIMIT)
        act = glu * jax.nn.sigmoid(SWIGLU_ALPHA * glu) * (lin + 1.0)
        o_ref[...] = (jnp.dot(act.astype(BF16), wo_s[...], preferred_element_type=F32)
                      + bo_ref[0]).astype(o_ref.dtype)


def moe_grouped_ffn(x_sorted, tile_expert, tile_valid, layer, w_in, bg, bl, w_out, bo):
    p, d = x_sorted.shape
    _, e, _, f2 = w_in.shape
    f = f2 // 2
    nt = p // MOE_TM
    assert f2 % MOE_PERM_W == 0 and MOE_PERM_W % (2 * LANES) == 0
    src = jnp.arange(MOE_PERM_W)
    dst = jnp.where(src % 2 == 0, src // 2, MOE_PERM_W // 2 + src // 2)
    perm = (dst[:, None] == jnp.arange(MOE_PERM_W)[None, :]).astype(BF16)
    grid_spec = pltpu.PrefetchScalarGridSpec(
        num_scalar_prefetch=2,
        grid=(nt,),
        in_specs=[pl.BlockSpec((MOE_TM, d), lambda t, te, tv: (t * tv[t], 0)),
                  pl.BlockSpec((1, 1, d, f2), lambda t, te, tv: (layer, te[t], 0, 0)),
                  pl.BlockSpec((1, 1, f), lambda t, te, tv: (te[t], 0, 0)),
                  pl.BlockSpec((1, 1, f), lambda t, te, tv: (te[t], 0, 0)),
                  pl.BlockSpec((1, 1, f, d), lambda t, te, tv: (layer, te[t], 0, 0)),
                  pl.BlockSpec((1, 1, d), lambda t, te, tv: (te[t], 0, 0)),
                  pl.BlockSpec((MOE_PERM_W, MOE_PERM_W), lambda t, te, tv: (0, 0))],
        out_specs=pl.BlockSpec((MOE_TM, d), lambda t, te, tv: (t, 0)),
        scratch_shapes=[pltpu.VMEM((d, f), BF16), pltpu.VMEM((d, f), BF16), pltpu.VMEM((f, d), BF16)],
    )
    return pl.pallas_call(
        _moe_ffn_kernel,
        out_shape=jax.ShapeDtypeStruct((p, d), BF16),
        grid_spec=grid_spec,
        compiler_params=pltpu.CompilerParams(dimension_semantics=("arbitrary",),
                                             vmem_limit_bytes=MOE_FFN_VMEM_LIMIT),
        name="moe_grouped_ffn",
    )(tile_expert, tile_valid, x_sorted, w_in, bg, bl, w_out, bo, perm)


def _moe_combine_kernel(z_ref, gt_ref, h_ref, mods_ref, o_ref):
    gt = gt_ref[0]
    acc = gt[:, 0:1] * z_ref[0, 0].astype(F32)
    for k in range(1, TOP_K):
        acc = acc + gt[:, k:k + 1] * z_ref[k, 0].astype(F32)
    o_ref[0] = h_ref[0] + _mod_row(mods_ref, 5) * acc


def moe_combine(z4, gates, h, mods, ctx_tiles):
    k, b, lp, d = z4.shape
    nt = lp // TILE
    return pl.pallas_call(
        _moe_combine_kernel,
        out_shape=jax.ShapeDtypeStruct((b, lp, d), F32),
        grid=(b, nt),
        in_specs=[pl.BlockSpec((k, 1, TILE, d), lambda bi, i: (0, bi, i, 0)),
                  pl.BlockSpec((1, TILE, k), lambda bi, i: (bi, i, 0)),
                  pl.BlockSpec((1, TILE, d), lambda bi, i: (bi, i, 0)),
                  pl.BlockSpec((1, 1, 6, d), lambda bi, i: (bi, jnp.minimum(i + 1 - ctx_tiles, 1), 0, 0))],
        out_specs=pl.BlockSpec((1, TILE, d), lambda bi, i: (bi, i, 0)),
        compiler_params=_cparams(("arbitrary", "arbitrary")),
        name="moe_combine",
    )(z4, gates, h, mods)


DISPATCH_T = 512
DISPATCH_BLK = SUBLANES
DISPATCH_ROWS = -(-(DISPATCH_T * TOP_K + N_EXPERTS * (DISPATCH_BLK - 1)) // (2 * SUBLANES)) * (2 * SUBLANES)
DISPATCH_COPY_ROWS = (4 * DISPATCH_BLK, DISPATCH_BLK)
DISPATCH_LIST = max(DISPATCH_ROWS // DISPATCH_COPY_ROWS[0],
                    N_EXPERTS * (DISPATCH_COPY_ROWS[0] // DISPATCH_COPY_ROWS[1] - 1))


def _moe_dispatch_kernel(cnt_ref, tab_ref, ntail_ref, tail_ref, y_ref, idx_ref, off_ref, base_ref,
                         xs_ref, slot_ref, local_ref, zero_ref, sem, zero_sem):
    t = pl.program_id(0)
    last_t = pl.num_programs(0) - 1
    dt = DISPATCH_T
    idx = idx_ref[0]
    e_iota = lax.broadcasted_iota(jnp.int32, (N_EXPERTS, dt), 0)
    picked = [e_iota == idx[k:k + 1, :] for k in range(TOP_K)]
    cnt = sum(jnp.where(p, 1.0, 0.0) for p in picked).astype(BF16)
    r0 = lax.broadcasted_iota(jnp.int32, (dt, dt), 0)
    r1 = lax.broadcasted_iota(jnp.int32, (dt, dt), 1)
    earlier = jnp.where(r0 < r1, 1.0, 0.0).astype(BF16)
    rank = jnp.dot(cnt, earlier, preferred_element_type=F32)
    local_row = rank + off_ref[0].astype(F32)
    global_row = rank + base_ref[0].astype(F32)
    rows = lax.broadcasted_iota(jnp.int32, (DISPATCH_ROWS, dt), 0)
    place = None
    for k in range(TOP_K):
        pos = jnp.sum(jnp.where(picked[k], local_row, 0.0), axis=0, keepdims=True).astype(jnp.int32)
        slot_ref[0, k:k + 1, :] = jnp.sum(jnp.where(picked[k], global_row, 0.0), axis=0,
                                          keepdims=True).astype(jnp.int32)
        hit = rows == pos
        place = hit if place is None else (place | hit)
    placement = jnp.where(place, 1.0, 0.0).astype(BF16)
    buf = lax.rem(t, 2)
    local_ref[buf] = jnp.dot(placement, y_ref[...], preferred_element_type=F32)

    def seg_copy(cls, step, j):
        rows = DISPATCH_COPY_ROWS[cls]
        src_buf = jnp.where(step == t, buf, 1 - buf)
        src_blk = tab_ref[step, (2 * cls) * DISPATCH_LIST + j]
        dst_blk = tab_ref[step, (2 * cls + 1) * DISPATCH_LIST + j]
        src = local_ref.at[src_buf, pl.ds(pl.multiple_of(src_blk * DISPATCH_BLK, DISPATCH_BLK), rows), :]
        dst = xs_ref.at[pl.ds(pl.multiple_of(dst_blk * DISPATCH_BLK, DISPATCH_BLK), rows), :]
        return pltpu.make_async_copy(src, dst, sem)

    def tail_copy(e, j):
        blk = tail_ref[e] + j
        dst = xs_ref.at[pl.ds(pl.multiple_of(blk * DISPATCH_BLK, DISPATCH_BLK), DISPATCH_BLK), :]
        return pltpu.make_async_copy(zero_ref, dst, zero_sem)

    def start_all(copy, n):
        lax.fori_loop(0, n, lambda j, c: (copy(j).start(), c)[1], 0)

    def wait_all(copy, n):
        lax.fori_loop(0, n, lambda j, c: (copy(j).wait(), c)[1], 0)

    classes = range(len(DISPATCH_COPY_ROWS))

    @pl.when(t > 0)
    def _():
        for cls in classes:
            wait_all(functools.partial(seg_copy, cls, t - 1), cnt_ref[t - 1, cls])

    for cls in classes:
        start_all(functools.partial(seg_copy, cls, t), cnt_ref[t, cls])

    @pl.when(t == last_t)
    def _():
        for cls in classes:
            wait_all(functools.partial(seg_copy, cls, t), cnt_ref[t, cls])

    @pl.when(t == 0)
    def _():
        zero_ref[...] = jnp.zeros_like(zero_ref)
        for e in range(N_EXPERTS + 1):
            start_all(functools.partial(tail_copy, e), ntail_ref[e])
        for e in range(N_EXPERTS + 1):
            wait_all(functools.partial(tail_copy, e), ntail_ref[e])


def moe_dispatch(y2, idx_t, copy_cnt, copy_tab, ntail, tail_blk, loc_off, base, p_rows):
    n, d = y2.shape
    nt = n // DISPATCH_T
    grid_spec = pltpu.PrefetchScalarGridSpec(
        num_scalar_prefetch=4,
        grid=(nt,),
        in_specs=[pl.BlockSpec((DISPATCH_T, d), lambda t, *_: (t, 0)),
                  pl.BlockSpec((1, TOP_K, DISPATCH_T), lambda t, *_: (t, 0, 0)),
                  pl.BlockSpec((1, N_EXPERTS, 1), lambda t, *_: (t, 0, 0)),
                  pl.BlockSpec((1, N_EXPERTS, 1), lambda t, *_: (t, 0, 0))],
        out_specs=(pl.BlockSpec(memory_space=pl.ANY),
                   pl.BlockSpec((1, TOP_K, DISPATCH_T), lambda t, *_: (t, 0, 0))),
        scratch_shapes=[pltpu.VMEM((2, DISPATCH_ROWS, d), F32), pltpu.VMEM((DISPATCH_BLK, d), F32),
                        pltpu.SemaphoreType.DMA(()), pltpu.SemaphoreType.DMA(())],
    )
    return pl.pallas_call(
        _moe_dispatch_kernel,
        out_shape=(jax.ShapeDtypeStruct((p_rows, d), F32),
                   jax.ShapeDtypeStruct((nt, TOP_K, DISPATCH_T), jnp.int32)),
        grid_spec=grid_spec,
        compiler_params=_cparams(("arbitrary",)),
        name="moe_dispatch",
    )(copy_cnt, copy_tab, ntail, tail_blk, y2, idx_t, loc_off, base)


def moe_layer(h, y, logits, mods, layer, w_in, b_in, w_out, b_out, ctx_tiles):
    b, lp, d = h.shape
    n = b * lp
    y2 = y.reshape(n, d)
    top_val, top_idx = lax.top_k(logits.reshape(n, N_EXPERTS), TOP_K)
    gates = jax.nn.softmax(top_val, axis=-1)

    assert n % DISPATCH_T == 0
    t_n = n // DISPATCH_T
    blk = DISPATCH_BLK
    experts = jnp.arange(N_EXPERTS, dtype=jnp.int32)
    idx_t = top_idx.astype(jnp.int32).reshape(t_n, DISPATCH_T, TOP_K).transpose(0, 2, 1)
    tile_cnt = jnp.sum((idx_t[..., None] == experts).astype(jnp.int32), axis=(1, 2))
    seg = (tile_cnt + blk - 1) // blk * blk
    loc_end = jnp.cumsum(seg, axis=1)
    loc_off = loc_end - seg
    tot = jnp.sum(seg, axis=0)
    padded = (tot + MOE_TM - 1) // MOE_TM * MOE_TM
    ends = jnp.cumsum(padded)
    starts = ends - padded
    base = starts[None, :] + jnp.cumsum(seg, axis=0) - seg
    q = jnp.arange(DISPATCH_LIST, dtype=jnp.int32)
    covered = jnp.zeros_like(seg)
    tabs, cnts = [], []
    for rows in DISPATCH_COPY_ROWS:
        n_c = (seg - covered) // rows
        c_end = jnp.cumsum(n_c, axis=1)
        sel = jnp.sum((c_end[:, None, :] <= q[None, :, None]).astype(jnp.int32), axis=2)[..., None] == experts
        pick = lambda a: jnp.sum(jnp.where(sel, a[:, None, :], 0), axis=2)
        inside = (q[None, :] - pick(c_end - n_c)) * rows
        tabs += [(pick(loc_off + covered) + inside) // blk, (pick(base + covered) + inside) // blk]
        cnts.append(c_end[:, -1])
        covered = covered + n_c * rows
    copy_tab = jnp.concatenate(tabs, axis=1)
    copy_cnt = jnp.stack(cnts, axis=1)
    n_tiles = -(-(n * TOP_K + t_n * N_EXPERTS * (blk - 1)) // MOE_TM) + N_EXPERTS
    tile_start = jnp.arange(n_tiles, dtype=jnp.int32) * MOE_TM
    tile_expert = jnp.minimum(jnp.sum((tile_start[:, None] >= ends[None, :]).astype(jnp.int32), axis=1),
                              N_EXPERTS - 1)
    tile_valid = (tile_start < ends[-1]).astype(jnp.int32)

    p_rows = n_tiles * MOE_TM
    gap_len = jnp.concatenate([padded - tot, p_rows - ends[-1:]]) // blk
    gap_start = jnp.concatenate([starts + tot, ends[-1:]]) // blk
    x_sorted, slot = moe_dispatch(y2, idx_t, copy_cnt, copy_tab, gap_len, gap_start,
                                  loc_off[..., None], base[..., None], p_rows)
    bg = b_in[layer, :, 0::2].reshape(N_EXPERTS, 1, D_FF)
    bl = b_in[layer, :, 1::2].reshape(N_EXPERTS, 1, D_FF)
    z = moe_grouped_ffn(x_sorted, tile_expert, tile_valid, layer, w_in, bg, bl, w_out,
                        b_out[layer].reshape(N_EXPERTS, 1, d))
    slot_kn = slot.transpose(1, 0, 2).reshape(TOP_K, n)
    z4 = z.at[slot_kn].get(mode="promise_in_bounds").reshape(TOP_K, b, lp, d)
    return moe_combine(z4, gates.reshape(b, lp, TOP_K), h, mods, ctx_tiles)


def _rwkv_feat_kernel(h_ref, hp_ref, hn_ref, g_ref, mods_ref, mu_ref, wr_ref, wk_ref, wv_ref,
                      w1_ref, w2_ref, a1_ref, a2_ref, g1_ref, g2_ref, w0_ref, a0_ref, kk_ref, ka_ref,
                      bd_ref, r_out, v_out, kk_out, g_out, lw_out, kd_out, a_out, *, n_tiles):
    i = pl.program_id(1)
    gain, shift, scale = g_ref[...], _mod_row(mods_ref, 0), _mod_row(mods_ref, 1)
    x = _norm_mod(h_ref[0], gain, shift, scale)
    has_prev = (i >= 2).astype(F32)
    has_next = ((i >= 1) & (i < n_tiles - 1)).astype(F32)
    x_before = _norm_mod(hp_ref[0], gain, shift, scale)[SUBLANES - 1:SUBLANES] * has_prev
    x_after = _norm_mod(hn_ref[0], gain, shift, scale)[0:1] * has_next
    row = lax.broadcasted_iota(jnp.int32, x.shape, 0)
    prev = jnp.where(row == 0, x_before, pltpu.roll(x, 1, 0))
    nxt = jnp.where(row == TILE - 1, x_after, pltpu.roll(x, TILE - 1, 0))
    xx = 0.5 * (prev + nxt) - x
    xr, xw, xk, xv, xa, xg = (x + xx * mu_ref[pl.ds(j, 1), :] for j in range(6))

    def mm(a, w):
        return jnp.dot(a.astype(BF16), w, preferred_element_type=F32)

    r = mm(xr, wr_ref[...])
    k = mm(xk, wk_ref[...])
    v = mm(xv, wv_ref[...])
    g = mm(jax.nn.sigmoid(mm(xg, g1_ref[...])), g2_ref[...])
    kk = k * kk_ref[...]
    kk = kk * lax.rsqrt(mm(kk * kk, bd_ref[...]) + 1e-12)
    r_out[0] = r.astype(BF16)
    v_out[0] = v.astype(BF16)
    kk_out[0] = kk.astype(BF16)
    g_out[0] = g.astype(BF16)
    tw = jnp.tanh(mm(xw, w1_ref[...]))
    av = mm(xa, a1_ref[...])
    lane = lax.broadcasted_iota(jnp.int32, tw.shape, 1)
    half = tw.shape[1] // 2
    for d in range(2):
        in_dir = (lane >= d * half) & (lane < (d + 1) * half)
        wz = w0_ref[pl.ds(d, 1), :] + mm(jnp.where(in_dir, tw, 0.0), w2_ref[...])
        lw_out[d, 0] = -RW_DECAY_SCALE * jax.nn.sigmoid(wz)
        a = jax.nn.sigmoid(a0_ref[pl.ds(d, 1), :] + mm(jnp.where(in_dir, av, 0.0), a2_ref[...]))
        kd_out[d, 0] = (k * (1.0 + (a - 1.0) * ka_ref[...])).astype(BF16)
        a_out[d, 0] = a.astype(BF16)


def rwkv_features(h, gain, mods, mu, w_r, w_k, w_v, w0, w1, w2, a0, a1, a2, g1, g2, k_k, k_a):
    b, lt, d = h.shape
    nt = lt // TILE
    hb = TILE // SUBLANES
    nhb = lt // SUBLANES
    hid = jnp.arange(d) // RW_HEAD_DIM
    bd = (hid[:, None] == hid[None, :]).astype(BF16)
    w1c = jnp.concatenate([w1[0], w1[1]], axis=1).astype(BF16)
    a1c = jnp.concatenate([a1[0], a1[1]], axis=1).astype(BF16)
    w2c = jnp.concatenate([w2[0], w2[1]], axis=0).astype(BF16)
    a2c = jnp.concatenate([a2[0], a2[1]], axis=0).astype(BF16)
    full = lambda shape: pl.BlockSpec(shape, lambda bi, i: (0,) * len(shape))
    tile_spec = pl.BlockSpec((1, TILE, d), lambda bi, i: (bi, i, 0))
    dir_spec = pl.BlockSpec((2, 1, TILE, d), lambda bi, i: (0, bi, i, 0))
    seq_bf = jax.ShapeDtypeStruct((b, lt, d), BF16)
    return pl.pallas_call(
        functools.partial(_rwkv_feat_kernel, n_tiles=nt),
        out_shape=(seq_bf, seq_bf, seq_bf, seq_bf,
                   jax.ShapeDtypeStruct((2, b, lt, d), F32),
                   jax.ShapeDtypeStruct((2, b, lt, d), BF16),
                   jax.ShapeDtypeStruct((2, b, lt, d), BF16)),
        grid=(b, nt),
        in_specs=[tile_spec,
                  pl.BlockSpec((1, SUBLANES, d), lambda bi, i: (bi, jnp.maximum(i * hb - 1, 0), 0)),
                  pl.BlockSpec((1, SUBLANES, d), lambda bi, i: (bi, jnp.minimum((i + 1) * hb, nhb - 1), 0)),
                  full((1, d)),
                  pl.BlockSpec((1, 1, 6, d), lambda bi, i: (bi, jnp.minimum(i, 1), 0, 0)),
                  full((6, d)), full((d, d)), full((d, d)), full((d, d)),
                  full(w1c.shape), full(w2c.shape), full(a1c.shape), full(a2c.shape),
                  full(g1.shape), full(g2.shape), full((2, d)), full((2, d)), full((1, d)), full((1, d)),
                  full((d, d))],
        out_specs=(tile_spec, tile_spec, tile_spec, tile_spec, dir_spec, dir_spec, dir_spec),
        compiler_params=_cparams(("arbitrary", "arbitrary")),
        name="rwkv_features",
    )(h, h, h, gain.reshape(1, d), mods, mu, w_r.astype(BF16), w_k.astype(BF16), w_v.astype(BF16),
      w1c, w2c, a1c, a2c, g1.astype(BF16), g2.astype(BF16), w0, a0, k_k.reshape(1, d), k_a.reshape(1, d), bd)


SCAN_C = 64
SCAN_SUB = 2
SCAN_G = 4
SCAN_GW = SCAN_G * RW_HEAD_DIM


def _scan_chunk_inputs(reverse, off, r_ref, v_ref, kk_ref, lw_ref, kd_ref, a_ref):
    cc = SCAN_C
    rows = slice(off, off + cc)
    ti = lax.broadcasted_iota(jnp.int32, (cc, cc), 0)
    si = lax.broadcasted_iota(jnp.int32, (cc, cc), 1)
    before_incl = (si >= ti) if reverse else (si <= ti)
    lw = lw_ref[0, 0, rows, :]
    tri = jnp.where(before_incl, 1.0, 0.0).astype(BF16)
    lw_hi = lw.astype(BF16)
    lw_lo = (lw - lw_hi.astype(F32)).astype(BF16)
    cl = (jnp.dot(tri, lw_hi, preferred_element_type=F32) + jnp.dot(tri, lw_lo, preferred_element_type=F32))
    g_in = jnp.exp(cl)
    g_ex = jnp.exp(cl - lw)
    g_inv = jnp.exp(-cl)
    g_end = jnp.exp(jnp.sum(lw, axis=0, keepdims=True))
    kk = kk_ref[0, rows, :].astype(F32)
    rt = (r_ref[0, rows, :].astype(F32) * g_in).astype(BF16)
    at = (-kk * g_ex).astype(BF16)
    bt = (kk * a_ref[0, 0, rows, :].astype(F32) * g_inv).astype(BF16)
    kt = (kd_ref[0, 0, rows, :].astype(F32) * g_inv).astype(BF16)
    return rt, at, bt, kt, v_ref[0, rows, :], g_end


def _rwkv_scan_kernel(rf_ref, vf_ref, kkf_ref, rb_ref, vb_ref, kkb_ref, lwf_ref, kdf_ref, af_ref,
                      lwb_ref, kdb_ref, ab_ref, yf_ref, yb_ref, h_ref):
    @pl.when(pl.program_id(1) == 0)
    def _():
        h_ref[...] = jnp.zeros_like(h_ref)

    cc, gw = SCAN_C, SCAN_GW
    y_refs = (yf_ref, yb_ref)

    tg = lax.broadcasted_iota(jnp.int32, (cc, gw), 0)
    sg = lax.broadcasted_iota(jnp.int32, (cc, gw), 1) % cc
    m_strict = (sg < tg, sg > tg)
    m_incl = (sg <= tg, sg >= tg)
    eye_g = (sg == tg).astype(F32)
    bi = lax.broadcasted_iota(jnp.int32, (gw, gw), 0) // cc
    bj = lax.broadcasted_iota(jnp.int32, (gw, gw), 1) // cc
    blk = bi == bj

    def bdiag(x):
        return jnp.where(blk, jnp.concatenate([x] * SCAN_G, axis=0), jnp.zeros((), x.dtype))

    def mm(a, b):
        return jnp.dot(a, b, preferred_element_type=F32)

    def mm_nt(a, b):
        return lax.dot_general(a, b, (((1,), (1,)), ((), ())), preferred_element_type=F32)

    n_g = RW_HEADS // SCAN_G
    chains = [(d, g) for d in range(2) for g in range(n_g)]
    cols = [slice(g * gw, (g + 1) * gw) for _, g in chains]
    idx = range(len(chains))

    def state_free_part(sub):
        offs = (sub * cc, (SCAN_SUB - 1 - sub) * cc)
        ops = (_scan_chunk_inputs(False, offs[0], rf_ref, vf_ref, kkf_ref, lwf_ref, kdf_ref, af_ref),
               _scan_chunk_inputs(True, offs[1], rb_ref, vb_ref, kkb_ref, lwb_ref, kdb_ref, ab_ref))
        rt, at, bt, kt, v, g_end = ([ops[d][j][:, cols[i]] for i, (d, _) in enumerate(chains)] for j in range(6))
        v_bd = [bdiag(x) for x in v]
        mats = [mm_nt(jnp.concatenate([at[i], rt[i]], axis=0),
                      jnp.concatenate([bdiag(bt[i]), bdiag(kt[i])], axis=0)) for i in idx]
        a_ab = [jnp.where(m_strict[chains[i][0]], mats[i][:cc, :gw], 0.0) for i in idx]
        a_ak = [jnp.where(m_strict[chains[i][0]], mats[i][:cc, gw:], 0.0).astype(BF16) for i in idx]
        m_rb = [jnp.where(m_incl[chains[i][0]], mats[i][cc:, :gw], 0.0).astype(BF16) for i in idx]
        m_rk = [jnp.where(m_incl[chains[i][0]], mats[i][cc:, gw:], 0.0).astype(BF16) for i in idx]
        from_v = [mm(jnp.concatenate([a_ak[i], m_rk[i]], axis=0), v_bd[i]) for i in idx]
        pw = [x.astype(BF16) for x in a_ab]
        s_acc = [eye_g + x for x in a_ab]
        pw = [mm(x, bdiag(x)).astype(BF16) for x in pw]
        n = 2
        while n < cc:
            last = 2 * n >= cc
            lhs = [s_acc[i].astype(BF16) if last else jnp.concatenate([pw[i], s_acc[i].astype(BF16)], axis=0)
                   for i in idx]
            prod = [mm(lhs[i], bdiag(pw[i])) for i in idx]
            s_acc = [s_acc[i] + (prod[i] if last else prod[i][cc:]) for i in idx]
            if not last:
                pw = [prod[i][:cc].astype(BF16) for i in idx]
            n *= 2
        ar = [jnp.concatenate([at[i], rt[i]], axis=0) for i in idx]
        bk = [jnp.concatenate([bt[i], kt[i]], axis=0) for i in idx]
        return offs, ar, bk, v, g_end, m_rb, from_v, [x.astype(BF16) for x in s_acc]

    parts = [state_free_part(sub) for sub in range(SCAN_SUB)]
    h_cur = [h_ref[i] for i in idx]
    for offs, ar, bk, v, g_end, m_rb, from_v, inv in parts:
        h0b = [x.astype(BF16) for x in h_cur]
        from_h = [mm(ar[i], h0b[i]) for i in idx]
        pq = [from_h[i][:cc] + from_v[i][:cc] for i in idx]
        ub = [mm(inv[i], bdiag(pq[i].astype(BF16))).astype(BF16) for i in idx]
        for i in idx:
            d = chains[i][0]
            y_refs[d][0, offs[d]:offs[d] + cc, cols[i]] = (from_h[i][cc:] + from_v[i][cc:]
                                                          + mm(m_rb[i], bdiag(ub[i])))
        nxt = []
        for i in idx:
            upd = lax.dot_general(bk[i], jnp.concatenate([ub[i], v[i]], axis=0),
                                  (((0,), (0,)), ((), ())), preferred_element_type=F32)
            ge = jnp.transpose(jnp.broadcast_to(g_end[i], (gw, gw)))
            nxt.append(ge * (h_cur[i] + jnp.where(blk, upd, 0.0)))
        h_cur = nxt
    for i in idx:
        h_ref[i] = h_cur[i]


def rwkv_scan(r, v, kk, lw, kd, a):
    b, lt, d = r.shape
    blk_rows = SCAN_SUB * SCAN_C
    nc = lt // blk_rows
    n_ctx = CTX_LEN // blk_rows
    assert SCAN_C == RW_HEAD_DIM and CTX_LEN % blk_rows == 0 and lt % blk_rows == 0

    def rev_chunk(c):
        return jnp.where(c < n_ctx, n_ctx - 1 - c, nc - 1 + n_ctx - c)

    fwd = pl.BlockSpec((1, blk_rows, d), lambda bi, c: (bi, c, 0))
    bwd = pl.BlockSpec((1, blk_rows, d), lambda bi, c: (bi, rev_chunk(c), 0))
    fwd_dir = pl.BlockSpec((1, 1, blk_rows, d), lambda bi, c: (0, bi, c, 0))
    bwd_dir = pl.BlockSpec((1, 1, blk_rows, d), lambda bi, c: (1, bi, rev_chunk(c), 0))
    y_shape = jax.ShapeDtypeStruct((b, lt, d), F32)
    return pl.pallas_call(
        _rwkv_scan_kernel,
        out_shape=(y_shape, y_shape),
        grid=(b, nc),
        in_specs=[fwd, fwd, fwd, bwd, bwd, bwd, fwd_dir, fwd_dir, fwd_dir, bwd_dir, bwd_dir, bwd_dir],
        out_specs=(fwd, bwd),
        scratch_shapes=[pltpu.VMEM((2 * RW_HEADS // SCAN_G, SCAN_GW, SCAN_GW), F32)],
        compiler_params=_cparams(("arbitrary", "arbitrary")),
        name="rwkv_scan",
    )(r, v, kk, r, v, kk, lw, kd, a, lw, kd, a)


def _rwkv_out_kernel(yf_ref, yb_ref, r_ref, v_ref, g_ref, kd_ref, h_ref, mods_ref, lnw_ref, lnb_ref, rk_ref,
                     bd_ref, wo_ref, gffn_ref, rwh_ref, rwl_ref, rb_ref, o_ref, y_ref, lg_ref):
    def head_sum(x):
        return jnp.dot(x.astype(BF16), bd_ref[...], preferred_element_type=F32)

    y = yf_ref[0] + yb_ref[0]
    inv_n = 1.0 / RW_HEAD_DIM
    mean = head_sum(y) * inv_n
    yc = y - mean
    var = head_sum(yc * yc) * inv_n
    yn = yc * lax.rsqrt(var + RW_GN_EPS) * lnw_ref[...] + lnb_ref[...]
    r = r_ref[0].astype(F32)
    kd = kd_ref[0, 0].astype(F32) + kd_ref[1, 0].astype(F32)
    yn = yn + head_sum(r * kd * rk_ref[...]) * v_ref[0].astype(F32)
    out = jnp.dot((yn * g_ref[0].astype(F32)).astype(BF16), wo_ref[...], preferred_element_type=F32)
    h_new = h_ref[0] + _mod_row(mods_ref, 2) * out
    o_ref[0] = h_new
    _router_tail(h_new, gffn_ref, mods_ref, rwh_ref, rwl_ref, rb_ref, y_ref, lg_ref)


def rwkv_readout(y_f, y_b, r, v, g, kd, h, mods, ln_w, ln_b, r_k, w_o, ffn_gain, router_w, router_b):
    b, lt, d = h.shape
    nt = lt // TILE - 1
    e = router_w.shape[1]
    r_ops, r_specs = _router_operands(ffn_gain, router_w, router_b)
    out_tile = lambda width: pl.BlockSpec((1, TILE, width), lambda bi, i: (bi, i, 0))
    hid = jnp.arange(d) // RW_HEAD_DIM
    bd = (hid[:, None] == hid[None, :]).astype(BF16)
    full = lambda shape: pl.BlockSpec(shape, lambda bi, i: (0,) * len(shape))
    tile_spec = pl.BlockSpec((1, TILE, d), lambda bi, i: (bi, i + 1, 0))
    dir_spec = pl.BlockSpec((2, 1, TILE, d), lambda bi, i: (0, bi, i + 1, 0))
    return pl.pallas_call(
        _rwkv_out_kernel,
        out_shape=(jax.ShapeDtypeStruct((b, nt * TILE, d), F32), jax.ShapeDtypeStruct((b, nt * TILE, d), BF16),
                   jax.ShapeDtypeStruct((b, nt * TILE, e), F32)),
        grid=(b, nt),
        in_specs=[tile_spec, tile_spec, tile_spec, tile_spec, tile_spec, dir_spec, tile_spec,
                  pl.BlockSpec((1, 1, 6, d), lambda bi, i: (bi, 1, 0, 0)),
                  full((1, d)), full((1, d)), full((1, d)), full((d, d)), full((d, d))] + r_specs,
        out_specs=(out_tile(d), out_tile(d), out_tile(e)),
        compiler_params=_cparams(("arbitrary", "arbitrary")),
        name="rwkv_readout",
    )(y_f, y_b, r, v, g, kd, h, mods, ln_w.reshape(1, d), ln_b.reshape(1, d), r_k.reshape(1, d), bd,
      w_o.astype(BF16), *r_ops)


def kernel(x, c, ctx, c_ctx, ada_w, ada_b, norm_mix_g, norm_ffn_g, router_w, router_b, exp_w_in, exp_b_in,
           exp_w_out, exp_b_out, ab_w_in, na_q_g, na_k_g, na_rpb, pool_w, pool_scale, ab_w_out,
           rw_mu, rw_w_r, rw_w_k, rw_w_v, rw_w_o, rw_w0, rw_w1, rw_w2, rw_a0, rw_a1, rw_a2,
           rw_g1, rw_g2, rw_k_k, rw_k_a, rw_r_k, rw_ln_w, rw_ln_b):
    b, seq, d = x.shape
    depth = ada_w.shape[0]
    assert ctx.shape[1] == CTX_LEN == TILE and seq % TILE == 0 and d == D_MODEL
    assert depth == 2, "layer schedule below is written for [neighbourhood/pool layer, RWKV layer]"
    rows = seq // GRID_W
    assert rows >= NA_QROWS * NA_KTILES and NA_QROWS * NA_KTILES >= NA_QROWS + NA_KH - 1

    n_c = 1 + b
    n_c_pad = -(-n_c // SUBLANES) * SUBLANES
    cvec = jnp.concatenate([c_ctx[None], c, jnp.zeros((n_c_pad - n_c, d), F32)], axis=0)

    def layer_mods(layer):
        m = ada_mod(cvec, ada_w[layer], ada_b[layer])
        m_ctx = jnp.broadcast_to(m[0].reshape(1, 6, d), (b, 6, d))
        return jnp.stack([m_ctx, m[1:n_c].reshape(b, 6, d)], axis=1)

    def moe(h, y, logits, layer, mods, ctx_tiles):
        return moe_layer(h, y, logits, mods, layer, exp_w_in, exp_b_in, exp_w_out, exp_b_out, ctx_tiles)

    h = jnp.concatenate([ctx, x], axis=1)

    mods = layer_mods(0)
    qk, v, u = ab_proj(h, norm_mix_g[0], mods, ab_w_in[0], na_q_g[0], na_k_g[0])
    o_na = na_attention(qk, v, _na_bias_tables(na_rpb[0], rows))
    o_pool = multiscale_pool(u, pool_w[0], pool_scale[0])
    h, y, logits = out_proj_residual(o_na, o_pool, h, mods, ab_w_out[0], norm_ffn_g[0], router_w[0], router_b[0])
    h = moe(h, y, logits, 0, mods, 1)

    mods = layer_mods(1)
    r, v, kk, g, lw, kd, a = rwkv_features(h, norm_mix_g[1], mods, rw_mu[0], rw_w_r[0], rw_w_k[0], rw_w_v[0],
                                           rw_w0[0], rw_w1[0], rw_w2[0], rw_a0[0], rw_a1[0], rw_a2[0],
                                           rw_g1[0], rw_g2[0], rw_k_k[0], rw_k_a[0])
    y_f, y_b = rwkv_scan(r, v, kk, lw, kd, a)
    h_lat, y, logits = rwkv_readout(y_f, y_b, r, v, g, kd, h, mods, rw_ln_w[0], rw_ln_b[0], rw_r_k[0], rw_w_o[0],
                                    norm_ffn_g[1], router_w[1], router_b[1])
    return moe(h_lat, y, logits, 1, mods, 0)
```

```python
import functools

import jax
import jax.numpy as jnp
import numpy as np
from jax import lax
from jax.experimental import pallas as pl
from jax.experimental.pallas import tpu as pltpu

F32 = jnp.float32
BF16 = jnp.bfloat16

D_MODEL = 1024
GRID_W = 64
CTX_LEN = 256
RMS_EPS = 1e-6
NA_HEADS = 8
NA_HEAD_DIM = 64
NA_WIDTH = NA_HEADS * NA_HEAD_DIM
NA_KH = 8
NA_KW = 16
POOL_WINDOWS = (2, 4, 8, 16)
POOL_GC = 128
POOL_WIDTH = 512
RW_HEAD_DIM = 64
RW_HEADS = D_MODEL // RW_HEAD_DIM
RW_GN_EPS = 64e-5
RW_DECAY_SCALE = float(np.exp(-0.5))
N_EXPERTS = 32
TOP_K = 4
D_FF = D_MODEL
SWIGLU_LIMIT = 7.0
SWIGLU_ALPHA = 1.702

TILE = 256
LANES = 128
SUBLANES = 8
MOE_TM = 512
NEG_BIG = -1e30
VMEM_LIMIT = 48 * 1024 * 1024
MOE_FFN_VMEM_LIMIT = 56 * 1024 * 1024


def _cparams(sem):
    return pltpu.CompilerParams(dimension_semantics=sem, vmem_limit_bytes=VMEM_LIMIT)


def _mod_row(mods_ref, k):
    return mods_ref[0, 0, pl.ds(k, 1), :]


def _norm_mod(x, gain, shift, scale):
    ms = jnp.mean(x * x, axis=-1, keepdims=True)
    return (x * lax.rsqrt(ms + RMS_EPS) * gain) * (1.0 + scale) + shift


def _ada_kernel(c_ref, w_ref, b_ref, o_ref):
    c = c_ref[...]
    s = c * jax.nn.sigmoid(c)
    o_ref[...] = jnp.dot(s, w_ref[...], preferred_element_type=F32,
                         precision=lax.Precision.HIGHEST) + b_ref[...]


def ada_mod(cvec, w, b):
    r, d = cvec.shape
    n = w.shape[1]
    tn = 1536
    return pl.pallas_call(
        _ada_kernel,
        out_shape=jax.ShapeDtypeStruct((r, n), F32),
        grid=(n // tn,),
        in_specs=[pl.BlockSpec((r, d), lambda j: (0, 0)),
                  pl.BlockSpec((d, tn), lambda j: (0, j)),
                  pl.BlockSpec((1, tn), lambda j: (0, j))],
        out_specs=pl.BlockSpec((r, tn), lambda j: (0, j)),
        compiler_params=_cparams(("arbitrary",)),
        name="ada_mod",
    )(cvec, w, b.reshape(1, n))


def _ab_proj_kernel(h_ref, g_ref, mods_ref, w_ref, bd_ref, qkg_ref, qk_ref, v_ref, u_ref):
    x = _norm_mod(h_ref[0], g_ref[...], _mod_row(mods_ref, 0), _mod_row(mods_ref, 1))
    p = jnp.dot(x.astype(BF16), w_ref[...], preferred_element_type=F32)
    qk = p[:, :2 * NA_WIDTH]
    ss = jnp.dot((qk * qk).astype(BF16), bd_ref[...], preferred_element_type=F32)
    qk_ref[0] = (qk * lax.rsqrt(ss + RMS_EPS) * qkg_ref[...]).astype(BF16)
    v_ref[0] = p[:, 2 * NA_WIDTH:3 * NA_WIDTH].astype(BF16)
    u_ref[0] = p[:, 3 * NA_WIDTH:]


def ab_proj(h, gain, mods, w_in, q_g, k_g):
    b, lt, d = h.shape
    nt = lt // TILE
    n_in = w_in.shape[1]
    hid = jnp.arange(2 * NA_WIDTH) // NA_HEAD_DIM
    bd = jnp.where(hid[:, None] == hid[None, :], 1.0 / NA_HEAD_DIM, 0.0).astype(BF16)
    qkg = jnp.concatenate([jnp.tile(q_g, NA_HEADS), jnp.tile(k_g, NA_HEADS)]).reshape(1, -1).astype(F32)
    return pl.pallas_call(
        _ab_proj_kernel,
        out_shape=(jax.ShapeDtypeStruct((b, lt, 2 * NA_WIDTH), BF16),
                   jax.ShapeDtypeStruct((b, lt, NA_WIDTH), BF16),
                   jax.ShapeDtypeStruct((b, lt, POOL_WIDTH), F32)),
        grid=(b, nt),
        in_specs=[pl.BlockSpec((1, TILE, d), lambda bi, i: (bi, i, 0)),
                  pl.BlockSpec((1, d), lambda bi, i: (0, 0)),
                  pl.BlockSpec((1, 1, 6, d), lambda bi, i: (bi, jnp.minimum(i, 1), 0, 0)),
                  pl.BlockSpec((d, n_in), lambda bi, i: (0, 0)),
                  pl.BlockSpec((2 * NA_WIDTH, 2 * NA_WIDTH), lambda bi, i: (0, 0)),
                  pl.BlockSpec((1, 2 * NA_WIDTH), lambda bi, i: (0, 0))],
        out_specs=(pl.BlockSpec((1, TILE, 2 * NA_WIDTH), lambda bi, i: (bi, i, 0)),
                   pl.BlockSpec((1, TILE, NA_WIDTH), lambda bi, i: (bi, i, 0)),
                   pl.BlockSpec((1, TILE, POOL_WIDTH), lambda bi, i: (bi, i, 0))),
        compiler_params=_cparams(("arbitrary", "arbitrary")),
        name="ab_proj",
    )(h, gain.reshape(1, d), mods, w_in.astype(BF16), bd, qkg)


NA_QROWS = TILE // GRID_W
NA_KTILES = 3
NA_NLOC = NA_KTILES * TILE
NA_NKEY = NA_NLOC + CTX_LEN


def _na_bias_tables(rpb, rows):
    n_r, n_c = 2 * NA_KH - 1, 2 * NA_KW - 1
    col = np.arange(GRID_W)
    c_start = np.clip(col - NA_KW // 2, 0, GRID_W - NA_KW)
    valid_c = (col[None, :] >= c_start[:, None]) & (col[None, :] < c_start[:, None] + NA_KW)
    c_rel = np.clip(col[None, :] - col[:, None] + NA_KW - 1, 0, n_c - 1)
    sel_c = (c_rel.reshape(-1)[None, :] == np.arange(n_c)[:, None]).astype(np.float32)
    by_col = jnp.einsum("hrc,cn->hrn", rpb.astype(F32), sel_c, precision=lax.Precision.HIGHEST)
    n_qt = rows // NA_QROWS
    qr, kr = np.arange(NA_QROWS), np.arange(NA_KTILES * NA_QROWS)
    tabs = []
    for jq in (0, 1, n_qt - 1):
        w0 = NA_QROWS * min(max(jq - 1, 0), n_qt - NA_KTILES)
        q_row = NA_QROWS * jq + qr
        key_row = w0 + kr
        r_start = np.clip(q_row - NA_KH // 2, 0, rows - NA_KH)
        valid_r = (key_row[None, :] >= r_start[:, None]) & (key_row[None, :] < r_start[:, None] + NA_KH)
        r_rel = np.clip(key_row[None, :] - q_row[:, None] + NA_KH - 1, 0, n_r - 1)
        sel_r = (r_rel.reshape(-1)[:, None] == np.arange(n_r)[None, :]).astype(np.float32)
        t = jnp.einsum("xr,hrn->hxn", sel_r, by_col, precision=lax.Precision.HIGHEST)
        t = t.reshape(NA_HEADS, NA_QROWS, NA_KTILES * NA_QROWS, GRID_W, GRID_W)
        t = t.transpose(0, 1, 3, 2, 4).reshape(NA_HEADS, TILE, NA_NLOC)
        valid = (valid_r[:, None, :, None] & valid_c[None, :, None, :]).reshape(TILE, NA_NLOC)
        tabs.append(jnp.where(valid[None], t, NEG_BIG))
    tabs = [jnp.full_like(tabs[0], NEG_BIG)] + tabs
    loc = jnp.stack(tabs)
    return jnp.concatenate([loc, jnp.zeros(loc.shape[:3] + (CTX_LEN,), F32)], axis=-1)


def _na_kernel(q_ref, k0_ref, k1_ref, k2_ref, kc_ref, v0_ref, v1_ref, v2_ref, vc_ref, bias_ref, o_ref):
    q = q_ref[0]
    kall = jnp.concatenate([k0_ref[0], k1_ref[0], k2_ref[0], kc_ref[0]], axis=0)
    vall = jnp.concatenate([v0_ref[0], v1_ref[0], v2_ref[0], vc_ref[0]], axis=0)
    lane = lax.broadcasted_iota(jnp.int32, (TILE, LANES), 1)
    scale = jnp.asarray(NA_HEAD_DIM ** -0.5, BF16)
    heads = range(NA_HEADS)
    pair = [slice((h // 2) * LANES, (h // 2 + 1) * LANES) for h in heads]
    scores = []
    for h in heads:
        lo = (h % 2) * NA_HEAD_DIM
        qh = jnp.where((lane >= lo) & (lane < lo + NA_HEAD_DIM), q[:, pair[h]], jnp.zeros((), BF16)) * scale
        s = lax.dot_general(qh, kall[:, pair[h]], (((1,), (1,)), ((), ())), preferred_element_type=F32)
        scores.append(s + bias_ref[0, h])
    probs, denom = [], []
    for s in scores:
        p = jnp.exp(s - jnp.max(s, axis=-1, keepdims=True))
        probs.append(p.astype(BF16))
        denom.append(jnp.sum(p, axis=-1, keepdims=True))
    outs = [jnp.dot(probs[h], vall[:, pair[h]], preferred_element_type=F32) / denom[h] for h in heads]
    for h in range(0, NA_HEADS, 2):
        o_ref[0, :, pair[h]] = jnp.where(lane < NA_HEAD_DIM, outs[h], outs[h + 1]).astype(BF16)


def na_attention(qk, v, bias):
    b, lt, _ = qk.shape
    nt = lt // TILE
    n_qt = nt - 1
    assert NA_HEAD_DIM ** -0.5 == 0.125 and NA_HEADS % 2 == 0
    w = NA_WIDTH

    def w0(j):
        return 1 + jnp.clip(j - 2, 0, n_qt - NA_KTILES)

    def kind(j):
        return jnp.where(j == 0, 0, jnp.where(j == 1, 1, jnp.where(j == nt - 1, 3, 2)))

    qspec = pl.BlockSpec((1, TILE, w), lambda j, bi: (bi, j, 0))
    kspecs = [pl.BlockSpec((1, TILE, w), functools.partial(lambda j, bi, t: (bi, w0(j) + t, 1), t=t))
              for t in range(NA_KTILES)]
    kcspec = pl.BlockSpec((1, TILE, w), lambda j, bi: (bi, 0, 1))
    vspecs = [pl.BlockSpec((1, TILE, w), functools.partial(lambda j, bi, t: (bi, w0(j) + t, 0), t=t))
              for t in range(NA_KTILES)]
    vcspec = pl.BlockSpec((1, TILE, w), lambda j, bi: (bi, 0, 0))
    bspec = pl.BlockSpec((1, NA_HEADS, TILE, NA_NKEY), lambda j, bi: (kind(j), 0, 0, 0))
    return pl.pallas_call(
        _na_kernel,
        out_shape=jax.ShapeDtypeStruct((b, lt, w), BF16),
        grid=(nt, b),
        in_specs=[qspec] + kspecs + [kcspec] + vspecs + [vcspec, bspec],
        out_specs=pl.BlockSpec((1, TILE, w), lambda j, bi: (bi, j, 0)),
        compiler_params=_cparams(("arbitrary", "arbitrary")),
        name="na_attention",
    )(qk, qk, qk, qk, qk, v, v, v, v, bias)


POOL_HALO = 8
POOL_ROWS = TILE + 2 * POOL_HALO


def _pool_kernel(cur_ref, prev_ref, next_ref, pw_ref, ps_ref, o_ref, xp_ref, *, n_tiles):
    i = pl.program_id(1)
    has_prev = (i >= 2).astype(F32)
    has_next = ((i >= 1) & (i < n_tiles - 1)).astype(F32)
    xp_ref[0:POOL_HALO, :] = prev_ref[0] * has_prev
    xp_ref[POOL_HALO:POOL_HALO + TILE, :] = cur_ref[0]
    xp_ref[POOL_HALO + TILE:POOL_ROWS, :] = next_ref[0] * has_next
    row = lax.broadcasted_iota(jnp.int32, (TILE, POOL_GC), 0)
    tpos = jnp.where(i == 0, row, row + (i - 1) * TILE)
    seq_len = jnp.where(i == 0, CTX_LEN, (n_tiles - 1) * TILE)
    for g, w in enumerate(POOL_WINDOWS):
        cols = slice(g * POOL_GC, (g + 1) * POOL_GC)
        x = xp_ref[:, cols]
        s = x + pltpu.roll(x, 1, 0)
        half = 1
        while 2 * half < w:
            s = pltpu.roll(s, half, 0) + pltpu.roll(s, POOL_ROWS - half, 0)
            half *= 2
        s = s[POOL_HALO:POOL_HALO + TILE]
        cnt = jnp.minimum(w // 2, tpos) + jnp.minimum(w // 2, seq_len - tpos)
        pooled = s / cnt.astype(F32) - cur_ref[0, :, cols]
        y = jnp.dot(pooled.astype(BF16), pw_ref[g], preferred_element_type=F32)
        o_ref[0, :, cols] = (y * ps_ref[:, cols]).astype(BF16)


def multiscale_pool(u, pool_w, pool_scale):
    b, lt, c = u.shape
    nt = lt // TILE
    hb = TILE // POOL_HALO
    nhb = lt // POOL_HALO
    assert POOL_HALO >= max(POOL_WINDOWS) // 2 and POOL_HALO == SUBLANES
    return pl.pallas_call(
        functools.partial(_pool_kernel, n_tiles=nt),
        out_shape=jax.ShapeDtypeStruct((b, lt, c), BF16),
        grid=(b, nt),
        in_specs=[pl.BlockSpec((1, TILE, c), lambda bi, i: (bi, i, 0)),
                  pl.BlockSpec((1, POOL_HALO, c), lambda bi, i: (bi, jnp.maximum(i * hb - 1, 0), 0)),
                  pl.BlockSpec((1, POOL_HALO, c), lambda bi, i: (bi, jnp.minimum((i + 1) * hb, nhb - 1), 0)),
                  pl.BlockSpec((len(POOL_WINDOWS), POOL_GC, POOL_GC), lambda bi, i: (0, 0, 0)),
                  pl.BlockSpec((1, c), lambda bi, i: (0, 0))],
        out_specs=pl.BlockSpec((1, TILE, c), lambda bi, i: (bi, i, 0)),
        scratch_shapes=[pltpu.VMEM((POOL_ROWS, c), F32)],
        compiler_params=_cparams(("arbitrary", "arbitrary")),
        name="multiscale_pool",
    )(u, u, u, pool_w.astype(BF16), pool_scale.reshape(1, c).astype(F32))


def _router_tail(h_new, g_ref, mods_ref, rwh_ref, rwl_ref, rb_ref, y_ref, lg_ref):
    x = _norm_mod(h_new, g_ref[...], _mod_row(mods_ref, 3), _mod_row(mods_ref, 4))
    x_hi = x.astype(BF16)
    y_ref[0] = x_hi
    x_lo = (x - x_hi.astype(F32)).astype(BF16)
    w_hi, w_lo = rwh_ref[...], rwl_ref[...]
    lg_ref[0] = (jnp.dot(x_hi, w_hi, preferred_element_type=F32)
                 + jnp.dot(x_lo, w_hi, preferred_element_type=F32)
                 + jnp.dot(x_hi, w_lo, preferred_element_type=F32)
                 + jnp.dot(x_lo, w_lo, preferred_element_type=F32)) + rb_ref[...]


def _router_operands(gain, router_w, router_b):
    d, e = router_w.shape
    head = lax.bitcast_convert_type(lax.bitcast_convert_type(router_w, jnp.uint32) & jnp.uint32(0xFFFF0000), F32)
    full = lambda shape: pl.BlockSpec(shape, lambda bi, i: (0,) * len(shape))
    return ((gain.reshape(1, d), head.astype(BF16), (router_w - head).astype(BF16), router_b.reshape(1, e)),
            [full((1, d)), full((d, e)), full((d, e)), full((1, e))])


def _out_proj_kernel(a_ref, b_ref, h_ref, mods_ref, w_ref, g_ref, rwh_ref, rwl_ref, rb_ref,
                     o_ref, y_ref, lg_ref):
    ka = a_ref.shape[-1]
    y = jnp.dot(a_ref[0], w_ref[:ka, :], preferred_element_type=F32)
    y = y + jnp.dot(b_ref[0], w_ref[ka:, :], preferred_element_type=F32)
    h_new = h_ref[0] + _mod_row(mods_ref, 2) * y
    o_ref[0] = h_new
    _router_tail(h_new, g_ref, mods_ref, rwh_ref, rwl_ref, rb_ref, y_ref, lg_ref)


def out_proj_residual(a, bb, h, mods, w, ffn_gain, router_w, router_b):
    b, lt, d = h.shape
    nt = lt // TILE
    ka, kb = a.shape[-1], bb.shape[-1]
    e = router_w.shape[1]
    r_ops, r_specs = _router_operands(ffn_gain, router_w, router_b)
    tile = lambda width: pl.BlockSpec((1, TILE, width), lambda bi, i: (bi, i, 0))
    return pl.pallas_call(
        _out_proj_kernel,
        out_shape=(jax.ShapeDtypeStruct((b, lt, d), F32), jax.ShapeDtypeStruct((b, lt, d), BF16),
                   jax.ShapeDtypeStruct((b, lt, e), F32)),
        grid=(b, nt),
        in_specs=[tile(ka), tile(kb), tile(d),
                  pl.BlockSpec((1, 1, 6, d), lambda bi, i: (bi, jnp.minimum(i, 1), 0, 0)),
                  pl.BlockSpec((ka + kb, d), lambda bi, i: (0, 0))] + r_specs,
        out_specs=(tile(d), tile(d), tile(e)),
        compiler_params=_cparams(("arbitrary", "arbitrary")),
        name="out_proj_residual",
    )(a, bb, h, mods, w.astype(BF16), *r_ops)


MOE_PERM_W = 256


def _moe_ffn_kernel(te_ref, tv_ref, x_ref, win_ref, bg_ref, bl_ref, wo_ref, bo_ref, perm_ref, o_ref,
                    wg_s, wl_s, wo_s):
    t = pl.program_id(0)
    valid = tv_ref[t] > 0
    new_expert = (t == 0) | (te_ref[t] != te_ref[jnp.maximum(t - 1, 0)])

    @pl.when(valid & new_expert)
    def _():
        half = MOE_PERM_W // 2
        for cb in range(win_ref.shape[3] // MOE_PERM_W):
            w = win_ref[0, 0, :, cb * MOE_PERM_W:(cb + 1) * MOE_PERM_W].astype(BF16)
            wp = jnp.dot(w, perm_ref[...], preferred_element_type=F32).astype(BF16)
            wg_s[:, cb * half:(cb + 1) * half] = wp[:, :half]
            wl_s[:, cb * half:(cb + 1) * half] = wp[:, half:]
        wo_s[...] = wo_ref[0, 0].astype(BF16)

    @pl.when(jnp.logical_not(valid))
    def _():
        o_ref[...] = jnp.zeros_like(o_ref)

    @pl.when(valid)
    def _():
        x = x_ref[...].astype(BF16)
        hg = jnp.dot(x, wg_s[...], preferred_element_type=F32) + bg_ref[0]
        hl = jnp.dot(x, wl_s[...], preferred_element_type=F32) + bl_ref[0]
        glu = jnp.minimum(hg, SWIGLU_LIMIT)
        lin = jnp.clip(hl, -SWIGLU_LIMIT, SWIGLU_LIMIT)
        act = glu * jax.nn.sigmoid(SWIGLU_ALPHA * glu) * (lin + 1.0)
        o_ref[...] = (jnp.dot(act.astype(BF16), wo_s[...], preferred_element_type=F32)
                      + bo_ref[0]).astype(o_ref.dtype)


def moe_grouped_ffn(x_sorted, tile_expert, tile_valid, layer, w_in, bg, bl, w_out, bo):
    p, d = x_sorted.shape
    _, e, _, f2 = w_in.shape
    f = f2 // 2
    nt = p // MOE_TM
    assert f2 % MOE_PERM_W == 0 and MOE_PERM_W % (2 * LANES) == 0
    src = jnp.arange(MOE_PERM_W)
    dst = jnp.where(src % 2 == 0, src // 2, MOE_PERM_W // 2 + src // 2)
    perm = (dst[:, None] == jnp.arange(MOE_PERM_W)[None, :]).astype(BF16)
    grid_spec = pltpu.PrefetchScalarGridSpec(
        num_scalar_prefetch=2,
        grid=(nt,),
        in_specs=[pl.BlockSpec((MOE_TM, d), lambda t, te, tv: (t * tv[t], 0)),
                  pl.BlockSpec((1, 1, d, f2), lambda t, te, tv: (layer, te[t], 0, 0)),
                  pl.BlockSpec((1, 1, f), lambda t, te, tv: (te[t], 0, 0)),
                  pl.BlockSpec((1, 1, f), lambda t, te, tv: (te[t], 0, 0)),
                  pl.BlockSpec((1, 1, f, d), lambda t, te, tv: (layer, te[t], 0, 0)),
                  pl.BlockSpec((1, 1, d), lambda t, te, tv: (te[t], 0, 0)),
                  pl.BlockSpec((MOE_PERM_W, MOE_PERM_W), lambda t, te, tv: (0, 0))],
        out_specs=pl.BlockSpec((MOE_TM, d), lambda t, te, tv: (t, 0)),
        scratch_shapes=[pltpu.VMEM((d, f), BF16), pltpu.VMEM((d, f), BF16), pltpu.VMEM((f, d), BF16)],
    )
    return pl.pallas_call(
        _moe_ffn_kernel,
        out_shape=jax.ShapeDtypeStruct((p, d), BF16),
        grid_spec=grid_spec,
        compiler_params=pltpu.CompilerParams(dimension_semantics=("arbitrary",),
                                             vmem_limit_bytes=MOE_FFN_VMEM_LIMIT),
        name="moe_grouped_ffn",
    )(tile_expert, tile_valid, x_sorted, w_in, bg, bl, w_out, bo, perm)


def _moe_combine_kernel(z_ref, gt_ref, h_ref, mods_ref, o_ref):
    gt = gt_ref[0]
    acc = gt[:, 0:1] * z_ref[0, 0].astype(F32)
    for k in range(1, TOP_K):
        acc = acc + gt[:, k:k + 1] * z_ref[k, 0].astype(F32)
    o_ref[0] = h_ref[0] + _mod_row(mods_ref, 5) * acc


def moe_combine(z4, gates, h, mods, ctx_tiles):
    k, b, lp, d = z4.shape
    nt = lp // TILE
    return pl.pallas_call(
        _moe_combine_kernel,
        out_shape=jax.ShapeDtypeStruct((b, lp, d), F32),
        grid=(b, nt),
        in_specs=[pl.BlockSpec((k, 1, TILE, d), lambda bi, i: (0, bi, i, 0)),
                  pl.BlockSpec((1, TILE, k), lambda bi, i: (bi, i, 0)),
                  pl.BlockSpec((1, TILE, d), lambda bi, i: (bi, i, 0)),
                  pl.BlockSpec((1, 1, 6, d), lambda bi, i: (bi, jnp.minimum(i + 1 - ctx_tiles, 1), 0, 0))],
        out_specs=pl.BlockSpec((1, TILE, d), lambda bi, i: (bi, i, 0)),
        compiler_params=_cparams(("arbitrary", "arbitrary")),
        name="moe_combine",
    )(z4, gates, h, mods)


DISPATCH_T = 512
DISPATCH_BLK = SUBLANES
DISPATCH_ROWS = -(-(DISPATCH_T * TOP_K + N_EXPERTS * (DISPATCH_BLK - 1)) // (2 * SUBLANES)) * (2 * SUBLANES)
DISPATCH_COPY_ROWS = (4 * DISPATCH_BLK, DISPATCH_BLK)
DISPATCH_LIST = max(DISPATCH_ROWS // DISPATCH_COPY_ROWS[0],
                    N_EXPERTS * (DISPATCH_COPY_ROWS[0] // DISPATCH_COPY_ROWS[1] - 1))


def _moe_dispatch_kernel(cnt_ref, tab_ref, ntail_ref, tail_ref, y_ref, idx_ref, off_ref, base_ref,
                         xs_ref, slot_ref, local_ref, zero_ref, sem, zero_sem):
    t = pl.program_id(0)
    last_t = pl.num_programs(0) - 1
    dt = DISPATCH_T
    idx = idx_ref[0]
    e_iota = lax.broadcasted_iota(jnp.int32, (N_EXPERTS, dt), 0)
    picked = [e_iota == idx[k:k + 1, :] for k in range(TOP_K)]
    cnt = sum(jnp.where(p, 1.0, 0.0) for p in picked).astype(BF16)
    r0 = lax.broadcasted_iota(jnp.int32, (dt, dt), 0)
    r1 = lax.broadcasted_iota(jnp.int32, (dt, dt), 1)
    earlier = jnp.where(r0 < r1, 1.0, 0.0).astype(BF16)
    rank = jnp.dot(cnt, earlier, preferred_element_type=F32)
    local_row = rank + off_ref[0].astype(F32)
    global_row = rank + base_ref[0].astype(F32)
    rows = lax.broadcasted_iota(jnp.int32, (DISPATCH_ROWS, dt), 0)
    place = None
    for k in range(TOP_K):
        pos = jnp.sum(jnp.where(picked[k], local_row, 0.0), axis=0, keepdims=True).astype(jnp.int32)
        slot_ref[0, k:k + 1, :] = jnp.sum(jnp.where(picked[k], global_row, 0.0), axis=0,
                                          keepdims=True).astype(jnp.int32)
        hit = rows == pos
        place = hit if place is None else (place | hit)
    placement = jnp.where(place, 1.0, 0.0).astype(BF16)
    buf = lax.rem(t, 2)
    local_ref[buf] = jnp.dot(placement, y_ref[...], preferred_element_type=F32)

    def seg_copy(cls, step, j):
        rows = DISPATCH_COPY_ROWS[cls]
        src_buf = jnp.where(step == t, buf, 1 - buf)
        src_blk = tab_ref[step, (2 * cls) * DISPATCH_LIST + j]
        dst_blk = tab_ref[step, (2 * cls + 1) * DISPATCH_LIST + j]
        src = local_ref.at[src_buf, pl.ds(pl.multiple_of(src_blk * DISPATCH_BLK, DISPATCH_BLK), rows), :]
        dst = xs_ref.at[pl.ds(pl.multiple_of(dst_blk * DISPATCH_BLK, DISPATCH_BLK), rows), :]
        return pltpu.make_async_copy(src, dst, sem)

    def tail_copy(e, j):
        blk = tail_ref[e] + j
        dst = xs_ref.at[pl.ds(pl.multiple_of(blk * DISPATCH_BLK, DISPATCH_BLK), DISPATCH_BLK), :]
        return pltpu.make_async_copy(zero_ref, dst, zero_sem)

    def start_all(copy, n):
        lax.fori_loop(0, n, lambda j, c: (copy(j).start(), c)[1], 0)

    def wait_all(copy, n):
        lax.fori_loop(0, n, lambda j, c: (copy(j).wait(), c)[1], 0)

    classes = range(len(DISPATCH_COPY_ROWS))

    @pl.when(t > 0)
    def _():
        for cls in classes:
            wait_all(functools.partial(seg_copy, cls, t - 1), cnt_ref[t - 1, cls])

    for cls in classes:
        start_all(functools.partial(seg_copy, cls, t), cnt_ref[t, cls])

    @pl.when(t == last_t)
    def _():
        for cls in classes:
            wait_all(functools.partial(seg_copy, cls, t), cnt_ref[t, cls])

    @pl.when(t == 0)
    def _():
        zero_ref[...] = jnp.zeros_like(zero_ref)
        for e in range(N_EXPERTS + 1):
            start_all(functools.partial(tail_copy, e), ntail_ref[e])
        for e in range(N_EXPERTS + 1):
            wait_all(functools.partial(tail_copy, e), ntail_ref[e])


def moe_dispatch(y2, idx_t, copy_cnt, copy_tab, ntail, tail_blk, loc_off, base, p_rows):
    n, d = y2.shape
    nt = n // DISPATCH_T
    grid_spec = pltpu.PrefetchScalarGridSpec(
        num_scalar_prefetch=4,
        grid=(nt,),
        in_specs=[pl.BlockSpec((DISPATCH_T, d), lambda t, *_: (t, 0)),
                  pl.BlockSpec((1, TOP_K, DISPATCH_T), lambda t, *_: (t, 0, 0)),
                  pl.BlockSpec((1, N_EXPERTS, 1), lambda t, *_: (t, 0, 0)),
                  pl.BlockSpec((1, N_EXPERTS, 1), lambda t, *_: (t, 0, 0))],
        out_specs=(pl.BlockSpec(memory_space=pl.ANY),
                   pl.BlockSpec((1, TOP_K, DISPATCH_T), lambda t, *_: (t, 0, 0))),
        scratch_shapes=[pltpu.VMEM((2, DISPATCH_ROWS, d), F32), pltpu.VMEM((DISPATCH_BLK, d), F32),
                        pltpu.SemaphoreType.DMA(()), pltpu.SemaphoreType.DMA(())],
    )
    return pl.pallas_call(
        _moe_dispatch_kernel,
        out_shape=(jax.ShapeDtypeStruct((p_rows, d), F32),
                   jax.ShapeDtypeStruct((nt, TOP_K, DISPATCH_T), jnp.int32)),
        grid_spec=grid_spec,
        compiler_params=_cparams(("arbitrary",)),
        name="moe_dispatch",
    )(copy_cnt, copy_tab, ntail, tail_blk, y2, idx_t, loc_off, base)


def moe_layer(h, y, logits, mods, layer, w_in, b_in, w_out, b_out, ctx_tiles):
    b, lp, d = h.shape
    n = b * lp
    y2 = y.reshape(n, d)
    top_val, top_idx = lax.top_k(logits.reshape(n, N_EXPERTS), TOP_K)
    gates = jax.nn.softmax(top_val, axis=-1)

    assert n % DISPATCH_T == 0
    t_n = n // DISPATCH_T
    blk = DISPATCH_BLK
    experts = jnp.arange(N_EXPERTS, dtype=jnp.int32)
    idx_t = top_idx.astype(jnp.int32).reshape(t_n, DISPATCH_T, TOP_K).transpose(0, 2, 1)
    tile_cnt = jnp.sum((idx_t[..., None] == experts).astype(jnp.int32), axis=(1, 2))
    seg = (tile_cnt + blk - 1) // blk * blk
    loc_end = jnp.cumsum(seg, axis=1)
    loc_off = loc_end - seg
    tot = jnp.sum(seg, axis=0)
    padded = (tot + MOE_TM - 1) // MOE_TM * MOE_TM
    ends = jnp.cumsum(padded)
    starts = ends - padded
    base = starts[None, :] + jnp.cumsum(seg, axis=0) - seg
    q = jnp.arange(DISPATCH_LIST, dtype=jnp.int32)
    covered = jnp.zeros_like(seg)
    tabs, cnts = [], []
    for rows in DISPATCH_COPY_ROWS:
        n_c = (seg - covered) // rows
        c_end = jnp.cumsum(n_c, axis=1)
        sel = jnp.sum((c_end[:, None, :] <= q[None, :, None]).astype(jnp.int32), axis=2)[..., None] == experts
        pick = lambda a: jnp.sum(jnp.where(sel, a[:, None, :], 0), axis=2)
        inside = (q[None, :] - pick(c_end - n_c)) * rows
        tabs += [(pick(loc_off + covered) + inside) // blk, (pick(base + covered) + inside) // blk]
        cnts.append(c_end[:, -1])
        covered = covered + n_c * rows
    copy_tab = jnp.concatenate(tabs, axis=1)
    copy_cnt = jnp.stack(cnts, axis=1)
    n_tiles = -(-(n * TOP_K + t_n * N_EXPERTS * (blk - 1)) // MOE_TM) + N_EXPERTS
    tile_start = jnp.arange(n_tiles, dtype=jnp.int32) * MOE_TM
    tile_expert = jnp.minimum(jnp.sum((tile_start[:, None] >= ends[None, :]).astype(jnp.int32), axis=1),
                              N_EXPERTS - 1)
    tile_valid = (tile_start < ends[-1]).astype(jnp.int32)

    p_rows = n_tiles * MOE_TM
    gap_len = jnp.concatenate([padded - tot, p_rows - ends[-1:]]) // blk
    gap_start = jnp.concatenate([starts + tot, ends[-1:]]) // blk
    x_sorted, slot = moe_dispatch(y2, idx_t, copy_cnt, copy_tab, gap_len, gap_start,
                                  loc_off[..., None], base[..., None], p_rows)
    bg = b_in[layer, :, 0::2].reshape(N_EXPERTS, 1, D_FF)
    bl = b_in[layer, :, 1::2].reshape(N_EXPERTS, 1, D_FF)
    z = moe_grouped_ffn(x_sorted, tile_expert, tile_valid, layer, w_in, bg, bl, w_out,
                        b_out[layer].reshape(N_EXPERTS, 1, d))
    slot_kn = slot.transpose(1, 0, 2).reshape(TOP_K, n)
    z4 = z.at[slot_kn].get(mode="promise_in_bounds").reshape(TOP_K, b, lp, d)
    return moe_combine(z4, gates.reshape(b, lp, TOP_K), h, mods, ctx_tiles)


def _rwkv_feat_kernel(h_ref, hp_ref, hn_ref, g_ref, mods_ref, mu_ref, wr_ref, wk_ref, wv_ref,
                      w1_ref, w2_ref, a1_ref, a2_ref, g1_ref, g2_ref, w0_ref, a0_ref, kk_ref, ka_ref,
                      bd_ref, r_out, v_out, kk_out, g_out, lw_out, kd_out, a_out, *, n_tiles):
    i = pl.program_id(1)
    gain, shift, scale = g_ref[...], _mod_row(mods_ref, 0), _mod_row(mods_ref, 1)
    x = _norm_mod(h_ref[0], gain, shift, scale)
    has_prev = (i >= 2).astype(F32)
    has_next = ((i >= 1) & (i < n_tiles - 1)).astype(F32)
    x_before = _norm_mod(hp_ref[0], gain, shift, scale)[SUBLANES - 1:SUBLANES] * has_prev
    x_after = _norm_mod(hn_ref[0], gain, shift, scale)[0:1] * has_next
    row = lax.broadcasted_iota(jnp.int32, x.shape, 0)
    prev = jnp.where(row == 0, x_before, pltpu.roll(x, 1, 0))
    nxt = jnp.where(row == TILE - 1, x_after, pltpu.roll(x, TILE - 1, 0))
    xx = 0.5 * (prev + nxt) - x
    xr, xw, xk, xv, xa, xg = (x + xx * mu_ref[pl.ds(j, 1), :] for j in range(6))

    def mm(a, w):
        return jnp.dot(a.astype(BF16), w, preferred_element_type=F32)

    r = mm(xr, wr_ref[...])
    k = mm(xk, wk_ref[...])
    v = mm(xv, wv_ref[...])
    g = mm(jax.nn.sigmoid(mm(xg, g1_ref[...])), g2_ref[...])
    kk = k * kk_ref[...]
    kk = kk * lax.rsqrt(mm(kk * kk, bd_ref[...]) + 1e-12)
    r_out[0] = r.astype(BF16)
    v_out[0] = v.astype(BF16)
    kk_out[0] = kk.astype(BF16)
    g_out[0] = g.astype(BF16)
    tw = jnp.tanh(mm(xw, w1_ref[...]))
    av = mm(xa, a1_ref[...])
    lane = lax.broadcasted_iota(jnp.int32, tw.shape, 1)
    half = tw.shape[1] // 2
    for d in range(2):
        in_dir = (lane >= d * half) & (lane < (d + 1) * half)
        wz = w0_ref[pl.ds(d, 1), :] + mm(jnp.where(in_dir, tw, 0.0), w2_ref[...])
        lw_out[d, 0] = -RW_DECAY_SCALE * jax.nn.sigmoid(wz)
        a = jax.nn.sigmoid(a0_ref[pl.ds(d, 1), :] + mm(jnp.where(in_dir, av, 0.0), a2_ref[...]))
        kd_out[d, 0] = (k * (1.0 + (a - 1.0) * ka_ref[...])).astype(BF16)
        a_out[d, 0] = a.astype(BF16)


def rwkv_features(h, gain, mods, mu, w_r, w_k, w_v, w0, w1, w2, a0, a1, a2, g1, g2, k_k, k_a):
    b, lt, d = h.shape
    nt = lt // TILE
    hb = TILE // SUBLANES
    nhb = lt // SUBLANES
    hid = jnp.arange(d) // RW_HEAD_DIM
    bd = (hid[:, None] == hid[None, :]).astype(BF16)
    w1c = jnp.concatenate([w1[0], w1[1]], axis=1).astype(BF16)
    a1c = jnp.concatenate([a1[0], a1[1]], axis=1).astype(BF16)
    w2c = jnp.concatenate([w2[0], w2[1]], axis=0).astype(BF16)
    a2c = jnp.concatenate([a2[0], a2[1]], axis=0).astype(BF16)
    full = lambda shape: pl.BlockSpec(shape, lambda bi, i: (0,) * len(shape))
    tile_spec = pl.BlockSpec((1, TILE, d), lambda bi, i: (bi, i, 0))
    dir_spec = pl.BlockSpec((2, 1, TILE, d), lambda bi, i: (0, bi, i, 0))
    seq_bf = jax.ShapeDtypeStruct((b, lt, d), BF16)
    return pl.pallas_call(
        functools.partial(_rwkv_feat_kernel, n_tiles=nt),
        out_shape=(seq_bf, seq_bf, seq_bf, seq_bf,
                   jax.ShapeDtypeStruct((2, b, lt, d), F32),
                   jax.ShapeDtypeStruct((2, b, lt, d), BF16),
                   jax.ShapeDtypeStruct((2, b, lt, d), BF16)),
        grid=(b, nt),
        in_specs=[tile_spec,
                  pl.BlockSpec((1, SUBLANES, d), lambda bi, i: (bi, jnp.maximum(i * hb - 1, 0), 0)),
                  pl.BlockSpec((1, SUBLANES, d), lambda bi, i: (bi, jnp.minimum((i + 1) * hb, nhb - 1), 0)),
                  full((1, d)),
                  pl.BlockSpec((1, 1, 6, d), lambda bi, i: (bi, jnp.minimum(i, 1), 0, 0)),
                  full((6, d)), full((d, d)), full((d, d)), full((d, d)),
                  full(w1c.shape), full(w2c.shape), full(a1c.shape), full(a2c.shape),
                  full(g1.shape), full(g2.shape), full((2, d)), full((2, d)), full((1, d)), full((1, d)),
                  full((d, d))],
        out_specs=(tile_spec, tile_spec, tile_spec, tile_spec, dir_spec, dir_spec, dir_spec),
        compiler_params=_cparams(("arbitrary", "arbitrary")),
        name="rwkv_features",
    )(h, h, h, gain.reshape(1, d), mods, mu, w_r.astype(BF16), w_k.astype(BF16), w_v.astype(BF16),
      w1c, w2c, a1c, a2c, g1.astype(BF16), g2.astype(BF16), w0, a0, k_k.reshape(1, d), k_a.reshape(1, d), bd)


SCAN_C = 64
SCAN_SUB = 2
SCAN_G = 4
SCAN_GW = SCAN_G * RW_HEAD_DIM


def _scan_chunk_inputs(reverse, off, r_ref, v_ref, kk_ref, lw_ref, kd_ref, a_ref):
    cc = SCAN_C
    rows = slice(off, off + cc)
    ti = lax.broadcasted_iota(jnp.int32, (cc, cc), 0)
    si = lax.broadcasted_iota(jnp.int32, (cc, cc), 1)
    before_incl = (si >= ti) if reverse else (si <= ti)
    lw = lw_ref[0, 0, rows, :]
    tri = jnp.where(before_incl, 1.0, 0.0).astype(BF16)
    lw_hi = lw.astype(BF16)
    lw_lo = (lw - lw_hi.astype(F32)).astype(BF16)
    cl = (jnp.dot(tri, lw_hi, preferred_element_type=F32) + jnp.dot(tri, lw_lo, preferred_element_type=F32))
    g_in = jnp.exp(cl)
    g_ex = jnp.exp(cl - lw)
    g_inv = jnp.exp(-cl)
    g_end = jnp.exp(jnp.sum(lw, axis=0, keepdims=True))
    kk = kk_ref[0, rows, :].astype(F32)
    rt = (r_ref[0, rows, :].astype(F32) * g_in).astype(BF16)
    at = (-kk * g_ex).astype(BF16)
    bt = (kk * a_ref[0, 0, rows, :].astype(F32) * g_inv).astype(BF16)
    kt = (kd_ref[0, 0, rows, :].astype(F32) * g_inv).astype(BF16)
    return rt, at, bt, kt, v_ref[0, rows, :], g_end


def _rwkv_scan_kernel(rf_ref, vf_ref, kkf_ref, rb_ref, vb_ref, kkb_ref, lwf_ref, kdf_ref, af_ref,
                      lwb_ref, kdb_ref, ab_ref, yf_ref, yb_ref, h_ref):
    @pl.when(pl.program_id(1) == 0)
    def _():
        h_ref[...] = jnp.zeros_like(h_ref)

    cc, gw = SCAN_C, SCAN_GW
    y_refs = (yf_ref, yb_ref)

    tg = lax.broadcasted_iota(jnp.int32, (cc, gw), 0)
    sg = lax.broadcasted_iota(jnp.int32, (cc, gw), 1) % cc
    m_strict = (sg < tg, sg > tg)
    m_incl = (sg <= tg, sg >= tg)
    eye_g = (sg == tg).astype(F32)
    bi = lax.broadcasted_iota(jnp.int32, (gw, gw), 0) // cc
    bj = lax.broadcasted_iota(jnp.int32, (gw, gw), 1) // cc
    blk = bi == bj

    def bdiag(x):
        return jnp.where(blk, jnp.concatenate([x] * SCAN_G, axis=0), jnp.zeros((), x.dtype))

    def mm(a, b):
        return jnp.dot(a, b, preferred_element_type=F32)

    def mm_nt(a, b):
        return lax.dot_general(a, b, (((1,), (1,)), ((), ())), preferred_element_type=F32)

    n_g = RW_HEADS // SCAN_G
    chains = [(d, g) for d in range(2) for g in range(n_g)]
    cols = [slice(g * gw, (g + 1) * gw) for _, g in chains]
    idx = range(len(chains))

    def state_free_part(sub):
        offs = (sub * cc, (SCAN_SUB - 1 - sub) * cc)
        ops = (_scan_chunk_inputs(False, offs[0], rf_ref, vf_ref, kkf_ref, lwf_ref, kdf_ref, af_ref),
               _scan_chunk_inputs(True, offs[1], rb_ref, vb_ref, kkb_ref, lwb_ref, kdb_ref, ab_ref))
        rt, at, bt, kt, v, g_end = ([ops[d][j][:, cols[i]] for i, (d, _) in enumerate(chains)] for j in range(6))
        v_bd = [bdiag(x) for x in v]
        mats = [mm_nt(jnp.concatenate([at[i], rt[i]], axis=0),
                      jnp.concatenate([bdiag(bt[i]), bdiag(kt[i])], axis=0)) for i in idx]
        a_ab = [jnp.where(m_strict[chains[i][0]], mats[i][:cc, :gw], 0.0) for i in idx]
        a_ak = [jnp.where(m_strict[chains[i][0]], mats[i][:cc, gw:], 0.0).astype(BF16) for i in idx]
        m_rb = [jnp.where(m_incl[chains[i][0]], mats[i][cc:, :gw], 0.0).astype(BF16) for i in idx]
        m_rk = [jnp.where(m_incl[chains[i][0]], mats[i][cc:, gw:], 0.0).astype(BF16) for i in idx]
        from_v = [mm(jnp.concatenate([a_ak[i], m_rk[i]], axis=0), v_bd[i]) for i in idx]
        ar = [jnp.concatenate([at[i], rt[i]], axis=0) for i in idx]
        bk = [jnp.concatenate([bt[i], kt[i]], axis=0) for i in idx]
        return offs, ar, bk, v, g_end, m_rb, from_v, a_ab

    def neumann_inverse(a_list):
        ks = range(len(a_list))
        s_acc = [eye_g + x for x in a_list]
        pw = [mm(x.astype(BF16), bdiag(x.astype(BF16))).astype(BF16) for x in a_list]
        n = 2
        while n < cc:
            last = 2 * n >= cc
            lhs = [s_acc[k].astype(BF16) if last else jnp.concatenate([pw[k], s_acc[k].astype(BF16)], axis=0)
                   for k in ks]
            prod = [mm(lhs[k], bdiag(pw[k])) for k in ks]
            s_acc = [s_acc[k] + (prod[k] if last else prod[k][cc:]) for k in ks]
            if not last:
                pw = [prod[k][:cc].astype(BF16) for k in ks]
            n *= 2
        return [x.astype(BF16) for x in s_acc]

    parts = [state_free_part(sub) for sub in range(SCAN_SUB)]
    inv_all = neumann_inverse([a for p in parts for a in p[-1]])
    h_cur = [h_ref[i] for i in idx]
    for sub, (offs, ar, bk, v, g_end, m_rb, from_v, _) in enumerate(parts):
        inv = inv_all[sub * len(chains):(sub + 1) * len(chains)]
        h0b = [x.astype(BF16) for x in h_cur]
        from_h = [mm(ar[i], h0b[i]) for i in idx]
        pq = [from_h[i][:cc] + from_v[i][:cc] for i in idx]
        ub = [mm(inv[i], bdiag(pq[i].astype(BF16))).astype(BF16) for i in idx]
        for i in idx:
            d = chains[i][0]
            y_refs[d][0, offs[d]:offs[d] + cc, cols[i]] = (from_h[i][cc:] + from_v[i][cc:]
                                                          + mm(m_rb[i], bdiag(ub[i])))
        nxt = []
        for i in idx:
            upd = lax.dot_general(bk[i], jnp.concatenate([ub[i], v[i]], axis=0),
                                  (((0,), (0,)), ((), ())), preferred_element_type=F32)
            ge = jnp.transpose(jnp.broadcast_to(g_end[i], (gw, gw)))
            nxt.append(ge * (h_cur[i] + jnp.where(blk, upd, 0.0)))
        h_cur = nxt
    for i in idx:
        h_ref[i] = h_cur[i]


def rwkv_scan(r, v, kk, lw, kd, a):
    b, lt, d = r.shape
    blk_rows = SCAN_SUB * SCAN_C
    nc = lt // blk_rows
    n_ctx = CTX_LEN // blk_rows
    assert SCAN_C == RW_HEAD_DIM and CTX_LEN % blk_rows == 0 and lt % blk_rows == 0

    def rev_chunk(c):
        return jnp.where(c < n_ctx, n_ctx - 1 - c, nc - 1 + n_ctx - c)

    fwd = pl.BlockSpec((1, blk_rows, d), lambda bi, c: (bi, c, 0))
    bwd = pl.BlockSpec((1, blk_rows, d), lambda bi, c: (bi, rev_chunk(c), 0))
    fwd_dir = pl.BlockSpec((1, 1, blk_rows, d), lambda bi, c: (0, bi, c, 0))
    bwd_dir = pl.BlockSpec((1, 1, blk_rows, d), lambda bi, c: (1, bi, rev_chunk(c), 0))
    y_shape = jax.ShapeDtypeStruct((b, lt, d), F32)
    return pl.pallas_call(
        _rwkv_scan_kernel,
        out_shape=(y_shape, y_shape),
        grid=(b, nc),
        in_specs=[fwd, fwd, fwd, bwd, bwd, bwd, fwd_dir, fwd_dir, fwd_dir, bwd_dir, bwd_dir, bwd_dir],
        out_specs=(fwd, bwd),
        scratch_shapes=[pltpu.VMEM((2 * RW_HEADS // SCAN_G, SCAN_GW, SCAN_GW), F32)],
        compiler_params=_cparams(("arbitrary", "arbitrary")),
        name="rwkv_scan",
    )(r, v, kk, r, v, kk, lw, kd, a, lw, kd, a)


def _rwkv_out_kernel(yf_ref, yb_ref, r_ref, v_ref, g_ref, kd_ref, h_ref, mods_ref, lnw_ref, lnb_ref, rk_ref,
                     bd_ref, wo_ref, gffn_ref, rwh_ref, rwl_ref, rb_ref, o_ref, y_ref, lg_ref):
    def head_sum(x):
        return jnp.dot(x.astype(BF16), bd_ref[...], preferred_element_type=F32)

    y = yf_ref[0] + yb_ref[0]
    inv_n = 1.0 / RW_HEAD_DIM
    mean = head_sum(y) * inv_n
    yc = y - mean
    var = head_sum(yc * yc) * inv_n
    yn = yc * lax.rsqrt(var + RW_GN_EPS) * lnw_ref[...] + lnb_ref[...]
    r = r_ref[0].astype(F32)
    kd = kd_ref[0, 0].astype(F32) + kd_ref[1, 0].astype(F32)
    yn = yn + head_sum(r * kd * rk_ref[...]) * v_ref[0].astype(F32)
    out = jnp.dot((yn * g_ref[0].astype(F32)).astype(BF16), wo_ref[...], preferred_element_type=F32)
    h_new = h_ref[0] + _mod_row(mods_ref, 2) * out
    o_ref[0] = h_new
    _router_tail(h_new, gffn_ref, mods_ref, rwh_ref, rwl_ref, rb_ref, y_ref, lg_ref)


def rwkv_readout(y_f, y_b, r, v, g, kd, h, mods, ln_w, ln_b, r_k, w_o, ffn_gain, router_w, router_b):
    b, lt, d = h.shape
    nt = lt // TILE - 1
    e = router_w.shape[1]
    r_ops, r_specs = _router_operands(ffn_gain, router_w, router_b)
    out_tile = lambda width: pl.BlockSpec((1, TILE, width), lambda bi, i: (bi, i, 0))
    hid = jnp.arange(d) // RW_HEAD_DIM
    bd = (hid[:, None] == hid[None, :]).astype(BF16)
    full = lambda shape: pl.BlockSpec(shape, lambda bi, i: (0,) * len(shape))
    tile_spec = pl.BlockSpec((1, TILE, d), lambda bi, i: (bi, i + 1, 0))
    dir_spec = pl.BlockSpec((2, 1, TILE, d), lambda bi, i: (0, bi, i + 1, 0))
    return pl.pallas_call(
        _rwkv_out_kernel,
        out_shape=(jax.ShapeDtypeStruct((b, nt * TILE, d), F32), jax.ShapeDtypeStruct((b, nt * TILE, d), BF16),
                   jax.ShapeDtypeStruct((b, nt * TILE, e), F32)),
        grid=(b, nt),
        in_specs=[tile_spec, tile_spec, tile_spec, tile_spec, tile_spec, dir_spec, tile_spec,
                  pl.BlockSpec((1, 1, 6, d), lambda bi, i: (bi, 1, 0, 0)),
                  full((1, d)), full((1, d)), full((1, d)), full((d, d)), full((d, d))] + r_specs,
        out_specs=(out_tile(d), out_tile(d), out_tile(e)),
        compiler_params=_cparams(("arbitrary", "arbitrary")),
        name="rwkv_readout",
    )(y_f, y_b, r, v, g, kd, h, mods, ln_w.reshape(1, d), ln_b.reshape(1, d), r_k.reshape(1, d), bd,
      w_o.astype(BF16), *r_ops)


def kernel(x, c, ctx, c_ctx, ada_w, ada_b, norm_mix_g, norm_ffn_g, router_w, router_b, exp_w_in, exp_b_in,
           exp_w_out, exp_b_out, ab_w_in, na_q_g, na_k_g, na_rpb, pool_w, pool_scale, ab_w_out,
           rw_mu, rw_w_r, rw_w_k, rw_w_v, rw_w_o, rw_w0, rw_w1, rw_w2, rw_a0, rw_a1, rw_a2,
           rw_g1, rw_g2, rw_k_k, rw_k_a, rw_r_k, rw_ln_w, rw_ln_b):
    b, seq, d = x.shape
    depth = ada_w.shape[0]
    assert ctx.shape[1] == CTX_LEN == TILE and seq % TILE == 0 and d == D_MODEL
    assert depth == 2, "layer schedule below is written for [neighbourhood/pool layer, RWKV layer]"
    rows = seq // GRID_W
    assert rows >= NA_QROWS * NA_KTILES and NA_QROWS * NA_KTILES >= NA_QROWS + NA_KH - 1

    n_c = 1 + b
    n_c_pad = -(-n_c // SUBLANES) * SUBLANES
    cvec = jnp.concatenate([c_ctx[None], c, jnp.zeros((n_c_pad - n_c, d), F32)], axis=0)

    def layer_mods(layer):
        m = ada_mod(cvec, ada_w[layer], ada_b[layer])
        m_ctx = jnp.broadcast_to(m[0].reshape(1, 6, d), (b, 6, d))
        return jnp.stack([m_ctx, m[1:n_c].reshape(b, 6, d)], axis=1)

    def moe(h, y, logits, layer, mods, ctx_tiles):
        return moe_layer(h, y, logits, mods, layer, exp_w_in, exp_b_in, exp_w_out, exp_b_out, ctx_tiles)

    h = jnp.concatenate([ctx, x], axis=1)

    mods = layer_mods(0)
    qk, v, u = ab_proj(h, norm_mix_g[0], mods, ab_w_in[0], na_q_g[0], na_k_g[0])
    o_na = na_attention(qk, v, _na_bias_tables(na_rpb[0], rows))
    o_pool = multiscale_pool(u, pool_w[0], pool_scale[0])
    h, y, logits = out_proj_residual(o_na, o_pool, h, mods, ab_w_out[0], norm_ffn_g[0], router_w[0], router_b[0])
    h = moe(h, y, logits, 0, mods, 1)

    mods = layer_mods(1)
    r, v, kk, g, lw, kd, a = rwkv_features(h, norm_mix_g[1], mods, rw_mu[0], rw_w_r[0], rw_w_k[0], rw_w_v[0],
                                           rw_w0[0], rw_w1[0], rw_w2[0], rw_a0[0], rw_a1[0], rw_a2[0],
                                           rw_g1[0], rw_g2[0], rw_k_k[0], rw_k_a[0])
    y_f, y_b = rwkv_scan(r, v, kk, lw, kd, a)
    h_lat, y, logits = rwkv_readout(y_f, y_b, r, v, g, kd, h, mods, rw_ln_w[0], rw_ln_b[0], rw_r_k[0], rw_w_o[0],
                                    norm_ffn_g[1], router_w[1], router_b[1])
    return moe(h_lat, y, logits, 1, mods, 0)
```

```python
import functools

import jax
import jax.numpy as jnp
import numpy as np
from jax import lax
from jax.experimental import pallas as pl
from jax.experimental.pallas import tpu as pltpu

F32 = jnp.float32
BF16 = jnp.bfloat16

D_MODEL = 1024
GRID_W = 64
CTX_LEN = 256
RMS_EPS = 1e-6
NA_HEADS = 8
NA_HEAD_DIM = 64
NA_WIDTH = NA_HEADS * NA_HEAD_DIM
NA_KH = 8
NA_KW = 16
POOL_WINDOWS = (2, 4, 8, 16)
POOL_GC = 128
POOL_WIDTH = 512
RW_HEAD_DIM = 64
RW_HEADS = D_MODEL // RW_HEAD_DIM
RW_GN_EPS = 64e-5
RW_DECAY_SCALE = float(np.exp(-0.5))
N_EXPERTS = 32
TOP_K = 4
D_FF = D_MODEL
SWIGLU_LIMIT = 7.0
SWIGLU_ALPHA = 1.702

TILE = 256
LANES = 128
SUBLANES = 8
MOE_TM = 512
NEG_BIG = -1e30
VMEM_LIMIT = 48 * 1024 * 1024
MOE_FFN_VMEM_LIMIT = 56 * 1024 * 1024


def _cparams(sem):
    return pltpu.CompilerParams(dimension_semantics=sem, vmem_limit_bytes=VMEM_LIMIT)


def _mod_row(mods_ref, k):
    return mods_ref[0, 0, pl.ds(k, 1), :]


def _norm_mod(x, gain, shift, scale):
    ms = jnp.mean(x * x, axis=-1, keepdims=True)
    return (x * lax.rsqrt(ms + RMS_EPS) * gain) * (1.0 + scale) + shift


def _ada_kernel(c_ref, w_ref, b_ref, o_ref):
    c = c_ref[...]
    s = c * jax.nn.sigmoid(c)
    o_ref[...] = jnp.dot(s, w_ref[...], preferred_element_type=F32,
                         precision=lax.Precision.HIGHEST) + b_ref[...]


def ada_mod(cvec, w, b):
    r, d = cvec.shape
    n = w.shape[1]
    tn = 1536
    return pl.pallas_call(
        _ada_kernel,
        out_shape=jax.ShapeDtypeStruct((r, n), F32),
        grid=(n // tn,),
        in_specs=[pl.BlockSpec((r, d), lambda j: (0, 0)),
                  pl.BlockSpec((d, tn), lambda j: (0, j)),
                  pl.BlockSpec((1, tn), lambda j: (0, j))],
        out_specs=pl.BlockSpec((r, tn), lambda j: (0, j)),
        compiler_params=_cparams(("arbitrary",)),
        name="ada_mod",
    )(cvec, w, b.reshape(1, n))


def _ab_proj_kernel(h_ref, g_ref, mods_ref, w_ref, bd_ref, qkg_ref, qk_ref, v_ref, u_ref):
    x = _norm_mod(h_ref[0], g_ref[...], _mod_row(mods_ref, 0), _mod_row(mods_ref, 1))
    p = jnp.dot(x.astype(BF16), w_ref[...], preferred_element_type=F32)
    qk = p[:, :2 * NA_WIDTH]
    ss = jnp.dot((qk * qk).astype(BF16), bd_ref[...], preferred_element_type=F32)
    qk_ref[0] = (qk * lax.rsqrt(ss + RMS_EPS) * qkg_ref[...]).astype(BF16)
    v_ref[0] = p[:, 2 * NA_WIDTH:3 * NA_WIDTH].astype(BF16)
    u_ref[0] = p[:, 3 * NA_WIDTH:]


def ab_proj(h, gain, mods, w_in, q_g, k_g):
    b, lt, d = h.shape
    nt = lt // TILE
    n_in = w_in.shape[1]
    hid = jnp.arange(2 * NA_WIDTH) // NA_HEAD_DIM
    bd = jnp.where(hid[:, None] == hid[None, :], 1.0 / NA_HEAD_DIM, 0.0).astype(BF16)
    qkg = jnp.concatenate([jnp.tile(q_g, NA_HEADS), jnp.tile(k_g, NA_HEADS)]).reshape(1, -1).astype(F32)
    return pl.pallas_call(
        _ab_proj_kernel,
        out_shape=(jax.ShapeDtypeStruct((b, lt, 2 * NA_WIDTH), BF16),
                   jax.ShapeDtypeStruct((b, lt, NA_WIDTH), BF16),
                   jax.ShapeDtypeStruct((b, lt, POOL_WIDTH), F32)),
        grid=(b, nt),
        in_specs=[pl.BlockSpec((1, TILE, d), lambda bi, i: (bi, i, 0)),
                  pl.BlockSpec((1, d), lambda bi, i: (0, 0)),
                  pl.BlockSpec((1, 1, 6, d), lambda bi, i: (bi, jnp.minimum(i, 1), 0, 0)),
                  pl.BlockSpec((d, n_in), lambda bi, i: (0, 0)),
                  pl.BlockSpec((2 * NA_WIDTH, 2 * NA_WIDTH), lambda bi, i: (0, 0)),
                  pl.BlockSpec((1, 2 * NA_WIDTH), lambda bi, i: (0, 0))],
        out_specs=(pl.BlockSpec((1, TILE, 2 * NA_WIDTH), lambda bi, i: (bi, i, 0)),
                   pl.BlockSpec((1, TILE, NA_WIDTH), lambda bi, i: (bi, i, 0)),
                   pl.BlockSpec((1, TILE, POOL_WIDTH), lambda bi, i: (bi, i, 0))),
        compiler_params=_cparams(("arbitrary", "arbitrary")),
        name="ab_proj",
    )(h, gain.reshape(1, d), mods, w_in.astype(BF16), bd, qkg)


NA_QROWS = TILE // GRID_W
NA_KTILES = 3
NA_NLOC = NA_KTILES * TILE
NA_NKEY = NA_NLOC + CTX_LEN


def _na_bias_tables(rpb, rows):
    n_r, n_c = 2 * NA_KH - 1, 2 * NA_KW - 1
    col = np.arange(GRID_W)
    c_start = np.clip(col - NA_KW // 2, 0, GRID_W - NA_KW)
    valid_c = (col[None, :] >= c_start[:, None]) & (col[None, :] < c_start[:, None] + NA_KW)
    c_rel = np.clip(col[None, :] - col[:, None] + NA_KW - 1, 0, n_c - 1)
    sel_c = (c_rel.reshape(-1)[None, :] == np.arange(n_c)[:, None]).astype(np.float32)
    by_col = jnp.einsum("hrc,cn->hrn", rpb.astype(F32), sel_c, precision=lax.Precision.HIGHEST)
    n_qt = rows // NA_QROWS
    qr, kr = np.arange(NA_QROWS), np.arange(NA_KTILES * NA_QROWS)
    tabs = []
    for jq in (0, 1, n_qt - 1):
        w0 = NA_QROWS * min(max(jq - 1, 0), n_qt - NA_KTILES)
        q_row = NA_QROWS * jq + qr
        key_row = w0 + kr
        r_start = np.clip(q_row - NA_KH // 2, 0, rows - NA_KH)
        valid_r = (key_row[None, :] >= r_start[:, None]) & (key_row[None, :] < r_start[:, None] + NA_KH)
        r_rel = np.clip(key_row[None, :] - q_row[:, None] + NA_KH - 1, 0, n_r - 1)
        sel_r = (r_rel.reshape(-1)[:, None] == np.arange(n_r)[None, :]).astype(np.float32)
        t = jnp.einsum("xr,hrn->hxn", sel_r, by_col, precision=lax.Precision.HIGHEST)
        t = t.reshape(NA_HEADS, NA_QROWS, NA_KTILES * NA_QROWS, GRID_W, GRID_W)
        t = t.transpose(0, 1, 3, 2, 4).reshape(NA_HEADS, TILE, NA_NLOC)
        valid = (valid_r[:, None, :, None] & valid_c[None, :, None, :]).reshape(TILE, NA_NLOC)
        tabs.append(jnp.where(valid[None], t, NEG_BIG))
    tabs = [jnp.full_like(tabs[0], NEG_BIG)] + tabs
    loc = jnp.stack(tabs)
    return jnp.concatenate([loc, jnp.zeros(loc.shape[:3] + (CTX_LEN,), F32)], axis=-1)


def _na_kernel(q_ref, k0_ref, k1_ref, k2_ref, kc_ref, v0_ref, v1_ref, v2_ref, vc_ref, bias_ref, o_ref):
    q = q_ref[0]
    kall = jnp.concatenate([k0_ref[0], k1_ref[0], k2_ref[0], kc_ref[0]], axis=0)
    vall = jnp.concatenate([v0_ref[0], v1_ref[0], v2_ref[0], vc_ref[0]], axis=0)
    lane = lax.broadcasted_iota(jnp.int32, (TILE, LANES), 1)
    scale = jnp.asarray(NA_HEAD_DIM ** -0.5, BF16)
    heads = range(NA_HEADS)
    pair = [slice((h // 2) * LANES, (h // 2 + 1) * LANES) for h in heads]
    scores = []
    for h in heads:
        lo = (h % 2) * NA_HEAD_DIM
        qh = jnp.where((lane >= lo) & (lane < lo + NA_HEAD_DIM), q[:, pair[h]], jnp.zeros((), BF16)) * scale
        s = lax.dot_general(qh, kall[:, pair[h]], (((1,), (1,)), ((), ())), preferred_element_type=F32)
        scores.append(s + bias_ref[0, h])
    probs, denom = [], []
    for s in scores:
        p = jnp.exp(s - jnp.max(s, axis=-1, keepdims=True))
        probs.append(p.astype(BF16))
        denom.append(jnp.sum(p, axis=-1, keepdims=True))
    outs = [jnp.dot(probs[h], vall[:, pair[h]], preferred_element_type=F32) / denom[h] for h in heads]
    for h in range(0, NA_HEADS, 2):
        o_ref[0, :, pair[h]] = jnp.where(lane < NA_HEAD_DIM, outs[h], outs[h + 1]).astype(BF16)


def na_attention(qk, v, bias):
    b, lt, _ = qk.shape
    nt = lt // TILE
    n_qt = nt - 1
    assert NA_HEAD_DIM ** -0.5 == 0.125 and NA_HEADS % 2 == 0
    w = NA_WIDTH

    def w0(j):
        return 1 + jnp.clip(j - 2, 0, n_qt - NA_KTILES)

    def kind(j):
        return jnp.where(j == 0, 0, jnp.where(j == 1, 1, jnp.where(j == nt - 1, 3, 2)))

    qspec = pl.BlockSpec((1, TILE, w), lambda j, bi: (bi, j, 0))
    kspecs = [pl.BlockSpec((1, TILE, w), functools.partial(lambda j, bi, t: (bi, w0(j) + t, 1), t=t))
              for t in range(NA_KTILES)]
    kcspec = pl.BlockSpec((1, TILE, w), lambda j, bi: (bi, 0, 1))
    vspecs = [pl.BlockSpec((1, TILE, w), functools.partial(lambda j, bi, t: (bi, w0(j) + t, 0), t=t))
              for t in range(NA_KTILES)]
    vcspec = pl.BlockSpec((1, TILE, w), lambda j, bi: (bi, 0, 0))
    bspec = pl.BlockSpec((1, NA_HEADS, TILE, NA_NKEY), lambda j, bi: (kind(j), 0, 0, 0))
    return pl.pallas_call(
        _na_kernel,
        out_shape=jax.ShapeDtypeStruct((b, lt, w), BF16),
        grid=(nt, b),
        in_specs=[qspec] + kspecs + [kcspec] + vspecs + [vcspec, bspec],
        out_specs=pl.BlockSpec((1, TILE, w), lambda j, bi: (bi, j, 0)),
        compiler_params=_cparams(("arbitrary", "arbitrary")),
        name="na_attention",
    )(qk, qk, qk, qk, qk, v, v, v, v, bias)


POOL_HALO = 8
POOL_ROWS = TILE + 2 * POOL_HALO


def _pool_kernel(cur_ref, prev_ref, next_ref, pw_ref, ps_ref, o_ref, xp_ref, *, n_tiles):
    i = pl.program_id(1)
    has_prev = (i >= 2).astype(F32)
    has_next = ((i >= 1) & (i < n_tiles - 1)).astype(F32)
    xp_ref[0:POOL_HALO, :] = prev_ref[0] * has_prev
    xp_ref[POOL_HALO:POOL_HALO + TILE, :] = cur_ref[0]
    xp_ref[POOL_HALO + TILE:POOL_ROWS, :] = next_ref[0] * has_next
    row = lax.broadcasted_iota(jnp.int32, (TILE, POOL_GC), 0)
    tpos = jnp.where(i == 0, row, row + (i - 1) * TILE)
    seq_len = jnp.where(i == 0, CTX_LEN, (n_tiles - 1) * TILE)
    for g, w in enumerate(POOL_WINDOWS):
        cols = slice(g * POOL_GC, (g + 1) * POOL_GC)
        x = xp_ref[:, cols]
        s = x + pltpu.roll(x, 1, 0)
        half = 1
        while 2 * half < w:
            s = pltpu.roll(s, half, 0) + pltpu.roll(s, POOL_ROWS - half, 0)
            half *= 2
        s = s[POOL_HALO:POOL_HALO + TILE]
        cnt = jnp.minimum(w // 2, tpos) + jnp.minimum(w // 2, seq_len - tpos)
        pooled = s / cnt.astype(F32) - cur_ref[0, :, cols]
        y = jnp.dot(pooled.astype(BF16), pw_ref[g], preferred_element_type=F32)
        o_ref[0, :, cols] = (y * ps_ref[:, cols]).astype(BF16)


def multiscale_pool(u, pool_w, pool_scale):
    b, lt, c = u.shape
    nt = lt // TILE
    hb = TILE // POOL_HALO
    nhb = lt // POOL_HALO
    assert POOL_HALO >= max(POOL_WINDOWS) // 2 and POOL_HALO == SUBLANES
    return pl.pallas_call(
        functools.partial(_pool_kernel, n_tiles=nt),
        out_shape=jax.ShapeDtypeStruct((b, lt, c), BF16),
        grid=(b, nt),
        in_specs=[pl.BlockSpec((1, TILE, c), lambda bi, i: (bi, i, 0)),
                  pl.BlockSpec((1, POOL_HALO, c), lambda bi, i: (bi, jnp.maximum(i * hb - 1, 0), 0)),
                  pl.BlockSpec((1, POOL_HALO, c), lambda bi, i: (bi, jnp.minimum((i + 1) * hb, nhb - 1), 0)),
                  pl.BlockSpec((len(POOL_WINDOWS), POOL_GC, POOL_GC), lambda bi, i: (0, 0, 0)),
                  pl.BlockSpec((1, c), lambda bi, i: (0, 0))],
        out_specs=pl.BlockSpec((1, TILE, c), lambda bi, i: (bi, i, 0)),
        scratch_shapes=[pltpu.VMEM((POOL_ROWS, c), F32)],
        compiler_params=_cparams(("arbitrary", "arbitrary")),
        name="multiscale_pool",
    )(u, u, u, pool_w.astype(BF16), pool_scale.reshape(1, c).astype(F32))


def _router_tail(h_new, g_ref, mods_ref, rwh_ref, rwl_ref, rb_ref, y_ref, lg_ref):
    x = _norm_mod(h_new, g_ref[...], _mod_row(mods_ref, 3), _mod_row(mods_ref, 4))
    x_hi = x.astype(BF16)
    y_ref[0] = x_hi
    x_lo = (x - x_hi.astype(F32)).astype(BF16)
    w_hi, w_lo = rwh_ref[...], rwl_ref[...]
    lg_ref[0] = (jnp.dot(x_hi, w_hi, preferred_element_type=F32)
                 + jnp.dot(x_lo, w_hi, preferred_element_type=F32)
                 + jnp.dot(x_hi, w_lo, preferred_element_type=F32)
                 + jnp.dot(x_lo, w_lo, preferred_element_type=F32)) + rb_ref[...]


def _router_operands(gain, router_w, router_b):
    d, e = router_w.shape
    head = lax.bitcast_convert_type(lax.bitcast_convert_type(router_w, jnp.uint32) & jnp.uint32(0xFFFF0000), F32)
    full = lambda shape: pl.BlockSpec(shape, lambda bi, i: (0,) * len(shape))
    return ((gain.reshape(1, d), head.astype(BF16), (router_w - head).astype(BF16), router_b.reshape(1, e)),
            [full((1, d)), full((d, e)), full((d, e)), full((1, e))])


def _out_proj_kernel(a_ref, b_ref, h_ref, mods_ref, w_ref, g_ref, rwh_ref, rwl_ref, rb_ref,
                     o_ref, y_ref, lg_ref):
    ka = a_ref.shape[-1]
    y = jnp.dot(a_ref[0], w_ref[:ka, :], preferred_element_type=F32)
    y = y + jnp.dot(b_ref[0], w_ref[ka:, :], preferred_element_type=F32)
    h_new = h_ref[0] + _mod_row(mods_ref, 2) * y
    o_ref[0] = h_new
    _router_tail(h_new, g_ref, mods_ref, rwh_ref, rwl_ref, rb_ref, y_ref, lg_ref)


def out_proj_residual(a, bb, h, mods, w, ffn_gain, router_w, router_b):
    b, lt, d = h.shape
    nt = lt // TILE
    ka, kb = a.shape[-1], bb.shape[-1]
    e = router_w.shape[1]
    r_ops, r_specs = _router_operands(ffn_gain, router_w, router_b)
    tile = lambda width: pl.BlockSpec((1, TILE, width), lambda bi, i: (bi, i, 0))
    return pl.pallas_call(
        _out_proj_kernel,
        out_shape=(jax.ShapeDtypeStruct((b, lt, d), F32), jax.ShapeDtypeStruct((b, lt, d), BF16),
                   jax.ShapeDtypeStruct((b, lt, e), F32)),
        grid=(b, nt),
        in_specs=[tile(ka), tile(kb), tile(d),
                  pl.BlockSpec((1, 1, 6, d), lambda bi, i: (bi, jnp.minimum(i, 1), 0, 0)),
                  pl.BlockSpec((ka + kb, d), lambda bi, i: (0, 0))] + r_specs,
        out_specs=(tile(d), tile(d), tile(e)),
        compiler_params=_cparams(("arbitrary", "arbitrary")),
        name="out_proj_residual",
    )(a, bb, h, mods, w.astype(BF16), *r_ops)


MOE_PERM_W = 256


def _moe_ffn_kernel(te_ref, tv_ref, x_ref, win_ref, bg_ref, bl_ref, wo_ref, bo_ref, perm_ref, o_ref,
                    wg_s, wl_s, wo_s):
    t = pl.program_id(0)
    valid = tv_ref[t] > 0
    new_expert = (t == 0) | (te_ref[t] != te_ref[jnp.maximum(t - 1, 0)])

    @pl.when(valid & new_expert)
    def _():
        half = MOE_PERM_W // 2
        for cb in range(win_ref.shape[3] // MOE_PERM_W):
            w = win_ref[0, 0, :, cb * MOE_PERM_W:(cb + 1) * MOE_PERM_W].astype(BF16)
            wp = jnp.dot(w, perm_ref[...], preferred_element_type=F32).astype(BF16)
            wg_s[:, cb * half:(cb + 1) * half] = wp[:, :half]
            wl_s[:, cb * half:(cb + 1) * half] = wp[:, half:]
        wo_s[...] = wo_ref[0, 0].astype(BF16)

    @pl.when(jnp.logical_not(valid))
    def _():
        o_ref[...] = jnp.zeros_like(o_ref)

    @pl.when(valid)
    def _():
        x = x_ref[...].astype(BF16)
        hg = jnp.dot(x, wg_s[...], preferred_element_type=F32) + bg_ref[0]
        hl = jnp.dot(x, wl_s[...], preferred_element_type=F32) + bl_ref[0]
        glu = jnp.minimum(hg, SWIGLU_LIMIT)
        lin = jnp.clip(hl, -SWIGLU_LIMIT, SWIGLU_LIMIT)
        act = glu * jax.nn.sigmoid(SWIGLU_ALPHA * glu) * (lin + 1.0)
        o_ref[...] = (jnp.dot(act.astype(BF16), wo_s[...], preferred_element_type=F32)
                      + bo_ref[0]).astype(o_ref.dtype)


def moe_grouped_ffn(x_sorted, tile_expert, tile_valid, layer, w_in, bg, bl, w_out, bo):
    p, d = x_sorted.shape
    _, e, _, f2 = w_in.shape
    f = f2 // 2
    nt = p // MOE_TM
    assert f2 % MOE_PERM_W == 0 and MOE_PERM_W % (2 * LANES) == 0
    src = jnp.arange(MOE_PERM_W)
    dst = jnp.where(src % 2 == 0, src // 2, MOE_PERM_W // 2 + src // 2)
    perm = (dst[:, None] == jnp.arange(MOE_PERM_W)[None, :]).astype(BF16)
    grid_spec = pltpu.PrefetchScalarGridSpec(
        num_scalar_prefetch=2,
        grid=(nt,),
        in_specs=[pl.BlockSpec((MOE_TM, d), lambda t, te, tv: (t * tv[t], 0)),
                  pl.BlockSpec((1, 1, d, f2), lambda t, te, tv: (layer, te[t], 0, 0)),
                  pl.BlockSpec((1, 1, f), lambda t, te, tv: (te[t], 0, 0)),
                  pl.BlockSpec((1, 1, f), lambda t, te, tv: (te[t], 0, 0)),
                  pl.BlockSpec((1, 1, f, d), lambda t, te, tv: (layer, te[t], 0, 0)),
                  pl.BlockSpec((1, 1, d), lambda t, te, tv: (te[t], 0, 0)),
                  pl.BlockSpec((MOE_PERM_W, MOE_PERM_W), lambda t, te, tv: (0, 0))],
        out_specs=pl.BlockSpec((MOE_TM, d), lambda t, te, tv: (t, 0)),
        scratch_shapes=[pltpu.VMEM((d, f), BF16), pltpu.VMEM((d, f), BF16), pltpu.VMEM((f, d), BF16)],
    )
    return pl.pallas_call(
        _moe_ffn_kernel,
        out_shape=jax.ShapeDtypeStruct((p, d), BF16),
        grid_spec=grid_spec,
        compiler_params=pltpu.CompilerParams(dimension_semantics=("arbitrary",),
                                             vmem_limit_bytes=MOE_FFN_VMEM_LIMIT),
        name="moe_grouped_ffn",
    )(tile_expert, tile_valid, x_sorted, w_in, bg, bl, w_out, bo, perm)


def _moe_combine_kernel(z_ref, gt_ref, h_ref, mods_ref, o_ref):
    gt = gt_ref[0]
    acc = gt[:, 0:1] * z_ref[0, 0].astype(F32)
    for k in range(1, TOP_K):
        acc = acc + gt[:, k:k + 1] * z_ref[k, 0].astype(F32)
    o_ref[0] = h_ref[0] + _mod_row(mods_ref, 5) * acc


def moe_combine(z4, gates, h, mods, ctx_tiles):
    k, b, lp, d = z4.shape
    nt = lp // TILE
    return pl.pallas_call(
        _moe_combine_kernel,
        out_shape=jax.ShapeDtypeStruct((b, lp, d), F32),
        grid=(b, nt),
        in_specs=[pl.BlockSpec((k, 1, TILE, d), lambda bi, i: (0, bi, i, 0)),
                  pl.BlockSpec((1, TILE, k), lambda bi, i: (bi, i, 0)),
                  pl.BlockSpec((1, TILE, d), lambda bi, i: (bi, i, 0)),
                  pl.BlockSpec((1, 1, 6, d), lambda bi, i: (bi, jnp.minimum(i + 1 - ctx_tiles, 1), 0, 0))],
        out_specs=pl.BlockSpec((1, TILE, d), lambda bi, i: (bi, i, 0)),
        compiler_params=_cparams(("arbitrary", "arbitrary")),
        name="moe_combine",
    )(z4, gates, h, mods)


DISPATCH_T = 512
DISPATCH_BLK = SUBLANES
DISPATCH_ROWS = -(-(DISPATCH_T * TOP_K + N_EXPERTS * (DISPATCH_BLK - 1)) // (2 * SUBLANES)) * (2 * SUBLANES)
DISPATCH_COPY_ROWS = (4 * DISPATCH_BLK, DISPATCH_BLK)
DISPATCH_LIST = max(DISPATCH_ROWS // DISPATCH_COPY_ROWS[0],
                    N_EXPERTS * (DISPATCH_COPY_ROWS[0] // DISPATCH_COPY_ROWS[1] - 1))


def _moe_dispatch_kernel(cnt_ref, tab_ref, ntail_ref, tail_ref, y_ref, idx_ref, off_ref, base_ref,
                         xs_ref, slot_ref, local_ref, zero_ref, sem, zero_sem):
    t = pl.program_id(0)
    last_t = pl.num_programs(0) - 1
    dt = DISPATCH_T
    idx = idx_ref[0]
    e_iota = lax.broadcasted_iota(jnp.int32, (N_EXPERTS, dt), 0)
    picked = [e_iota == idx[k:k + 1, :] for k in range(TOP_K)]
    cnt = sum(jnp.where(p, 1.0, 0.0) for p in picked).astype(BF16)
    r0 = lax.broadcasted_iota(jnp.int32, (dt, dt), 0)
    r1 = lax.broadcasted_iota(jnp.int32, (dt, dt), 1)
    earlier = jnp.where(r0 < r1, 1.0, 0.0).astype(BF16)
    rank = jnp.dot(cnt, earlier, preferred_element_type=F32)
    local_row = rank + off_ref[0].astype(F32)
    global_row = rank + base_ref[0].astype(F32)
    rows = lax.broadcasted_iota(jnp.int32, (DISPATCH_ROWS, dt), 0)
    place = None
    for k in range(TOP_K):
        pos = jnp.sum(jnp.where(picked[k], local_row, 0.0), axis=0, keepdims=True).astype(jnp.int32)
        slot_ref[0, k:k + 1, :] = jnp.sum(jnp.where(picked[k], global_row, 0.0), axis=0,
                                          keepdims=True).astype(jnp.int32)
        hit = rows == pos
        place = hit if place is None else (place | hit)
    placement = jnp.where(place, 1.0, 0.0).astype(BF16)
    buf = lax.rem(t, 2)
    local_ref[buf] = jnp.dot(placement, y_ref[...], preferred_element_type=F32)

    def seg_copy(cls, step, j):
        rows = DISPATCH_COPY_ROWS[cls]
        src_buf = jnp.where(step == t, buf, 1 - buf)
        src_blk = tab_ref[step, (2 * cls) * DISPATCH_LIST + j]
        dst_blk = tab_ref[step, (2 * cls + 1) * DISPATCH_LIST + j]
        src = local_ref.at[src_buf, pl.ds(pl.multiple_of(src_blk * DISPATCH_BLK, DISPATCH_BLK), rows), :]
        dst = xs_ref.at[pl.ds(pl.multiple_of(dst_blk * DISPATCH_BLK, DISPATCH_BLK), rows), :]
        return pltpu.make_async_copy(src, dst, sem)

    def tail_copy(e, j):
        blk = tail_ref[e] + j
        dst = xs_ref.at[pl.ds(pl.multiple_of(blk * DISPATCH_BLK, DISPATCH_BLK), DISPATCH_BLK), :]
        return pltpu.make_async_copy(zero_ref, dst, zero_sem)

    def start_all(copy, n):
        lax.fori_loop(0, n, lambda j, c: (copy(j).start(), c)[1], 0)

    def wait_all(copy, n):
        lax.fori_loop(0, n, lambda j, c: (copy(j).wait(), c)[1], 0)

    classes = range(len(DISPATCH_COPY_ROWS))

    @pl.when(t > 0)
    def _():
        for cls in classes:
            wait_all(functools.partial(seg_copy, cls, t - 1), cnt_ref[t - 1, cls])

    for cls in classes:
        start_all(functools.partial(seg_copy, cls, t), cnt_ref[t, cls])

    @pl.when(t == last_t)
    def _():
        for cls in classes:
            wait_all(functools.partial(seg_copy, cls, t), cnt_ref[t, cls])

    @pl.when(t == 0)
    def _():
        zero_ref[...] = jnp.zeros_like(zero_ref)
        for e in range(N_EXPERTS + 1):
            start_all(functools.partial(tail_copy, e), ntail_ref[e])
        for e in range(N_EXPERTS + 1):
            wait_all(functools.partial(tail_copy, e), ntail_ref[e])


def moe_dispatch(y2, idx_t, copy_cnt, copy_tab, ntail, tail_blk, loc_off, base, p_rows):
    n, d = y2.shape
    nt = n // DISPATCH_T
    grid_spec = pltpu.PrefetchScalarGridSpec(
        num_scalar_prefetch=4,
        grid=(nt,),
        in_specs=[pl.BlockSpec((DISPATCH_T, d), lambda t, *_: (t, 0)),
                  pl.BlockSpec((1, TOP_K, DISPATCH_T), lambda t, *_: (t, 0, 0)),
                  pl.BlockSpec((1, N_EXPERTS, 1), lambda t, *_: (t, 0, 0)),
                  pl.BlockSpec((1, N_EXPERTS, 1), lambda t, *_: (t, 0, 0))],
        out_specs=(pl.BlockSpec(memory_space=pl.ANY),
                   pl.BlockSpec((1, TOP_K, DISPATCH_T), lambda t, *_: (t, 0, 0))),
        scratch_shapes=[pltpu.VMEM((2, DISPATCH_ROWS, d), F32), pltpu.VMEM((DISPATCH_BLK, d), F32),
                        pltpu.SemaphoreType.DMA(()), pltpu.SemaphoreType.DMA(())],
    )
    return pl.pallas_call(
        _moe_dispatch_kernel,
        out_shape=(jax.ShapeDtypeStruct((p_rows, d), F32),
                   jax.ShapeDtypeStruct((nt, TOP_K, DISPATCH_T), jnp.int32)),
        grid_spec=grid_spec,
        compiler_params=_cparams(("arbitrary",)),
        name="moe_dispatch",
    )(copy_cnt, copy_tab, ntail, tail_blk, y2, idx_t, loc_off, base)


def moe_layer(h, y, logits, mods, layer, w_in, b_in, w_out, b_out, ctx_tiles):
    b, lp, d = h.shape
    n = b * lp
    y2 = y.reshape(n, d)
    top_val, top_idx = lax.top_k(logits.reshape(n, N_EXPERTS), TOP_K)
    gates = jax.nn.softmax(top_val, axis=-1)

    assert n % DISPATCH_T == 0
    t_n = n // DISPATCH_T
    blk = DISPATCH_BLK
    experts = jnp.arange(N_EXPERTS, dtype=jnp.int32)
    idx_t = top_idx.astype(jnp.int32).reshape(t_n, DISPATCH_T, TOP_K).transpose(0, 2, 1)
    tile_cnt = jnp.sum((idx_t[..., None] == experts).astype(jnp.int32), axis=(1, 2))
    seg = (tile_cnt + blk - 1) // blk * blk
    loc_end = jnp.cumsum(seg, axis=1)
    loc_off = loc_end - seg
    tot = jnp.sum(seg, axis=0)
    padded = (tot + MOE_TM - 1) // MOE_TM * MOE_TM
    ends = jnp.cumsum(padded)
    starts = ends - padded
    base = starts[None, :] + jnp.cumsum(seg, axis=0) - seg
    q = jnp.arange(DISPATCH_LIST, dtype=jnp.int32)
    covered = jnp.zeros_like(seg)
    tabs, cnts = [], []
    for rows in DISPATCH_COPY_ROWS:
        n_c = (seg - covered) // rows
        c_end = jnp.cumsum(n_c, axis=1)
        sel = jnp.sum((c_end[:, None, :] <= q[None, :, None]).astype(jnp.int32), axis=2)[..., None] == experts
        pick = lambda a: jnp.sum(jnp.where(sel, a[:, None, :], 0), axis=2)
        inside = (q[None, :] - pick(c_end - n_c)) * rows
        tabs += [(pick(loc_off + covered) + inside) // blk, (pick(base + covered) + inside) // blk]
        cnts.append(c_end[:, -1])
        covered = covered + n_c * rows
    copy_tab = jnp.concatenate(tabs, axis=1)
    copy_cnt = jnp.stack(cnts, axis=1)
    n_tiles = -(-(n * TOP_K + t_n * N_EXPERTS * (blk - 1)) // MOE_TM) + N_EXPERTS
    tile_start = jnp.arange(n_tiles, dtype=jnp.int32) * MOE_TM
    tile_expert = jnp.minimum(jnp.sum((tile_start[:, None] >= ends[None, :]).astype(jnp.int32), axis=1),
                              N_EXPERTS - 1)
    tile_valid = (tile_start < ends[-1]).astype(jnp.int32)

    p_rows = n_tiles * MOE_TM
    gap_len = jnp.concatenate([padded - tot, p_rows - ends[-1:]]) // blk
    gap_start = jnp.concatenate([starts + tot, ends[-1:]]) // blk
    x_sorted, slot = moe_dispatch(y2, idx_t, copy_cnt, copy_tab, gap_len, gap_start,
                                  loc_off[..., None], base[..., None], p_rows)
    bg = b_in[layer, :, 0::2].reshape(N_EXPERTS, 1, D_FF)
    bl = b_in[layer, :, 1::2].reshape(N_EXPERTS, 1, D_FF)
    z = moe_grouped_ffn(x_sorted, tile_expert, tile_valid, layer, w_in, bg, bl, w_out,
                        b_out[layer].reshape(N_EXPERTS, 1, d))
    slot_kn = slot.transpose(1, 0, 2).reshape(TOP_K, n)
    z4 = z.at[slot_kn].get(mode="promise_in_bounds").reshape(TOP_K, b, lp, d)
    return moe_combine(z4, gates.reshape(b, lp, TOP_K), h, mods, ctx_tiles)


def _rwkv_feat_kernel(h_ref, hp_ref, hn_ref, g_ref, mods_ref, mu_ref, wr_ref, wk_ref, wv_ref,
                      w1_ref, w2_ref, a1_ref, a2_ref, g1_ref, g2_ref, w0_ref, a0_ref, kk_ref, ka_ref,
                      bd_ref, r_out, v_out, kk_out, g_out, lw_out, kd_out, a_out, *, n_tiles):
    i = pl.program_id(1)
    gain, shift, scale = g_ref[...], _mod_row(mods_ref, 0), _mod_row(mods_ref, 1)
    x = _norm_mod(h_ref[0], gain, shift, scale)
    has_prev = (i >= 2).astype(F32)
    has_next = ((i >= 1) & (i < n_tiles - 1)).astype(F32)
    x_before = _norm_mod(hp_ref[0], gain, shift, scale)[SUBLANES - 1:SUBLANES] * has_prev
    x_after = _norm_mod(hn_ref[0], gain, shift, scale)[0:1] * has_next
    row = lax.broadcasted_iota(jnp.int32, x.shape, 0)
    prev = jnp.where(row == 0, x_before, pltpu.roll(x, 1, 0))
    nxt = jnp.where(row == TILE - 1, x_after, pltpu.roll(x, TILE - 1, 0))
    xx = 0.5 * (prev + nxt) - x
    xr, xw, xk, xv, xa, xg = (x + xx * mu_ref[pl.ds(j, 1), :] for j in range(6))

    def mm(a, w):
        return jnp.dot(a.astype(BF16), w, preferred_element_type=F32)

    r = mm(xr, wr_ref[...])
    k = mm(xk, wk_ref[...])
    v = mm(xv, wv_ref[...])
    g = mm(jax.nn.sigmoid(mm(xg, g1_ref[...])), g2_ref[...])
    kk = k * kk_ref[...]
    kk = kk * lax.rsqrt(mm(kk * kk, bd_ref[...]) + 1e-12)
    r_out[0] = r.astype(BF16)
    v_out[0] = v.astype(BF16)
    kk_out[0] = kk.astype(BF16)
    g_out[0] = g.astype(BF16)
    tw = jnp.tanh(mm(xw, w1_ref[...]))
    av = mm(xa, a1_ref[...])
    lane = lax.broadcasted_iota(jnp.int32, tw.shape, 1)
    half = tw.shape[1] // 2
    for d in range(2):
        in_dir = (lane >= d * half) & (lane < (d + 1) * half)
        wz = w0_ref[pl.ds(d, 1), :] + mm(jnp.where(in_dir, tw, 0.0), w2_ref[...])
        lw_out[d, 0] = -RW_DECAY_SCALE * jax.nn.sigmoid(wz)
        a = jax.nn.sigmoid(a0_ref[pl.ds(d, 1), :] + mm(jnp.where(in_dir, av, 0.0), a2_ref[...]))
        kd_out[d, 0] = (k * (1.0 + (a - 1.0) * ka_ref[...])).astype(BF16)
        a_out[d, 0] = a.astype(BF16)


def rwkv_features(h, gain, mods, mu, w_r, w_k, w_v, w0, w1, w2, a0, a1, a2, g1, g2, k_k, k_a):
    b, lt, d = h.shape
    nt = lt // TILE
    hb = TILE // SUBLANES
    nhb = lt // SUBLANES
    hid = jnp.arange(d) // RW_HEAD_DIM
    bd = (hid[:, None] == hid[None, :]).astype(BF16)
    w1c = jnp.concatenate([w1[0], w1[1]], axis=1).astype(BF16)
    a1c = jnp.concatenate([a1[0], a1[1]], axis=1).astype(BF16)
    w2c = jnp.concatenate([w2[0], w2[1]], axis=0).astype(BF16)
    a2c = jnp.concatenate([a2[0], a2[1]], axis=0).astype(BF16)
    full = lambda shape: pl.BlockSpec(shape, lambda bi, i: (0,) * len(shape))
    tile_spec = pl.BlockSpec((1, TILE, d), lambda bi, i: (bi, i, 0))
    dir_spec = pl.BlockSpec((2, 1, TILE, d), lambda bi, i: (0, bi, i, 0))
    seq_bf = jax.ShapeDtypeStruct((b, lt, d), BF16)
    return pl.pallas_call(
        functools.partial(_rwkv_feat_kernel, n_tiles=nt),
        out_shape=(seq_bf, seq_bf, seq_bf, seq_bf,
                   jax.ShapeDtypeStruct((2, b, lt, d), F32),
                   jax.ShapeDtypeStruct((2, b, lt, d), BF16),
                   jax.ShapeDtypeStruct((2, b, lt, d), BF16)),
        grid=(b, nt),
        in_specs=[tile_spec,
                  pl.BlockSpec((1, SUBLANES, d), lambda bi, i: (bi, jnp.maximum(i * hb - 1, 0), 0)),
                  pl.BlockSpec((1, SUBLANES, d), lambda bi, i: (bi, jnp.minimum((i + 1) * hb, nhb - 1), 0)),
                  full((1, d)),
                  pl.BlockSpec((1, 1, 6, d), lambda bi, i: (bi, jnp.minimum(i, 1), 0, 0)),
                  full((6, d)), full((d, d)), full((d, d)), full((d, d)),
                  full(w1c.shape), full(w2c.shape), full(a1c.shape), full(a2c.shape),
                  full(g1.shape), full(g2.shape), full((2, d)), full((2, d)), full((1, d)), full((1, d)),
                  full((d, d))],
        out_specs=(tile_spec, tile_spec, tile_spec, tile_spec, dir_spec, dir_spec, dir_spec),
        compiler_params=_cparams(("arbitrary", "arbitrary")),
        name="rwkv_features",
    )(h, h, h, gain.reshape(1, d), mods, mu, w_r.astype(BF16), w_k.astype(BF16), w_v.astype(BF16),
      w1c, w2c, a1c, a2c, g1.astype(BF16), g2.astype(BF16), w0, a0, k_k.reshape(1, d), k_a.reshape(1, d), bd)


SCAN_C = 64
SCAN_SUB = 4
SCAN_G = 4
SCAN_GW = SCAN_G * RW_HEAD_DIM


def _scan_chunk_inputs(reverse, off, r_ref, v_ref, kk_ref, lw_ref, kd_ref, a_ref):
    cc = SCAN_C
    rows = slice(off, off + cc)
    ti = lax.broadcasted_iota(jnp.int32, (cc, cc), 0)
    si = lax.broadcasted_iota(jnp.int32, (cc, cc), 1)
    before_incl = (si >= ti) if reverse else (si <= ti)
    lw = lw_ref[0, 0, rows, :]
    tri = jnp.where(before_incl, 1.0, 0.0).astype(BF16)
    lw_hi = lw.astype(BF16)
    lw_lo = (lw - lw_hi.astype(F32)).astype(BF16)
    cl = (jnp.dot(tri, lw_hi, preferred_element_type=F32) + jnp.dot(tri, lw_lo, preferred_element_type=F32))
    g_in = jnp.exp(cl)
    g_ex = jnp.exp(cl - lw)
    g_inv = jnp.exp(-cl)
    g_end = jnp.exp(jnp.sum(lw, axis=0, keepdims=True))
    kk = kk_ref[0, rows, :].astype(F32)
    rt = (r_ref[0, rows, :].astype(F32) * g_in).astype(BF16)
    at = (-kk * g_ex).astype(BF16)
    bt = (kk * a_ref[0, 0, rows, :].astype(F32) * g_inv).astype(BF16)
    kt = (kd_ref[0, 0, rows, :].astype(F32) * g_inv).astype(BF16)
    return rt, at, bt, kt, v_ref[0, rows, :], g_end


def _rwkv_scan_kernel(rf_ref, vf_ref, kkf_ref, rb_ref, vb_ref, kkb_ref, lwf_ref, kdf_ref, af_ref,
                      lwb_ref, kdb_ref, ab_ref, yf_ref, yb_ref, h_ref):
    @pl.when(pl.program_id(1) == 0)
    def _():
        h_ref[...] = jnp.zeros_like(h_ref)

    cc, gw = SCAN_C, SCAN_GW
    y_refs = (yf_ref, yb_ref)

    tg = lax.broadcasted_iota(jnp.int32, (cc, gw), 0)
    sg = lax.broadcasted_iota(jnp.int32, (cc, gw), 1) % cc
    m_strict = (sg < tg, sg > tg)
    m_incl = (sg <= tg, sg >= tg)
    eye_g = (sg == tg).astype(F32)
    bi = lax.broadcasted_iota(jnp.int32, (gw, gw), 0) // cc
    bj = lax.broadcasted_iota(jnp.int32, (gw, gw), 1) // cc
    blk = bi == bj

    def bdiag(x):
        return jnp.where(blk, jnp.concatenate([x] * SCAN_G, axis=0), jnp.zeros((), x.dtype))

    def mm(a, b):
        return jnp.dot(a, b, preferred_element_type=F32)

    def mm_nt(a, b):
        return lax.dot_general(a, b, (((1,), (1,)), ((), ())), preferred_element_type=F32)

    n_g = RW_HEADS // SCAN_G
    chains = [(d, g) for d in range(2) for g in range(n_g)]
    cols = [slice(g * gw, (g + 1) * gw) for _, g in chains]
    idx = range(len(chains))

    def state_free_part(sub):
        offs = (sub * cc, (SCAN_SUB - 1 - sub) * cc)
        ops = (_scan_chunk_inputs(False, offs[0], rf_ref, vf_ref, kkf_ref, lwf_ref, kdf_ref, af_ref),
               _scan_chunk_inputs(True, offs[1], rb_ref, vb_ref, kkb_ref, lwb_ref, kdb_ref, ab_ref))
        rt, at, bt, kt, v, g_end = ([ops[d][j][:, cols[i]] for i, (d, _) in enumerate(chains)] for j in range(6))
        v_bd = [bdiag(x) for x in v]
        mats = [mm_nt(jnp.concatenate([at[i], rt[i]], axis=0),
                      jnp.concatenate([bdiag(bt[i]), bdiag(kt[i])], axis=0)) for i in idx]
        a_ab = [jnp.where(m_strict[chains[i][0]], mats[i][:cc, :gw], 0.0) for i in idx]
        a_ak = [jnp.where(m_strict[chains[i][0]], mats[i][:cc, gw:], 0.0).astype(BF16) for i in idx]
        m_rb = [jnp.where(m_incl[chains[i][0]], mats[i][cc:, :gw], 0.0).astype(BF16) for i in idx]
        m_rk = [jnp.where(m_incl[chains[i][0]], mats[i][cc:, gw:], 0.0).astype(BF16) for i in idx]
        from_v = [mm(jnp.concatenate([a_ak[i], m_rk[i]], axis=0), v_bd[i]) for i in idx]
        ar = [jnp.concatenate([at[i], rt[i]], axis=0) for i in idx]
        bk = [jnp.concatenate([bt[i], kt[i]], axis=0) for i in idx]
        return offs, ar, bk, v, g_end, m_rb, from_v, a_ab

    def neumann_inverse(a_list):
        ks = range(len(a_list))
        s_acc = [eye_g + x for x in a_list]
        pw = [mm(x.astype(BF16), bdiag(x.astype(BF16))).astype(BF16) for x in a_list]
        n = 2
        while n < cc:
            last = 2 * n >= cc
            lhs = [s_acc[k].astype(BF16) if last else jnp.concatenate([pw[k], s_acc[k].astype(BF16)], axis=0)
                   for k in ks]
            prod = [mm(lhs[k], bdiag(pw[k])) for k in ks]
            s_acc = [s_acc[k] + (prod[k] if last else prod[k][cc:]) for k in ks]
            if not last:
                pw = [prod[k][:cc].astype(BF16) for k in ks]
            n *= 2
        return [x.astype(BF16) for x in s_acc]

    parts = [state_free_part(sub) for sub in range(SCAN_SUB)]
    inv_all = neumann_inverse([a for p in parts for a in p[-1]])
    h_cur = [h_ref[i] for i in idx]
    for sub, (offs, ar, bk, v, g_end, m_rb, from_v, _) in enumerate(parts):
        inv = inv_all[sub * len(chains):(sub + 1) * len(chains)]
        h0b = [x.astype(BF16) for x in h_cur]
        from_h = [mm(ar[i], h0b[i]) for i in idx]
        pq = [from_h[i][:cc] + from_v[i][:cc] for i in idx]
        ub = [mm(inv[i], bdiag(pq[i].astype(BF16))).astype(BF16) for i in idx]
        for i in idx:
            d = chains[i][0]
            y_refs[d][0, offs[d]:offs[d] + cc, cols[i]] = (from_h[i][cc:] + from_v[i][cc:]
                                                          + mm(m_rb[i], bdiag(ub[i])))
        nxt = []
        for i in idx:
            upd = lax.dot_general(bk[i], jnp.concatenate([ub[i], v[i]], axis=0),
                                  (((0,), (0,)), ((), ())), preferred_element_type=F32)
            ge = jnp.transpose(jnp.broadcast_to(g_end[i], (gw, gw)))
            nxt.append(ge * (h_cur[i] + jnp.where(blk, upd, 0.0)))
        h_cur = nxt
    for i in idx:
        h_ref[i] = h_cur[i]


def rwkv_scan(r, v, kk, lw, kd, a):
    b, lt, d = r.shape
    blk_rows = SCAN_SUB * SCAN_C
    nc = lt // blk_rows
    n_ctx = CTX_LEN // blk_rows
    assert SCAN_C == RW_HEAD_DIM and CTX_LEN % blk_rows == 0 and lt % blk_rows == 0

    def rev_chunk(c):
        return jnp.where(c < n_ctx, n_ctx - 1 - c, nc - 1 + n_ctx - c)

    fwd = pl.BlockSpec((1, blk_rows, d), lambda bi, c: (bi, c, 0))
    bwd = pl.BlockSpec((1, blk_rows, d), lambda bi, c: (bi, rev_chunk(c), 0))
    fwd_dir = pl.BlockSpec((1, 1, blk_rows, d), lambda bi, c: (0, bi, c, 0))
    bwd_dir = pl.BlockSpec((1, 1, blk_rows, d), lambda bi, c: (1, bi, rev_chunk(c), 0))
    y_shape = jax.ShapeDtypeStruct((b, lt, d), F32)
    return pl.pallas_call(
        _rwkv_scan_kernel,
        out_shape=(y_shape, y_shape),
        grid=(b, nc),
        in_specs=[fwd, fwd, fwd, bwd, bwd, bwd, fwd_dir, fwd_dir, fwd_dir, bwd_dir, bwd_dir, bwd_dir],
        out_specs=(fwd, bwd),
        scratch_shapes=[pltpu.VMEM((2 * RW_HEADS // SCAN_G, SCAN_GW, SCAN_GW), F32)],
        compiler_params=_cparams(("arbitrary", "arbitrary")),
        name="rwkv_scan",
    )(r, v, kk, r, v, kk, lw, kd, a, lw, kd, a)


def _rwkv_out_kernel(yf_ref, yb_ref, r_ref, v_ref, g_ref, kd_ref, h_ref, mods_ref, lnw_ref, lnb_ref, rk_ref,
                     bd_ref, wo_ref, gffn_ref, rwh_ref, rwl_ref, rb_ref, o_ref, y_ref, lg_ref):
    def head_sum(x):
        return jnp.dot(x.astype(BF16), bd_ref[...], preferred_element_type=F32)

    y = yf_ref[0] + yb_ref[0]
    inv_n = 1.0 / RW_HEAD_DIM
    mean = head_sum(y) * inv_n
    yc = y - mean
    var = head_sum(yc * yc) * inv_n
    yn = yc * lax.rsqrt(var + RW_GN_EPS) * lnw_ref[...] + lnb_ref[...]
    r = r_ref[0].astype(F32)
    kd = kd_ref[0, 0].astype(F32) + kd_ref[1, 0].astype(F32)
    yn = yn + head_sum(r * kd * rk_ref[...]) * v_ref[0].astype(F32)
    out = jnp.dot((yn * g_ref[0].astype(F32)).astype(BF16), wo_ref[...], preferred_element_type=F32)
    h_new = h_ref[0] + _mod_row(mods_ref, 2) * out
    o_ref[0] = h_new
    _router_tail(h_new, gffn_ref, mods_ref, rwh_ref, rwl_ref, rb_ref, y_ref, lg_ref)


def rwkv_readout(y_f, y_b, r, v, g, kd, h, mods, ln_w, ln_b, r_k, w_o, ffn_gain, router_w, router_b):
    b, lt, d = h.shape
    nt = lt // TILE - 1
    e = router_w.shape[1]
    r_ops, r_specs = _router_operands(ffn_gain, router_w, router_b)
    out_tile = lambda width: pl.BlockSpec((1, TILE, width), lambda bi, i: (bi, i, 0))
    hid = jnp.arange(d) // RW_HEAD_DIM
    bd = (hid[:, None] == hid[None, :]).astype(BF16)
    full = lambda shape: pl.BlockSpec(shape, lambda bi, i: (0,) * len(shape))
    tile_spec = pl.BlockSpec((1, TILE, d), lambda bi, i: (bi, i + 1, 0))
    dir_spec = pl.BlockSpec((2, 1, TILE, d), lambda bi, i: (0, bi, i + 1, 0))
    return pl.pallas_call(
        _rwkv_out_kernel,
        out_shape=(jax.ShapeDtypeStruct((b, nt * TILE, d), F32), jax.ShapeDtypeStruct((b, nt * TILE, d), BF16),
                   jax.ShapeDtypeStruct((b, nt * TILE, e), F32)),
        grid=(b, nt),
        in_specs=[tile_spec, tile_spec, tile_spec, tile_spec, tile_spec, dir_spec, tile_spec,
                  pl.BlockSpec((1, 1, 6, d), lambda bi, i: (bi, 1, 0, 0)),
                  full((1, d)), full((1, d)), full((1, d)), full((d, d)), full((d, d))] + r_specs,
        out_specs=(out_tile(d), out_tile(d), out_tile(e)),
        compiler_params=_cparams(("arbitrary", "arbitrary")),
        name="rwkv_readout",
    )(y_f, y_b, r, v, g, kd, h, mods, ln_w.reshape(1, d), ln_b.reshape(1, d), r_k.reshape(1, d), bd,
      w_o.astype(BF16), *r_ops)


def kernel(x, c, ctx, c_ctx, ada_w, ada_b, norm_mix_g, norm_ffn_g, router_w, router_b, exp_w_in, exp_b_in,
           exp_w_out, exp_b_out, ab_w_in, na_q_g, na_k_g, na_rpb, pool_w, pool_scale, ab_w_out,
           rw_mu, rw_w_r, rw_w_k, rw_w_v, rw_w_o, rw_w0, rw_w1, rw_w2, rw_a0, rw_a1, rw_a2,
           rw_g1, rw_g2, rw_k_k, rw_k_a, rw_r_k, rw_ln_w, rw_ln_b):
    b, seq, d = x.shape
    depth = ada_w.shape[0]
    assert ctx.shape[1] == CTX_LEN == TILE and seq % TILE == 0 and d == D_MODEL
    assert depth == 2, "layer schedule below is written for [neighbourhood/pool layer, RWKV layer]"
    rows = seq // GRID_W
    assert rows >= NA_QROWS * NA_KTILES and NA_QROWS * NA_KTILES >= NA_QROWS + NA_KH - 1

    n_c = 1 + b
    n_c_pad = -(-n_c // SUBLANES) * SUBLANES
    cvec = jnp.concatenate([c_ctx[None], c, jnp.zeros((n_c_pad - n_c, d), F32)], axis=0)

    def layer_mods(layer):
        m = ada_mod(cvec, ada_w[layer], ada_b[layer])
        m_ctx = jnp.broadcast_to(m[0].reshape(1, 6, d), (b, 6, d))
        return jnp.stack([m_ctx, m[1:n_c].reshape(b, 6, d)], axis=1)

    def moe(h, y, logits, layer, mods, ctx_tiles):
        return moe_layer(h, y, logits, mods, layer, exp_w_in, exp_b_in, exp_w_out, exp_b_out, ctx_tiles)

    h = jnp.concatenate([ctx, x], axis=1)

    mods = layer_mods(0)
    qk, v, u = ab_proj(h, norm_mix_g[0], mods, ab_w_in[0], na_q_g[0], na_k_g[0])
    o_na = na_attention(qk, v, _na_bias_tables(na_rpb[0], rows))
    o_pool = multiscale_pool(u, pool_w[0], pool_scale[0])
    h, y, logits = out_proj_residual(o_na, o_pool, h, mods, ab_w_out[0], norm_ffn_g[0], router_w[0], router_b[0])
    h = moe(h, y, logits, 0, mods, 1)

    mods = layer_mods(1)
    r, v, kk, g, lw, kd, a = rwkv_features(h, norm_mix_g[1], mods, rw_mu[0], rw_w_r[0], rw_w_k[0], rw_w_v[0],
                                           rw_w0[0], rw_w1[0], rw_w2[0], rw_a0[0], rw_a1[0], rw_a2[0],
                                           rw_g1[0], rw_g2[0], rw_k_k[0], rw_k_a[0])
    y_f, y_b = rwkv_scan(r, v, kk, lw, kd, a)
    h_lat, y, logits = rwkv_readout(y_f, y_b, r, v, g, kd, h, mods, rw_ln_w[0], rw_ln_b[0], rw_r_k[0], rw_w_o[0],
                                    norm_ffn_g[1], router_w[1], router_b[1])
    return moe(h_lat, y, logits, 1, mods, 0)
```

```python
import functools

import jax
import jax.numpy as jnp
import numpy as np
from jax import lax
from jax.experimental import pallas as pl
from jax.experimental.pallas import tpu as pltpu

F32 = jnp.float32
BF16 = jnp.bfloat16

D_MODEL = 1024
GRID_W = 64
CTX_LEN = 256
RMS_EPS = 1e-6
NA_HEADS = 8
NA_HEAD_DIM = 64
NA_WIDTH = NA_HEADS * NA_HEAD_DIM
NA_KH = 8
NA_KW = 16
POOL_WINDOWS = (2, 4, 8, 16)
POOL_GC = 128
POOL_WIDTH = 512
RW_HEAD_DIM = 64
RW_HEADS = D_MODEL // RW_HEAD_DIM
RW_GN_EPS = 64e-5
RW_DECAY_SCALE = float(np.exp(-0.5))
N_EXPERTS = 32
TOP_K = 4
D_FF = D_MODEL
SWIGLU_LIMIT = 7.0
SWIGLU_ALPHA = 1.702

TILE = 256
LANES = 128
SUBLANES = 8
MOE_TM = 512
NEG_BIG = -1e30
VMEM_LIMIT = 48 * 1024 * 1024
MOE_FFN_VMEM_LIMIT = 56 * 1024 * 1024


def _cparams(sem):
    return pltpu.CompilerParams(dimension_semantics=sem, vmem_limit_bytes=VMEM_LIMIT)


def _mod_row(mods_ref, k):
    return mods_ref[0, 0, pl.ds(k, 1), :]


def _norm_mod(x, gain, shift, scale):
    ms = jnp.mean(x * x, axis=-1, keepdims=True)
    return (x * lax.rsqrt(ms + RMS_EPS) * gain) * (1.0 + scale) + shift


def _ada_kernel(c_ref, w_ref, b_ref, o_ref):
    c = c_ref[...]
    s = c * jax.nn.sigmoid(c)
    o_ref[...] = jnp.dot(s, w_ref[...], preferred_element_type=F32,
                         precision=lax.Precision.HIGHEST) + b_ref[...]


def ada_mod(cvec, w, b):
    r, d = cvec.shape
    n = w.shape[1]
    tn = 1536
    return pl.pallas_call(
        _ada_kernel,
        out_shape=jax.ShapeDtypeStruct((r, n), F32),
        grid=(n // tn,),
        in_specs=[pl.BlockSpec((r, d), lambda j: (0, 0)),
                  pl.BlockSpec((d, tn), lambda j: (0, j)),
                  pl.BlockSpec((1, tn), lambda j: (0, j))],
        out_specs=pl.BlockSpec((r, tn), lambda j: (0, j)),
        compiler_params=_cparams(("arbitrary",)),
        name="ada_mod",
    )(cvec, w, b.reshape(1, n))


def _ab_proj_kernel(h_ref, g_ref, mods_ref, w_ref, bd_ref, qkg_ref, qk_ref, v_ref, u_ref):
    x = _norm_mod(h_ref[0], g_ref[...], _mod_row(mods_ref, 0), _mod_row(mods_ref, 1))
    p = jnp.dot(x.astype(BF16), w_ref[...], preferred_element_type=F32)
    qk = p[:, :2 * NA_WIDTH]
    ss = jnp.dot((qk * qk).astype(BF16), bd_ref[...], preferred_element_type=F32)
    qk_ref[0] = (qk * lax.rsqrt(ss + RMS_EPS) * qkg_ref[...]).astype(BF16)
    v_ref[0] = p[:, 2 * NA_WIDTH:3 * NA_WIDTH].astype(BF16)
    u_ref[0] = p[:, 3 * NA_WIDTH:]


def ab_proj(h, gain, mods, w_in, q_g, k_g):
    b, lt, d = h.shape
    nt = lt // TILE
    n_in = w_in.shape[1]
    hid = jnp.arange(2 * NA_WIDTH) // NA_HEAD_DIM
    bd = jnp.where(hid[:, None] == hid[None, :], 1.0 / NA_HEAD_DIM, 0.0).astype(BF16)
    qkg = jnp.concatenate([jnp.tile(q_g, NA_HEADS), jnp.tile(k_g, NA_HEADS)]).reshape(1, -1).astype(F32)
    return pl.pallas_call(
        _ab_proj_kernel,
        out_shape=(jax.ShapeDtypeStruct((b, lt, 2 * NA_WIDTH), BF16),
                   jax.ShapeDtypeStruct((b, lt, NA_WIDTH), BF16),
                   jax.ShapeDtypeStruct((b, lt, POOL_WIDTH), F32)),
        grid=(b, nt),
        in_specs=[pl.BlockSpec((1, TILE, d), lambda bi, i: (bi, i, 0)),
                  pl.BlockSpec((1, d), lambda bi, i: (0, 0)),
                  pl.BlockSpec((1, 1, 6, d), lambda bi, i: (bi, jnp.minimum(i, 1), 0, 0)),
                  pl.BlockSpec((d, n_in), lambda bi, i: (0, 0)),
                  pl.BlockSpec((2 * NA_WIDTH, 2 * NA_WIDTH), lambda bi, i: (0, 0)),
                  pl.BlockSpec((1, 2 * NA_WIDTH), lambda bi, i: (0, 0))],
        out_specs=(pl.BlockSpec((1, TILE, 2 * NA_WIDTH), lambda bi, i: (bi, i, 0)),
                   pl.BlockSpec((1, TILE, NA_WIDTH), lambda bi, i: (bi, i, 0)),
                   pl.BlockSpec((1, TILE, POOL_WIDTH), lambda bi, i: (bi, i, 0))),
        compiler_params=_cparams(("arbitrary", "arbitrary")),
        name="ab_proj",
    )(h, gain.reshape(1, d), mods, w_in.astype(BF16), bd, qkg)


NA_QROWS = TILE // GRID_W
NA_KTILES = 3
NA_NLOC = NA_KTILES * TILE
NA_NKEY = NA_NLOC + CTX_LEN


def _na_bias_tables(rpb, rows):
    n_r, n_c = 2 * NA_KH - 1, 2 * NA_KW - 1
    col = np.arange(GRID_W)
    c_start = np.clip(col - NA_KW // 2, 0, GRID_W - NA_KW)
    valid_c = (col[None, :] >= c_start[:, None]) & (col[None, :] < c_start[:, None] + NA_KW)
    c_rel = np.clip(col[None, :] - col[:, None] + NA_KW - 1, 0, n_c - 1)
    sel_c = (c_rel.reshape(-1)[None, :] == np.arange(n_c)[:, None]).astype(np.float32)
    by_col = jnp.einsum("hrc,cn->hrn", rpb.astype(F32), sel_c, precision=lax.Precision.HIGHEST)
    n_qt = rows // NA_QROWS
    qr, kr = np.arange(NA_QROWS), np.arange(NA_KTILES * NA_QROWS)
    tabs = []
    for jq in (0, 1, n_qt - 1):
        w0 = NA_QROWS * min(max(jq - 1, 0), n_qt - NA_KTILES)
        q_row = NA_QROWS * jq + qr
        key_row = w0 + kr
        r_start = np.clip(q_row - NA_KH // 2, 0, rows - NA_KH)
        valid_r = (key_row[None, :] >= r_start[:, None]) & (key_row[None, :] < r_start[:, None] + NA_KH)
        r_rel = np.clip(key_row[None, :] - q_row[:, None] + NA_KH - 1, 0, n_r - 1)
        sel_r = (r_rel.reshape(-1)[:, None] == np.arange(n_r)[None, :]).astype(np.float32)
        t = jnp.einsum("xr,hrn->hxn", sel_r, by_col, precision=lax.Precision.HIGHEST)
        t = t.reshape(NA_HEADS, NA_QROWS, NA_KTILES * NA_QROWS, GRID_W, GRID_W)
        t = t.transpose(0, 1, 3, 2, 4).reshape(NA_HEADS, TILE, NA_NLOC)
        valid = (valid_r[:, None, :, None] & valid_c[None, :, None, :]).reshape(TILE, NA_NLOC)
        tabs.append(jnp.where(valid[None], t, NEG_BIG))
    tabs = [jnp.full_like(tabs[0], NEG_BIG)] + tabs
    loc = jnp.stack(tabs)
    return jnp.concatenate([loc, jnp.zeros(loc.shape[:3] + (CTX_LEN,), F32)], axis=-1)


def _na_kernel(q_ref, k0_ref, k1_ref, k2_ref, kc_ref, v0_ref, v1_ref, v2_ref, vc_ref, bias_ref, o_ref):
    q = q_ref[0]
    kall = jnp.concatenate([k0_ref[0], k1_ref[0], k2_ref[0], kc_ref[0]], axis=0)
    vall = jnp.concatenate([v0_ref[0], v1_ref[0], v2_ref[0], vc_ref[0]], axis=0)
    lane = lax.broadcasted_iota(jnp.int32, (TILE, LANES), 1)
    scale = jnp.asarray(NA_HEAD_DIM ** -0.5, BF16)
    heads = range(NA_HEADS)
    pair = [slice((h // 2) * LANES, (h // 2 + 1) * LANES) for h in heads]
    scores = []
    for h in heads:
        lo = (h % 2) * NA_HEAD_DIM
        qh = jnp.where((lane >= lo) & (lane < lo + NA_HEAD_DIM), q[:, pair[h]], jnp.zeros((), BF16)) * scale
        s = lax.dot_general(qh, kall[:, pair[h]], (((1,), (1,)), ((), ())), preferred_element_type=F32)
        scores.append(s + bias_ref[0, h])
    probs, denom = [], []
    for s in scores:
        p = jnp.exp(s - jnp.max(s, axis=-1, keepdims=True))
        probs.append(p.astype(BF16))
        denom.append(jnp.sum(p, axis=-1, keepdims=True))
    outs = [jnp.dot(probs[h], vall[:, pair[h]], preferred_element_type=F32) / denom[h] for h in heads]
    for h in range(0, NA_HEADS, 2):
        o_ref[0, :, pair[h]] = jnp.where(lane < NA_HEAD_DIM, outs[h], outs[h + 1]).astype(BF16)


def na_attention(qk, v, bias):
    b, lt, _ = qk.shape
    nt = lt // TILE
    n_qt = nt - 1
    assert NA_HEAD_DIM ** -0.5 == 0.125 and NA_HEADS % 2 == 0
    w = NA_WIDTH

    def w0(j):
        return 1 + jnp.clip(j - 2, 0, n_qt - NA_KTILES)

    def kind(j):
        return jnp.where(j == 0, 0, jnp.where(j == 1, 1, jnp.where(j == nt - 1, 3, 2)))

    qspec = pl.BlockSpec((1, TILE, w), lambda j, bi: (bi, j, 0))
    kspecs = [pl.BlockSpec((1, TILE, w), functools.partial(lambda j, bi, t: (bi, w0(j) + t, 1), t=t))
              for t in range(NA_KTILES)]
    kcspec = pl.BlockSpec((1, TILE, w), lambda j, bi: (bi, 0, 1))
    vspecs = [pl.BlockSpec((1, TILE, w), functools.partial(lambda j, bi, t: (bi, w0(j) + t, 0), t=t))
              for t in range(NA_KTILES)]
    vcspec = pl.BlockSpec((1, TILE, w), lambda j, bi: (bi, 0, 0))
    bspec = pl.BlockSpec((1, NA_HEADS, TILE, NA_NKEY), lambda j, bi: (kind(j), 0, 0, 0))
    return pl.pallas_call(
        _na_kernel,
        out_shape=jax.ShapeDtypeStruct((b, lt, w), BF16),
        grid=(nt, b),
        in_specs=[qspec] + kspecs + [kcspec] + vspecs + [vcspec, bspec],
        out_specs=pl.BlockSpec((1, TILE, w), lambda j, bi: (bi, j, 0)),
        compiler_params=_cparams(("arbitrary", "arbitrary")),
        name="na_attention",
    )(qk, qk, qk, qk, qk, v, v, v, v, bias)


POOL_HALO = 8
POOL_ROWS = TILE + 2 * POOL_HALO


def _pool_kernel(cur_ref, prev_ref, next_ref, pw_ref, ps_ref, o_ref, xp_ref, *, n_tiles):
    i = pl.program_id(1)
    has_prev = (i >= 2).astype(F32)
    has_next = ((i >= 1) & (i < n_tiles - 1)).astype(F32)
    xp_ref[0:POOL_HALO, :] = prev_ref[0] * has_prev
    xp_ref[POOL_HALO:POOL_HALO + TILE, :] = cur_ref[0]
    xp_ref[POOL_HALO + TILE:POOL_ROWS, :] = next_ref[0] * has_next
    row = lax.broadcasted_iota(jnp.int32, (TILE, POOL_GC), 0)
    tpos = jnp.where(i == 0, row, row + (i - 1) * TILE)
    seq_len = jnp.where(i == 0, CTX_LEN, (n_tiles - 1) * TILE)
    for g, w in enumerate(POOL_WINDOWS):
        cols = slice(g * POOL_GC, (g + 1) * POOL_GC)
        x = xp_ref[:, cols]
        s = x + pltpu.roll(x, 1, 0)
        half = 1
        while 2 * half < w:
            s = pltpu.roll(s, half, 0) + pltpu.roll(s, POOL_ROWS - half, 0)
            half *= 2
        s = s[POOL_HALO:POOL_HALO + TILE]
        cnt = jnp.minimum(w // 2, tpos) + jnp.minimum(w // 2, seq_len - tpos)
        pooled = s / cnt.astype(F32) - cur_ref[0, :, cols]
        y = jnp.dot(pooled.astype(BF16), pw_ref[g], preferred_element_type=F32)
        o_ref[0, :, cols] = (y * ps_ref[:, cols]).astype(BF16)


def multiscale_pool(u, pool_w, pool_scale):
    b, lt, c = u.shape
    nt = lt // TILE
    hb = TILE // POOL_HALO
    nhb = lt // POOL_HALO
    assert POOL_HALO >= max(POOL_WINDOWS) // 2 and POOL_HALO == SUBLANES
    return pl.pallas_call(
        functools.partial(_pool_kernel, n_tiles=nt),
        out_shape=jax.ShapeDtypeStruct((b, lt, c), BF16),
        grid=(b, nt),
        in_specs=[pl.BlockSpec((1, TILE, c), lambda bi, i: (bi, i, 0)),
                  pl.BlockSpec((1, POOL_HALO, c), lambda bi, i: (bi, jnp.maximum(i * hb - 1, 0), 0)),
                  pl.BlockSpec((1, POOL_HALO, c), lambda bi, i: (bi, jnp.minimum((i + 1) * hb, nhb - 1), 0)),
                  pl.BlockSpec((len(POOL_WINDOWS), POOL_GC, POOL_GC), lambda bi, i: (0, 0, 0)),
                  pl.BlockSpec((1, c), lambda bi, i: (0, 0))],
        out_specs=pl.BlockSpec((1, TILE, c), lambda bi, i: (bi, i, 0)),
        scratch_shapes=[pltpu.VMEM((POOL_ROWS, c), F32)],
        compiler_params=_cparams(("arbitrary", "arbitrary")),
        name="multiscale_pool",
    )(u, u, u, pool_w.astype(BF16), pool_scale.reshape(1, c).astype(F32))


def _router_tail(h_new, g_ref, mods_ref, rwh_ref, rwl_ref, rb_ref, y_ref, lg_ref):
    x = _norm_mod(h_new, g_ref[...], _mod_row(mods_ref, 3), _mod_row(mods_ref, 4))
    x_hi = x.astype(BF16)
    y_ref[0] = x_hi
    x_lo = (x - x_hi.astype(F32)).astype(BF16)
    w_hi, w_lo = rwh_ref[...], rwl_ref[...]
    lg_ref[0] = (jnp.dot(x_hi, w_hi, preferred_element_type=F32)
                 + jnp.dot(x_lo, w_hi, preferred_element_type=F32)
                 + jnp.dot(x_hi, w_lo, preferred_element_type=F32)
                 + jnp.dot(x_lo, w_lo, preferred_element_type=F32)) + rb_ref[...]


def _router_operands(gain, router_w, router_b):
    d, e = router_w.shape
    head = lax.bitcast_convert_type(lax.bitcast_convert_type(router_w, jnp.uint32) & jnp.uint32(0xFFFF0000), F32)
    full = lambda shape: pl.BlockSpec(shape, lambda bi, i: (0,) * len(shape))
    return ((gain.reshape(1, d), head.astype(BF16), (router_w - head).astype(BF16), router_b.reshape(1, e)),
            [full((1, d)), full((d, e)), full((d, e)), full((1, e))])


def _out_proj_kernel(a_ref, b_ref, h_ref, mods_ref, w_ref, g_ref, rwh_ref, rwl_ref, rb_ref,
                     o_ref, y_ref, lg_ref):
    ka = a_ref.shape[-1]
    y = jnp.dot(a_ref[0], w_ref[:ka, :], preferred_element_type=F32)
    y = y + jnp.dot(b_ref[0], w_ref[ka:, :], preferred_element_type=F32)
    h_new = h_ref[0] + _mod_row(mods_ref, 2) * y
    o_ref[0] = h_new
    _router_tail(h_new, g_ref, mods_ref, rwh_ref, rwl_ref, rb_ref, y_ref, lg_ref)


def out_proj_residual(a, bb, h, mods, w, ffn_gain, router_w, router_b):
    b, lt, d = h.shape
    nt = lt // TILE
    ka, kb = a.shape[-1], bb.shape[-1]
    e = router_w.shape[1]
    r_ops, r_specs = _router_operands(ffn_gain, router_w, router_b)
    tile = lambda width: pl.BlockSpec((1, TILE, width), lambda bi, i: (bi, i, 0))
    return pl.pallas_call(
        _out_proj_kernel,
        out_shape=(jax.ShapeDtypeStruct((b, lt, d), F32), jax.ShapeDtypeStruct((b, lt, d), BF16),
                   jax.ShapeDtypeStruct((b, lt, e), F32)),
        grid=(b, nt),
        in_specs=[tile(ka), tile(kb), tile(d),
                  pl.BlockSpec((1, 1, 6, d), lambda bi, i: (bi, jnp.minimum(i, 1), 0, 0)),
                  pl.BlockSpec((ka + kb, d), lambda bi, i: (0, 0))] + r_specs,
        out_specs=(tile(d), tile(d), tile(e)),
        compiler_params=_cparams(("arbitrary", "arbitrary")),
        name="out_proj_residual",
    )(a, bb, h, mods, w.astype(BF16), *r_ops)


MOE_PERM_W = 256


def _moe_ffn_kernel(te_ref, tv_ref, x_ref, win_ref, bg_ref, bl_ref, wo_ref, bo_ref, perm_ref, o_ref,
                    wg_s, wl_s, wo_s):
    t = pl.program_id(0)
    valid = tv_ref[t] > 0
    new_expert = (t == 0) | (te_ref[t] != te_ref[jnp.maximum(t - 1, 0)])

    @pl.when(valid & new_expert)
    def _():
        half = MOE_PERM_W // 2
        for cb in range(win_ref.shape[3] // MOE_PERM_W):
            w = win_ref[0, 0, :, cb * MOE_PERM_W:(cb + 1) * MOE_PERM_W].astype(BF16)
            wp = jnp.dot(w, perm_ref[...], preferred_element_type=F32).astype(BF16)
            wg_s[:, cb * half:(cb + 1) * half] = wp[:, :half]
            wl_s[:, cb * half:(cb + 1) * half] = wp[:, half:]
        wo_s[...] = wo_ref[0, 0].astype(BF16)

    @pl.when(jnp.logical_not(valid))
    def _():
        o_ref[...] = jnp.zeros_like(o_ref)

    @pl.when(valid)
    def _():
        x = x_ref[...].astype(BF16)
        hg = jnp.dot(x, wg_s[...], preferred_element_type=F32) + bg_ref[0]
        hl = jnp.dot(x, wl_s[...], preferred_element_type=F32) + bl_ref[0]
        glu = jnp.minimum(hg, SWIGLU_LIMIT)
        lin = jnp.clip(hl, -SWIGLU_LIMIT, SWIGLU_LIMIT)
        act = glu * jax.nn.sigmoid(SWIGLU_ALPHA * glu) * (lin + 1.0)
        o_ref[...] = (jnp.dot(act.astype(BF16), wo_s[...], preferred_element_type=F32)
                      + bo_ref[0]).astype(o_ref.dtype)


def moe_grouped_ffn(x_sorted, tile_expert, tile_valid, layer, w_in, bg, bl, w_out, bo):
    p, d = x_sorted.shape
    _, e, _, f2 = w_in.shape
    f = f2 // 2
    nt = p // MOE_TM
    assert f2 % MOE_PERM_W == 0 and MOE_PERM_W % (2 * LANES) == 0
    src = jnp.arange(MOE_PERM_W)
    dst = jnp.where(src % 2 == 0, src // 2, MOE_PERM_W // 2 + src // 2)
    perm = (dst[:, None] == jnp.arange(MOE_PERM_W)[None, :]).astype(BF16)
    grid_spec = pltpu.PrefetchScalarGridSpec(
        num_scalar_prefetch=2,
        grid=(nt,),
        in_specs=[pl.BlockSpec((MOE_TM, d), lambda t, te, tv: (t * tv[t], 0)),
                  pl.BlockSpec((1, 1, d, f2), lambda t, te, tv: (layer, te[t], 0, 0)),
                  pl.BlockSpec((1, 1, f), lambda t, te, tv: (te[t], 0, 0)),
                  pl.BlockSpec((1, 1, f), lambda t, te, tv: (te[t], 0, 0)),
                  pl.BlockSpec((1, 1, f, d), lambda t, te, tv: (layer, te[t], 0, 0)),
                  pl.BlockSpec((1, 1, d), lambda t, te, tv: (te[t], 0, 0)),
                  pl.BlockSpec((MOE_PERM_W, MOE_PERM_W), lambda t, te, tv: (0, 0))],
        out_specs=pl.BlockSpec((MOE_TM, d), lambda t, te, tv: (t, 0)),
        scratch_shapes=[pltpu.VMEM((d, f), BF16), pltpu.VMEM((d, f), BF16), pltpu.VMEM((f, d), BF16)],
    )
    return pl.pallas_call(
        _moe_ffn_kernel,
        out_shape=jax.ShapeDtypeStruct((p, d), BF16),
        grid_spec=grid_spec,
        compiler_params=pltpu.CompilerParams(dimension_semantics=("arbitrary",),
                                             vmem_limit_bytes=MOE_FFN_VMEM_LIMIT),
        name="moe_grouped_ffn",
    )(tile_expert, tile_valid, x_sorted, w_in, bg, bl, w_out, bo, perm)


def _moe_combine_kernel(z_ref, gt_ref, h_ref, mods_ref, o_ref):
    gt = gt_ref[0]
    acc = gt[:, 0:1] * z_ref[0, 0].astype(F32)
    for k in range(1, TOP_K):
        acc = acc + gt[:, k:k + 1] * z_ref[k, 0].astype(F32)
    o_ref[0] = h_ref[0] + _mod_row(mods_ref, 5) * acc


def moe_combine(z4, gates, h, mods, ctx_tiles):
    k, b, lp, d = z4.shape
    nt = lp // TILE
    return pl.pallas_call(
        _moe_combine_kernel,
        out_shape=jax.ShapeDtypeStruct((b, lp, d), F32),
        grid=(b, nt),
        in_specs=[pl.BlockSpec((k, 1, TILE, d), lambda bi, i: (0, bi, i, 0)),
                  pl.BlockSpec((1, TILE, k), lambda bi, i: (bi, i, 0)),
                  pl.BlockSpec((1, TILE, d), lambda bi, i: (bi, i, 0)),
                  pl.BlockSpec((1, 1, 6, d), lambda bi, i: (bi, jnp.minimum(i + 1 - ctx_tiles, 1), 0, 0))],
        out_specs=pl.BlockSpec((1, TILE, d), lambda bi, i: (bi, i, 0)),
        compiler_params=_cparams(("arbitrary", "arbitrary")),
        name="moe_combine",
    )(z4, gates, h, mods)


DISPATCH_T = 512
DISPATCH_BLK = SUBLANES
DISPATCH_ROWS = -(-(DISPATCH_T * TOP_K + N_EXPERTS * (DISPATCH_BLK - 1)) // (2 * SUBLANES)) * (2 * SUBLANES)
DISPATCH_COPY_ROWS = (4 * DISPATCH_BLK, DISPATCH_BLK)
DISPATCH_LIST = max(DISPATCH_ROWS // DISPATCH_COPY_ROWS[0],
                    N_EXPERTS * (DISPATCH_COPY_ROWS[0] // DISPATCH_COPY_ROWS[1] - 1))


def _moe_dispatch_kernel(cnt_ref, tab_ref, ntail_ref, tail_ref, y_ref, idx_ref, off_ref, base_ref,
                         xs_ref, slot_ref, local_ref, zero_ref, sem, zero_sem):
    t = pl.program_id(0)
    last_t = pl.num_programs(0) - 1
    dt = DISPATCH_T
    idx = idx_ref[0]
    e_iota = lax.broadcasted_iota(jnp.int32, (N_EXPERTS, dt), 0)
    picked = [e_iota == idx[k:k + 1, :] for k in range(TOP_K)]
    cnt = sum(jnp.where(p, 1.0, 0.0) for p in picked).astype(BF16)
    r0 = lax.broadcasted_iota(jnp.int32, (dt, dt), 0)
    r1 = lax.broadcasted_iota(jnp.int32, (dt, dt), 1)
    earlier = jnp.where(r0 < r1, 1.0, 0.0).astype(BF16)
    rank = jnp.dot(cnt, earlier, preferred_element_type=F32)
    local_row = rank + off_ref[0].astype(F32)
    global_row = rank + base_ref[0].astype(F32)
    rows = lax.broadcasted_iota(jnp.int32, (DISPATCH_ROWS, dt), 0)
    place = None
    for k in range(TOP_K):
        pos = jnp.sum(jnp.where(picked[k], local_row, 0.0), axis=0, keepdims=True).astype(jnp.int32)
        slot_ref[0, k:k + 1, :] = jnp.sum(jnp.where(picked[k], global_row, 0.0), axis=0,
                                          keepdims=True).astype(jnp.int32)
        hit = rows == pos
        place = hit if place is None else (place | hit)
    placement = jnp.where(place, 1.0, 0.0).astype(BF16)
    buf = lax.rem(t, 2)
    local_ref[buf] = jnp.dot(placement, y_ref[...], preferred_element_type=F32)

    def seg_copy(cls, step, j):
        rows = DISPATCH_COPY_ROWS[cls]
        src_buf = jnp.where(step == t, buf, 1 - buf)
        src_blk = tab_ref[step, (2 * cls) * DISPATCH_LIST + j]
        dst_blk = tab_ref[step, (2 * cls + 1) * DISPATCH_LIST + j]
        src = local_ref.at[src_buf, pl.ds(pl.multiple_of(src_blk * DISPATCH_BLK, DISPATCH_BLK), rows), :]
        dst = xs_ref.at[pl.ds(pl.multiple_of(dst_blk * DISPATCH_BLK, DISPATCH_BLK), rows), :]
        return pltpu.make_async_copy(src, dst, sem)

    def tail_copy(e, j):
        blk = tail_ref[e] + j
        dst = xs_ref.at[pl.ds(pl.multiple_of(blk * DISPATCH_BLK, DISPATCH_BLK), DISPATCH_BLK), :]
        return pltpu.make_async_copy(zero_ref, dst, zero_sem)

    def start_all(copy, n, priority=0):
        lax.fori_loop(0, n, lambda j, c: (copy(j).start(priority=priority), c)[1], 0)

    def wait_all(copy, n):
        lax.fori_loop(0, n, lambda j, c: (copy(j).wait(), c)[1], 0)

    classes = range(len(DISPATCH_COPY_ROWS))

    @pl.when(t > 0)
    def _():
        for cls in classes:
            wait_all(functools.partial(seg_copy, cls, t - 1), cnt_ref[t - 1, cls])

    for cls in classes:
        start_all(functools.partial(seg_copy, cls, t), cnt_ref[t, cls], priority=cls % 2)

    @pl.when(t == last_t)
    def _():
        for cls in classes:
            wait_all(functools.partial(seg_copy, cls, t), cnt_ref[t, cls])

    @pl.when(t == 0)
    def _():
        zero_ref[...] = jnp.zeros_like(zero_ref)
        for e in range(N_EXPERTS + 1):
            start_all(functools.partial(tail_copy, e), ntail_ref[e])
        for e in range(N_EXPERTS + 1):
            wait_all(functools.partial(tail_copy, e), ntail_ref[e])


def moe_dispatch(y2, idx_t, copy_cnt, copy_tab, ntail, tail_blk, loc_off, base, p_rows):
    n, d = y2.shape
    nt = n // DISPATCH_T
    grid_spec = pltpu.PrefetchScalarGridSpec(
        num_scalar_prefetch=4,
        grid=(nt,),
        in_specs=[pl.BlockSpec((DISPATCH_T, d), lambda t, *_: (t, 0)),
                  pl.BlockSpec((1, TOP_K, DISPATCH_T), lambda t, *_: (t, 0, 0)),
                  pl.BlockSpec((1, N_EXPERTS, 1), lambda t, *_: (t, 0, 0)),
                  pl.BlockSpec((1, N_EXPERTS, 1), lambda t, *_: (t, 0, 0))],
        out_specs=(pl.BlockSpec(memory_space=pl.ANY),
                   pl.BlockSpec((1, TOP_K, DISPATCH_T), lambda t, *_: (t, 0, 0))),
        scratch_shapes=[pltpu.VMEM((2, DISPATCH_ROWS, d), F32), pltpu.VMEM((DISPATCH_BLK, d), F32),
                        pltpu.SemaphoreType.DMA(()), pltpu.SemaphoreType.DMA(())],
    )
    return pl.pallas_call(
        _moe_dispatch_kernel,
        out_shape=(jax.ShapeDtypeStruct((p_rows, d), F32),
                   jax.ShapeDtypeStruct((nt, TOP_K, DISPATCH_T), jnp.int32)),
        grid_spec=grid_spec,
        compiler_params=_cparams(("arbitrary",)),
        name="moe_dispatch",
    )(copy_cnt, copy_tab, ntail, tail_blk, y2, idx_t, loc_off, base)


def moe_layer(h, y, logits, mods, layer, w_in, b_in, w_out, b_out, ctx_tiles):
    b, lp, d = h.shape
    n = b * lp
    y2 = y.reshape(n, d)
    top_val, top_idx = lax.top_k(logits.reshape(n, N_EXPERTS), TOP_K)
    gates = jax.nn.softmax(top_val, axis=-1)

    assert n % DISPATCH_T == 0
    t_n = n // DISPATCH_T
    blk = DISPATCH_BLK
    experts = jnp.arange(N_EXPERTS, dtype=jnp.int32)
    idx_t = top_idx.astype(jnp.int32).reshape(t_n, DISPATCH_T, TOP_K).transpose(0, 2, 1)
    tile_cnt = jnp.sum((idx_t[..., None] == experts).astype(jnp.int32), axis=(1, 2))
    seg = (tile_cnt + blk - 1) // blk * blk
    loc_end = jnp.cumsum(seg, axis=1)
    loc_off = loc_end - seg
    tot = jnp.sum(seg, axis=0)
    padded = (tot + MOE_TM - 1) // MOE_TM * MOE_TM
    ends = jnp.cumsum(padded)
    starts = ends - padded
    base = starts[None, :] + jnp.cumsum(seg, axis=0) - seg
    q = jnp.arange(DISPATCH_LIST, dtype=jnp.int32)
    covered = jnp.zeros_like(seg)
    tabs, cnts = [], []
    for rows in DISPATCH_COPY_ROWS:
        n_c = (seg - covered) // rows
        c_end = jnp.cumsum(n_c, axis=1)
        sel = jnp.sum((c_end[:, None, :] <= q[None, :, None]).astype(jnp.int32), axis=2)[..., None] == experts
        pick = lambda a: jnp.sum(jnp.where(sel, a[:, None, :], 0), axis=2)
        inside = (q[None, :] - pick(c_end - n_c)) * rows
        tabs += [(pick(loc_off + covered) + inside) // blk, (pick(base + covered) + inside) // blk]
        cnts.append(c_end[:, -1])
        covered = covered + n_c * rows
    copy_tab = jnp.concatenate(tabs, axis=1)
    copy_cnt = jnp.stack(cnts, axis=1)
    n_tiles = -(-(n * TOP_K + t_n * N_EXPERTS * (blk - 1)) // MOE_TM) + N_EXPERTS
    tile_start = jnp.arange(n_tiles, dtype=jnp.int32) * MOE_TM
    tile_expert = jnp.minimum(jnp.sum((tile_start[:, None] >= ends[None, :]).astype(jnp.int32), axis=1),
                              N_EXPERTS - 1)
    tile_valid = (tile_start < ends[-1]).astype(jnp.int32)

    p_rows = n_tiles * MOE_TM
    gap_len = jnp.concatenate([padded - tot, p_rows - ends[-1:]]) // blk
    gap_start = jnp.concatenate([starts + tot, ends[-1:]]) // blk
    x_sorted, slot = moe_dispatch(y2, idx_t, copy_cnt, copy_tab, gap_len, gap_start,
                                  loc_off[..., None], base[..., None], p_rows)
    bg = b_in[layer, :, 0::2].reshape(N_EXPERTS, 1, D_FF)
    bl = b_in[layer, :, 1::2].reshape(N_EXPERTS, 1, D_FF)
    z = moe_grouped_ffn(x_sorted, tile_expert, tile_valid, layer, w_in, bg, bl, w_out,
                        b_out[layer].reshape(N_EXPERTS, 1, d))
    slot_kn = slot.transpose(1, 0, 2).reshape(TOP_K, n)
    z4 = z.at[slot_kn].get(mode="promise_in_bounds").reshape(TOP_K, b, lp, d)
    return moe_combine(z4, gates.reshape(b, lp, TOP_K), h, mods, ctx_tiles)


def _rwkv_feat_kernel(h_ref, hp_ref, hn_ref, g_ref, mods_ref, mu_ref, wr_ref, wk_ref, wv_ref,
                      w1_ref, w2_ref, a1_ref, a2_ref, g1_ref, g2_ref, w0_ref, a0_ref, kk_ref, ka_ref,
                      bd_ref, r_out, v_out, kk_out, g_out, lw_out, kd_out, a_out, *, n_tiles):
    i = pl.program_id(1)
    gain, shift, scale = g_ref[...], _mod_row(mods_ref, 0), _mod_row(mods_ref, 1)
    x = _norm_mod(h_ref[0], gain, shift, scale)
    has_prev = (i >= 2).astype(F32)
    has_next = ((i >= 1) & (i < n_tiles - 1)).astype(F32)
    x_before = _norm_mod(hp_ref[0], gain, shift, scale)[SUBLANES - 1:SUBLANES] * has_prev
    x_after = _norm_mod(hn_ref[0], gain, shift, scale)[0:1] * has_next
    row = lax.broadcasted_iota(jnp.int32, x.shape, 0)
    prev = jnp.where(row == 0, x_before, pltpu.roll(x, 1, 0))
    nxt = jnp.where(row == TILE - 1, x_after, pltpu.roll(x, TILE - 1, 0))
    xx = 0.5 * (prev + nxt) - x
    xr, xw, xk, xv, xa, xg = (x + xx * mu_ref[pl.ds(j, 1), :] for j in range(6))

    def mm(a, w):
        return jnp.dot(a.astype(BF16), w, preferred_element_type=F32)

    r = mm(xr, wr_ref[...])
    k = mm(xk, wk_ref[...])
    v = mm(xv, wv_ref[...])
    g = mm(jax.nn.sigmoid(mm(xg, g1_ref[...])), g2_ref[...])
    kk = k * kk_ref[...]
    kk = kk * lax.rsqrt(mm(kk * kk, bd_ref[...]) + 1e-12)
    r_out[0] = r.astype(BF16)
    v_out[0] = v.astype(BF16)
    kk_out[0] = kk.astype(BF16)
    g_out[0] = g.astype(BF16)
    tw = jnp.tanh(mm(xw, w1_ref[...]))
    av = mm(xa, a1_ref[...])
    lane = lax.broadcasted_iota(jnp.int32, tw.shape, 1)
    half = tw.shape[1] // 2
    for d in range(2):
        in_dir = (lane >= d * half) & (lane < (d + 1) * half)
        wz = w0_ref[pl.ds(d, 1), :] + mm(jnp.where(in_dir, tw, 0.0), w2_ref[...])
        lw_out[d, 0] = -RW_DECAY_SCALE * jax.nn.sigmoid(wz)
        a = jax.nn.sigmoid(a0_ref[pl.ds(d, 1), :] + mm(jnp.where(in_dir, av, 0.0), a2_ref[...]))
        kd_out[d, 0] = (k * (1.0 + (a - 1.0) * ka_ref[...])).astype(BF16)
        a_out[d, 0] = a.astype(BF16)


def rwkv_features(h, gain, mods, mu, w_r, w_k, w_v, w0, w1, w2, a0, a1, a2, g1, g2, k_k, k_a):
    b, lt, d = h.shape
    nt = lt // TILE
    hb = TILE // SUBLANES
    nhb = lt // SUBLANES
    hid = jnp.arange(d) // RW_HEAD_DIM
    bd = (hid[:, None] == hid[None, :]).astype(BF16)
    w1c = jnp.concatenate([w1[0], w1[1]], axis=1).astype(BF16)
    a1c = jnp.concatenate([a1[0], a1[1]], axis=1).astype(BF16)
    w2c = jnp.concatenate([w2[0], w2[1]], axis=0).astype(BF16)
    a2c = jnp.concatenate([a2[0], a2[1]], axis=0).astype(BF16)
    full = lambda shape: pl.BlockSpec(shape, lambda bi, i: (0,) * len(shape))
    tile_spec = pl.BlockSpec((1, TILE, d), lambda bi, i: (bi, i, 0))
    dir_spec = pl.BlockSpec((2, 1, TILE, d), lambda bi, i: (0, bi, i, 0))
    seq_bf = jax.ShapeDtypeStruct((b, lt, d), BF16)
    return pl.pallas_call(
        functools.partial(_rwkv_feat_kernel, n_tiles=nt),
        out_shape=(seq_bf, seq_bf, seq_bf, seq_bf,
                   jax.ShapeDtypeStruct((2, b, lt, d), F32),
                   jax.ShapeDtypeStruct((2, b, lt, d), BF16),
                   jax.ShapeDtypeStruct((2, b, lt, d), BF16)),
        grid=(b, nt),
        in_specs=[tile_spec,
                  pl.BlockSpec((1, SUBLANES, d), lambda bi, i: (bi, jnp.maximum(i * hb - 1, 0), 0)),
                  pl.BlockSpec((1, SUBLANES, d), lambda bi, i: (bi, jnp.minimum((i + 1) * hb, nhb - 1), 0)),
                  full((1, d)),
                  pl.BlockSpec((1, 1, 6, d), lambda bi, i: (bi, jnp.minimum(i, 1), 0, 0)),
                  full((6, d)), full((d, d)), full((d, d)), full((d, d)),
                  full(w1c.shape), full(w2c.shape), full(a1c.shape), full(a2c.shape),
                  full(g1.shape), full(g2.shape), full((2, d)), full((2, d)), full((1, d)), full((1, d)),
                  full((d, d))],
        out_specs=(tile_spec, tile_spec, tile_spec, tile_spec, dir_spec, dir_spec, dir_spec),
        compiler_params=_cparams(("arbitrary", "arbitrary")),
        name="rwkv_features",
    )(h, h, h, gain.reshape(1, d), mods, mu, w_r.astype(BF16), w_k.astype(BF16), w_v.astype(BF16),
      w1c, w2c, a1c, a2c, g1.astype(BF16), g2.astype(BF16), w0, a0, k_k.reshape(1, d), k_a.reshape(1, d), bd)


SCAN_C = 64
SCAN_SUB = 4
SCAN_G = 4
SCAN_GW = SCAN_G * RW_HEAD_DIM


def _scan_chunk_inputs(reverse, off, r_ref, v_ref, kk_ref, lw_ref, kd_ref, a_ref):
    cc = SCAN_C
    rows = slice(off, off + cc)
    ti = lax.broadcasted_iota(jnp.int32, (cc, cc), 0)
    si = lax.broadcasted_iota(jnp.int32, (cc, cc), 1)
    before_incl = (si >= ti) if reverse else (si <= ti)
    lw = lw_ref[0, 0, rows, :]
    tri = jnp.where(before_incl, 1.0, 0.0).astype(BF16)
    lw_hi = lw.astype(BF16)
    lw_lo = (lw - lw_hi.astype(F32)).astype(BF16)
    cl = (jnp.dot(tri, lw_hi, preferred_element_type=F32) + jnp.dot(tri, lw_lo, preferred_element_type=F32))
    g_in = jnp.exp(cl)
    g_ex = jnp.exp(cl - lw)
    g_inv = jnp.exp(-cl)
    g_end = jnp.exp(jnp.sum(lw, axis=0, keepdims=True))
    kk = kk_ref[0, rows, :].astype(F32)
    rt = (r_ref[0, rows, :].astype(F32) * g_in).astype(BF16)
    at = (-kk * g_ex).astype(BF16)
    bt = (kk * a_ref[0, 0, rows, :].astype(F32) * g_inv).astype(BF16)
    kt = (kd_ref[0, 0, rows, :].astype(F32) * g_inv).astype(BF16)
    return rt, at, bt, kt, v_ref[0, rows, :], g_end


def _rwkv_scan_kernel(rf_ref, vf_ref, kkf_ref, rb_ref, vb_ref, kkb_ref, lwf_ref, kdf_ref, af_ref,
                      lwb_ref, kdb_ref, ab_ref, yf_ref, yb_ref, h_ref):
    @pl.when(pl.program_id(1) == 0)
    def _():
        h_ref[...] = jnp.zeros_like(h_ref)

    cc, gw = SCAN_C, SCAN_GW
    y_refs = (yf_ref, yb_ref)

    tg = lax.broadcasted_iota(jnp.int32, (cc, gw), 0)
    sg = lax.broadcasted_iota(jnp.int32, (cc, gw), 1) % cc
    m_strict = (sg < tg, sg > tg)
    m_incl = (sg <= tg, sg >= tg)
    eye_g = (sg == tg).astype(F32)
    bi = lax.broadcasted_iota(jnp.int32, (gw, gw), 0) // cc
    bj = lax.broadcasted_iota(jnp.int32, (gw, gw), 1) // cc
    blk = bi == bj

    def bdiag(x):
        return jnp.where(blk, jnp.concatenate([x] * SCAN_G, axis=0), jnp.zeros((), x.dtype))

    def mm(a, b):
        return jnp.dot(a, b, preferred_element_type=F32)

    def mm_nt(a, b):
        return lax.dot_general(a, b, (((1,), (1,)), ((), ())), preferred_element_type=F32)

    n_g = RW_HEADS // SCAN_G
    chains = [(d, g) for d in range(2) for g in range(n_g)]
    cols = [slice(g * gw, (g + 1) * gw) for _, g in chains]
    idx = range(len(chains))

    def state_free_part(sub):
        offs = (sub * cc, (SCAN_SUB - 1 - sub) * cc)
        ops = (_scan_chunk_inputs(False, offs[0], rf_ref, vf_ref, kkf_ref, lwf_ref, kdf_ref, af_ref),
               _scan_chunk_inputs(True, offs[1], rb_ref, vb_ref, kkb_ref, lwb_ref, kdb_ref, ab_ref))
        rt, at, bt, kt, v, g_end = ([ops[d][j][:, cols[i]] for i, (d, _) in enumerate(chains)] for j in range(6))
        v_bd = [bdiag(x) for x in v]
        mats = [mm_nt(jnp.concatenate([at[i], rt[i]], axis=0),
                      jnp.concatenate([bdiag(bt[i]), bdiag(kt[i])], axis=0)) for i in idx]
        a_ab = [jnp.where(m_strict[chains[i][0]], mats[i][:cc, :gw], 0.0) for i in idx]
        a_ak = [jnp.where(m_strict[chains[i][0]], mats[i][:cc, gw:], 0.0).astype(BF16) for i in idx]
        m_rb = [jnp.where(m_incl[chains[i][0]], mats[i][cc:, :gw], 0.0).astype(BF16) for i in idx]
        m_rk = [jnp.where(m_incl[chains[i][0]], mats[i][cc:, gw:], 0.0).astype(BF16) for i in idx]
        from_v = [mm(jnp.concatenate([a_ak[i], m_rk[i]], axis=0), v_bd[i]) for i in idx]
        ar = [jnp.concatenate([at[i], rt[i]], axis=0) for i in idx]
        bk = [jnp.concatenate([bt[i], kt[i]], axis=0) for i in idx]
        return offs, ar, bk, v, g_end, m_rb, from_v, a_ab

    def neumann_inverse(a_list):
        ks = range(len(a_list))
        s_acc = [eye_g + x for x in a_list]
        pw = [mm(x.astype(BF16), bdiag(x.astype(BF16))).astype(BF16) for x in a_list]
        n = 2
        while n < cc:
            last = 2 * n >= cc
            lhs = [s_acc[k].astype(BF16) if last else jnp.concatenate([pw[k], s_acc[k].astype(BF16)], axis=0)
                   for k in ks]
            prod = [mm(lhs[k], bdiag(pw[k])) for k in ks]
            s_acc = [s_acc[k] + (prod[k] if last else prod[k][cc:]) for k in ks]
            if not last:
                pw = [prod[k][:cc].astype(BF16) for k in ks]
            n *= 2
        return [x.astype(BF16) for x in s_acc]

    parts = [state_free_part(sub) for sub in range(SCAN_SUB)]
    inv_all = neumann_inverse([a for p in parts for a in p[-1]])
    h_cur = [h_ref[i] for i in idx]
    for sub, (offs, ar, bk, v, g_end, m_rb, from_v, _) in enumerate(parts):
        inv = inv_all[sub * len(chains):(sub + 1) * len(chains)]
        h0b = [x.astype(BF16) for x in h_cur]
        from_h = [mm(ar[i], h0b[i]) for i in idx]
        pq = [from_h[i][:cc] + from_v[i][:cc] for i in idx]
        ub = [mm(inv[i], bdiag(pq[i].astype(BF16))).astype(BF16) for i in idx]
        for i in idx:
            d = chains[i][0]
            y_refs[d][0, offs[d]:offs[d] + cc, cols[i]] = (from_h[i][cc:] + from_v[i][cc:]
                                                          + mm(m_rb[i], bdiag(ub[i])))
        nxt = []
        for i in idx:
            upd = lax.dot_general(bk[i], jnp.concatenate([ub[i], v[i]], axis=0),
                                  (((0,), (0,)), ((), ())), preferred_element_type=F32)
            ge = jnp.transpose(jnp.broadcast_to(g_end[i], (gw, gw)))
            nxt.append(ge * (h_cur[i] + jnp.where(blk, upd, 0.0)))
        h_cur = nxt
    for i in idx:
        h_ref[i] = h_cur[i]


def rwkv_scan(r, v, kk, lw, kd, a):
    b, lt, d = r.shape
    blk_rows = SCAN_SUB * SCAN_C
    nc = lt // blk_rows
    n_ctx = CTX_LEN // blk_rows
    assert SCAN_C == RW_HEAD_DIM and CTX_LEN % blk_rows == 0 and lt % blk_rows == 0

    def rev_chunk(c):
        return jnp.where(c < n_ctx, n_ctx - 1 - c, nc - 1 + n_ctx - c)

    fwd = pl.BlockSpec((1, blk_rows, d), lambda bi, c: (bi, c, 0))
    bwd = pl.BlockSpec((1, blk_rows, d), lambda bi, c: (bi, rev_chunk(c), 0))
    fwd_dir = pl.BlockSpec((1, 1, blk_rows, d), lambda bi, c: (0, bi, c, 0))
    bwd_dir = pl.BlockSpec((1, 1, blk_rows, d), lambda bi, c: (1, bi, rev_chunk(c), 0))
    y_shape = jax.ShapeDtypeStruct((b, lt, d), F32)
    return pl.pallas_call(
        _rwkv_scan_kernel,
        out_shape=(y_shape, y_shape),
        grid=(b, nc),
        in_specs=[fwd, fwd, fwd, bwd, bwd, bwd, fwd_dir, fwd_dir, fwd_dir, bwd_dir, bwd_dir, bwd_dir],
        out_specs=(fwd, bwd),
        scratch_shapes=[pltpu.VMEM((2 * RW_HEADS // SCAN_G, SCAN_GW, SCAN_GW), F32)],
        compiler_params=_cparams(("arbitrary", "arbitrary")),
        name="rwkv_scan",
    )(r, v, kk, r, v, kk, lw, kd, a, lw, kd, a)


def _rwkv_out_kernel(yf_ref, yb_ref, r_ref, v_ref, g_ref, kd_ref, h_ref, mods_ref, lnw_ref, lnb_ref, rk_ref,
                     bd_ref, wo_ref, gffn_ref, rwh_ref, rwl_ref, rb_ref, o_ref, y_ref, lg_ref):
    def head_sum(x):
        return jnp.dot(x.astype(BF16), bd_ref[...], preferred_element_type=F32)

    y = yf_ref[0] + yb_ref[0]
    inv_n = 1.0 / RW_HEAD_DIM
    mean = head_sum(y) * inv_n
    yc = y - mean
    var = head_sum(yc * yc) * inv_n
    yn = yc * lax.rsqrt(var + RW_GN_EPS) * lnw_ref[...] + lnb_ref[...]
    r = r_ref[0].astype(F32)
    kd = kd_ref[0, 0].astype(F32) + kd_ref[1, 0].astype(F32)
    yn = yn + head_sum(r * kd * rk_ref[...]) * v_ref[0].astype(F32)
    out = jnp.dot((yn * g_ref[0].astype(F32)).astype(BF16), wo_ref[...], preferred_element_type=F32)
    h_new = h_ref[0] + _mod_row(mods_ref, 2) * out
    o_ref[0] = h_new
    _router_tail(h_new, gffn_ref, mods_ref, rwh_ref, rwl_ref, rb_ref, y_ref, lg_ref)


def rwkv_readout(y_f, y_b, r, v, g, kd, h, mods, ln_w, ln_b, r_k, w_o, ffn_gain, router_w, router_b):
    b, lt, d = h.shape
    nt = lt // TILE - 1
    e = router_w.shape[1]
    r_ops, r_specs = _router_operands(ffn_gain, router_w, router_b)
    out_tile = lambda width: pl.BlockSpec((1, TILE, width), lambda bi, i: (bi, i, 0))
    hid = jnp.arange(d) // RW_HEAD_DIM
    bd = (hid[:, None] == hid[None, :]).astype(BF16)
    full = lambda shape: pl.BlockSpec(shape, lambda bi, i: (0,) * len(shape))
    tile_spec = pl.BlockSpec((1, TILE, d), lambda bi, i: (bi, i + 1, 0))
    dir_spec = pl.BlockSpec((2, 1, TILE, d), lambda bi, i: (0, bi, i + 1, 0))
    return pl.pallas_call(
        _rwkv_out_kernel,
        out_shape=(jax.ShapeDtypeStruct((b, nt * TILE, d), F32), jax.ShapeDtypeStruct((b, nt * TILE, d), BF16),
                   jax.ShapeDtypeStruct((b, nt * TILE, e), F32)),
        grid=(b, nt),
        in_specs=[tile_spec, tile_spec, tile_spec, tile_spec, tile_spec, dir_spec, tile_spec,
                  pl.BlockSpec((1, 1, 6, d), lambda bi, i: (bi, 1, 0, 0)),
                  full((1, d)), full((1, d)), full((1, d)), full((d, d)), full((d, d))] + r_specs,
        out_specs=(out_tile(d), out_tile(d), out_tile(e)),
        compiler_params=_cparams(("arbitrary", "arbitrary")),
        name="rwkv_readout",
    )(y_f, y_b, r, v, g, kd, h, mods, ln_w.reshape(1, d), ln_b.reshape(1, d), r_k.reshape(1, d), bd,
      w_o.astype(BF16), *r_ops)


def kernel(x, c, ctx, c_ctx, ada_w, ada_b, norm_mix_g, norm_ffn_g, router_w, router_b, exp_w_in, exp_b_in,
           exp_w_out, exp_b_out, ab_w_in, na_q_g, na_k_g, na_rpb, pool_w, pool_scale, ab_w_out,
           rw_mu, rw_w_r, rw_w_k, rw_w_v, rw_w_o, rw_w0, rw_w1, rw_w2, rw_a0, rw_a1, rw_a2,
           rw_g1, rw_g2, rw_k_k, rw_k_a, rw_r_k, rw_ln_w, rw_ln_b):
    b, seq, d = x.shape
    depth = ada_w.shape[0]
    assert ctx.shape[1] == CTX_LEN == TILE and seq % TILE == 0 and d == D_MODEL
    assert depth == 2, "layer schedule below is written for [neighbourhood/pool layer, RWKV layer]"
    rows = seq // GRID_W
    assert rows >= NA_QROWS * NA_KTILES and NA_QROWS * NA_KTILES >= NA_QROWS + NA_KH - 1

    n_c = 1 + b
    n_c_pad = -(-n_c // SUBLANES) * SUBLANES
    cvec = jnp.concatenate([c_ctx[None], c, jnp.zeros((n_c_pad - n_c, d), F32)], axis=0)

    def layer_mods(layer):
        m = ada_mod(cvec, ada_w[layer], ada_b[layer])
        m_ctx = jnp.broadcast_to(m[0].reshape(1, 6, d), (b, 6, d))
        return jnp.stack([m_ctx, m[1:n_c].reshape(b, 6, d)], axis=1)

    def moe(h, y, logits, layer, mods, ctx_tiles):
        return moe_layer(h, y, logits, mods, layer, exp_w_in, exp_b_in, exp_w_out, exp_b_out, ctx_tiles)

    h = jnp.concatenate([ctx, x], axis=1)

    mods = layer_mods(0)
    qk, v, u = ab_proj(h, norm_mix_g[0], mods, ab_w_in[0], na_q_g[0], na_k_g[0])
    o_na = na_attention(qk, v, _na_bias_tables(na_rpb[0], rows))
    o_pool = multiscale_pool(u, pool_w[0], pool_scale[0])
    h, y, logits = out_proj_residual(o_na, o_pool, h, mods, ab_w_out[0], norm_ffn_g[0], router_w[0], router_b[0])
    h = moe(h, y, logits, 0, mods, 1)

    mods = layer_mods(1)
    r, v, kk, g, lw, kd, a = rwkv_features(h, norm_mix_g[1], mods, rw_mu[0], rw_w_r[0], rw_w_k[0], rw_w_v[0],
                                           rw_w0[0], rw_w1[0], rw_w2[0], rw_a0[0], rw_a1[0], rw_a2[0],
                                           rw_g1[0], rw_g2[0], rw_k_k[0], rw_k_a[0])
    y_f, y_b = rwkv_scan(r, v, kk, lw, kd, a)
    h_lat, y, logits = rwkv_readout(y_f, y_b, r, v, g, kd, h, mods, rw_ln_w[0], rw_ln_b[0], rw_r_k[0], rw_w_o[0],
                                    norm_ffn_g[1], router_w[1], router_b[1])
    return moe(h_lat, y, logits, 1, mods, 0)
```
